```python
import jax
import jax.numpy as jnp
from jax import lax
import numpy as np

D_MODEL = 4096
BATCH = 2
SEQ = 8192
DEPTH = 1

HEAD_DIM = 128
ATTN_PATTERNS = ((128, 1), (512, 4), (2048, 16))
N_ATTN_GROUPS = len(ATTN_PATTERNS)
HEADS_PER_GROUP = D_MODEL // 512
ATTN_WIDTH = N_ATTN_GROUPS * HEADS_PER_GROUP * HEAD_DIM
ATTN_OUT_WIDTH = HEADS_PER_GROUP * HEAD_DIM
ATTN_BLOCK = 128
ROPE_THETA = 10000.0
CONV_CHANNELS = D_MODEL // 2
CONV_WIDTH = 31
IN_WIDTH = 3 * ATTN_WIDTH + 2 * CONV_CHANNELS + 2 * D_MODEL
N_EXPERT_GROUPS = 8
EXPERTS_PER_GROUP = 8
N_EXPERTS = N_EXPERT_GROUPS * EXPERTS_PER_GROUP
TOP_K_IN_GROUP = 2
EXPERT_FF = D_MODEL // 8
MOE_CHUNK = 128
NORM_EPS = 1e-6

kernel_name = 'hybrid_conformer_dilated_hmoe_block'


def rms_norm(x, gain):
    xf = x.astype(jnp.float32)
    y = xf * lax.rsqrt(jnp.mean(xf * xf, axis=-1, keepdims=True) + NORM_EPS)
    return y.astype(x.dtype) * gain


def layer_norm(x, gain, bias):
    xf = x.astype(jnp.float32)
    mu = jnp.mean(xf, axis=-1, keepdims=True)
    var = jnp.mean(jnp.square(xf - mu), axis=-1, keepdims=True)
    y = (xf - mu) * lax.rsqrt(var + NORM_EPS)
    return y.astype(x.dtype) * gain + bias


def rope_tables(positions, dim):
    inv_freq = ROPE_THETA ** (-jnp.arange(0, dim, 2, dtype=jnp.float32) / dim)
    ang = positions.astype(jnp.float32)[..., None] * inv_freq
    return jnp.cos(ang)[:, :, None, :], jnp.sin(ang)[:, :, None, :]


def apply_rope(t, cos, sin):
    half = t.shape[-1] // 2
    tf = t.astype(jnp.float32)
    t1, t2 = tf[..., :half], tf[..., half:]
    out = jnp.concatenate([t1 * cos - t2 * sin, t2 * cos + t1 * sin], axis=-1)
    return out.astype(t.dtype)


def dilated_window_attention(q, k, v, dilation, span):
    assert span <= ATTN_BLOCK
    B, S, H, hd = q.shape
    unit = dilation * ATTN_BLOCK
    s_pad = -(-S // unit) * unit
    L = s_pad // dilation
    nb = L // ATTN_BLOCK

    def to_blocks(t):
        t = jnp.pad(t, ((0, 0), (0, s_pad - S), (0, 0), (0, 0)))
        t = t.reshape(B, L, dilation, H, hd).transpose(0, 2, 3, 1, 4)
        return t.reshape(B, dilation, H, nb, ATTN_BLOCK, hd)

    def with_prev(t):
        prev = jnp.pad(t[:, :, :, :-1], ((0, 0), (0, 0), (0, 0), (1, 0), (0, 0), (0, 0)))
        return jnp.concatenate([prev, t], axis=4)

    qb = to_blocks(q)
    kc = with_prev(to_blocks(k))
    vc = with_prev(to_blocks(v))
    logits = jnp.einsum('brhnqd,brhnkd->brhnqk', qb, kc).astype(jnp.float32) * (hd ** -0.5)
    qi = jnp.arange(ATTN_BLOCK)[:, None]
    ki = jnp.arange(2 * ATTN_BLOCK)[None, :]
    dist = qi + ATTN_BLOCK - ki
    band = (dist >= 0) & (dist <= span)
    has_prev = (jnp.arange(nb)[:, None, None] > 0) | (ki >= ATTN_BLOCK)[None]
    mask = band[None] & has_prev
    logits = jnp.where(mask, logits, -jnp.inf)
    m = jnp.max(logits, axis=-1)
    p = jnp.exp(logits - m[..., None])
    s = jnp.sum(p, axis=-1)
    o = jnp.einsum('brhnqk,brhnkd->brhnqd', p, vc.astype(jnp.float32)) / s[..., None]
    o = o.reshape(B, dilation, H, L, hd).transpose(0, 3, 1, 2, 4).reshape(B, s_pad, H, hd)[:, :S]

    def back(t):
        return t.reshape(B, dilation, H, L).transpose(0, 3, 1, 2).reshape(B, s_pad, H)[:, :S]

    return o, back(m), back(s)


def dilated_attention_branch(q, k, v, w_attn_out):
    B, S = q.shape[:2]
    outs, maxes, denoms = [], [], []
    for g, (window, dilation) in enumerate(ATTN_PATTERNS):
        o, m, s = dilated_window_attention(q[:, :, g], k[:, :, g], v[:, :, g], dilation, window // dilation)
        outs.append(o)
        maxes.append(m)
        denoms.append(s)
    o = jnp.stack(outs)
    m = jnp.stack(maxes)
    s = jnp.stack(denoms)
    wgt = s * jnp.exp(m - jnp.max(m, axis=0, keepdims=True))
    merged = jnp.sum(wgt[..., None] * o, axis=0) / jnp.sum(wgt, axis=0)[..., None]
    return merged.reshape(B, S, ATTN_OUT_WIDTH).astype(q.dtype) @ w_attn_out


def conformer_conv_branch(glu_in, conv_dw, conv_dw_bias, ln_gain, ln_bias, w_conv_out):
    a, b = jnp.split(glu_in, 2, axis=-1)
    u = a * jax.nn.sigmoid(b)
    u = lax.conv_general_dilated(u, conv_dw[:, None, :], window_strides=(1,),
                                 padding=[(CONV_WIDTH - 1, 0)],
                                 dimension_numbers=('NWC', 'WIO', 'NWC'),
                                 feature_group_count=CONV_CHANNELS) + conv_dw_bias
    u = jax.nn.silu(layer_norm(u, ln_gain, ln_bias))
    return u @ w_conv_out


def hybrid_mixer(h, cos, sin, w_in, conv_dw, conv_dw_bias, conv_ln_gain, conv_ln_bias,
                 w_conv_out, w_attn_out, w_out):
    B, S, _ = h.shape
    proj = h @ w_in
    q, k, v, glu_in, gate_logits = jnp.split(
        proj, [ATTN_WIDTH, 2 * ATTN_WIDTH, 3 * ATTN_WIDTH, 3 * ATTN_WIDTH + 2 * CONV_CHANNELS], axis=-1)
    gh = N_ATTN_GROUPS * HEADS_PER_GROUP
    q = apply_rope(q.reshape(B, S, gh, HEAD_DIM), cos, sin).reshape(B, S, N_ATTN_GROUPS, HEADS_PER_GROUP, HEAD_DIM)
    k = apply_rope(k.reshape(B, S, gh, HEAD_DIM), cos, sin).reshape(B, S, N_ATTN_GROUPS, HEADS_PER_GROUP, HEAD_DIM)
    v = v.reshape(B, S, N_ATTN_GROUPS, HEADS_PER_GROUP, HEAD_DIM)
    attn = dilated_attention_branch(q, k, v, w_attn_out)
    conv = conformer_conv_branch(glu_in, conv_dw, conv_dw_bias, conv_ln_gain, conv_ln_bias, w_conv_out)
    gate_conv, gate_attn = jnp.split(jax.nn.sigmoid(gate_logits), 2, axis=-1)
    return (gate_conv * conv + gate_attn * attn) @ w_out


def hierarchical_moe(h, router_group, router_expert, w1, w3, w2):
    B, S, D = h.shape
    N = B * S
    t = h.reshape(N, D)
    g_logits = (t @ router_group).astype(jnp.float32)
    g_top, grp = lax.top_k(g_logits, 1)
    p_grp = jnp.exp(g_top[:, 0] - jax.nn.logsumexp(g_logits, axis=-1))
    e_logits = (t @ router_expert).astype(jnp.float32).reshape(N, N_EXPERT_GROUPS, EXPERTS_PER_GROUP)
    e_logits = jnp.take_along_axis(e_logits, grp[:, :, None], axis=1)[:, 0]
    top_val, top_idx = lax.top_k(e_logits, TOP_K_IN_GROUP)
    weight = p_grp[:, None] * jax.nn.softmax(top_val, axis=-1)
    expert = grp * EXPERTS_PER_GROUP + top_idx

    A = N * TOP_K_IN_GROUP
    flat_e = expert.reshape(A)
    flat_w = weight.reshape(A)
    flat_tok = jnp.repeat(jnp.arange(N, dtype=jnp.int32), TOP_K_IN_GROUP)
    order = jnp.argsort(flat_e)
    e_s, tok_s, w_s = flat_e[order], flat_tok[order], flat_w[order]
    counts = jnp.bincount(flat_e, length=N_EXPERTS).astype(jnp.int32)
    padded = (counts + MOE_CHUNK - 1) // MOE_CHUNK * MOE_CHUNK
    start = jnp.cumsum(counts) - counts
    pad_end = jnp.cumsum(padded)
    pad_start = pad_end - padded
    dest = pad_start[e_s] + (jnp.arange(A, dtype=jnp.int32) - start[e_s])
    n_chunks = -(-(A + N_EXPERTS * (MOE_CHUNK - 1)) // MOE_CHUNK)
    P = n_chunks * MOE_CHUNK
    buf_tok = jnp.zeros((P,), jnp.int32).at[dest].set(tok_s)
    buf_w = jnp.zeros((P,), jnp.float32).at[dest].set(w_s)
    chunk_expert = jnp.minimum(
        jnp.searchsorted(pad_end, jnp.arange(n_chunks, dtype=jnp.int32) * MOE_CHUNK, side='right'),
        N_EXPERTS - 1)

    def run_chunk(args):
        tok, e = args
        xc = t[tok]
        return (jax.nn.silu(xc @ w1[e]) * (xc @ w3[e])) @ w2[e]

    out = lax.map(run_chunk, (buf_tok.reshape(n_chunks, MOE_CHUNK), chunk_expert))
    out = out.reshape(P, D).astype(jnp.float32) * buf_w[:, None]
    y = jax.ops.segment_sum(out, buf_tok, num_segments=N)
    return y.astype(h.dtype).reshape(B, S, D)


def setup_inputs(seed: int = 0) -> dict:
    key = jax.random.key(seed)
    ks = jax.random.split(key, 22)
    f32 = jnp.float32
    D = D_MODEL
    L = DEPTH

    def nrm(k, shape, scale):
        return jax.random.normal(k, shape, f32) * scale

    def gain(k, shape):
        return 1.0 + nrm(k, shape, 0.1)

    return {
        'x': nrm(ks[0], (BATCH, SEQ, D), 1.0),
        'c': nrm(ks[1], (BATCH, D), 1.0),
        'positions': jnp.broadcast_to(jnp.arange(SEQ, dtype=jnp.int32), (BATCH, SEQ)),
        'ada_w': nrm(ks[2], (L, D, 6 * D), 0.5 * D ** -0.5),
        'ada_b': nrm(ks[3], (L, 6 * D), 0.02),
        'mix_norm_pre': gain(ks[4], (L, D)),
        'mix_norm_post': gain(ks[5], (L, D)),
        'w_in': nrm(ks[6], (L, D, IN_WIDTH), D ** -0.5),
        'conv_dw': nrm(ks[7], (L, CONV_WIDTH, CONV_CHANNELS), CONV_WIDTH ** -0.5),
        'conv_dw_bias': nrm(ks[8], (L, CONV_CHANNELS), 0.02),
        'conv_ln_gain': gain(ks[9], (L, CONV_CHANNELS)),
        'conv_ln_bias': nrm(ks[10], (L, CONV_CHANNELS), 0.02),
        'w_conv_out': nrm(ks[11], (L, CONV_CHANNELS, D), CONV_CHANNELS ** -0.5),
        'w_attn_out': nrm(ks[12], (L, ATTN_OUT_WIDTH, D), ATTN_OUT_WIDTH ** -0.5),
        'w_out': nrm(ks[13], (L, D, D), D ** -0.5),
        'ffn_norm_pre': gain(ks[14], (L, D)),
        'ffn_norm_post': gain(ks[15], (L, D)),
        'router_group': nrm(ks[16], (L, D, N_EXPERT_GROUPS), D ** -0.5),
        'router_expert': nrm(ks[17], (L, D, N_EXPERTS), D ** -0.5),
        'expert_w1': nrm(ks[18], (L, N_EXPERTS, D, EXPERT_FF), D ** -0.5),
        'expert_w3': nrm(ks[19], (L, N_EXPERTS, D, EXPERT_FF), D ** -0.5),
        'expert_w2': nrm(ks[20], (L, N_EXPERTS, EXPERT_FF, D), EXPERT_FF ** -0.5),
    }


def reference(x, c, positions, ada_w, ada_b, mix_norm_pre, mix_norm_post, w_in, conv_dw, conv_dw_bias,
              conv_ln_gain, conv_ln_bias, w_conv_out, w_attn_out, w_out, ffn_norm_pre, ffn_norm_post,
              router_group, router_expert, expert_w1, expert_w3, expert_w2):
    cos, sin = rope_tables(positions, HEAD_DIM)
    c_act = jax.nn.silu(c)
    for layer in range(DEPTH):
        mod = c_act @ ada_w[layer] + ada_b[layer]
        sh1, sc1, g1, sh2, sc2, g2 = [m[:, None, :] for m in jnp.split(mod, 6, axis=-1)]
        h = rms_norm(x, mix_norm_pre[layer]) * (1 + sc1) + sh1
        y = hybrid_mixer(h, cos, sin, w_in[layer], conv_dw[layer], conv_dw_bias[layer], conv_ln_gain[layer],
                         conv_ln_bias[layer], w_conv_out[layer], w_attn_out[layer], w_out[layer])
        x = x + g1 * rms_norm(y, mix_norm_post[layer])
        h = rms_norm(x, ffn_norm_pre[layer]) * (1 + sc2) + sh2
        y = hierarchical_moe(h, router_group[layer], router_expert[layer], expert_w1[layer],
                             expert_w3[layer], expert_w2[layer])
        x = x + g2 * rms_norm(y, ffn_norm_post[layer])
    return x
```

```python
import functools

import jax
import jax.numpy as jnp
from jax import lax
from jax.experimental import pallas as pl
from jax.experimental.pallas import tpu as pltpu

F32 = jnp.float32
BF16 = jnp.bfloat16
I32 = jnp.int32
U32 = jnp.uint32

HEAD_DIM = 128
LANES = 128
ATTN_BLOCK = 128
ATTN_PATTERNS = ((128, 1), (512, 4), (2048, 16))
ROPE_THETA = 10000.0
CONV_WIDTH = 31
CONV_HALO = 32
N_EXPERT_GROUPS = 8
EXPERTS_PER_GROUP = 8
N_EXPERTS = 64
NORM_EPS = 1e-6
NEG_BIG = -1e30
MOE_ROWS = 256
V7X_VMEM_LIMIT = 56 * 1024 * 1024


def _cparams(sem):
    return pltpu.CompilerParams(dimension_semantics=sem, vmem_limit_bytes=V7X_VMEM_LIMIT)


def _sigmoid(x):
    return 1.0 / (1.0 + jnp.exp(-x))


def _pack_halves(lo, hi):
    lo_b = lax.bitcast_convert_type(lo.astype(BF16).astype(F32), U32) >> 16
    hi_b = lax.bitcast_convert_type(hi.astype(BF16).astype(F32), U32) & jnp.uint32(0xFFFF0000)
    return hi_b | lo_b


def _unpack_halves(w):
    lo = lax.bitcast_convert_type(w << 16, F32)
    hi = lax.bitcast_convert_type(w & jnp.uint32(0xFFFF0000), F32)
    return lo, hi


def _ada_kernel(c_ref, w_ref, b_ref, o_ref):
    c = c_ref[...]
    cact = (c * _sigmoid(c)).astype(BF16)
    o_ref[...] = jnp.dot(cact, w_ref[...].astype(BF16), preferred_element_type=F32) + b_ref[...]


def ada_mod(c, ada_w, ada_b):
    B, D = c.shape
    W = ada_w.shape[1]
    rows = 8
    cp = jnp.zeros((rows, D), F32).at[:B].set(c)
    tn = min(512, W)
    out = pl.pallas_call(
        _ada_kernel,
        grid=(W // tn,),
        in_specs=[pl.BlockSpec((rows, D), lambda j: (0, 0)),
                  pl.BlockSpec((D, tn), lambda j: (0, j)),
                  pl.BlockSpec((1, tn), lambda j: (0, j))],
        out_specs=pl.BlockSpec((rows, tn), lambda j: (0, j)),
        out_shape=jax.ShapeDtypeStruct((rows, W), F32),
        compiler_params=_cparams(("parallel",)),
        name="ada_mod",
    )(cp, ada_w, ada_b.reshape(1, W))
    return out[:B]


def _prenorm_kernel(x_ref, mod_ref, gain_ref, pos_ref, invf_ref, h_ref, cos_ref, sin_ref):
    x = x_ref[...]
    ms = jnp.mean(x * x, axis=-1, keepdims=True)
    y = x * lax.rsqrt(ms + NORM_EPS) * gain_ref[...]
    y = y * (1.0 + mod_ref[1:2, :]) + mod_ref[0:1, :]
    h_ref[...] = y.astype(BF16)
    ang = pos_ref[...].astype(F32) * invf_ref[...]
    lane = lax.broadcasted_iota(I32, ang.shape, 1)
    sn = jnp.sin(ang)
    cos_ref[...] = jnp.cos(ang)
    sin_ref[...] = jnp.where(lane < HEAD_DIM // 2, -sn, sn)


def prenorm_rope(x2d, mod, gain, positions, S):
    N, D = x2d.shape
    tr = min(256, S)
    tiles_per_b = S // tr
    half = HEAD_DIM // 2
    inv = ROPE_THETA ** (-jnp.arange(0, HEAD_DIM, 2, dtype=F32) / HEAD_DIM)
    invf = jnp.concatenate([inv, inv]).reshape(1, HEAD_DIM)
    del half
    return pl.pallas_call(
        _prenorm_kernel,
        grid=(N // tr,),
        in_specs=[pl.BlockSpec((tr, D), lambda i: (i, 0)),
                  pl.BlockSpec((None, 6, D), lambda i: (i // tiles_per_b, 0, 0)),
                  pl.BlockSpec((1, D), lambda i: (0, 0)),
                  pl.BlockSpec((tr, 1), lambda i: (i, 0)),
                  pl.BlockSpec((1, HEAD_DIM), lambda i: (0, 0))],
        out_specs=[pl.BlockSpec((tr, D), lambda i: (i, 0)),
                   pl.BlockSpec((tr, HEAD_DIM), lambda i: (i, 0)),
                   pl.BlockSpec((tr, HEAD_DIM), lambda i: (i, 0))],
        out_shape=[jax.ShapeDtypeStruct((N, D), BF16),
                   jax.ShapeDtypeStruct((N, HEAD_DIM), F32),
                   jax.ShapeDtypeStruct((N, HEAD_DIM), F32)],
        compiler_params=_cparams(("parallel",)),
        name="prenorm_rope",
    )(x2d, mod, gain.reshape(1, D), positions.reshape(N, 1), invf)


def _inproj_kernel(a_ref, w_ref, cos_ref, sin_ref, o_ref, acc_ref, *, n_q, n_rope, heads, scale):
    j = pl.program_id(0)
    acc_ref[...] = jnp.dot(a_ref[...], w_ref[...], preferred_element_type=F32)

    @pl.when(j >= n_rope)
    def _():
        o_ref[...] = acc_ref[...].astype(o_ref.dtype)

    @pl.when(j < n_rope)
    def _():
        sc = jnp.where(j < n_q, scale, 1.0).astype(F32)
        c = cos_ref[...] * sc
        s = sin_ref[...] * sc
        for h in range(heads):
            t = acc_ref[:, h * HEAD_DIM:(h + 1) * HEAD_DIM]
            r = t * c + pltpu.roll(t, HEAD_DIM // 2, 1) * s
            o_ref[:, h * HEAD_DIM:(h + 1) * HEAD_DIM] = r.astype(o_ref.dtype)


def in_proj(h, w_bf16, cosf, sinf, U):
    N, D = h.shape
    IN = w_bf16.shape[1]
    tn = U
    tm = min(1024, N)
    n_q = 3
    kern = functools.partial(_inproj_kernel, n_q=n_q, n_rope=2 * n_q, heads=tn // HEAD_DIM,
                             scale=HEAD_DIM ** -0.5)
    return pl.pallas_call(
        kern,
        grid=(IN // tn, N // tm),
        in_specs=[pl.BlockSpec((tm, D), lambda j, i: (i, 0)),
                  pl.BlockSpec((D, tn), lambda j, i: (0, j)),
                  pl.BlockSpec((tm, HEAD_DIM), lambda j, i: (i, 0)),
                  pl.BlockSpec((tm, HEAD_DIM), lambda j, i: (i, 0))],
        out_specs=pl.BlockSpec((tm, tn), lambda j, i: (i, j)),
        out_shape=jax.ShapeDtypeStruct((N, IN), BF16),
        scratch_shapes=[pltpu.VMEM((tm, tn), F32)],
        compiler_params=_cparams(("parallel", "parallel")),
        name="in_proj",
    )(h, w_bf16, cosf, sinf)


def _attn_kernel(q_ref, kc_ref, kp_ref, vc_ref, vp_ref, o_ref, st_ref, kx_ref, vx_ref, *, QB, H):
    i = pl.program_id(2)
    blk = ATTN_BLOCK
    kx_ref[0:blk, :] = kp_ref[...]
    kx_ref[blk:, :] = kc_ref[...]
    vx_ref[0:blk, :] = vp_ref[...]
    vx_ref[blk:, :] = vc_ref[...]
    rows = lax.broadcasted_iota(I32, (blk, blk), 0)
    cols = lax.broadcasted_iota(I32, (blk, blk), 1)
    cur_ok = cols <= rows
    prev_band = cols >= rows
    lane = lax.broadcasted_iota(I32, (blk, LANES), 1)
    dn = (((1,), (1,)), ((), ()))

    def body(qb, carry):
        r0 = pl.multiple_of(qb * blk, blk)
        r1 = pl.multiple_of(qb * blk + blk, blk)
        prev_ok = prev_band & (i * QB + qb > 0)
        st = jnp.zeros((blk, LANES), F32)
        for h in range(H):
            hs = slice(h * HEAD_DIM, (h + 1) * HEAD_DIM)
            q = q_ref[pl.ds(r0, blk), hs]
            kp = kx_ref[pl.ds(r0, blk), hs]
            kc = kx_ref[pl.ds(r1, blk), hs]
            vp = vx_ref[pl.ds(r0, blk), hs]
            vc = vx_ref[pl.ds(r1, blk), hs]
            s_c = jnp.where(cur_ok, lax.dot_general(q, kc, dn, preferred_element_type=F32), NEG_BIG)
            s_p = jnp.where(prev_ok, lax.dot_general(q, kp, dn, preferred_element_type=F32), NEG_BIG)
            m = jnp.maximum(jnp.max(s_c, axis=-1, keepdims=True), jnp.max(s_p, axis=-1, keepdims=True))
            p_c = jnp.exp(s_c - m)
            p_p = jnp.exp(s_p - m)
            l = jnp.sum(p_c, axis=-1, keepdims=True) + jnp.sum(p_p, axis=-1, keepdims=True)
            acc = (jnp.dot(p_c.astype(BF16), vc, preferred_element_type=F32)
                   + jnp.dot(p_p.astype(BF16), vp, preferred_element_type=F32))
            o_ref[pl.ds(r0, blk), hs] = (acc / l).astype(o_ref.dtype)
            st = jnp.where(lane == h, m, st)
            st = jnp.where(lane == H + h, l, st)
        st_ref[pl.ds(r0, blk), :] = st
        return carry

    lax.fori_loop(0, QB, body, 0)


def dilated_attention(proj, B, S, U, g, d):
    H = U // HEAD_DIM
    IN = proj.shape[1]
    nu = IN // U
    L = S // d
    R = min(512, L)
    QB = R // ATTN_BLOCK
    pv = proj.reshape(B, L, d * IN)
    qmap = lambda b, r, i: (b, i, r * nu + g)
    kcmap = lambda b, r, i: (b, i, r * nu + 3 + g)
    kpmap = lambda b, r, i: (b, jnp.maximum(i * QB - 1, 0), r * nu + 3 + g)
    vcmap = lambda b, r, i: (b, i, r * nu + 6 + g)
    vpmap = lambda b, r, i: (b, jnp.maximum(i * QB - 1, 0), r * nu + 6 + g)
    kern = functools.partial(_attn_kernel, QB=QB, H=H)
    o, st = pl.pallas_call(
        kern,
        grid=(B, d, L // R),
        in_specs=[pl.BlockSpec((None, R, U), qmap),
                  pl.BlockSpec((None, R, U), kcmap),
                  pl.BlockSpec((None, ATTN_BLOCK, U), kpmap),
                  pl.BlockSpec((None, R, U), vcmap),
                  pl.BlockSpec((None, ATTN_BLOCK, U), vpmap)],
        out_specs=[pl.BlockSpec((None, R, U), lambda b, r, i: (b, i, r)),
                   pl.BlockSpec((None, R, LANES), lambda b, r, i: (b, i, r))],
        out_shape=[jax.ShapeDtypeStruct((B, L, d * U), BF16),
                   jax.ShapeDtypeStruct((B, L, d * LANES), F32)],
        scratch_shapes=[pltpu.VMEM((R + ATTN_BLOCK, U), BF16), pltpu.VMEM((R + ATTN_BLOCK, U), BF16)],
        compiler_params=_cparams(("parallel", "parallel", "parallel")),
        name=f"dilated_attn_d{d}",
    )(pv, pv, pv, pv, pv)
    return o.reshape(B * S, U), st.reshape(B * S, LANES)


def _merge_kernel(o0_ref, o1_ref, o2_ref, s0_ref, s1_ref, s2_ref, out_ref, *, H):
    o_refs = (o0_ref, o1_ref, o2_ref)
    st = [s0_ref[...], s1_ref[...], s2_ref[...]]
    mx = jnp.maximum(jnp.maximum(st[0], st[1]), st[2])
    w = [pltpu.roll(s, LANES - H, 1) * jnp.exp(s - mx) for s in st]
    tot = w[0] + w[1] + w[2]
    coef = [x / tot for x in w]
    for h in range(H):
        hs = slice(h * HEAD_DIM, (h + 1) * HEAD_DIM)
        acc = coef[0][:, h:h + 1] * o_refs[0][:, hs].astype(F32)
        acc += coef[1][:, h:h + 1] * o_refs[1][:, hs].astype(F32)
        acc += coef[2][:, h:h + 1] * o_refs[2][:, hs].astype(F32)
        out_ref[:, hs] = acc.astype(out_ref.dtype)


def merge_groups(outs, stats, U):
    N = outs[0].shape[0]
    H = U // HEAD_DIM
    tm = min(512, N)
    ospec = pl.BlockSpec((tm, U), lambda i: (i, 0))
    sspec = pl.BlockSpec((tm, LANES), lambda i: (i, 0))
    return pl.pallas_call(
        functools.partial(_merge_kernel, H=H),
        grid=(N // tm,),
        in_specs=[ospec, ospec, ospec, sspec, sspec, sspec],
        out_specs=ospec,
        out_shape=jax.ShapeDtypeStruct((N, U), BF16),
        compiler_params=_cparams(("parallel",)),
        name="merge_groups",
    )(*outs, *stats)


def _conv_kernel(a0_ref, a1_ref, b0_ref, b1_ref, ha0_ref, ha1_ref, hb0_ref, hb1_ref,
                 w_ref, bias_ref, g_ref, be_ref, o_ref, u_ref, c_ref, *, ts, U):
    i = pl.program_id(1)
    halo = CONV_HALO
    for half, (a_ref, b_ref, ha_ref, hb_ref) in enumerate(((a0_ref, b0_ref, ha0_ref, hb0_ref),
                                                            (a1_ref, b1_ref, ha1_ref, hb1_ref))):
        cs = slice(half * U, (half + 1) * U)
        u_ref[halo:halo + ts, cs] = a_ref[...].astype(F32) * _sigmoid(b_ref[...].astype(F32))
        hu = ha_ref[...].astype(F32) * _sigmoid(hb_ref[...].astype(F32))
        u_ref[0:halo, cs] = jnp.where(i > 0, hu, 0.0)
    C = 2 * U
    rc = 64
    off = halo - (CONV_WIDTH - 1)

    def chan_body(cc, carry):
        c0 = pl.multiple_of(cc * LANES, LANES)
        for rb in range(ts // rc):
            acc = jnp.zeros((rc, LANES), F32) + bias_ref[:, pl.ds(c0, LANES)]
            for j in range(CONV_WIDTH):
                acc += w_ref[j:j + 1, pl.ds(c0, LANES)] * u_ref[rb * rc + off + j:rb * rc + off + j + rc, pl.ds(c0, LANES)]
            c_ref[rb * rc:(rb + 1) * rc, pl.ds(c0, LANES)] = acc
        return carry

    lax.fori_loop(0, C // LANES, chan_body, 0)

    rn = 32

    def norm_body(rb, carry):
        r0 = pl.multiple_of(rb * rn, rn)
        v = c_ref[pl.ds(r0, rn), :]
        mu = jnp.mean(v, axis=-1, keepdims=True)
        dv = v - mu
        var = jnp.mean(dv * dv, axis=-1, keepdims=True)
        y = dv * lax.rsqrt(var + NORM_EPS) * g_ref[...] + be_ref[...]
        o_ref[pl.ds(r0, rn), :] = (y * _sigmoid(y)).astype(o_ref.dtype)
        return carry

    lax.fori_loop(0, ts // rn, norm_body, 0)


def conv_branch(proj, B, S, U, conv_dw, conv_bias, ln_gain, ln_bias):
    IN = proj.shape[1]
    C = 2 * U
    ts = min(256, S)
    pv = proj.reshape(B, S, IN)
    hb = ts // CONV_HALO
    cur = lambda blk: pl.BlockSpec((None, ts, U), lambda b, i, blk=blk: (b, i, blk))
    prv = lambda blk: pl.BlockSpec((None, CONV_HALO, U), lambda b, i, blk=blk: (b, jnp.maximum(i * hb - 1, 0), blk))
    vec = pl.BlockSpec((1, C), lambda b, i: (0, 0))
    out = pl.pallas_call(
        functools.partial(_conv_kernel, ts=ts, U=U),
        grid=(B, S // ts),
        in_specs=[cur(9), cur(10), cur(11), cur(12), prv(9), prv(10), prv(11), prv(12),
                  pl.BlockSpec((CONV_WIDTH, C), lambda b, i: (0, 0)), vec, vec, vec],
        out_specs=pl.BlockSpec((None, ts, C), lambda b, i: (b, i, 0)),
        out_shape=jax.ShapeDtypeStruct((B, S, C), BF16),
        scratch_shapes=[pltpu.VMEM((ts + CONV_HALO, C), F32), pltpu.VMEM((ts, C), F32)],
        compiler_params=_cparams(("parallel", "parallel")),
        name="conv_branch",
    )(pv, pv, pv, pv, pv, pv, pv, pv, conv_dw, conv_bias.reshape(1, C), ln_gain.reshape(1, C), ln_bias.reshape(1, C))
    return out.reshape(B * S, C)


def _gateproj_kernel(cn_ref, am_ref, wc_ref, wa_ref, gc_ref, ga_ref, o_ref):
    conv = jnp.dot(cn_ref[...], wc_ref[...], preferred_element_type=F32)
    z = _sigmoid(gc_ref[...].astype(F32)) * conv
    attn = jnp.dot(am_ref[...], wa_ref[...], preferred_element_type=F32)
    z += _sigmoid(ga_ref[...].astype(F32)) * attn
    o_ref[...] = z.astype(o_ref.dtype)


def gate_proj(cn, am, wc, wa, proj, U):
    N, C = cn.shape
    D = wc.shape[1]
    tn = U
    tm = min(512, N)
    return pl.pallas_call(
        _gateproj_kernel,
        grid=(D // tn, N // tm),
        in_specs=[pl.BlockSpec((tm, C), lambda j, i: (i, 0)),
                  pl.BlockSpec((tm, U), lambda j, i: (i, 0)),
                  pl.BlockSpec((C, tn), lambda j, i: (0, j)),
                  pl.BlockSpec((U, tn), lambda j, i: (0, j)),
                  pl.BlockSpec((tm, tn), lambda j, i: (i, 13 + j)),
                  pl.BlockSpec((tm, tn), lambda j, i: (i, 17 + j))],
        out_specs=pl.BlockSpec((tm, tn), lambda j, i: (i, j)),
        out_shape=jax.ShapeDtypeStruct((N, D), BF16),
        compiler_params=_cparams(("parallel", "parallel")),
        name="gate_proj",
    )(cn, am, wc, wa, proj, proj)


def _outproj_kernel(z_ref, w_ref, x_ref, mod_ref, gain_ref, o_ref, *, nk):
    k = pl.program_id(1)

    @pl.when(k == 0)
    def _():
        o_ref[...] = jnp.zeros_like(o_ref)

    o_ref[...] += jnp.dot(z_ref[...], w_ref[...], preferred_element_type=F32)

    @pl.when(k == nk - 1)
    def _():
        y = o_ref[...]
        ms = jnp.mean(y * y, axis=-1, keepdims=True)
        yn = y * lax.rsqrt(ms + NORM_EPS) * gain_ref[...]
        o_ref[...] = x_ref[...] + mod_ref[2:3, :] * yn


def out_proj(z, w, x2d, mod, gain, S):
    N, D = x2d.shape
    K = z.shape[1]
    tm = min(512, S)
    tk = min(512, K)
    tiles_per_b = S // tm
    nk = K // tk
    return pl.pallas_call(
        functools.partial(_outproj_kernel, nk=nk),
        grid=(N // tm, nk),
        in_specs=[pl.BlockSpec((tm, tk), lambda i, k: (i, k)),
                  pl.BlockSpec((tk, D), lambda i, k: (k, 0)),
                  pl.BlockSpec((tm, D), lambda i, k: (i, 0)),
                  pl.BlockSpec((None, 6, D), lambda i, k: (i // tiles_per_b, 0, 0)),
                  pl.BlockSpec((1, D), lambda i, k: (0, 0))],
        out_specs=pl.BlockSpec((tm, D), lambda i, k: (i, 0)),
        out_shape=jax.ShapeDtypeStruct((N, D), F32),
        compiler_params=_cparams(("parallel", "arbitrary")),
        name="out_proj",
    )(z, w, x2d, mod, gain.reshape(1, D))


def _router_kernel(x_ref, mod_ref, gain_ref, r_ref, hp_ref, eid_ref, wt_ref, rank_ref, cnt_ref, carry_ref, *, tr):
    step = pl.program_id(0)

    @pl.when(step == 0)
    def _():
        carry_ref[...] = jnp.zeros_like(carry_ref)

    x = x_ref[...]
    D = x.shape[1]
    ms = jnp.mean(x * x, axis=-1, keepdims=True)
    h = x * lax.rsqrt(ms + NORM_EPS) * gain_ref[...]
    h = h * (1.0 + mod_ref[4:5, :]) + mod_ref[3:4, :]
    lo = h[:, :D // 2]
    hi = h[:, D // 2:]
    hp_ref[...] = _pack_halves(lo, hi)
    logits = (jnp.dot(lo.astype(BF16), r_ref[:D // 2, :], preferred_element_type=F32)
              + jnp.dot(hi.astype(BF16), r_ref[D // 2:, :], preferred_element_type=F32))
    lane = lax.broadcasted_iota(I32, logits.shape, 1)
    G = N_EXPERT_GROUPS
    is_g = lane < G
    gl = jnp.where(is_g, logits, NEG_BIG)
    gmax = jnp.max(gl, axis=-1, keepdims=True)
    grp = jnp.min(jnp.where(gl == gmax, lane, LANES), axis=-1, keepdims=True)
    p_grp = 1.0 / jnp.sum(jnp.where(is_g, jnp.exp(gl - gmax), 0.0), axis=-1, keepdims=True)
    in_grp = (lane >= G) & (lane < G + N_EXPERTS) & (((lane - G) // EXPERTS_PER_GROUP) == grp)
    el = jnp.where(in_grp, logits, NEG_BIG)
    v0 = jnp.max(el, axis=-1, keepdims=True)
    i0 = jnp.min(jnp.where(in_grp & (el == v0), lane, LANES), axis=-1, keepdims=True)
    in2 = in_grp & (lane != i0)
    el2 = jnp.where(in2, logits, NEG_BIG)
    v1 = jnp.max(el2, axis=-1, keepdims=True)
    i1 = jnp.min(jnp.where(in2 & (el2 == v1), lane, LANES), axis=-1, keepdims=True)
    e1 = jnp.exp(v1 - v0)
    w0 = p_grp / (1.0 + e1)
    w1 = p_grp * e1 / (1.0 + e1)
    ex0 = i0 - G
    ex1 = i1 - G
    eid_ref[...] = jnp.where(lane == 0, ex0, jnp.where(lane == 1, ex1, 0))
    wt_ref[...] = jnp.where(lane == 0, w0, jnp.where(lane == 1, w1, 0.0))
    oh0 = (lane == ex0).astype(F32)
    oh1 = (lane == ex1).astype(F32)
    both = oh0 + oh1
    rr = lax.broadcasted_iota(I32, (tr, tr), 0)
    cc = lax.broadcasted_iota(I32, (tr, tr), 1)
    tril = (cc < rr).astype(BF16)
    before = jnp.dot(tril, both.astype(BF16), preferred_element_type=F32) + carry_ref[0:1, :]
    rk0 = jnp.sum(before * oh0, axis=-1, keepdims=True)
    rk1 = jnp.sum(before * oh1, axis=-1, keepdims=True)
    rank_ref[...] = jnp.where(lane == 0, rk0, jnp.where(lane == 1, rk1, 0.0))
    newc = carry_ref[0:1, :] + jnp.sum(both, axis=0, keepdims=True)
    carry_ref[...] = jnp.broadcast_to(newc, carry_ref.shape)
    cnt_ref[...] = jnp.broadcast_to(newc, cnt_ref.shape)


def prenorm_router(x1, mod, gain, rcat, S):
    N, D = x1.shape
    tr = min(256, S)
    tiles_per_b = S // tr
    lane_spec = pl.BlockSpec((tr, LANES), lambda i: (i, 0))
    return pl.pallas_call(
        functools.partial(_router_kernel, tr=tr),
        grid=(N // tr,),
        in_specs=[pl.BlockSpec((tr, D), lambda i: (i, 0)),
                  pl.BlockSpec((None, 6, D), lambda i: (i // tiles_per_b, 0, 0)),
                  pl.BlockSpec((1, D), lambda i: (0, 0)),
                  pl.BlockSpec((D, LANES), lambda i: (0, 0))],
        out_specs=[pl.BlockSpec((tr, D // 2), lambda i: (i, 0)), lane_spec, lane_spec, lane_spec,
                   pl.BlockSpec((8, LANES), lambda i: (0, 0))],
        out_shape=[jax.ShapeDtypeStruct((N, D // 2), U32),
                   jax.ShapeDtypeStruct((N, LANES), I32),
                   jax.ShapeDtypeStruct((N, LANES), F32),
                   jax.ShapeDtypeStruct((N, LANES), F32),
                   jax.ShapeDtypeStruct((8, LANES), F32)],
        scratch_shapes=[pltpu.VMEM((8, LANES), F32)],
        compiler_params=_cparams(("arbitrary",)),
        name="prenorm_router",
    )(x1, mod, gain.reshape(1, D), rcat)


def _dest_kernel(eid_ref, rank_ref, start_ref, o_ref):
    lane = lax.broadcasted_iota(I32, eid_ref.shape, 1)
    eid = eid_ref[...]
    rank = rank_ref[...]
    start = start_ref[0:1, :]
    d = []
    for k in range(2):
        oh = (lane == eid[:, k:k + 1]).astype(F32)
        d.append(jnp.sum(oh * start, axis=-1, keepdims=True) + rank[:, k:k + 1])
    o_ref[...] = jnp.where(lane == 0, d[0], jnp.where(lane == 1, d[1], 0.0)).astype(I32)


def dest_rows(eid, rank, pad_start):
    N = eid.shape[0]
    tr = min(1024, N)
    spec = pl.BlockSpec((tr, LANES), lambda i: (i, 0))
    return pl.pallas_call(
        _dest_kernel,
        grid=(N // tr,),
        in_specs=[spec, spec, pl.BlockSpec((8, LANES), lambda i: (0, 0))],
        out_specs=spec,
        out_shape=jax.ShapeDtypeStruct((N, LANES), I32),
        compiler_params=_cparams(("parallel",)),
        name="dest_rows",
    )(eid, rank, pad_start)


def _scatter_kernel(dest_ref, h_ref, xs_in_ref, xs_ref, sem, *, T):
    del xs_in_ref

    def issue(t, carry):
        for k in range(2):
            d = dest_ref[0, 0, 2 * t + k]
            pltpu.make_async_copy(h_ref.at[pl.ds(t, 1)], xs_ref.at[pl.ds(d, 1)], sem).start()
        return carry

    lax.fori_loop(0, T, issue, 0)

    def drain(t, carry):
        for k in range(2):
            pltpu.make_async_copy(h_ref.at[pl.ds(0, 1)], xs_ref.at[pl.ds(0, 1)], sem).wait()
        return carry

    lax.fori_loop(0, T, drain, 0)


def scatter_rows(hp, dest, P):
    N, W = hp.shape
    T = min(256, N)
    dest_s = dest[:, :2].reshape(N // T, 1, 2 * T)
    xs0 = jnp.zeros((P, W), hp.dtype)
    return pl.pallas_call(
        functools.partial(_scatter_kernel, T=T),
        grid=(N // T,),
        in_specs=[pl.BlockSpec((1, 1, 2 * T), lambda i: (i, 0, 0), memory_space=pltpu.SMEM),
                  pl.BlockSpec((T, W), lambda i: (i, 0)),
                  pl.BlockSpec(memory_space=pl.ANY)],
        out_specs=pl.BlockSpec(memory_space=pl.ANY),
        out_shape=jax.ShapeDtypeStruct((P, W), hp.dtype),
        scratch_shapes=[pltpu.SemaphoreType.DMA(())],
        input_output_aliases={2: 0},
        compiler_params=_cparams(("arbitrary",)),
        name="scatter_rows",
    )(dest_s, hp, xs0)


def _expert_changed(ce_ref, c):
    return (c == 0) | (ce_ref[c] != ce_ref[jnp.maximum(c - 1, 0)])


def _moe1_kernel(ce_ref, nu_ref, xs_ref, w1_ref, w3_ref, o_ref, wb_ref, *, F):
    c = pl.program_id(0)

    @pl.when(c < nu_ref[0])
    def _():
        @pl.when(_expert_changed(ce_ref, c))
        def _():
            wb_ref[:, :F] = w1_ref[...].astype(BF16)
            wb_ref[:, F:] = w3_ref[...].astype(BF16)

        lo, hi = _unpack_halves(xs_ref[...])
        half = wb_ref.shape[0] // 2
        hcat = (jnp.dot(lo.astype(BF16), wb_ref[:half, :], preferred_element_type=F32)
                + jnp.dot(hi.astype(BF16), wb_ref[half:, :], preferred_element_type=F32))
        a = hcat[:, :F]
        o_ref[...] = (a * _sigmoid(a) * hcat[:, F:]).astype(o_ref.dtype)

    @pl.when(c >= nu_ref[0])
    def _():
        o_ref[...] = jnp.zeros_like(o_ref)


def _moe2_kernel(ce_ref, nu_ref, h_ref, w2_ref, o_ref, wb_ref):
    c = pl.program_id(0)

    @pl.when(c < nu_ref[0])
    def _():
        @pl.when(_expert_changed(ce_ref, c))
        def _():
            wb_ref[...] = w2_ref[...].astype(BF16)

        y = jnp.dot(h_ref[...], wb_ref[...], preferred_element_type=F32)
        half = y.shape[1] // 2
        o_ref[...] = _pack_halves(y[:, :half], y[:, half:])

    @pl.when(c >= nu_ref[0])
    def _():
        o_ref[...] = jnp.zeros_like(o_ref)


def expert_ffn(xs, chunk_expert, n_used, w1, w3, w2):
    P, Wp = xs.shape
    E, D, F = w1.shape
    TM = MOE_ROWS
    n_chunks = P // TM
    hmid = pl.pallas_call(
        functools.partial(_moe1_kernel, F=F),
        grid_spec=pltpu.PrefetchScalarGridSpec(
            num_scalar_prefetch=2,
            grid=(n_chunks,),
            in_specs=[pl.BlockSpec((TM, Wp), lambda c, ce, nu: (c, 0)),
                      pl.BlockSpec((None, D, F), lambda c, ce, nu: (ce[c], 0, 0)),
                      pl.BlockSpec((None, D, F), lambda c, ce, nu: (ce[c], 0, 0))],
            out_specs=pl.BlockSpec((TM, F), lambda c, ce, nu: (c, 0)),
            scratch_shapes=[pltpu.VMEM((D, 2 * F), BF16)]),
        out_shape=jax.ShapeDtypeStruct((P, F), BF16),
        compiler_params=_cparams(("arbitrary",)),
        name="moe_up",
    )(chunk_expert, n_used, xs, w1, w3)
    return pl.pallas_call(
        _moe2_kernel,
        grid_spec=pltpu.PrefetchScalarGridSpec(
            num_scalar_prefetch=2,
            grid=(n_chunks,),
            in_specs=[pl.BlockSpec((TM, F), lambda c, ce, nu: (c, 0)),
                      pl.BlockSpec((None, F, D), lambda c, ce, nu: (ce[c], 0, 0))],
            out_specs=pl.BlockSpec((TM, D // 2), lambda c, ce, nu: (c, 0)),
            scratch_shapes=[pltpu.VMEM((F, D), BF16)]),
        out_shape=jax.ShapeDtypeStruct((P, D // 2), U32),
        compiler_params=_cparams(("arbitrary",)),
        name="moe_down",
    )(chunk_expert, n_used, hmid, w2)


def _combine_kernel(pos_ref, wt_ref, x_ref, mod_ref, gain_ref, y_hbm, o_ref, b0_ref, b1_ref, sem, *, T):
    bufs = (b0_ref, b1_ref)

    def issue(t, carry):
        for k in range(2):
            p = pos_ref[0, 0, 2 * t + k]
            pltpu.make_async_copy(y_hbm.at[pl.ds(p, 1)], bufs[k].at[pl.ds(t, 1)], sem).start()
        return carry

    lax.fori_loop(0, T, issue, 0)

    def drain(t, carry):
        for k in range(2):
            pltpu.make_async_copy(y_hbm.at[pl.ds(0, 1)], bufs[k].at[pl.ds(0, 1)], sem).wait()
        return carry

    lax.fori_loop(0, T, drain, 0)

    wt = wt_ref[...]
    w0 = wt[:, 0:1]
    w1 = wt[:, 1:2]
    lo0, hi0 = _unpack_halves(b0_ref[...])
    lo1, hi1 = _unpack_halves(b1_ref[...])
    ylo = w0 * lo0 + w1 * lo1
    yhi = w0 * hi0 + w1 * hi1
    D = x_ref.shape[1]
    half = D // 2
    ms = (jnp.sum(ylo * ylo, axis=-1, keepdims=True) + jnp.sum(yhi * yhi, axis=-1, keepdims=True)) / D
    inv = lax.rsqrt(ms + NORM_EPS)
    o_ref[:, :half] = x_ref[:, :half] + mod_ref[5:6, :half] * (ylo * inv * gain_ref[:, :half])
    o_ref[:, half:] = x_ref[:, half:] + mod_ref[5:6, half:] * (yhi * inv * gain_ref[:, half:])


def combine(yp, dest, wts, x1, mod, gain, S):
    N, D = x1.shape
    T = min(128, S)
    tiles_per_b = S // T
    pos_s = dest[:, :2].reshape(N // T, 1, 2 * T)
    return pl.pallas_call(
        functools.partial(_combine_kernel, T=T),
        grid=(N // T,),
        in_specs=[pl.BlockSpec((1, 1, 2 * T), lambda i: (i, 0, 0), memory_space=pltpu.SMEM),
                  pl.BlockSpec((T, LANES), lambda i: (i, 0)),
                  pl.BlockSpec((T, D), lambda i: (i, 0)),
                  pl.BlockSpec((None, 6, D), lambda i: (i // tiles_per_b, 0, 0)),
                  pl.BlockSpec((1, D), lambda i: (0, 0)),
                  pl.BlockSpec(memory_space=pl.ANY)],
        out_specs=pl.BlockSpec((T, D), lambda i: (i, 0)),
        out_shape=jax.ShapeDtypeStruct((N, D), F32),
        scratch_shapes=[pltpu.VMEM((T, D // 2), U32), pltpu.VMEM((T, D // 2), U32),
                        pltpu.SemaphoreType.DMA(())],
        compiler_params=_cparams(("arbitrary",)),
        name="combine",
    )(pos_s, wts, x1, mod, gain.reshape(1, D), yp)


def _moe_layout(counts):
    TM = MOE_ROWS
    cnt = counts.astype(I32)
    padded = (cnt + TM - 1) // TM * TM
    pad_end = jnp.cumsum(padded)
    pad_start = pad_end - padded
    return pad_start, pad_end


def kernel(x, c, positions, ada_w, ada_b, mix_norm_pre, mix_norm_post, w_in, conv_dw, conv_dw_bias, conv_ln_gain, conv_ln_bias, w_conv_out, w_attn_out, w_out, ffn_norm_pre, ffn_norm_post, router_group, router_expert, expert_w1, expert_w3, expert_w2):
    B, S, D = x.shape
    N = B * S
    U = D // 4
    depth = ada_w.shape[0]
    xc = x.reshape(N, D)
    for layer in range(depth):
        mod = ada_mod(c, ada_w[layer], ada_b[layer]).reshape(B, 6, D)
        h, cosf, sinf = prenorm_rope(xc, mod, mix_norm_pre[layer], positions, S)
        proj = in_proj(h, w_in[layer].astype(BF16), cosf, sinf, U)
        outs, stats = [], []
        for g, (window, d) in enumerate(ATTN_PATTERNS):
            assert window // d == ATTN_BLOCK and S % (d * ATTN_BLOCK) == 0
            o, st = dilated_attention(proj, B, S, U, g, d)
            outs.append(o)
            stats.append(st)
        am = merge_groups(outs, stats, U)
        cn = conv_branch(proj, B, S, U, conv_dw[layer], conv_dw_bias[layer], conv_ln_gain[layer], conv_ln_bias[layer])
        z = gate_proj(cn, am, w_conv_out[layer].astype(BF16), w_attn_out[layer].astype(BF16), proj, U)
        x1 = out_proj(z, w_out[layer].astype(BF16), xc, mod, mix_norm_post[layer], S)
        rcat = jnp.zeros((D, LANES), F32)
        rcat = rcat.at[:, :N_EXPERT_GROUPS].set(router_group[layer])
        rcat = rcat.at[:, N_EXPERT_GROUPS:N_EXPERT_GROUPS + N_EXPERTS].set(router_expert[layer]).astype(BF16)
        hp, eid, wts, rank, cnt = prenorm_router(x1, mod, ffn_norm_pre[layer], rcat, S)
        TM = MOE_ROWS
        n_chunks = (2 * N + N_EXPERTS * (TM - 1)) // TM
        P = n_chunks * TM
        pad_start, pad_end = _moe_layout(cnt[0, N_EXPERT_GROUPS * 0:N_EXPERTS])
        n_used = (pad_end[-1] // TM).astype(I32).reshape(1)
        cidx = jnp.minimum(jnp.arange(n_chunks, dtype=I32), n_used[0] - 1)
        chunk_expert = jnp.minimum(jnp.searchsorted(pad_end, cidx * TM, side='right'), N_EXPERTS - 1).astype(I32)
        start_row = jnp.zeros((8, LANES), F32).at[:, :N_EXPERTS].set(pad_start.astype(F32)[None, :])
        dest = dest_rows(eid, rank, start_row)
        xs = scatter_rows(hp, dest, P)
        yp = expert_ffn(xs, chunk_expert, n_used, expert_w1[layer], expert_w3[layer], expert_w2[layer])
        xc = combine(yp, dest, wts, x1, mod, ffn_norm_post[layer], S)
    return xc.reshape(B, S, D)
```

```python
import functools

import jax
import jax.numpy as jnp
from jax import lax
from jax.experimental import pallas as pl
from jax.experimental.pallas import tpu as pltpu

F32 = jnp.float32
BF16 = jnp.bfloat16
I32 = jnp.int32
U32 = jnp.uint32

HEAD_DIM = 128
LANES = 128
ATTN_BLOCK = 128
ATTN_PATTERNS = ((128, 1), (512, 4), (2048, 16))
ROPE_THETA = 10000.0
CONV_WIDTH = 31
CONV_HALO = 32
N_EXPERT_GROUPS = 8
EXPERTS_PER_GROUP = 8
N_EXPERTS = 64
NORM_EPS = 1e-6
NEG_BIG = -1e30
MOE_ROWS = 256
V7X_VMEM_LIMIT = 56 * 1024 * 1024


def _cparams(sem):
    return pltpu.CompilerParams(dimension_semantics=sem, vmem_limit_bytes=V7X_VMEM_LIMIT)


def _sigmoid(x):
    return 1.0 / (1.0 + jnp.exp(-x))


def _pack_halves(lo, hi):
    lo_b = lax.bitcast_convert_type(lo.astype(BF16).astype(F32), U32) >> 16
    hi_b = lax.bitcast_convert_type(hi.astype(BF16).astype(F32), U32) & jnp.uint32(0xFFFF0000)
    return hi_b | lo_b


def _unpack_halves(w):
    lo = lax.bitcast_convert_type(w << 16, F32)
    hi = lax.bitcast_convert_type(w & jnp.uint32(0xFFFF0000), F32)
    return lo, hi


def _ada_kernel(c_ref, w_ref, b_ref, o_ref):
    c = c_ref[...]
    cact = (c * _sigmoid(c)).astype(BF16)
    o_ref[...] = jnp.dot(cact, w_ref[...].astype(BF16), preferred_element_type=F32) + b_ref[...]


def ada_mod(c, ada_w, ada_b):
    B, D = c.shape
    W = ada_w.shape[1]
    rows = 8
    cp = jnp.zeros((rows, D), F32).at[:B].set(c)
    tn = min(512, W)
    out = pl.pallas_call(
        _ada_kernel,
        grid=(W // tn,),
        in_specs=[pl.BlockSpec((rows, D), lambda j: (0, 0)),
                  pl.BlockSpec((D, tn), lambda j: (0, j)),
                  pl.BlockSpec((1, tn), lambda j: (0, j))],
        out_specs=pl.BlockSpec((rows, tn), lambda j: (0, j)),
        out_shape=jax.ShapeDtypeStruct((rows, W), F32),
        compiler_params=_cparams(("parallel",)),
        name="ada_mod",
    )(cp, ada_w, ada_b.reshape(1, W))
    return out[:B]


def _prenorm_kernel(x_ref, mod_ref, gain_ref, pos_ref, invf_ref, h_ref, cos_ref, sin_ref):
    x = x_ref[...]
    ms = jnp.mean(x * x, axis=-1, keepdims=True)
    y = x * lax.rsqrt(ms + NORM_EPS) * gain_ref[...]
    y = y * (1.0 + mod_ref[1:2, :]) + mod_ref[0:1, :]
    h_ref[...] = y.astype(BF16)
    ang = pos_ref[...].astype(F32) * invf_ref[...]
    lane = lax.broadcasted_iota(I32, ang.shape, 1)
    sn = jnp.sin(ang)
    cos_ref[...] = jnp.cos(ang)
    sin_ref[...] = jnp.where(lane < HEAD_DIM // 2, -sn, sn)


def prenorm_rope(x2d, mod, gain, positions, S):
    N, D = x2d.shape
    tr = min(256, S)
    tiles_per_b = S // tr
    half = HEAD_DIM // 2
    inv = ROPE_THETA ** (-jnp.arange(0, HEAD_DIM, 2, dtype=F32) / HEAD_DIM)
    invf = jnp.concatenate([inv, inv]).reshape(1, HEAD_DIM)
    del half
    return pl.pallas_call(
        _prenorm_kernel,
        grid=(N // tr,),
        in_specs=[pl.BlockSpec((tr, D), lambda i: (i, 0)),
                  pl.BlockSpec((None, 6, D), lambda i: (i // tiles_per_b, 0, 0)),
                  pl.BlockSpec((1, D), lambda i: (0, 0)),
                  pl.BlockSpec((tr, 1), lambda i: (i, 0)),
                  pl.BlockSpec((1, HEAD_DIM), lambda i: (0, 0))],
        out_specs=[pl.BlockSpec((tr, D), lambda i: (i, 0)),
                   pl.BlockSpec((tr, HEAD_DIM), lambda i: (i, 0)),
                   pl.BlockSpec((tr, HEAD_DIM), lambda i: (i, 0))],
        out_shape=[jax.ShapeDtypeStruct((N, D), BF16),
                   jax.ShapeDtypeStruct((N, HEAD_DIM), F32),
                   jax.ShapeDtypeStruct((N, HEAD_DIM), F32)],
        compiler_params=_cparams(("parallel",)),
        name="prenorm_rope",
    )(x2d, mod, gain.reshape(1, D), positions.reshape(N, 1), invf)


def _residue_rows(ref, r, d):
    if d == 1:
        return ref[...]
    return ref[pl.ds(r, ref.shape[0] // d, stride=d), :]


def _qkv_kernel(a_ref, w_ref, cos_ref, sin_ref, o_ref, acc_ref, *, d, heads, scale):
    kind = pl.program_id(0)
    acc = jnp.dot(a_ref[...], w_ref[...], preferred_element_type=F32)
    for h in range(heads):
        acc_ref[h] = acc[:, h * HEAD_DIM:(h + 1) * HEAD_DIM]

    @pl.when(kind == 2)
    def _():
        for r in range(d):
            for h in range(heads):
                hs = slice(h * HEAD_DIM, (h + 1) * HEAD_DIM)
                o_ref[r, :, hs] = _residue_rows(acc_ref.at[h], r, d).astype(o_ref.dtype)

    @pl.when(kind < 2)
    def _():
        sc = jnp.where(kind == 0, scale, 1.0).astype(F32)
        for r in range(d):
            c = _residue_rows(cos_ref, r, d) * sc
            s = _residue_rows(sin_ref, r, d) * sc
            for h in range(heads):
                hs = slice(h * HEAD_DIM, (h + 1) * HEAD_DIM)
                t = _residue_rows(acc_ref.at[h], r, d)
                o_ref[r, :, hs] = (t * c + pltpu.roll(t, HEAD_DIM // 2, 1) * s).astype(o_ref.dtype)


def qkv_proj(h, w_bf16, cosf, sinf, B, S, U, g, d):
    N, D = h.shape
    tm = min(1024, S)
    tiles_per_b = S // tm
    kern = functools.partial(_qkv_kernel, d=d, heads=U // HEAD_DIM, scale=HEAD_DIM ** -0.5)
    return pl.pallas_call(
        kern,
        grid=(3, N // tm),
        in_specs=[pl.BlockSpec((tm, D), lambda k, i: (i, 0)),
                  pl.BlockSpec((D, U), lambda k, i: (0, k * 3 + g)),
                  pl.BlockSpec((tm, HEAD_DIM), lambda k, i: (i, 0)),
                  pl.BlockSpec((tm, HEAD_DIM), lambda k, i: (i, 0))],
        out_specs=pl.BlockSpec((None, None, d, tm // d, U),
                               lambda k, i: (k, i // tiles_per_b, 0, i % tiles_per_b, 0)),
        out_shape=jax.ShapeDtypeStruct((3, B, d, S // d, U), BF16),
        scratch_shapes=[pltpu.VMEM((U // HEAD_DIM, tm, HEAD_DIM), F32)],
        compiler_params=_cparams(("parallel", "parallel")),
        name=f"qkv_proj_d{d}",
    )(h, w_bf16, cosf, sinf)


def _restproj_kernel(a_ref, w_ref, o_ref):
    o_ref[...] = jnp.dot(a_ref[...], w_ref[...], preferred_element_type=F32).astype(o_ref.dtype)


def rest_proj(h, w_bf16, U, first_tile):
    N, D = h.shape
    n_tiles = w_bf16.shape[1] // U - first_tile
    tm = min(1024, N)
    return pl.pallas_call(
        _restproj_kernel,
        grid=(n_tiles, N // tm),
        in_specs=[pl.BlockSpec((tm, D), lambda j, i: (i, 0)),
                  pl.BlockSpec((D, U), lambda j, i: (0, first_tile + j))],
        out_specs=pl.BlockSpec((tm, U), lambda j, i: (i, j)),
        out_shape=jax.ShapeDtypeStruct((N, n_tiles * U), BF16),
        compiler_params=_cparams(("parallel", "parallel")),
        name="rest_proj",
    )(h, w_bf16)


def _attn_kernel(q_ref, kc_ref, kp_ref, vc_ref, vp_ref, o_ref, st_ref, kx_ref, vx_ref, *, QB, H):
    i = pl.program_id(2)
    blk = ATTN_BLOCK
    kx_ref[0:blk, :] = kp_ref[...]
    kx_ref[blk:, :] = kc_ref[...]
    vx_ref[0:blk, :] = vp_ref[...]
    vx_ref[blk:, :] = vc_ref[...]
    rows = lax.broadcasted_iota(I32, (blk, blk), 0)
    cols = lax.broadcasted_iota(I32, (blk, blk), 1)
    cur_ok = cols <= rows
    prev_band = cols >= rows
    lane = lax.broadcasted_iota(I32, (blk, LANES), 1)
    dn = (((1,), (1,)), ((), ()))

    def body(qb, carry):
        r0 = pl.multiple_of(qb * blk, blk)
        r1 = pl.multiple_of(qb * blk + blk, blk)
        prev_ok = prev_band & (i * QB + qb > 0)
        st = jnp.zeros((blk, LANES), F32)
        for h in range(H):
            hs = slice(h * HEAD_DIM, (h + 1) * HEAD_DIM)
            q = q_ref[pl.ds(r0, blk), hs]
            kp = kx_ref[pl.ds(r0, blk), hs]
            kc = kx_ref[pl.ds(r1, blk), hs]
            vp = vx_ref[pl.ds(r0, blk), hs]
            vc = vx_ref[pl.ds(r1, blk), hs]
            s_c = jnp.where(cur_ok, lax.dot_general(q, kc, dn, preferred_element_type=F32), NEG_BIG)
            s_p = jnp.where(prev_ok, lax.dot_general(q, kp, dn, preferred_element_type=F32), NEG_BIG)
            m = jnp.maximum(jnp.max(s_c, axis=-1, keepdims=True), jnp.max(s_p, axis=-1, keepdims=True))
            p_c = jnp.exp(s_c - m)
            p_p = jnp.exp(s_p - m)
            l = jnp.sum(p_c, axis=-1, keepdims=True) + jnp.sum(p_p, axis=-1, keepdims=True)
            acc = (jnp.dot(p_c.astype(BF16), vc, preferred_element_type=F32)
                   + jnp.dot(p_p.astype(BF16), vp, preferred_element_type=F32))
            o_ref[pl.ds(r0, blk), hs] = (acc / l).astype(o_ref.dtype)
            st = jnp.where(lane == h, m, st)
            st = jnp.where(lane == H + h, l, st)
        st_ref[pl.ds(r0, blk), :] = st
        return carry

    lax.fori_loop(0, QB, body, 0)


def dilated_attention(qkv, d):
    _, B, _, L, U = qkv.shape
    H = U // HEAD_DIM
    R = min(512, L)
    QB = R // ATTN_BLOCK

    def cur(kind):
        return pl.BlockSpec((None, None, None, R, U), lambda b, r, i: (kind, b, r, i, 0))

    def prev(kind):
        return pl.BlockSpec((None, None, None, ATTN_BLOCK, U),
                            lambda b, r, i: (kind, b, r, jnp.maximum(i * QB - 1, 0), 0))

    kern = functools.partial(_attn_kernel, QB=QB, H=H)
    return pl.pallas_call(
        kern,
        grid=(B, d, L // R),
        in_specs=[cur(0), cur(1), prev(1), cur(2), prev(2)],
        out_specs=[pl.BlockSpec((None, None, R, U), lambda b, r, i: (b, r, i, 0)),
                   pl.BlockSpec((None, None, R, LANES), lambda b, r, i: (b, r, i, 0))],
        out_shape=[jax.ShapeDtypeStruct((B, d, L, U), BF16),
                   jax.ShapeDtypeStruct((B, d, L, LANES), F32)],
        scratch_shapes=[pltpu.VMEM((R + ATTN_BLOCK, U), BF16), pltpu.VMEM((R + ATTN_BLOCK, U), BF16)],
        compiler_params=_cparams(("parallel", "parallel", "parallel")),
        name=f"dilated_attn_d{d}",
    )(qkv, qkv, qkv, qkv, qkv)


def _merge_kernel(o0_ref, o1_ref, o2_ref, s0_ref, s1_ref, s2_ref, out_ref, of_ref, sf_ref, *, H, dils):
    for g, (o_ref, s_ref) in enumerate(((o0_ref, s0_ref), (o1_ref, s1_ref), (o2_ref, s2_ref))):
        d = dils[g]
        n = sf_ref.shape[1] // d
        for r in range(d):
            rows = slice(None) if d == 1 else pl.ds(r, n, stride=d)
            sf_ref[g, rows, :] = s_ref[r]
            for h in range(H):
                of_ref[g, h, rows, :] = o_ref[r, :, h * HEAD_DIM:(h + 1) * HEAD_DIM].astype(F32)
    st = [sf_ref[0], sf_ref[1], sf_ref[2]]
    mx = jnp.maximum(jnp.maximum(st[0], st[1]), st[2])
    w = [pltpu.roll(s, LANES - H, 1) * jnp.exp(s - mx) for s in st]
    tot = w[0] + w[1] + w[2]
    coef = [x / tot for x in w]
    for h in range(H):
        hs = slice(h * HEAD_DIM, (h + 1) * HEAD_DIM)
        acc = coef[0][:, h:h + 1] * of_ref[0, h]
        acc += coef[1][:, h:h + 1] * of_ref[1, h]
        acc += coef[2][:, h:h + 1] * of_ref[2, h]
        out_ref[:, hs] = acc.astype(out_ref.dtype)


def merge_groups(outs, stats, dils):
    B, d0, L0, U = outs[0].shape
    S = d0 * L0
    H = U // HEAD_DIM
    tm = min(512, S)
    tiles_per_b = S // tm

    def ospec(d, w):
        return pl.BlockSpec((None, d, tm // d, w), lambda i: (i // tiles_per_b, 0, i % tiles_per_b, 0))

    return pl.pallas_call(
        functools.partial(_merge_kernel, H=H, dils=dils),
        grid=(B * S // tm,),
        in_specs=[ospec(d, U) for d in dils] + [ospec(d, LANES) for d in dils],
        out_specs=pl.BlockSpec((tm, U), lambda i: (i, 0)),
        out_shape=jax.ShapeDtypeStruct((B * S, U), BF16),
        scratch_shapes=[pltpu.VMEM((3, H, tm, HEAD_DIM), F32), pltpu.VMEM((3, tm, LANES), F32)],
        compiler_params=_cparams(("parallel",)),
        name="merge_groups",
    )(*outs, *stats)


def _conv_kernel(a0_ref, a1_ref, b0_ref, b1_ref, ha0_ref, ha1_ref, hb0_ref, hb1_ref,
                 w_ref, bias_ref, g_ref, be_ref, o_ref, u_ref, c_ref, *, ts, U):
    i = pl.program_id(1)
    halo = CONV_HALO
    for half, (a_ref, b_ref, ha_ref, hb_ref) in enumerate(((a0_ref, b0_ref, ha0_ref, hb0_ref),
                                                            (a1_ref, b1_ref, ha1_ref, hb1_ref))):
        cs = slice(half * U, (half + 1) * U)
        u_ref[halo:halo + ts, cs] = a_ref[...].astype(F32) * _sigmoid(b_ref[...].astype(F32))
        hu = ha_ref[...].astype(F32) * _sigmoid(hb_ref[...].astype(F32))
        u_ref[0:halo, cs] = jnp.where(i > 0, hu, 0.0)
    C = 2 * U
    rc = 64
    off = halo - (CONV_WIDTH - 1)

    def chan_body(cc, carry):
        c0 = pl.multiple_of(cc * LANES, LANES)
        for rb in range(ts // rc):
            acc = jnp.zeros((rc, LANES), F32) + bias_ref[:, pl.ds(c0, LANES)]
            for j in range(CONV_WIDTH):
                acc += w_ref[j:j + 1, pl.ds(c0, LANES)] * u_ref[rb * rc + off + j:rb * rc + off + j + rc, pl.ds(c0, LANES)]
            c_ref[rb * rc:(rb + 1) * rc, pl.ds(c0, LANES)] = acc
        return carry

    lax.fori_loop(0, C // LANES, chan_body, 0)

    rn = 32

    def norm_body(rb, carry):
        r0 = pl.multiple_of(rb * rn, rn)
        v = c_ref[pl.ds(r0, rn), :]
        mu = jnp.mean(v, axis=-1, keepdims=True)
        dv = v - mu
        var = jnp.mean(dv * dv, axis=-1, keepdims=True)
        y = dv * lax.rsqrt(var + NORM_EPS) * g_ref[...] + be_ref[...]
        o_ref[pl.ds(r0, rn), :] = (y * _sigmoid(y)).astype(o_ref.dtype)
        return carry

    lax.fori_loop(0, ts // rn, norm_body, 0)


def conv_branch(proj, B, S, U, conv_dw, conv_bias, ln_gain, ln_bias):
    IN = proj.shape[1]
    C = 2 * U
    ts = min(256, S)
    pv = proj.reshape(B, S, IN)
    hb = ts // CONV_HALO
    cur = lambda blk: pl.BlockSpec((None, ts, U), lambda b, i, blk=blk: (b, i, blk))
    prv = lambda blk: pl.BlockSpec((None, CONV_HALO, U), lambda b, i, blk=blk: (b, jnp.maximum(i * hb - 1, 0), blk))
    vec = pl.BlockSpec((1, C), lambda b, i: (0, 0))
    out = pl.pallas_call(
        functools.partial(_conv_kernel, ts=ts, U=U),
        grid=(B, S // ts),
        in_specs=[cur(0), cur(1), cur(2), cur(3), prv(0), prv(1), prv(2), prv(3),
                  pl.BlockSpec((CONV_WIDTH, C), lambda b, i: (0, 0)), vec, vec, vec],
        out_specs=pl.BlockSpec((None, ts, C), lambda b, i: (b, i, 0)),
        out_shape=jax.ShapeDtypeStruct((B, S, C), BF16),
        scratch_shapes=[pltpu.VMEM((ts + CONV_HALO, C), F32), pltpu.VMEM((ts, C), F32)],
        compiler_params=_cparams(("parallel", "parallel")),
        name="conv_branch",
    )(pv, pv, pv, pv, pv, pv, pv, pv, conv_dw, conv_bias.reshape(1, C), ln_gain.reshape(1, C), ln_bias.reshape(1, C))
    return out.reshape(B * S, C)


def _gateproj_kernel(cn_ref, am_ref, wc_ref, wa_ref, gc_ref, ga_ref, o_ref):
    conv = jnp.dot(cn_ref[...], wc_ref[...], preferred_element_type=F32)
    z = _sigmoid(gc_ref[...].astype(F32)) * conv
    attn = jnp.dot(am_ref[...], wa_ref[...], preferred_element_type=F32)
    z += _sigmoid(ga_ref[...].astype(F32)) * attn
    o_ref[...] = z.astype(o_ref.dtype)


def gate_proj(cn, am, wc, wa, proj, U):
    N, C = cn.shape
    D = wc.shape[1]
    tn = U
    tm = min(512, N)
    return pl.pallas_call(
        _gateproj_kernel,
        grid=(D // tn, N // tm),
        in_specs=[pl.BlockSpec((tm, C), lambda j, i: (i, 0)),
                  pl.BlockSpec((tm, U), lambda j, i: (i, 0)),
                  pl.BlockSpec((C, tn), lambda j, i: (0, j)),
                  pl.BlockSpec((U, tn), lambda j, i: (0, j)),
                  pl.BlockSpec((tm, tn), lambda j, i: (i, 4 + j)),
                  pl.BlockSpec((tm, tn), lambda j, i: (i, 8 + j))],
        out_specs=pl.BlockSpec((tm, tn), lambda j, i: (i, j)),
        out_shape=jax.ShapeDtypeStruct((N, D), BF16),
        compiler_params=_cparams(("parallel", "parallel")),
        name="gate_proj",
    )(cn, am, wc, wa, proj, proj)


def _outproj_kernel(z_ref, w_ref, x_ref, mod_ref, gain_ref, o_ref, *, nk):
    k = pl.program_id(1)

    @pl.when(k == 0)
    def _():
        o_ref[...] = jnp.zeros_like(o_ref)

    o_ref[...] += jnp.dot(z_ref[...], w_ref[...], preferred_element_type=F32)

    @pl.when(k == nk - 1)
    def _():
        y = o_ref[...]
        ms = jnp.mean(y * y, axis=-1, keepdims=True)
        yn = y * lax.rsqrt(ms + NORM_EPS) * gain_ref[...]
        o_ref[...] = x_ref[...] + mod_ref[2:3, :] * yn


def out_proj(z, w, x2d, mod, gain, S):
    N, D = x2d.shape
    K = z.shape[1]
    tm = min(512, S)
    tk = min(512, K)
    tiles_per_b = S // tm
    nk = K // tk
    return pl.pallas_call(
        functools.partial(_outproj_kernel, nk=nk),
        grid=(N // tm, nk),
        in_specs=[pl.BlockSpec((tm, tk), lambda i, k: (i, k)),
                  pl.BlockSpec((tk, D), lambda i, k: (k, 0)),
                  pl.BlockSpec((tm, D), lambda i, k: (i, 0)),
                  pl.BlockSpec((None, 6, D), lambda i, k: (i // tiles_per_b, 0, 0)),
                  pl.BlockSpec((1, D), lambda i, k: (0, 0))],
        out_specs=pl.BlockSpec((tm, D), lambda i, k: (i, 0)),
        out_shape=jax.ShapeDtypeStruct((N, D), F32),
        compiler_params=_cparams(("parallel", "arbitrary")),
        name="out_proj",
    )(z, w, x2d, mod, gain.reshape(1, D))


def _router_kernel(x_ref, mod_ref, gain_ref, r_ref, hp_ref, eid_ref, wt_ref, rank_ref, cnt_ref, carry_ref, *, tr):
    step = pl.program_id(0)

    @pl.when(step == 0)
    def _():
        carry_ref[...] = jnp.zeros_like(carry_ref)

    x = x_ref[...]
    D = x.shape[1]
    ms = jnp.mean(x * x, axis=-1, keepdims=True)
    h = x * lax.rsqrt(ms + NORM_EPS) * gain_ref[...]
    h = h * (1.0 + mod_ref[4:5, :]) + mod_ref[3:4, :]
    lo = h[:, :D // 2]
    hi = h[:, D // 2:]
    hp_ref[...] = _pack_halves(lo, hi)
    logits = (jnp.dot(lo.astype(BF16), r_ref[:D // 2, :], preferred_element_type=F32)
              + jnp.dot(hi.astype(BF16), r_ref[D // 2:, :], preferred_element_type=F32))
    lane = lax.broadcasted_iota(I32, logits.shape, 1)
    G = N_EXPERT_GROUPS
    is_g = lane < G
    gl = jnp.where(is_g, logits, NEG_BIG)
    gmax = jnp.max(gl, axis=-1, keepdims=True)
    grp = jnp.min(jnp.where(gl == gmax, lane, LANES), axis=-1, keepdims=True)
    p_grp = 1.0 / jnp.sum(jnp.where(is_g, jnp.exp(gl - gmax), 0.0), axis=-1, keepdims=True)
    in_grp = (lane >= G) & (lane < G + N_EXPERTS) & (((lane - G) // EXPERTS_PER_GROUP) == grp)
    el = jnp.where(in_grp, logits, NEG_BIG)
    v0 = jnp.max(el, axis=-1, keepdims=True)
    i0 = jnp.min(jnp.where(in_grp & (el == v0), lane, LANES), axis=-1, keepdims=True)
    in2 = in_grp & (lane != i0)
    el2 = jnp.where(in2, logits, NEG_BIG)
    v1 = jnp.max(el2, axis=-1, keepdims=True)
    i1 = jnp.min(jnp.where(in2 & (el2 == v1), lane, LANES), axis=-1, keepdims=True)
    e1 = jnp.exp(v1 - v0)
    w0 = p_grp / (1.0 + e1)
    w1 = p_grp * e1 / (1.0 + e1)
    ex0 = i0 - G
    ex1 = i1 - G
    eid_ref[...] = jnp.where(lane == 0, ex0, jnp.where(lane == 1, ex1, 0))
    wt_ref[...] = jnp.where(lane == 0, w0, jnp.where(lane == 1, w1, 0.0))
    oh0 = (lane == ex0).astype(F32)
    oh1 = (lane == ex1).astype(F32)
    both = oh0 + oh1
    rr = lax.broadcasted_iota(I32, (tr, tr), 0)
    cc = lax.broadcasted_iota(I32, (tr, tr), 1)
    tril = (cc < rr).astype(BF16)
    before = jnp.dot(tril, both.astype(BF16), preferred_element_type=F32) + carry_ref[0:1, :]
    rk0 = jnp.sum(before * oh0, axis=-1, keepdims=True)
    rk1 = jnp.sum(before * oh1, axis=-1, keepdims=True)
    rank_ref[...] = jnp.where(lane == 0, rk0, jnp.where(lane == 1, rk1, 0.0))
    newc = carry_ref[0:1, :] + jnp.sum(both, axis=0, keepdims=True)
    carry_ref[...] = jnp.broadcast_to(newc, carry_ref.shape)
    cnt_ref[...] = jnp.broadcast_to(newc, cnt_ref.shape)


def prenorm_router(x1, mod, gain, rcat, S):
    N, D = x1.shape
    tr = min(256, S)
    tiles_per_b = S // tr
    lane_spec = pl.BlockSpec((tr, LANES), lambda i: (i, 0))
    return pl.pallas_call(
        functools.partial(_router_kernel, tr=tr),
        grid=(N // tr,),
        in_specs=[pl.BlockSpec((tr, D), lambda i: (i, 0)),
                  pl.BlockSpec((None, 6, D), lambda i: (i // tiles_per_b, 0, 0)),
                  pl.BlockSpec((1, D), lambda i: (0, 0)),
                  pl.BlockSpec((D, LANES), lambda i: (0, 0))],
        out_specs=[pl.BlockSpec((tr, D // 2), lambda i: (i, 0)), lane_spec, lane_spec, lane_spec,
                   pl.BlockSpec((8, LANES), lambda i: (0, 0))],
        out_shape=[jax.ShapeDtypeStruct((N, D // 2), U32),
                   jax.ShapeDtypeStruct((N, LANES), I32),
                   jax.ShapeDtypeStruct((N, LANES), F32),
                   jax.ShapeDtypeStruct((N, LANES), F32),
                   jax.ShapeDtypeStruct((8, LANES), F32)],
        scratch_shapes=[pltpu.VMEM((8, LANES), F32)],
        compiler_params=_cparams(("arbitrary",)),
        name="prenorm_router",
    )(x1, mod, gain.reshape(1, D), rcat)


def _dest_kernel(eid_ref, rank_ref, start_ref, o_ref):
    lane = lax.broadcasted_iota(I32, eid_ref.shape, 1)
    eid = eid_ref[...]
    rank = rank_ref[...]
    start = start_ref[0:1, :]
    d = []
    for k in range(2):
        oh = (lane == eid[:, k:k + 1]).astype(F32)
        d.append(jnp.sum(oh * start, axis=-1, keepdims=True) + rank[:, k:k + 1])
    o_ref[...] = jnp.where(lane == 0, d[0], jnp.where(lane == 1, d[1], 0.0)).astype(I32)


def dest_rows(eid, rank, pad_start):
    N = eid.shape[0]
    tr = min(1024, N)
    spec = pl.BlockSpec((tr, LANES), lambda i: (i, 0))
    return pl.pallas_call(
        _dest_kernel,
        grid=(N // tr,),
        in_specs=[spec, spec, pl.BlockSpec((8, LANES), lambda i: (0, 0))],
        out_specs=spec,
        out_shape=jax.ShapeDtypeStruct((N, LANES), I32),
        compiler_params=_cparams(("parallel",)),
        name="dest_rows",
    )(eid, rank, pad_start)


def _scatter_kernel(dest_ref, h_ref, xs_in_ref, xs_ref, sem, *, T):
    del xs_in_ref

    def issue(t, carry):
        for k in range(2):
            d = dest_ref[0, 0, 2 * t + k]
            pltpu.make_async_copy(h_ref.at[pl.ds(t, 1)], xs_ref.at[pl.ds(d, 1)], sem).start()
        return carry

    lax.fori_loop(0, T, issue, 0)

    def drain(t, carry):
        for k in range(2):
            pltpu.make_async_copy(h_ref.at[pl.ds(0, 1)], xs_ref.at[pl.ds(0, 1)], sem).wait()
        return carry

    lax.fori_loop(0, T, drain, 0)


def scatter_rows(hp, dest, P):
    N, W = hp.shape
    T = min(256, N)
    dest_s = dest[:, :2].reshape(N // T, 1, 2 * T)
    xs0 = jnp.zeros((P, W), hp.dtype)
    return pl.pallas_call(
        functools.partial(_scatter_kernel, T=T),
        grid=(N // T,),
        in_specs=[pl.BlockSpec((1, 1, 2 * T), lambda i: (i, 0, 0), memory_space=pltpu.SMEM),
                  pl.BlockSpec((T, W), lambda i: (i, 0)),
                  pl.BlockSpec(memory_space=pl.ANY)],
        out_specs=pl.BlockSpec(memory_space=pl.ANY),
        out_shape=jax.ShapeDtypeStruct((P, W), hp.dtype),
        scratch_shapes=[pltpu.SemaphoreType.DMA(())],
        input_output_aliases={2: 0},
        compiler_params=_cparams(("arbitrary",)),
        name="scatter_rows",
    )(dest_s, hp, xs0)


def _expert_changed(ce_ref, c):
    return (c == 0) | (ce_ref[c] != ce_ref[jnp.maximum(c - 1, 0)])


def _moe1_kernel(ce_ref, nu_ref, xs_ref, w1_ref, w3_ref, o_ref, wb_ref, *, F):
    c = pl.program_id(0)

    @pl.when(c < nu_ref[0])
    def _():
        @pl.when(_expert_changed(ce_ref, c))
        def _():
            wb_ref[:, :F] = w1_ref[...].astype(BF16)
            wb_ref[:, F:] = w3_ref[...].astype(BF16)

        lo, hi = _unpack_halves(xs_ref[...])
        half = wb_ref.shape[0] // 2
        hcat = (jnp.dot(lo.astype(BF16), wb_ref[:half, :], preferred_element_type=F32)
                + jnp.dot(hi.astype(BF16), wb_ref[half:, :], preferred_element_type=F32))
        a = hcat[:, :F]
        o_ref[...] = (a * _sigmoid(a) * hcat[:, F:]).astype(o_ref.dtype)

    @pl.when(c >= nu_ref[0])
    def _():
        o_ref[...] = jnp.zeros_like(o_ref)


def _moe2_kernel(ce_ref, nu_ref, h_ref, w2_ref, o_ref, wb_ref):
    c = pl.program_id(0)

    @pl.when(c < nu_ref[0])
    def _():
        @pl.when(_expert_changed(ce_ref, c))
        def _():
            wb_ref[...] = w2_ref[...].astype(BF16)

        y = jnp.dot(h_ref[...], wb_ref[...], preferred_element_type=F32)
        half = y.shape[1] // 2
        o_ref[...] = _pack_halves(y[:, :half], y[:, half:])

    @pl.when(c >= nu_ref[0])
    def _():
        o_ref[...] = jnp.zeros_like(o_ref)


def expert_ffn(xs, chunk_expert, n_used, w1, w3, w2):
    P, Wp = xs.shape
    E, D, F = w1.shape
    TM = MOE_ROWS
    n_chunks = P // TM
    hmid = pl.pallas_call(
        functools.partial(_moe1_kernel, F=F),
        grid_spec=pltpu.PrefetchScalarGridSpec(
            num_scalar_prefetch=2,
            grid=(n_chunks,),
            in_specs=[pl.BlockSpec((TM, Wp), lambda c, ce, nu: (c, 0)),
                      pl.BlockSpec((None, D, F), lambda c, ce, nu: (ce[c], 0, 0)),
                      pl.BlockSpec((None, D, F), lambda c, ce, nu: (ce[c], 0, 0))],
            out_specs=pl.BlockSpec((TM, F), lambda c, ce, nu: (c, 0)),
            scratch_shapes=[pltpu.VMEM((D, 2 * F), BF16)]),
        out_shape=jax.ShapeDtypeStruct((P, F), BF16),
        compiler_params=_cparams(("arbitrary",)),
        name="moe_up",
    )(chunk_expert, n_used, xs, w1, w3)
    return pl.pallas_call(
        _moe2_kernel,
        grid_spec=pltpu.PrefetchScalarGridSpec(
            num_scalar_prefetch=2,
            grid=(n_chunks,),
            in_specs=[pl.BlockSpec((TM, F), lambda c, ce, nu: (c, 0)),
                      pl.BlockSpec((None, F, D), lambda c, ce, nu: (ce[c], 0, 0))],
            out_specs=pl.BlockSpec((TM, D // 2), lambda c, ce, nu: (c, 0)),
            scratch_shapes=[pltpu.VMEM((F, D), BF16)]),
        out_shape=jax.ShapeDtypeStruct((P, D // 2), U32),
        compiler_params=_cparams(("arbitrary",)),
        name="moe_down",
    )(chunk_expert, n_used, hmid, w2)


def _combine_kernel(pos_ref, wt_ref, x_ref, mod_ref, gain_ref, y_hbm, o_ref, b0_ref, b1_ref, sem, *, T):
    bufs = (b0_ref, b1_ref)

    def issue(t, carry):
        for k in range(2):
            p = pos_ref[0, 0, 2 * t + k]
            pltpu.make_async_copy(y_hbm.at[pl.ds(p, 1)], bufs[k].at[pl.ds(t, 1)], sem).start()
        return carry

    lax.fori_loop(0, T, issue, 0)

    def drain(t, carry):
        for k in range(2):
            pltpu.make_async_copy(y_hbm.at[pl.ds(0, 1)], bufs[k].at[pl.ds(0, 1)], sem).wait()
        return carry

    lax.fori_loop(0, T, drain, 0)

    wt = wt_ref[...]
    w0 = wt[:, 0:1]
    w1 = wt[:, 1:2]
    lo0, hi0 = _unpack_halves(b0_ref[...])
    lo1, hi1 = _unpack_halves(b1_ref[...])
    ylo = w0 * lo0 + w1 * lo1
    yhi = w0 * hi0 + w1 * hi1
    D = x_ref.shape[1]
    half = D // 2
    ms = (jnp.sum(ylo * ylo, axis=-1, keepdims=True) + jnp.sum(yhi * yhi, axis=-1, keepdims=True)) / D
    inv = lax.rsqrt(ms + NORM_EPS)
    o_ref[:, :half] = x_ref[:, :half] + mod_ref[5:6, :half] * (ylo * inv * gain_ref[:, :half])
    o_ref[:, half:] = x_ref[:, half:] + mod_ref[5:6, half:] * (yhi * inv * gain_ref[:, half:])


def combine(yp, dest, wts, x1, mod, gain, S):
    N, D = x1.shape
    T = min(128, S)
    tiles_per_b = S // T
    pos_s = dest[:, :2].reshape(N // T, 1, 2 * T)
    return pl.pallas_call(
        functools.partial(_combine_kernel, T=T),
        grid=(N // T,),
        in_specs=[pl.BlockSpec((1, 1, 2 * T), lambda i: (i, 0, 0), memory_space=pltpu.SMEM),
                  pl.BlockSpec((T, LANES), lambda i: (i, 0)),
                  pl.BlockSpec((T, D), lambda i: (i, 0)),
                  pl.BlockSpec((None, 6, D), lambda i: (i // tiles_per_b, 0, 0)),
                  pl.BlockSpec((1, D), lambda i: (0, 0)),
                  pl.BlockSpec(memory_space=pl.ANY)],
        out_specs=pl.BlockSpec((T, D), lambda i: (i, 0)),
        out_shape=jax.ShapeDtypeStruct((N, D), F32),
        scratch_shapes=[pltpu.VMEM((T, D // 2), U32), pltpu.VMEM((T, D // 2), U32),
                        pltpu.SemaphoreType.DMA(())],
        compiler_params=_cparams(("arbitrary",)),
        name="combine",
    )(pos_s, wts, x1, mod, gain.reshape(1, D), yp)


def _moe_layout(counts):
    TM = MOE_ROWS
    cnt = counts.astype(I32)
    padded = (cnt + TM - 1) // TM * TM
    pad_end = jnp.cumsum(padded)
    pad_start = pad_end - padded
    return pad_start, pad_end


def kernel(x, c, positions, ada_w, ada_b, mix_norm_pre, mix_norm_post, w_in, conv_dw, conv_dw_bias, conv_ln_gain, conv_ln_bias, w_conv_out, w_attn_out, w_out, ffn_norm_pre, ffn_norm_post, router_group, router_expert, expert_w1, expert_w3, expert_w2):
    B, S, D = x.shape
    N = B * S
    U = D // 4
    depth = ada_w.shape[0]
    xc = x.reshape(N, D)
    for layer in range(depth):
        mod = ada_mod(c, ada_w[layer], ada_b[layer]).reshape(B, 6, D)
        h, cosf, sinf = prenorm_rope(xc, mod, mix_norm_pre[layer], positions, S)
        w_in_b = w_in[layer].astype(BF16)
        outs, stats = [], []
        for g, (window, d) in enumerate(ATTN_PATTERNS):
            assert window // d == ATTN_BLOCK and S % (d * ATTN_BLOCK) == 0
            qkv = qkv_proj(h, w_in_b, cosf, sinf, B, S, U, g, d)
            o, st = dilated_attention(qkv, d)
            outs.append(o)
            stats.append(st)
        am = merge_groups(outs, stats, tuple(d for _, d in ATTN_PATTERNS))
        proj = rest_proj(h, w_in_b, U, 9)
        cn = conv_branch(proj, B, S, U, conv_dw[layer], conv_dw_bias[layer], conv_ln_gain[layer], conv_ln_bias[layer])
        z = gate_proj(cn, am, w_conv_out[layer].astype(BF16), w_attn_out[layer].astype(BF16), proj, U)
        x1 = out_proj(z, w_out[layer].astype(BF16), xc, mod, mix_norm_post[layer], S)
        rcat = jnp.zeros((D, LANES), F32)
        rcat = rcat.at[:, :N_EXPERT_GROUPS].set(router_group[layer])
        rcat = rcat.at[:, N_EXPERT_GROUPS:N_EXPERT_GROUPS + N_EXPERTS].set(router_expert[layer]).astype(BF16)
        hp, eid, wts, rank, cnt = prenorm_router(x1, mod, ffn_norm_pre[layer], rcat, S)
        TM = MOE_ROWS
        n_chunks = (2 * N + N_EXPERTS * (TM - 1)) // TM
        P = n_chunks * TM
        pad_start, pad_end = _moe_layout(cnt[0, N_EXPERT_GROUPS * 0:N_EXPERTS])
        n_used = (pad_end[-1] // TM).astype(I32).reshape(1)
        cidx = jnp.minimum(jnp.arange(n_chunks, dtype=I32), n_used[0] - 1)
        chunk_expert = jnp.minimum(jnp.searchsorted(pad_end, cidx * TM, side='right'), N_EXPERTS - 1).astype(I32)
        start_row = jnp.zeros((8, LANES), F32).at[:, :N_EXPERTS].set(pad_start.astype(F32)[None, :])
        dest = dest_rows(eid, rank, start_row)
        xs = scatter_rows(hp, dest, P)
        yp = expert_ffn(xs, chunk_expert, n_used, expert_w1[layer], expert_w3[layer], expert_w2[layer])
        xc = combine(yp, dest, wts, x1, mod, ffn_norm_post[layer], S)
    return xc.reshape(B, S, D)
```

```python
import functools

import jax
import jax.numpy as jnp
from jax import lax
from jax.experimental import pallas as pl
from jax.experimental.pallas import tpu as pltpu

F32 = jnp.float32
BF16 = jnp.bfloat16
I32 = jnp.int32
U32 = jnp.uint32

HEAD_DIM = 128
LANES = 128
ATTN_BLOCK = 128
ATTN_PATTERNS = ((128, 1), (512, 4), (2048, 16))
ROPE_THETA = 10000.0
CONV_WIDTH = 31
CONV_HALO = 32
N_EXPERT_GROUPS = 8
EXPERTS_PER_GROUP = 8
N_EXPERTS = 64
NORM_EPS = 1e-6
NEG_BIG = -1e30
MOE_ROWS = 256
V7X_VMEM_LIMIT = 60 * 1024 * 1024


def _cparams(sem):
    return pltpu.CompilerParams(dimension_semantics=sem, vmem_limit_bytes=V7X_VMEM_LIMIT)


def _sigmoid(x):
    return 1.0 / (1.0 + jnp.exp(-x))


def _pack_halves(lo, hi):
    lo_b = lax.bitcast_convert_type(lo.astype(BF16).astype(F32), U32) >> 16
    hi_b = lax.bitcast_convert_type(hi.astype(BF16).astype(F32), U32) & jnp.uint32(0xFFFF0000)
    return hi_b | lo_b


def _unpack_halves(w):
    lo = lax.bitcast_convert_type(w << 16, F32)
    hi = lax.bitcast_convert_type(w & jnp.uint32(0xFFFF0000), F32)
    return lo, hi


def _ada_kernel(c_ref, w_ref, b_ref, o_ref):
    c = c_ref[...]
    cact = (c * _sigmoid(c)).astype(BF16)
    o_ref[...] = jnp.dot(cact, w_ref[...].astype(BF16), preferred_element_type=F32) + b_ref[...]


def ada_mod(c, ada_w, ada_b):
    B, D = c.shape
    W = ada_w.shape[1]
    rows = 8
    cp = jnp.zeros((rows, D), F32).at[:B].set(c)
    tn = min(512, W)
    out = pl.pallas_call(
        _ada_kernel,
        grid=(W // tn,),
        in_specs=[pl.BlockSpec((rows, D), lambda j: (0, 0)),
                  pl.BlockSpec((D, tn), lambda j: (0, j)),
                  pl.BlockSpec((1, tn), lambda j: (0, j))],
        out_specs=pl.BlockSpec((rows, tn), lambda j: (0, j)),
        out_shape=jax.ShapeDtypeStruct((rows, W), F32),
        compiler_params=_cparams(("parallel",)),
        name="ada_mod",
    )(cp, ada_w, ada_b.reshape(1, W))
    return out[:B]


def _prenorm_kernel(x_ref, mod_ref, gain_ref, pos_ref, invf_ref, h_ref, cos_ref, sin_ref):
    x = x_ref[...]
    ms = jnp.mean(x * x, axis=-1, keepdims=True)
    y = x * lax.rsqrt(ms + NORM_EPS) * gain_ref[...]
    y = y * (1.0 + mod_ref[1:2, :]) + mod_ref[0:1, :]
    h_ref[...] = y.astype(BF16)
    ang = pos_ref[...].astype(F32) * invf_ref[...]
    lane = lax.broadcasted_iota(I32, ang.shape, 1)
    sn = jnp.sin(ang)
    cos_ref[...] = jnp.cos(ang)
    sin_ref[...] = jnp.where(lane < HEAD_DIM // 2, -sn, sn)


def prenorm_rope(x2d, mod, gain, positions, S):
    N, D = x2d.shape
    tr = min(256, S)
    tiles_per_b = S // tr
    half = HEAD_DIM // 2
    inv = ROPE_THETA ** (-jnp.arange(0, HEAD_DIM, 2, dtype=F32) / HEAD_DIM)
    invf = jnp.concatenate([inv, inv]).reshape(1, HEAD_DIM)
    del half
    return pl.pallas_call(
        _prenorm_kernel,
        grid=(N // tr,),
        in_specs=[pl.BlockSpec((tr, D), lambda i: (i, 0)),
                  pl.BlockSpec((None, 6, D), lambda i: (i // tiles_per_b, 0, 0)),
                  pl.BlockSpec((1, D), lambda i: (0, 0)),
                  pl.BlockSpec((tr, 1), lambda i: (i, 0)),
                  pl.BlockSpec((1, HEAD_DIM), lambda i: (0, 0))],
        out_specs=[pl.BlockSpec((tr, D), lambda i: (i, 0)),
                   pl.BlockSpec((tr, HEAD_DIM), lambda i: (i, 0)),
                   pl.BlockSpec((tr, HEAD_DIM), lambda i: (i, 0))],
        out_shape=[jax.ShapeDtypeStruct((N, D), BF16),
                   jax.ShapeDtypeStruct((N, HEAD_DIM), F32),
                   jax.ShapeDtypeStruct((N, HEAD_DIM), F32)],
        compiler_params=_cparams(("parallel",)),
        name="prenorm_rope",
    )(x2d, mod, gain.reshape(1, D), positions.reshape(N, 1), invf)


def _residue_rows(ref, r, d):
    if d == 1:
        return ref[...]
    return ref[pl.ds(r, ref.shape[0] // d, stride=d), :]


def _qkv_kernel(a_ref, w_ref, cos_ref, sin_ref, o_ref, acc_ref, *, d, heads, scale):
    kind = pl.program_id(0)
    acc = jnp.dot(a_ref[...], w_ref[...], preferred_element_type=F32)
    for h in range(heads):
        acc_ref[h] = acc[:, h * HEAD_DIM:(h + 1) * HEAD_DIM]

    @pl.when(kind == 2)
    def _():
        for r in range(d):
            for h in range(heads):
                hs = slice(h * HEAD_DIM, (h + 1) * HEAD_DIM)
                o_ref[r, :, hs] = _residue_rows(acc_ref.at[h], r, d).astype(o_ref.dtype)

    @pl.when(kind < 2)
    def _():
        sc = jnp.where(kind == 0, scale, 1.0).astype(F32)
        for r in range(d):
            c = _residue_rows(cos_ref, r, d) * sc
            s = _residue_rows(sin_ref, r, d) * sc
            for h in range(heads):
                hs = slice(h * HEAD_DIM, (h + 1) * HEAD_DIM)
                t = _residue_rows(acc_ref.at[h], r, d)
                o_ref[r, :, hs] = (t * c + pltpu.roll(t, HEAD_DIM // 2, 1) * s).astype(o_ref.dtype)


def qkv_proj(h, w_bf16, cosf, sinf, B, S, U, g, d):
    N, D = h.shape
    tm = min(1024, S)
    tiles_per_b = S // tm
    kern = functools.partial(_qkv_kernel, d=d, heads=U // HEAD_DIM, scale=HEAD_DIM ** -0.5)
    return pl.pallas_call(
        kern,
        grid=(3, N // tm),
        in_specs=[pl.BlockSpec((tm, D), lambda k, i: (i, 0)),
                  pl.BlockSpec((D, U), lambda k, i: (0, k * 3 + g)),
                  pl.BlockSpec((tm, HEAD_DIM), lambda k, i: (i, 0)),
                  pl.BlockSpec((tm, HEAD_DIM), lambda k, i: (i, 0))],
        out_specs=pl.BlockSpec((None, None, d, tm // d, U),
                               lambda k, i: (k, i // tiles_per_b, 0, i % tiles_per_b, 0)),
        out_shape=jax.ShapeDtypeStruct((3, B, d, S // d, U), BF16),
        scratch_shapes=[pltpu.VMEM((U // HEAD_DIM, tm, HEAD_DIM), F32)],
        compiler_params=_cparams(("parallel", "parallel")),
        name=f"qkv_proj_d{d}",
    )(h, w_bf16, cosf, sinf)


def _restproj_kernel(a_ref, w_ref, o_ref):
    o_ref[...] = jnp.dot(a_ref[...], w_ref[...], preferred_element_type=F32).astype(o_ref.dtype)


def rest_proj(h, w_bf16, U, first_tile):
    N, D = h.shape
    n_tiles = w_bf16.shape[1] // U - first_tile
    tm = min(1024, N)
    return pl.pallas_call(
        _restproj_kernel,
        grid=(n_tiles, N // tm),
        in_specs=[pl.BlockSpec((tm, D), lambda j, i: (i, 0)),
                  pl.BlockSpec((D, U), lambda j, i: (0, first_tile + j))],
        out_specs=pl.BlockSpec((tm, U), lambda j, i: (i, j)),
        out_shape=jax.ShapeDtypeStruct((N, n_tiles * U), BF16),
        compiler_params=_cparams(("parallel", "parallel")),
        name="rest_proj",
    )(h, w_bf16)


def _attn_kernel(q_ref, kc_ref, kp_ref, vc_ref, vp_ref, o_ref, st_ref, kx_ref, vx_ref, *, QB, H):
    i = pl.program_id(2)
    blk = ATTN_BLOCK
    kx_ref[0:blk, :] = kp_ref[...]
    kx_ref[blk:, :] = kc_ref[...]
    ones = jnp.ones((vx_ref.shape[0], HEAD_DIM), vx_ref.dtype)
    for h in range(H):
        hs = slice(h * HEAD_DIM, (h + 1) * HEAD_DIM)
        vx_ref[0:blk, 2 * h * HEAD_DIM:(2 * h + 1) * HEAD_DIM] = vp_ref[:, hs]
        vx_ref[blk:, 2 * h * HEAD_DIM:(2 * h + 1) * HEAD_DIM] = vc_ref[:, hs]
        vx_ref[:, (2 * h + 1) * HEAD_DIM:(2 * h + 2) * HEAD_DIM] = ones
    rows = lax.broadcasted_iota(I32, (blk, 2 * blk), 0)
    keys = lax.broadcasted_iota(I32, (blk, 2 * blk), 1)
    band = (keys >= rows) & (keys - blk <= rows)
    lane = lax.broadcasted_iota(I32, (blk, LANES), 1)
    dn = (((1,), (1,)), ((), ()))

    def body(qb, carry):
        r0 = pl.multiple_of(qb * blk, blk)
        ok = band & ((keys >= blk) | (i * QB + qb > 0))
        q = [q_ref[pl.ds(r0, blk), h * HEAD_DIM:(h + 1) * HEAD_DIM] for h in range(H)]
        k = [kx_ref[pl.ds(r0, 2 * blk), h * HEAD_DIM:(h + 1) * HEAD_DIM] for h in range(H)]
        s = [jnp.where(ok, lax.dot_general(q[h], k[h], dn, preferred_element_type=F32), NEG_BIG) for h in range(H)]
        m = [jnp.max(s[h], axis=-1, keepdims=True) for h in range(H)]
        p = [jnp.exp(s[h] - m[h]).astype(BF16) for h in range(H)]
        st = jnp.zeros((blk, LANES), F32)
        for h in range(H):
            v1 = vx_ref[pl.ds(r0, 2 * blk), 2 * h * HEAD_DIM:(2 * h + 2) * HEAD_DIM]
            acc = jnp.dot(p[h], v1, preferred_element_type=F32)
            l = acc[:, HEAD_DIM:]
            o_ref[pl.ds(r0, blk), h * HEAD_DIM:(h + 1) * HEAD_DIM] = (acc[:, :HEAD_DIM] / l).astype(o_ref.dtype)
            st = jnp.where(lane == h, m[h], st)
            st = jnp.where(lane == H + h, l, st)
        st_ref[pl.ds(r0, blk), :] = st
        return carry

    lax.fori_loop(0, QB, body, 0)


def dilated_attention(qkv, d):
    _, B, _, L, U = qkv.shape
    H = U // HEAD_DIM
    R = min(512, L)
    QB = R // ATTN_BLOCK

    def cur(kind):
        return pl.BlockSpec((None, None, None, R, U), lambda b, r, i: (kind, b, r, i, 0))

    def prev(kind):
        return pl.BlockSpec((None, None, None, ATTN_BLOCK, U),
                            lambda b, r, i: (kind, b, r, jnp.maximum(i * QB - 1, 0), 0))

    kern = functools.partial(_attn_kernel, QB=QB, H=H)
    return pl.pallas_call(
        kern,
        grid=(B, d, L // R),
        in_specs=[cur(0), cur(1), prev(1), cur(2), prev(2)],
        out_specs=[pl.BlockSpec((None, None, R, U), lambda b, r, i: (b, r, i, 0)),
                   pl.BlockSpec((None, None, R, LANES), lambda b, r, i: (b, r, i, 0))],
        out_shape=[jax.ShapeDtypeStruct((B, d, L, U), BF16),
                   jax.ShapeDtypeStruct((B, d, L, LANES), F32)],
        scratch_shapes=[pltpu.VMEM((R + ATTN_BLOCK, U), BF16), pltpu.VMEM((R + ATTN_BLOCK, 2 * U), BF16)],
        compiler_params=_cparams(("parallel", "parallel", "parallel")),
        name=f"dilated_attn_d{d}",
    )(qkv, qkv, qkv, qkv, qkv)


def _merge_kernel(o0_ref, o1_ref, o2_ref, s0_ref, s1_ref, s2_ref, out_ref, of_ref, sf_ref, *, H, dils):
    for g, (o_ref, s_ref) in enumerate(((o0_ref, s0_ref), (o1_ref, s1_ref), (o2_ref, s2_ref))):
        d = dils[g]
        n = sf_ref.shape[1] // d
        for r in range(d):
            rows = slice(None) if d == 1 else pl.ds(r, n, stride=d)
            sf_ref[g, rows, :] = s_ref[r]
            for h in range(H):
                of_ref[g, h, rows, :] = o_ref[r, :, h * HEAD_DIM:(h + 1) * HEAD_DIM].astype(F32)
    st = [sf_ref[0], sf_ref[1], sf_ref[2]]
    mx = jnp.maximum(jnp.maximum(st[0], st[1]), st[2])
    w = [pltpu.roll(s, LANES - H, 1) * jnp.exp(s - mx) for s in st]
    tot = w[0] + w[1] + w[2]
    coef = [x / tot for x in w]
    for h in range(H):
        hs = slice(h * HEAD_DIM, (h + 1) * HEAD_DIM)
        acc = coef[0][:, h:h + 1] * of_ref[0, h]
        acc += coef[1][:, h:h + 1] * of_ref[1, h]
        acc += coef[2][:, h:h + 1] * of_ref[2, h]
        out_ref[:, hs] = acc.astype(out_ref.dtype)


def merge_groups(outs, stats, dils):
    B, d0, L0, U = outs[0].shape
    S = d0 * L0
    H = U // HEAD_DIM
    tm = min(512, S)
    tiles_per_b = S // tm

    def ospec(d, w):
        return pl.BlockSpec((None, d, tm // d, w), lambda i: (i // tiles_per_b, 0, i % tiles_per_b, 0))

    return pl.pallas_call(
        functools.partial(_merge_kernel, H=H, dils=dils),
        grid=(B * S // tm,),
        in_specs=[ospec(d, U) for d in dils] + [ospec(d, LANES) for d in dils],
        out_specs=pl.BlockSpec((tm, U), lambda i: (i, 0)),
        out_shape=jax.ShapeDtypeStruct((B * S, U), BF16),
        scratch_shapes=[pltpu.VMEM((3, H, tm, HEAD_DIM), F32), pltpu.VMEM((3, tm, LANES), F32)],
        compiler_params=_cparams(("parallel",)),
        name="merge_groups",
    )(*outs, *stats)


def _conv_kernel(a0_ref, a1_ref, b0_ref, b1_ref, ha0_ref, ha1_ref, hb0_ref, hb1_ref,
                 w_ref, bias_ref, g_ref, be_ref, o_ref, u_ref, c_ref, sh_ref, *, ts, U):
    i = pl.program_id(1)
    halo = CONV_HALO
    for half, (a_ref, b_ref, ha_ref, hb_ref) in enumerate(((a0_ref, b0_ref, ha0_ref, hb0_ref),
                                                            (a1_ref, b1_ref, ha1_ref, hb1_ref))):
        cs = slice(half * U, (half + 1) * U)
        u_ref[halo:halo + ts, cs] = a_ref[...].astype(F32) * _sigmoid(b_ref[...].astype(F32))
        hu = ha_ref[...].astype(F32) * _sigmoid(hb_ref[...].astype(F32))
        u_ref[0:halo, cs] = jnp.where(i > 0, hu, 0.0)
    C = 2 * U
    rc = 64
    off = halo - (CONV_WIDTH - 1)

    sub = 8
    n_al = ts + halo - sub

    def chan_body(cc, carry):
        c0 = pl.multiple_of(cc * LANES, LANES)
        sh_ref[0] = u_ref[:, pl.ds(c0, LANES)]
        for b in range(1, sub):
            sh_ref[b, 0:n_al, :] = u_ref[b:b + n_al, pl.ds(c0, LANES)]
        for rb in range(ts // rc):
            acc = jnp.zeros((rc, LANES), F32) + bias_ref[:, pl.ds(c0, LANES)]
            for j in range(CONV_WIDTH):
                a, b = divmod(off + j, sub)
                r0 = rb * rc + a * sub
                acc += w_ref[j:j + 1, pl.ds(c0, LANES)] * sh_ref[b, r0:r0 + rc, :]
            c_ref[rb * rc:(rb + 1) * rc, pl.ds(c0, LANES)] = acc
        return carry

    lax.fori_loop(0, C // LANES, chan_body, 0)

    rn = 32

    def norm_body(rb, carry):
        r0 = pl.multiple_of(rb * rn, rn)
        v = c_ref[pl.ds(r0, rn), :]
        mu = jnp.mean(v, axis=-1, keepdims=True)
        dv = v - mu
        var = jnp.mean(dv * dv, axis=-1, keepdims=True)
        y = dv * lax.rsqrt(var + NORM_EPS) * g_ref[...] + be_ref[...]
        o_ref[pl.ds(r0, rn), :] = (y * _sigmoid(y)).astype(o_ref.dtype)
        return carry

    lax.fori_loop(0, ts // rn, norm_body, 0)


def conv_branch(proj, B, S, U, conv_dw, conv_bias, ln_gain, ln_bias):
    IN = proj.shape[1]
    C = 2 * U
    ts = min(256, S)
    pv = proj.reshape(B, S, IN)
    hb = ts // CONV_HALO
    cur = lambda blk: pl.BlockSpec((None, ts, U), lambda b, i, blk=blk: (b, i, blk))
    prv = lambda blk: pl.BlockSpec((None, CONV_HALO, U), lambda b, i, blk=blk: (b, jnp.maximum(i * hb - 1, 0), blk))
    vec = pl.BlockSpec((1, C), lambda b, i: (0, 0))
    out = pl.pallas_call(
        functools.partial(_conv_kernel, ts=ts, U=U),
        grid=(B, S // ts),
        in_specs=[cur(0), cur(1), cur(2), cur(3), prv(0), prv(1), prv(2), prv(3),
                  pl.BlockSpec((CONV_WIDTH, C), lambda b, i: (0, 0)), vec, vec, vec],
        out_specs=pl.BlockSpec((None, ts, C), lambda b, i: (b, i, 0)),
        out_shape=jax.ShapeDtypeStruct((B, S, C), BF16),
        scratch_shapes=[pltpu.VMEM((ts + CONV_HALO, C), F32), pltpu.VMEM((ts, C), F32),
                        pltpu.VMEM((8, ts + CONV_HALO, LANES), F32)],
        compiler_params=_cparams(("parallel", "parallel")),
        name="conv_branch",
    )(pv, pv, pv, pv, pv, pv, pv, pv, conv_dw, conv_bias.reshape(1, C), ln_gain.reshape(1, C), ln_bias.reshape(1, C))
    return out.reshape(B * S, C)


def _gateproj_kernel(cn_ref, am_ref, wc_ref, wa_ref, gc_ref, ga_ref, o_ref):
    conv = jnp.dot(cn_ref[...], wc_ref[...], preferred_element_type=F32)
    z = _sigmoid(gc_ref[...].astype(F32)) * conv
    attn = jnp.dot(am_ref[...], wa_ref[...], preferred_element_type=F32)
    z += _sigmoid(ga_ref[...].astype(F32)) * attn
    o_ref[...] = z.astype(o_ref.dtype)


def gate_proj(cn, am, wc, wa, proj, U):
    N, C = cn.shape
    D = wc.shape[1]
    tn = U
    tm = min(512, N)
    return pl.pallas_call(
        _gateproj_kernel,
        grid=(D // tn, N // tm),
        in_specs=[pl.BlockSpec((tm, C), lambda j, i: (i, 0)),
                  pl.BlockSpec((tm, U), lambda j, i: (i, 0)),
                  pl.BlockSpec((C, tn), lambda j, i: (0, j)),
                  pl.BlockSpec((U, tn), lambda j, i: (0, j)),
                  pl.BlockSpec((tm, tn), lambda j, i: (i, 4 + j)),
                  pl.BlockSpec((tm, tn), lambda j, i: (i, 8 + j))],
        out_specs=pl.BlockSpec((tm, tn), lambda j, i: (i, j)),
        out_shape=jax.ShapeDtypeStruct((N, D), BF16),
        compiler_params=_cparams(("parallel", "parallel")),
        name="gate_proj",
    )(cn, am, wc, wa, proj, proj)


def _outproj_kernel(z_ref, w_ref, x_ref, mod_ref, gain_ref, o_ref, *, nj, tn):
    j = pl.program_id(1)
    c0 = pl.multiple_of(j * tn, tn)
    o_ref[:, pl.ds(c0, tn)] = jnp.dot(z_ref[...], w_ref[...], preferred_element_type=F32)

    @pl.when(j == nj - 1)
    def _():
        y = o_ref[...]
        ms = jnp.mean(y * y, axis=-1, keepdims=True)
        yn = y * lax.rsqrt(ms + NORM_EPS) * gain_ref[...]
        o_ref[...] = x_ref[...] + mod_ref[2:3, :] * yn


def out_proj(z, w, x2d, mod, gain, S):
    N, D = x2d.shape
    K = z.shape[1]
    tm = min(512, S)
    tn = min(512, D)
    tiles_per_b = S // tm
    nj = D // tn
    return pl.pallas_call(
        functools.partial(_outproj_kernel, nj=nj, tn=tn),
        grid=(N // tm, nj),
        in_specs=[pl.BlockSpec((tm, K), lambda i, j: (i, 0)),
                  pl.BlockSpec((K, tn), lambda i, j: (0, j)),
                  pl.BlockSpec((tm, D), lambda i, j: (i, 0)),
                  pl.BlockSpec((None, 6, D), lambda i, j: (i // tiles_per_b, 0, 0)),
                  pl.BlockSpec((1, D), lambda i, j: (0, 0))],
        out_specs=pl.BlockSpec((tm, D), lambda i, j: (i, 0)),
        out_shape=jax.ShapeDtypeStruct((N, D), F32),
        compiler_params=_cparams(("parallel", "arbitrary")),
        name="out_proj",
    )(z, w, x2d, mod, gain.reshape(1, D))


def _router_kernel(x_ref, mod_ref, gain_ref, r_ref, hp_ref, eid_ref, wt_ref, rank_ref, cnt_ref, carry_ref, *, tr):
    step = pl.program_id(0)

    @pl.when(step == 0)
    def _():
        carry_ref[...] = jnp.zeros_like(carry_ref)

    x = x_ref[...]
    D = x.shape[1]
    ms = jnp.mean(x * x, axis=-1, keepdims=True)
    h = x * lax.rsqrt(ms + NORM_EPS) * gain_ref[...]
    h = h * (1.0 + mod_ref[4:5, :]) + mod_ref[3:4, :]
    lo = h[:, :D // 2]
    hi = h[:, D // 2:]
    hp_ref[...] = _pack_halves(lo, hi)
    logits = (jnp.dot(lo.astype(BF16), r_ref[:D // 2, :], preferred_element_type=F32)
              + jnp.dot(hi.astype(BF16), r_ref[D // 2:, :], preferred_element_type=F32))
    lane = lax.broadcasted_iota(I32, logits.shape, 1)
    G = N_EXPERT_GROUPS
    is_g = lane < G
    gl = jnp.where(is_g, logits, NEG_BIG)
    gmax = jnp.max(gl, axis=-1, keepdims=True)
    grp = jnp.min(jnp.where(gl == gmax, lane, LANES), axis=-1, keepdims=True)
    p_grp = 1.0 / jnp.sum(jnp.where(is_g, jnp.exp(gl - gmax), 0.0), axis=-1, keepdims=True)
    in_grp = (lane >= G) & (lane < G + N_EXPERTS) & (((lane - G) // EXPERTS_PER_GROUP) == grp)
    el = jnp.where(in_grp, logits, NEG_BIG)
    v0 = jnp.max(el, axis=-1, keepdims=True)
    i0 = jnp.min(jnp.where(in_grp & (el == v0), lane, LANES), axis=-1, keepdims=True)
    in2 = in_grp & (lane != i0)
    el2 = jnp.where(in2, logits, NEG_BIG)
    v1 = jnp.max(el2, axis=-1, keepdims=True)
    i1 = jnp.min(jnp.where(in2 & (el2 == v1), lane, LANES), axis=-1, keepdims=True)
    e1 = jnp.exp(v1 - v0)
    w0 = p_grp / (1.0 + e1)
    w1 = p_grp * e1 / (1.0 + e1)
    ex0 = i0 - G
    ex1 = i1 - G
    eid_ref[...] = jnp.where(lane == 0, ex0, jnp.where(lane == 1, ex1, 0))
    wt_ref[...] = jnp.where(lane == 0, w0, jnp.where(lane == 1, w1, 0.0))
    oh0 = (lane == ex0).astype(F32)
    oh1 = (lane == ex1).astype(F32)
    both = oh0 + oh1
    rr = lax.broadcasted_iota(I32, (tr, tr), 0)
    cc = lax.broadcasted_iota(I32, (tr, tr), 1)
    tril = (cc < rr).astype(BF16)
    before = jnp.dot(tril, both.astype(BF16), preferred_element_type=F32) + carry_ref[0:1, :]
    rk0 = jnp.sum(before * oh0, axis=-1, keepdims=True)
    rk1 = jnp.sum(before * oh1, axis=-1, keepdims=True)
    rank_ref[...] = jnp.where(lane == 0, rk0, jnp.where(lane == 1, rk1, 0.0))
    newc = carry_ref[0:1, :] + jnp.sum(both, axis=0, keepdims=True)
    carry_ref[...] = jnp.broadcast_to(newc, carry_ref.shape)
    cnt_ref[...] = jnp.broadcast_to(newc, cnt_ref.shape)


def prenorm_router(x1, mod, gain, rcat, S):
    N, D = x1.shape
    tr = min(256, S)
    tiles_per_b = S // tr
    lane_spec = pl.BlockSpec((tr, LANES), lambda i: (i, 0))
    return pl.pallas_call(
        functools.partial(_router_kernel, tr=tr),
        grid=(N // tr,),
        in_specs=[pl.BlockSpec((tr, D), lambda i: (i, 0)),
                  pl.BlockSpec((None, 6, D), lambda i: (i // tiles_per_b, 0, 0)),
                  pl.BlockSpec((1, D), lambda i: (0, 0)),
                  pl.BlockSpec((D, LANES), lambda i: (0, 0))],
        out_specs=[pl.BlockSpec((tr, D // 2), lambda i: (i, 0)), lane_spec, lane_spec, lane_spec,
                   pl.BlockSpec((8, LANES), lambda i: (0, 0))],
        out_shape=[jax.ShapeDtypeStruct((N, D // 2), U32),
                   jax.ShapeDtypeStruct((N, LANES), I32),
                   jax.ShapeDtypeStruct((N, LANES), F32),
                   jax.ShapeDtypeStruct((N, LANES), F32),
                   jax.ShapeDtypeStruct((8, LANES), F32)],
        scratch_shapes=[pltpu.VMEM((8, LANES), F32)],
        compiler_params=_cparams(("arbitrary",)),
        name="prenorm_router",
    )(x1, mod, gain.reshape(1, D), rcat)


def _dest_kernel(eid_ref, rank_ref, start_ref, o_ref):
    lane = lax.broadcasted_iota(I32, eid_ref.shape, 1)
    eid = eid_ref[...]
    rank = rank_ref[...]
    start = start_ref[0:1, :]
    d = []
    for k in range(2):
        oh = (lane == eid[:, k:k + 1]).astype(F32)
        d.append(jnp.sum(oh * start, axis=-1, keepdims=True) + rank[:, k:k + 1])
    o_ref[...] = jnp.where(lane == 0, d[0], jnp.where(lane == 1, d[1], 0.0)).astype(I32)


def dest_rows(eid, rank, pad_start):
    N = eid.shape[0]
    tr = min(1024, N)
    spec = pl.BlockSpec((tr, LANES), lambda i: (i, 0))
    return pl.pallas_call(
        _dest_kernel,
        grid=(N // tr,),
        in_specs=[spec, spec, pl.BlockSpec((8, LANES), lambda i: (0, 0))],
        out_specs=spec,
        out_shape=jax.ShapeDtypeStruct((N, LANES), I32),
        compiler_params=_cparams(("parallel",)),
        name="dest_rows",
    )(eid, rank, pad_start)


def _scatter_kernel(dest_ref, h_ref, xs_in_ref, xs_ref, sem, *, T):
    del xs_in_ref

    def issue(t, carry):
        for k in range(2):
            d = dest_ref[0, 0, 2 * t + k]
            pltpu.make_async_copy(h_ref.at[pl.ds(t, 1)], xs_ref.at[pl.ds(d, 1)], sem).start()
        return carry

    lax.fori_loop(0, T, issue, 0, unroll=4)
    for k in range(2):
        pltpu.make_async_copy(h_ref, xs_ref.at[pl.ds(0, T)], sem).wait()


def scatter_rows(hp, dest, P):
    N, W = hp.shape
    T = min(256, N)
    dest_s = dest[:, :2].reshape(N // T, 1, 2 * T)
    xs0 = jnp.zeros((P, W), hp.dtype)
    return pl.pallas_call(
        functools.partial(_scatter_kernel, T=T),
        grid=(N // T,),
        in_specs=[pl.BlockSpec((1, 1, 2 * T), lambda i: (i, 0, 0), memory_space=pltpu.SMEM),
                  pl.BlockSpec((T, W), lambda i: (i, 0)),
                  pl.BlockSpec(memory_space=pl.ANY)],
        out_specs=pl.BlockSpec(memory_space=pl.ANY),
        out_shape=jax.ShapeDtypeStruct((P, W), hp.dtype),
        scratch_shapes=[pltpu.SemaphoreType.DMA(())],
        input_output_aliases={2: 0},
        compiler_params=_cparams(("arbitrary",)),
        name="scatter_rows",
    )(dest_s, hp, xs0)


def _expert_changed(ce_ref, c):
    return (c == 0) | (ce_ref[c] != ce_ref[jnp.maximum(c - 1, 0)])


def _moe1_kernel(ce_ref, nu_ref, xs_ref, w1_ref, w3_ref, o_ref, wb_ref, *, F):
    c = pl.program_id(0)

    @pl.when(c < nu_ref[0])
    def _():
        @pl.when(_expert_changed(ce_ref, c))
        def _():
            wb_ref[:, :F] = w1_ref[...].astype(BF16)
            wb_ref[:, F:] = w3_ref[...].astype(BF16)

        lo, hi = _unpack_halves(xs_ref[...])
        half = wb_ref.shape[0] // 2
        hcat = (jnp.dot(lo.astype(BF16), wb_ref[:half, :], preferred_element_type=F32)
                + jnp.dot(hi.astype(BF16), wb_ref[half:, :], preferred_element_type=F32))
        a = hcat[:, :F]
        o_ref[...] = (a * _sigmoid(a) * hcat[:, F:]).astype(o_ref.dtype)

    @pl.when(c >= nu_ref[0])
    def _():
        o_ref[...] = jnp.zeros_like(o_ref)


def _moe2_kernel(ce_ref, nu_ref, h_ref, w2_ref, o_ref, wb_ref):
    c = pl.program_id(0)

    @pl.when(c < nu_ref[0])
    def _():
        @pl.when(_expert_changed(ce_ref, c))
        def _():
            wb_ref[...] = w2_ref[...].astype(BF16)

        y = jnp.dot(h_ref[...], wb_ref[...], preferred_element_type=F32)
        half = y.shape[1] // 2
        o_ref[...] = _pack_halves(y[:, :half], y[:, half:])

    @pl.when(c >= nu_ref[0])
    def _():
        o_ref[...] = jnp.zeros_like(o_ref)


def expert_ffn(xs, chunk_expert, n_used, w1, w3, w2):
    P, Wp = xs.shape
    E, D, F = w1.shape
    TM = MOE_ROWS
    n_chunks = P // TM
    hmid = pl.pallas_call(
        functools.partial(_moe1_kernel, F=F),
        grid_spec=pltpu.PrefetchScalarGridSpec(
            num_scalar_prefetch=2,
            grid=(n_chunks,),
            in_specs=[pl.BlockSpec((TM, Wp), lambda c, ce, nu: (c, 0)),
                      pl.BlockSpec((None, D, F), lambda c, ce, nu: (ce[c], 0, 0)),
                      pl.BlockSpec((None, D, F), lambda c, ce, nu: (ce[c], 0, 0))],
            out_specs=pl.BlockSpec((TM, F), lambda c, ce, nu: (c, 0)),
            scratch_shapes=[pltpu.VMEM((D, 2 * F), BF16)]),
        out_shape=jax.ShapeDtypeStruct((P, F), BF16),
        compiler_params=_cparams(("arbitrary",)),
        name="moe_up",
    )(chunk_expert, n_used, xs, w1, w3)
    return pl.pallas_call(
        _moe2_kernel,
        grid_spec=pltpu.PrefetchScalarGridSpec(
            num_scalar_prefetch=2,
            grid=(n_chunks,),
            in_specs=[pl.BlockSpec((TM, F), lambda c, ce, nu: (c, 0)),
                      pl.BlockSpec((None, F, D), lambda c, ce, nu: (ce[c], 0, 0))],
            out_specs=pl.BlockSpec((TM, D // 2), lambda c, ce, nu: (c, 0)),
            scratch_shapes=[pltpu.VMEM((F, D), BF16)]),
        out_shape=jax.ShapeDtypeStruct((P, D // 2), U32),
        compiler_params=_cparams(("arbitrary",)),
        name="moe_down",
    )(chunk_expert, n_used, hmid, w2)


def _combine_kernel(pos_ref, nxt_ref, wt_ref, x_ref, mod_ref, gain_ref, y_hbm, o_ref, buf_ref, sem, *, T):
    i = pl.program_id(0)
    n = pl.num_programs(0)
    slot = i % 2

    def gather(p_ref, s):
        def issue(t, carry):
            for k in range(2):
                p = p_ref[0, 0, 2 * t + k]
                pltpu.make_async_copy(y_hbm.at[pl.ds(p, 1)], buf_ref.at[s, k, pl.ds(t, 1)], sem.at[s]).start()
            return carry

        lax.fori_loop(0, T, issue, 0, unroll=4)

    @pl.when(i == 0)
    def _():
        gather(pos_ref, 0)

    @pl.when(i + 1 < n)
    def _():
        gather(nxt_ref, 1 - slot)

    for k in range(2):
        pltpu.make_async_copy(y_hbm.at[pl.ds(0, T)], buf_ref.at[slot, k], sem.at[slot]).wait()

    wt = wt_ref[...]
    w0 = wt[:, 0:1]
    w1 = wt[:, 1:2]
    lo0, hi0 = _unpack_halves(buf_ref[slot, 0])
    lo1, hi1 = _unpack_halves(buf_ref[slot, 1])
    ylo = w0 * lo0 + w1 * lo1
    yhi = w0 * hi0 + w1 * hi1
    D = x_ref.shape[1]
    half = D // 2
    ms = (jnp.sum(ylo * ylo, axis=-1, keepdims=True) + jnp.sum(yhi * yhi, axis=-1, keepdims=True)) / D
    inv = lax.rsqrt(ms + NORM_EPS)
    o_ref[:, :half] = x_ref[:, :half] + mod_ref[5:6, :half] * (ylo * inv * gain_ref[:, :half])
    o_ref[:, half:] = x_ref[:, half:] + mod_ref[5:6, half:] * (yhi * inv * gain_ref[:, half:])


def combine(yp, dest, wts, x1, mod, gain, S):
    N, D = x1.shape
    T = min(128, S)
    tiles_per_b = S // T
    pos_s = dest[:, :2].reshape(N // T, 1, 2 * T)
    n_tiles = N // T
    return pl.pallas_call(
        functools.partial(_combine_kernel, T=T),
        grid=(n_tiles,),
        in_specs=[pl.BlockSpec((1, 1, 2 * T), lambda i: (i, 0, 0), memory_space=pltpu.SMEM),
                  pl.BlockSpec((1, 1, 2 * T), lambda i: (jnp.minimum(i + 1, n_tiles - 1), 0, 0),
                               memory_space=pltpu.SMEM),
                  pl.BlockSpec((T, LANES), lambda i: (i, 0)),
                  pl.BlockSpec((T, D), lambda i: (i, 0)),
                  pl.BlockSpec((None, 6, D), lambda i: (i // tiles_per_b, 0, 0)),
                  pl.BlockSpec((1, D), lambda i: (0, 0)),
                  pl.BlockSpec(memory_space=pl.ANY)],
        out_specs=pl.BlockSpec((T, D), lambda i: (i, 0)),
        out_shape=jax.ShapeDtypeStruct((N, D), F32),
        scratch_shapes=[pltpu.VMEM((2, 2, T, D // 2), U32), pltpu.SemaphoreType.DMA((2,))],
        compiler_params=_cparams(("arbitrary",)),
        name="combine",
    )(pos_s, pos_s, wts, x1, mod, gain.reshape(1, D), yp)


def _moe_layout(counts):
    TM = MOE_ROWS
    cnt = counts.astype(I32)
    padded = (cnt + TM - 1) // TM * TM
    pad_end = jnp.cumsum(padded)
    pad_start = pad_end - padded
    return pad_start, pad_end


def kernel(x, c, positions, ada_w, ada_b, mix_norm_pre, mix_norm_post, w_in, conv_dw, conv_dw_bias, conv_ln_gain, conv_ln_bias, w_conv_out, w_attn_out, w_out, ffn_norm_pre, ffn_norm_post, router_group, router_expert, expert_w1, expert_w3, expert_w2):
    B, S, D = x.shape
    N = B * S
    U = D // 4
    depth = ada_w.shape[0]
    xc = x.reshape(N, D)
    for layer in range(depth):
        mod = ada_mod(c, ada_w[layer], ada_b[layer]).reshape(B, 6, D)
        h, cosf, sinf = prenorm_rope(xc, mod, mix_norm_pre[layer], positions, S)
        w_in_b = w_in[layer].astype(BF16)
        outs, stats = [], []
        for g, (window, d) in enumerate(ATTN_PATTERNS):
            assert window // d == ATTN_BLOCK and S % (d * ATTN_BLOCK) == 0
            qkv = qkv_proj(h, w_in_b, cosf, sinf, B, S, U, g, d)
            o, st = dilated_attention(qkv, d)
            outs.append(o)
            stats.append(st)
        am = merge_groups(outs, stats, tuple(d for _, d in ATTN_PATTERNS))
        proj = rest_proj(h, w_in_b, U, 9)
        cn = conv_branch(proj, B, S, U, conv_dw[layer], conv_dw_bias[layer], conv_ln_gain[layer], conv_ln_bias[layer])
        z = gate_proj(cn, am, w_conv_out[layer].astype(BF16), w_attn_out[layer].astype(BF16), proj, U)
        x1 = out_proj(z, w_out[layer].astype(BF16), xc, mod, mix_norm_post[layer], S)
        rcat = jnp.zeros((D, LANES), F32)
        rcat = rcat.at[:, :N_EXPERT_GROUPS].set(router_group[layer])
        rcat = rcat.at[:, N_EXPERT_GROUPS:N_EXPERT_GROUPS + N_EXPERTS].set(router_expert[layer]).astype(BF16)
        hp, eid, wts, rank, cnt = prenorm_router(x1, mod, ffn_norm_pre[layer], rcat, S)
        TM = MOE_ROWS
        n_chunks = (2 * N + N_EXPERTS * (TM - 1)) // TM
        P = n_chunks * TM
        pad_start, pad_end = _moe_layout(cnt[0, N_EXPERT_GROUPS * 0:N_EXPERTS])
        n_used = (pad_end[-1] // TM).astype(I32).reshape(1)
        cidx = jnp.minimum(jnp.arange(n_chunks, dtype=I32), n_used[0] - 1)
        chunk_expert = jnp.minimum(jnp.searchsorted(pad_end, cidx * TM, side='right'), N_EXPERTS - 1).astype(I32)
        start_row = jnp.zeros((8, LANES), F32).at[:, :N_EXPERTS].set(pad_start.astype(F32)[None, :])
        dest = dest_rows(eid, rank, start_row)
        xs = scatter_rows(hp, dest, P)
        yp = expert_ffn(xs, chunk_expert, n_used, expert_w1[layer], expert_w3[layer], expert_w2[layer])
        xc = combine(yp, dest, wts, x1, mod, ffn_norm_post[layer], S)
    return xc.reshape(B, S, D)
```

```python
import functools

import jax
import jax.numpy as jnp
from jax import lax
from jax.experimental import pallas as pl
from jax.experimental.pallas import tpu as pltpu

F32 = jnp.float32
BF16 = jnp.bfloat16
I32 = jnp.int32
U32 = jnp.uint32

HEAD_DIM = 128
LANES = 128
ATTN_BLOCK = 128
ATTN_PATTERNS = ((128, 1), (512, 4), (2048, 16))
ROPE_THETA = 10000.0
CONV_WIDTH = 31
CONV_HALO = 32
N_EXPERT_GROUPS = 8
EXPERTS_PER_GROUP = 8
N_EXPERTS = 64
NORM_EPS = 1e-6
NEG_BIG = -1e30
MOE_ROWS = 256
V7X_VMEM_LIMIT = 60 * 1024 * 1024


def _cparams(sem):
    return pltpu.CompilerParams(dimension_semantics=sem, vmem_limit_bytes=V7X_VMEM_LIMIT)


def _sigmoid(x):
    return 1.0 / (1.0 + jnp.exp(-x))


def _pack_halves(lo, hi):
    lo_b = lax.bitcast_convert_type(lo.astype(BF16).astype(F32), U32) >> 16
    hi_b = lax.bitcast_convert_type(hi.astype(BF16).astype(F32), U32) & jnp.uint32(0xFFFF0000)
    return hi_b | lo_b


def _unpack_halves(w):
    lo = lax.bitcast_convert_type(w << 16, F32)
    hi = lax.bitcast_convert_type(w & jnp.uint32(0xFFFF0000), F32)
    return lo, hi


def _ada_kernel(c_ref, w_ref, b_ref, o_ref):
    c = c_ref[...]
    cact = (c * _sigmoid(c)).astype(BF16)
    o_ref[...] = jnp.dot(cact, w_ref[...].astype(BF16), preferred_element_type=F32) + b_ref[...]


def ada_mod(c, ada_w, ada_b):
    B, D = c.shape
    W = ada_w.shape[1]
    rows = 8
    cp = jnp.zeros((rows, D), F32).at[:B].set(c)
    tn = min(512, W)
    out = pl.pallas_call(
        _ada_kernel,
        grid=(W // tn,),
        in_specs=[pl.BlockSpec((rows, D), lambda j: (0, 0)),
                  pl.BlockSpec((D, tn), lambda j: (0, j)),
                  pl.BlockSpec((1, tn), lambda j: (0, j))],
        out_specs=pl.BlockSpec((rows, tn), lambda j: (0, j)),
        out_shape=jax.ShapeDtypeStruct((rows, W), F32),
        compiler_params=_cparams(("parallel",)),
        name="ada_mod",
    )(cp, ada_w, ada_b.reshape(1, W))
    return out[:B]


def _prenorm_kernel(x_ref, mod_ref, gain_ref, pos_ref, invf_ref, h_ref, cos_ref, sin_ref):
    x = x_ref[...]
    ms = jnp.mean(x * x, axis=-1, keepdims=True)
    y = x * lax.rsqrt(ms + NORM_EPS) * gain_ref[...]
    y = y * (1.0 + mod_ref[1:2, :]) + mod_ref[0:1, :]
    h_ref[...] = y.astype(BF16)
    ang = pos_ref[...].astype(F32) * invf_ref[...]
    lane = lax.broadcasted_iota(I32, ang.shape, 1)
    sn = jnp.sin(ang)
    cos_ref[...] = jnp.cos(ang)
    sin_ref[...] = jnp.where(lane < HEAD_DIM // 2, -sn, sn)


def prenorm_rope(x2d, mod, gain, positions, S):
    N, D = x2d.shape
    tr = min(256, S)
    tiles_per_b = S // tr
    half = HEAD_DIM // 2
    inv = ROPE_THETA ** (-jnp.arange(0, HEAD_DIM, 2, dtype=F32) / HEAD_DIM)
    invf = jnp.concatenate([inv, inv]).reshape(1, HEAD_DIM)
    del half
    return pl.pallas_call(
        _prenorm_kernel,
        grid=(N // tr,),
        in_specs=[pl.BlockSpec((tr, D), lambda i: (i, 0)),
                  pl.BlockSpec((None, 6, D), lambda i: (i // tiles_per_b, 0, 0)),
                  pl.BlockSpec((1, D), lambda i: (0, 0)),
                  pl.BlockSpec((tr, 1), lambda i: (i, 0)),
                  pl.BlockSpec((1, HEAD_DIM), lambda i: (0, 0))],
        out_specs=[pl.BlockSpec((tr, D), lambda i: (i, 0)),
                   pl.BlockSpec((tr, HEAD_DIM), lambda i: (i, 0)),
                   pl.BlockSpec((tr, HEAD_DIM), lambda i: (i, 0))],
        out_shape=[jax.ShapeDtypeStruct((N, D), BF16),
                   jax.ShapeDtypeStruct((N, HEAD_DIM), F32),
                   jax.ShapeDtypeStruct((N, HEAD_DIM), F32)],
        compiler_params=_cparams(("parallel",)),
        name="prenorm_rope",
    )(x2d, mod, gain.reshape(1, D), positions.reshape(N, 1), invf)


def _residue_rows(ref, r, d):
    if d == 1:
        return ref[...]
    return ref[pl.ds(r, ref.shape[0] // d, stride=d), :]


def _qkv_kernel(a_ref, w_ref, cos_ref, sin_ref, o_ref, acc_ref, *, d, heads, scale):
    kind = pl.program_id(0)
    acc = jnp.dot(a_ref[...], w_ref[...], preferred_element_type=F32)
    for h in range(heads):
        acc_ref[h] = acc[:, h * HEAD_DIM:(h + 1) * HEAD_DIM]

    @pl.when(kind == 2)
    def _():
        for r in range(d):
            for h in range(heads):
                hs = slice(h * HEAD_DIM, (h + 1) * HEAD_DIM)
                o_ref[r, :, hs] = _residue_rows(acc_ref.at[h], r, d).astype(o_ref.dtype)

    @pl.when(kind < 2)
    def _():
        sc = jnp.where(kind == 0, scale, 1.0).astype(F32)
        for r in range(d):
            c = _residue_rows(cos_ref, r, d) * sc
            s = _residue_rows(sin_ref, r, d) * sc
            for h in range(heads):
                hs = slice(h * HEAD_DIM, (h + 1) * HEAD_DIM)
                t = _residue_rows(acc_ref.at[h], r, d)
                o_ref[r, :, hs] = (t * c + pltpu.roll(t, HEAD_DIM // 2, 1) * s).astype(o_ref.dtype)


def qkv_proj(h, w_bf16, cosf, sinf, B, S, U, g, d):
    N, D = h.shape
    tm = min(1024, S)
    tiles_per_b = S // tm
    kern = functools.partial(_qkv_kernel, d=d, heads=U // HEAD_DIM, scale=HEAD_DIM ** -0.5)
    return pl.pallas_call(
        kern,
        grid=(3, N // tm),
        in_specs=[pl.BlockSpec((tm, D), lambda k, i: (i, 0)),
                  pl.BlockSpec((D, U), lambda k, i: (0, k * 3 + g)),
                  pl.BlockSpec((tm, HEAD_DIM), lambda k, i: (i, 0)),
                  pl.BlockSpec((tm, HEAD_DIM), lambda k, i: (i, 0))],
        out_specs=pl.BlockSpec((None, None, d, tm // d, U),
                               lambda k, i: (k, i // tiles_per_b, 0, i % tiles_per_b, 0)),
        out_shape=jax.ShapeDtypeStruct((3, B, d, S // d, U), BF16),
        scratch_shapes=[pltpu.VMEM((U // HEAD_DIM, tm, HEAD_DIM), F32)],
        compiler_params=_cparams(("parallel", "parallel")),
        name=f"qkv_proj_d{d}",
    )(h, w_bf16, cosf, sinf)


def _matmul_kernel(a_ref, w_ref, o_ref):
    o_ref[...] = jnp.dot(a_ref[...], w_ref[...], preferred_element_type=F32).astype(o_ref.dtype)


def matmul_cols(a, w_bf16, U, first_tile, name):
    N, K = a.shape
    n_tiles = w_bf16.shape[1] // U - first_tile
    tm = min(1024, N)
    return pl.pallas_call(
        _matmul_kernel,
        grid=(n_tiles, N // tm),
        in_specs=[pl.BlockSpec((tm, K), lambda j, i: (i, 0)),
                  pl.BlockSpec((K, U), lambda j, i: (0, first_tile + j))],
        out_specs=pl.BlockSpec((tm, U), lambda j, i: (i, j)),
        out_shape=jax.ShapeDtypeStruct((N, n_tiles * U), BF16),
        compiler_params=_cparams(("parallel", "parallel")),
        name=name,
    )(a, w_bf16)


def _attn_kernel(q_ref, kc_ref, kp_ref, vc_ref, vp_ref, o_ref, st_ref, kx_ref, vx_ref, *, QB, H):
    i = pl.program_id(2)
    blk = ATTN_BLOCK
    kx_ref[0:blk, :] = kp_ref[...]
    kx_ref[blk:, :] = kc_ref[...]
    ones = jnp.ones((vx_ref.shape[0], HEAD_DIM), vx_ref.dtype)
    for h in range(H):
        hs = slice(h * HEAD_DIM, (h + 1) * HEAD_DIM)
        vx_ref[0:blk, 2 * h * HEAD_DIM:(2 * h + 1) * HEAD_DIM] = vp_ref[:, hs]
        vx_ref[blk:, 2 * h * HEAD_DIM:(2 * h + 1) * HEAD_DIM] = vc_ref[:, hs]
        vx_ref[:, (2 * h + 1) * HEAD_DIM:(2 * h + 2) * HEAD_DIM] = ones
    rows = lax.broadcasted_iota(I32, (blk, 2 * blk), 0)
    keys = lax.broadcasted_iota(I32, (blk, 2 * blk), 1)
    band = (keys >= rows) & (keys - blk <= rows)
    lane = lax.broadcasted_iota(I32, (blk, LANES), 1)
    dn = (((1,), (1,)), ((), ()))

    def body(qb, carry):
        r0 = pl.multiple_of(qb * blk, blk)
        ok = band & ((keys >= blk) | (i * QB + qb > 0))
        q = [q_ref[pl.ds(r0, blk), h * HEAD_DIM:(h + 1) * HEAD_DIM] for h in range(H)]
        k = [kx_ref[pl.ds(r0, 2 * blk), h * HEAD_DIM:(h + 1) * HEAD_DIM] for h in range(H)]
        s = [jnp.where(ok, lax.dot_general(q[h], k[h], dn, preferred_element_type=F32), NEG_BIG) for h in range(H)]
        m = [jnp.max(s[h], axis=-1, keepdims=True) for h in range(H)]
        p = [jnp.exp(s[h] - m[h]).astype(BF16) for h in range(H)]
        st = jnp.zeros((blk, LANES), F32)
        for h in range(H):
            v1 = vx_ref[pl.ds(r0, 2 * blk), 2 * h * HEAD_DIM:(2 * h + 2) * HEAD_DIM]
            acc = jnp.dot(p[h], v1, preferred_element_type=F32)
            l = acc[:, HEAD_DIM:]
            o_ref[pl.ds(r0, blk), h * HEAD_DIM:(h + 1) * HEAD_DIM] = (acc[:, :HEAD_DIM] / l).astype(o_ref.dtype)
            st = jnp.where(lane == h, m[h], st)
            st = jnp.where(lane == H + h, l, st)
        st_ref[pl.ds(r0, blk), :] = st
        return carry

    lax.fori_loop(0, QB, body, 0)


def dilated_attention(qkv, d):
    _, B, _, L, U = qkv.shape
    H = U // HEAD_DIM
    R = min(512, L)
    QB = R // ATTN_BLOCK

    def cur(kind):
        return pl.BlockSpec((None, None, None, R, U), lambda b, r, i: (kind, b, r, i, 0))

    def prev(kind):
        return pl.BlockSpec((None, None, None, ATTN_BLOCK, U),
                            lambda b, r, i: (kind, b, r, jnp.maximum(i * QB - 1, 0), 0))

    kern = functools.partial(_attn_kernel, QB=QB, H=H)
    return pl.pallas_call(
        kern,
        grid=(B, d, L // R),
        in_specs=[cur(0), cur(1), prev(1), cur(2), prev(2)],
        out_specs=[pl.BlockSpec((None, None, R, U), lambda b, r, i: (b, r, i, 0)),
                   pl.BlockSpec((None, None, R, LANES), lambda b, r, i: (b, r, i, 0))],
        out_shape=[jax.ShapeDtypeStruct((B, d, L, U), BF16),
                   jax.ShapeDtypeStruct((B, d, L, LANES), F32)],
        scratch_shapes=[pltpu.VMEM((R + ATTN_BLOCK, U), BF16), pltpu.VMEM((R + ATTN_BLOCK, 2 * U), BF16)],
        compiler_params=_cparams(("parallel", "parallel", "parallel")),
        name=f"dilated_attn_d{d}",
    )(qkv, qkv, qkv, qkv, qkv)


def _merge_kernel(o0_ref, o1_ref, o2_ref, s0_ref, s1_ref, s2_ref, out_ref, of_ref, sf_ref, *, H, dils):
    for g, (o_ref, s_ref) in enumerate(((o0_ref, s0_ref), (o1_ref, s1_ref), (o2_ref, s2_ref))):
        d = dils[g]
        n = sf_ref.shape[1] // d
        for r in range(d):
            rows = slice(None) if d == 1 else pl.ds(r, n, stride=d)
            sf_ref[g, rows, :] = s_ref[r]
            for h in range(H):
                of_ref[g, h, rows, :] = o_ref[r, :, h * HEAD_DIM:(h + 1) * HEAD_DIM].astype(F32)
    st = [sf_ref[0], sf_ref[1], sf_ref[2]]
    mx = jnp.maximum(jnp.maximum(st[0], st[1]), st[2])
    w = [pltpu.roll(s, LANES - H, 1) * jnp.exp(s - mx) for s in st]
    tot = w[0] + w[1] + w[2]
    coef = [x / tot for x in w]
    for h in range(H):
        hs = slice(h * HEAD_DIM, (h + 1) * HEAD_DIM)
        acc = coef[0][:, h:h + 1] * of_ref[0, h]
        acc += coef[1][:, h:h + 1] * of_ref[1, h]
        acc += coef[2][:, h:h + 1] * of_ref[2, h]
        out_ref[:, hs] = acc.astype(out_ref.dtype)


def merge_groups(outs, stats, dils):
    B, d0, L0, U = outs[0].shape
    S = d0 * L0
    H = U // HEAD_DIM
    tm = min(512, S)
    tiles_per_b = S // tm

    def ospec(d, w):
        return pl.BlockSpec((None, d, tm // d, w), lambda i: (i // tiles_per_b, 0, i % tiles_per_b, 0))

    return pl.pallas_call(
        functools.partial(_merge_kernel, H=H, dils=dils),
        grid=(B * S // tm,),
        in_specs=[ospec(d, U) for d in dils] + [ospec(d, LANES) for d in dils],
        out_specs=pl.BlockSpec((tm, U), lambda i: (i, 0)),
        out_shape=jax.ShapeDtypeStruct((B * S, U), BF16),
        scratch_shapes=[pltpu.VMEM((3, H, tm, HEAD_DIM), F32), pltpu.VMEM((3, tm, LANES), F32)],
        compiler_params=_cparams(("parallel",)),
        name="merge_groups",
    )(*outs, *stats)


def _conv_kernel(a0_ref, a1_ref, b0_ref, b1_ref, ha0_ref, ha1_ref, hb0_ref, hb1_ref,
                 w_ref, bias_ref, g_ref, be_ref, o_ref, u_ref, c_ref, sh_ref, *, ts, U):
    i = pl.program_id(1)
    halo = CONV_HALO
    for half, (a_ref, b_ref, ha_ref, hb_ref) in enumerate(((a0_ref, b0_ref, ha0_ref, hb0_ref),
                                                            (a1_ref, b1_ref, ha1_ref, hb1_ref))):
        cs = slice(half * U, (half + 1) * U)
        u_ref[halo:halo + ts, cs] = a_ref[...].astype(F32) * _sigmoid(b_ref[...].astype(F32))
        hu = ha_ref[...].astype(F32) * _sigmoid(hb_ref[...].astype(F32))
        u_ref[0:halo, cs] = jnp.where(i > 0, hu, 0.0)
    C = 2 * U
    rc = 64
    off = halo - (CONV_WIDTH - 1)

    sub = 8
    n_al = ts + halo - sub

    def chan_body(cc, carry):
        c0 = pl.multiple_of(cc * LANES, LANES)
        sh_ref[0] = u_ref[:, pl.ds(c0, LANES)]
        for b in range(1, sub):
            sh_ref[b, 0:n_al, :] = u_ref[b:b + n_al, pl.ds(c0, LANES)]
        for rb in range(ts // rc):
            acc = jnp.zeros((rc, LANES), F32) + bias_ref[:, pl.ds(c0, LANES)]
            for j in range(CONV_WIDTH):
                a, b = divmod(off + j, sub)
                r0 = rb * rc + a * sub
                acc += w_ref[j:j + 1, pl.ds(c0, LANES)] * sh_ref[b, r0:r0 + rc, :]
            c_ref[rb * rc:(rb + 1) * rc, pl.ds(c0, LANES)] = acc
        return carry

    lax.fori_loop(0, C // LANES, chan_body, 0)

    rn = 32

    def norm_body(rb, carry):
        r0 = pl.multiple_of(rb * rn, rn)
        v = c_ref[pl.ds(r0, rn), :]
        mu = jnp.mean(v, axis=-1, keepdims=True)
        dv = v - mu
        var = jnp.mean(dv * dv, axis=-1, keepdims=True)
        y = dv * lax.rsqrt(var + NORM_EPS) * g_ref[...] + be_ref[...]
        o_ref[pl.ds(r0, rn), :] = (y * _sigmoid(y)).astype(o_ref.dtype)
        return carry

    lax.fori_loop(0, ts // rn, norm_body, 0)


def conv_branch(proj, B, S, U, conv_dw, conv_bias, ln_gain, ln_bias):
    IN = proj.shape[1]
    C = 2 * U
    ts = min(256, S)
    pv = proj.reshape(B, S, IN)
    hb = ts // CONV_HALO
    cur = lambda blk: pl.BlockSpec((None, ts, U), lambda b, i, blk=blk: (b, i, blk))
    prv = lambda blk: pl.BlockSpec((None, CONV_HALO, U), lambda b, i, blk=blk: (b, jnp.maximum(i * hb - 1, 0), blk))
    vec = pl.BlockSpec((1, C), lambda b, i: (0, 0))
    out = pl.pallas_call(
        functools.partial(_conv_kernel, ts=ts, U=U),
        grid=(B, S // ts),
        in_specs=[cur(0), cur(1), cur(2), cur(3), prv(0), prv(1), prv(2), prv(3),
                  pl.BlockSpec((CONV_WIDTH, C), lambda b, i: (0, 0)), vec, vec, vec],
        out_specs=pl.BlockSpec((None, ts, C), lambda b, i: (b, i, 0)),
        out_shape=jax.ShapeDtypeStruct((B, S, C), BF16),
        scratch_shapes=[pltpu.VMEM((ts + CONV_HALO, C), F32), pltpu.VMEM((ts, C), F32),
                        pltpu.VMEM((8, ts + CONV_HALO, LANES), F32)],
        compiler_params=_cparams(("parallel", "parallel")),
        name="conv_branch",
    )(pv, pv, pv, pv, pv, pv, pv, pv, conv_dw, conv_bias.reshape(1, C), ln_gain.reshape(1, C), ln_bias.reshape(1, C))
    return out.reshape(B * S, C)


def _gateproj_kernel(cn_ref, am_ref, wc_ref, wa_ref, gc_ref, ga_ref, o_ref):
    conv = jnp.dot(cn_ref[...], wc_ref[...], preferred_element_type=F32)
    z = _sigmoid(gc_ref[...].astype(F32)) * conv
    attn = jnp.dot(am_ref[...], wa_ref[...], preferred_element_type=F32)
    z += _sigmoid(ga_ref[...].astype(F32)) * attn
    o_ref[...] = z.astype(o_ref.dtype)


def gate_proj(cn, am, wc, wa, proj, U):
    N, C = cn.shape
    D = wc.shape[1]
    tn = U
    tm = min(512, N)
    return pl.pallas_call(
        _gateproj_kernel,
        grid=(D // tn, N // tm),
        in_specs=[pl.BlockSpec((tm, C), lambda j, i: (i, 0)),
                  pl.BlockSpec((tm, U), lambda j, i: (i, 0)),
                  pl.BlockSpec((C, tn), lambda j, i: (0, j)),
                  pl.BlockSpec((U, tn), lambda j, i: (0, j)),
                  pl.BlockSpec((tm, tn), lambda j, i: (i, 4 + j)),
                  pl.BlockSpec((tm, tn), lambda j, i: (i, 8 + j))],
        out_specs=pl.BlockSpec((tm, tn), lambda j, i: (i, j)),
        out_shape=jax.ShapeDtypeStruct((N, D), BF16),
        compiler_params=_cparams(("parallel", "parallel")),
        name="gate_proj",
    )(cn, am, wc, wa, proj, proj)


def _router_kernel(y_ref, xin_ref, mod_ref, gpost_ref, gain_ref, r_ref,
                   x1_ref, hp_ref, eid_ref, wt_ref, rank_ref, cnt_ref, carry_ref, *, tr):
    step = pl.program_id(0)

    @pl.when(step == 0)
    def _():
        carry_ref[...] = jnp.zeros_like(carry_ref)

    y = y_ref[...].astype(F32)
    yms = jnp.mean(y * y, axis=-1, keepdims=True)
    x = xin_ref[...] + mod_ref[2:3, :] * (y * lax.rsqrt(yms + NORM_EPS) * gpost_ref[...])
    x1_ref[...] = x
    D = x.shape[1]
    ms = jnp.mean(x * x, axis=-1, keepdims=True)
    h = x * lax.rsqrt(ms + NORM_EPS) * gain_ref[...]
    h = h * (1.0 + mod_ref[4:5, :]) + mod_ref[3:4, :]
    lo = h[:, :D // 2]
    hi = h[:, D // 2:]
    hp_ref[...] = _pack_halves(lo, hi)
    logits = (jnp.dot(lo.astype(BF16), r_ref[:D // 2, :], preferred_element_type=F32)
              + jnp.dot(hi.astype(BF16), r_ref[D // 2:, :], preferred_element_type=F32))
    lane = lax.broadcasted_iota(I32, logits.shape, 1)
    G = N_EXPERT_GROUPS
    is_g = lane < G
    gl = jnp.where(is_g, logits, NEG_BIG)
    gmax = jnp.max(gl, axis=-1, keepdims=True)
    grp = jnp.min(jnp.where(gl == gmax, lane, LANES), axis=-1, keepdims=True)
    p_grp = 1.0 / jnp.sum(jnp.where(is_g, jnp.exp(gl - gmax), 0.0), axis=-1, keepdims=True)
    in_grp = (lane >= G) & (lane < G + N_EXPERTS) & (((lane - G) // EXPERTS_PER_GROUP) == grp)
    el = jnp.where(in_grp, logits, NEG_BIG)
    v0 = jnp.max(el, axis=-1, keepdims=True)
    i0 = jnp.min(jnp.where(in_grp & (el == v0), lane, LANES), axis=-1, keepdims=True)
    in2 = in_grp & (lane != i0)
    el2 = jnp.where(in2, logits, NEG_BIG)
    v1 = jnp.max(el2, axis=-1, keepdims=True)
    i1 = jnp.min(jnp.where(in2 & (el2 == v1), lane, LANES), axis=-1, keepdims=True)
    e1 = jnp.exp(v1 - v0)
    w0 = p_grp / (1.0 + e1)
    w1 = p_grp * e1 / (1.0 + e1)
    ex0 = i0 - G
    ex1 = i1 - G
    eid_ref[...] = jnp.where(lane == 0, ex0, jnp.where(lane == 1, ex1, 0))
    wt_ref[...] = jnp.where(lane == 0, w0, jnp.where(lane == 1, w1, 0.0))
    oh0 = (lane == ex0).astype(F32)
    oh1 = (lane == ex1).astype(F32)
    both = oh0 + oh1
    rr = lax.broadcasted_iota(I32, (tr, tr), 0)
    cc = lax.broadcasted_iota(I32, (tr, tr), 1)
    tril = (cc < rr).astype(BF16)
    before = jnp.dot(tril, both.astype(BF16), preferred_element_type=F32) + carry_ref[0:1, :]
    rk0 = jnp.sum(before * oh0, axis=-1, keepdims=True)
    rk1 = jnp.sum(before * oh1, axis=-1, keepdims=True)
    rank_ref[...] = jnp.where(lane == 0, rk0, jnp.where(lane == 1, rk1, 0.0))
    newc = carry_ref[0:1, :] + jnp.sum(both, axis=0, keepdims=True)
    carry_ref[...] = jnp.broadcast_to(newc, carry_ref.shape)
    cnt_ref[...] = jnp.broadcast_to(newc, cnt_ref.shape)


def residual_prenorm_router(y, x2d, mod, gain_post, gain, rcat, S):
    N, D = x2d.shape
    tr = min(256, S)
    tiles_per_b = S // tr
    lane_spec = pl.BlockSpec((tr, LANES), lambda i: (i, 0))
    row_spec = pl.BlockSpec((tr, D), lambda i: (i, 0))
    vec_spec = pl.BlockSpec((1, D), lambda i: (0, 0))
    return pl.pallas_call(
        functools.partial(_router_kernel, tr=tr),
        grid=(N // tr,),
        in_specs=[row_spec, row_spec,
                  pl.BlockSpec((None, 6, D), lambda i: (i // tiles_per_b, 0, 0)),
                  vec_spec, vec_spec,
                  pl.BlockSpec((D, LANES), lambda i: (0, 0))],
        out_specs=[row_spec, pl.BlockSpec((tr, D // 2), lambda i: (i, 0)), lane_spec, lane_spec, lane_spec,
                   pl.BlockSpec((8, LANES), lambda i: (0, 0))],
        out_shape=[jax.ShapeDtypeStruct((N, D), F32),
                   jax.ShapeDtypeStruct((N, D // 2), U32),
                   jax.ShapeDtypeStruct((N, LANES), I32),
                   jax.ShapeDtypeStruct((N, LANES), F32),
                   jax.ShapeDtypeStruct((N, LANES), F32),
                   jax.ShapeDtypeStruct((8, LANES), F32)],
        scratch_shapes=[pltpu.VMEM((8, LANES), F32)],
        compiler_params=_cparams(("arbitrary",)),
        name="residual_prenorm_router",
    )(y, x2d, mod, gain_post.reshape(1, D), gain.reshape(1, D), rcat)


def _dest_kernel(eid_ref, rank_ref, start_ref, o_ref):
    lane = lax.broadcasted_iota(I32, eid_ref.shape, 1)
    eid = eid_ref[...]
    rank = rank_ref[...]
    start = start_ref[0:1, :]
    d = []
    for k in range(2):
        oh = (lane == eid[:, k:k + 1]).astype(F32)
        d.append(jnp.sum(oh * start, axis=-1, keepdims=True) + rank[:, k:k + 1])
    o_ref[...] = jnp.where(lane == 0, d[0], jnp.where(lane == 1, d[1], 0.0)).astype(I32)


def dest_rows(eid, rank, pad_start):
    N = eid.shape[0]
    tr = min(1024, N)
    spec = pl.BlockSpec((tr, LANES), lambda i: (i, 0))
    return pl.pallas_call(
        _dest_kernel,
        grid=(N // tr,),
        in_specs=[spec, spec, pl.BlockSpec((8, LANES), lambda i: (0, 0))],
        out_specs=spec,
        out_shape=jax.ShapeDtypeStruct((N, LANES), I32),
        compiler_params=_cparams(("parallel",)),
        name="dest_rows",
    )(eid, rank, pad_start)


def _scatter_kernel(dest_ref, h_ref, xs_in_ref, xs_ref, sem, *, T):
    del xs_in_ref

    def issue(t, carry):
        for k in range(2):
            d = dest_ref[0, 0, 2 * t + k]
            pltpu.make_async_copy(h_ref.at[pl.ds(t, 1)], xs_ref.at[pl.ds(d, 1)], sem).start()
        return carry

    lax.fori_loop(0, T, issue, 0, unroll=4)
    for k in range(2):
        pltpu.make_async_copy(h_ref, xs_ref.at[pl.ds(0, T)], sem).wait()


def scatter_rows(hp, dest, P):
    N, W = hp.shape
    T = min(256, N)
    dest_s = dest[:, :2].reshape(N // T, 1, 2 * T)
    xs0 = jnp.zeros((P, W), hp.dtype)
    return pl.pallas_call(
        functools.partial(_scatter_kernel, T=T),
        grid=(N // T,),
        in_specs=[pl.BlockSpec((1, 1, 2 * T), lambda i: (i, 0, 0), memory_space=pltpu.SMEM),
                  pl.BlockSpec((T, W), lambda i: (i, 0)),
                  pl.BlockSpec(memory_space=pl.ANY)],
        out_specs=pl.BlockSpec(memory_space=pl.ANY),
        out_shape=jax.ShapeDtypeStruct((P, W), hp.dtype),
        scratch_shapes=[pltpu.SemaphoreType.DMA(())],
        input_output_aliases={2: 0},
        compiler_params=_cparams(("arbitrary",)),
        name="scatter_rows",
    )(dest_s, hp, xs0)


def _expert_chunks(st_ref, nc_ref, in_hbm, out_hbm, ibuf, obuf, isem, osem, compute):
    e = pl.program_id(0)
    n_e = pl.num_programs(0)
    n = nc_ref[e]
    TM = ibuf.shape[1]
    base = pl.multiple_of(st_ref[e], TM)

    def icopy(row, slot):
        return pltpu.make_async_copy(in_hbm.at[pl.ds(row, TM)], ibuf.at[slot], isem.at[slot])

    def ocopy(row, slot):
        return pltpu.make_async_copy(obuf.at[slot], out_hbm.at[pl.ds(row, TM)], osem.at[slot])

    @pl.when((e == 0) & (n > 0))
    def _():
        icopy(base, 0).start()

    def body(c, carry):
        slot = c % 2
        row = pl.multiple_of(base + c * TM, TM)

        @pl.when(c + 1 < n)
        def _():
            icopy(row + TM, 1 - slot).start()

        icopy(row, slot).wait()

        @pl.when(c >= 2)
        def _():
            ocopy(row, slot).wait()

        compute(ibuf.at[slot], obuf.at[slot])
        ocopy(row, slot).start()
        return carry

    lax.fori_loop(0, n, body, 0)

    @pl.when(n >= 2)
    def _():
        ocopy(base, n % 2).wait()

    @pl.when(n >= 1)
    def _():
        ocopy(base, (n + 1) % 2).wait()

    @pl.when(e + 1 < n_e)
    def _():
        @pl.when(nc_ref[e + 1] > 0)
        def _():
            icopy(pl.multiple_of(st_ref[e + 1], TM), 0).start()

    @pl.when(e == n_e - 1)
    def _():
        used = pl.multiple_of(base + n * TM, TM)
        n_tail = (out_hbm.shape[0] - used) // TM
        obuf[0] = jnp.zeros(obuf.shape[1:], obuf.dtype)

        def zstart(c, carry):
            ocopy(pl.multiple_of(used + c * TM, TM), 0).start()
            return carry

        def zwait(c, carry):
            ocopy(used, 0).wait()
            return carry

        lax.fori_loop(0, n_tail, zstart, 0)
        lax.fori_loop(0, n_tail, zwait, 0)


def _moe_up_kernel(st_ref, nc_ref, xs_hbm, w1_ref, w3_ref, o_hbm, wb_ref, ibuf, obuf, isem, osem, *, F):
    @pl.when(nc_ref[pl.program_id(0)] > 0)
    def _():
        wb_ref[:, :F] = w1_ref[...].astype(BF16)
        wb_ref[:, F:] = w3_ref[...].astype(BF16)

    def compute(x_ref, o_ref):
        lo, hi = _unpack_halves(x_ref[...])
        half = wb_ref.shape[0] // 2
        hcat = (jnp.dot(lo.astype(BF16), wb_ref[:half, :], preferred_element_type=F32)
                + jnp.dot(hi.astype(BF16), wb_ref[half:, :], preferred_element_type=F32))
        a = hcat[:, :F]
        o_ref[...] = (a * _sigmoid(a) * hcat[:, F:]).astype(o_ref.dtype)

    _expert_chunks(st_ref, nc_ref, xs_hbm, o_hbm, ibuf, obuf, isem, osem, compute)


def _moe_down_kernel(st_ref, nc_ref, h_hbm, w2_ref, o_hbm, wb_ref, ibuf, obuf, isem, osem):
    @pl.when(nc_ref[pl.program_id(0)] > 0)
    def _():
        wb_ref[...] = w2_ref[...].astype(BF16)

    def compute(h_ref, o_ref):
        y = jnp.dot(h_ref[...], wb_ref[...], preferred_element_type=F32)
        half = y.shape[1] // 2
        o_ref[...] = _pack_halves(y[:, :half], y[:, half:])

    _expert_chunks(st_ref, nc_ref, h_hbm, o_hbm, ibuf, obuf, isem, osem, compute)


def expert_ffn(xs, seg_start, seg_chunks, w1, w3, w2):
    P, Wp = xs.shape
    E, D, F = w1.shape
    TM = MOE_ROWS
    any_spec = pl.BlockSpec(memory_space=pl.ANY)
    dma2 = pltpu.SemaphoreType.DMA((2,))
    hmid = pl.pallas_call(
        functools.partial(_moe_up_kernel, F=F),
        grid_spec=pltpu.PrefetchScalarGridSpec(
            num_scalar_prefetch=2,
            grid=(E,),
            in_specs=[any_spec,
                      pl.BlockSpec((None, D, F), lambda e, st, nc: (e, 0, 0)),
                      pl.BlockSpec((None, D, F), lambda e, st, nc: (e, 0, 0))],
            out_specs=any_spec,
            scratch_shapes=[pltpu.VMEM((D, 2 * F), BF16), pltpu.VMEM((2, TM, Wp), U32),
                            pltpu.VMEM((2, TM, F), BF16), dma2, dma2]),
        out_shape=jax.ShapeDtypeStruct((P, F), BF16),
        compiler_params=_cparams(("arbitrary",)),
        name="moe_up",
    )(seg_start, seg_chunks, xs, w1, w3)
    return pl.pallas_call(
        _moe_down_kernel,
        grid_spec=pltpu.PrefetchScalarGridSpec(
            num_scalar_prefetch=2,
            grid=(E,),
            in_specs=[any_spec,
                      pl.BlockSpec((None, F, D), lambda e, st, nc: (e, 0, 0))],
            out_specs=any_spec,
            scratch_shapes=[pltpu.VMEM((F, D), BF16), pltpu.VMEM((2, TM, F), BF16),
                            pltpu.VMEM((2, TM, D // 2), U32), dma2, dma2]),
        out_shape=jax.ShapeDtypeStruct((P, D // 2), U32),
        compiler_params=_cparams(("arbitrary",)),
        name="moe_down",
    )(seg_start, seg_chunks, hmid, w2)


def _combine_kernel(pos_ref, nxt_ref, wt_ref, x_ref, mod_ref, gain_ref, y_hbm, o_ref, buf_ref, sem, *, T):
    i = pl.program_id(0)
    n = pl.num_programs(0)
    slot = i % 2

    def gather(p_ref, s):
        def issue(t, carry):
            for k in range(2):
                p = p_ref[0, 0, 2 * t + k]
                pltpu.make_async_copy(y_hbm.at[pl.ds(p, 1)], buf_ref.at[s, k, pl.ds(t, 1)], sem.at[s]).start()
            return carry

        lax.fori_loop(0, T, issue, 0, unroll=4)

    @pl.when(i == 0)
    def _():
        gather(pos_ref, 0)

    @pl.when(i + 1 < n)
    def _():
        gather(nxt_ref, 1 - slot)

    for k in range(2):
        pltpu.make_async_copy(y_hbm.at[pl.ds(0, T)], buf_ref.at[slot, k], sem.at[slot]).wait()

    wt = wt_ref[...]
    w0 = wt[:, 0:1]
    w1 = wt[:, 1:2]
    lo0, hi0 = _unpack_halves(buf_ref[slot, 0])
    lo1, hi1 = _unpack_halves(buf_ref[slot, 1])
    ylo = w0 * lo0 + w1 * lo1
    yhi = w0 * hi0 + w1 * hi1
    D = x_ref.shape[1]
    half = D // 2
    ms = (jnp.sum(ylo * ylo, axis=-1, keepdims=True) + jnp.sum(yhi * yhi, axis=-1, keepdims=True)) / D
    inv = lax.rsqrt(ms + NORM_EPS)
    o_ref[:, :half] = x_ref[:, :half] + mod_ref[5:6, :half] * (ylo * inv * gain_ref[:, :half])
    o_ref[:, half:] = x_ref[:, half:] + mod_ref[5:6, half:] * (yhi * inv * gain_ref[:, half:])


def combine(yp, dest, wts, x1, mod, gain, S):
    N, D = x1.shape
    T = min(128, S)
    tiles_per_b = S // T
    pos_s = dest[:, :2].reshape(N // T, 1, 2 * T)
    n_tiles = N // T
    return pl.pallas_call(
        functools.partial(_combine_kernel, T=T),
        grid=(n_tiles,),
        in_specs=[pl.BlockSpec((1, 1, 2 * T), lambda i: (i, 0, 0), memory_space=pltpu.SMEM),
                  pl.BlockSpec((1, 1, 2 * T), lambda i: (jnp.minimum(i + 1, n_tiles - 1), 0, 0),
                               memory_space=pltpu.SMEM),
                  pl.BlockSpec((T, LANES), lambda i: (i, 0)),
                  pl.BlockSpec((T, D), lambda i: (i, 0)),
                  pl.BlockSpec((None, 6, D), lambda i: (i // tiles_per_b, 0, 0)),
                  pl.BlockSpec((1, D), lambda i: (0, 0)),
                  pl.BlockSpec(memory_space=pl.ANY)],
        out_specs=pl.BlockSpec((T, D), lambda i: (i, 0)),
        out_shape=jax.ShapeDtypeStruct((N, D), F32),
        scratch_shapes=[pltpu.VMEM((2, 2, T, D // 2), U32), pltpu.SemaphoreType.DMA((2,))],
        compiler_params=_cparams(("arbitrary",)),
        name="combine",
    )(pos_s, pos_s, wts, x1, mod, gain.reshape(1, D), yp)


def _moe_layout(counts):
    TM = MOE_ROWS
    chunks = (counts.astype(I32) + TM - 1) // TM
    start = (jnp.cumsum(chunks) - chunks) * TM
    return start, chunks


def kernel(x, c, positions, ada_w, ada_b, mix_norm_pre, mix_norm_post, w_in, conv_dw, conv_dw_bias, conv_ln_gain, conv_ln_bias, w_conv_out, w_attn_out, w_out, ffn_norm_pre, ffn_norm_post, router_group, router_expert, expert_w1, expert_w3, expert_w2):
    B, S, D = x.shape
    N = B * S
    U = D // 4
    depth = ada_w.shape[0]
    xc = x.reshape(N, D)
    for layer in range(depth):
        mod = ada_mod(c, ada_w[layer], ada_b[layer]).reshape(B, 6, D)
        h, cosf, sinf = prenorm_rope(xc, mod, mix_norm_pre[layer], positions, S)
        w_in_b = w_in[layer].astype(BF16)
        outs, stats = [], []
        for g, (window, d) in enumerate(ATTN_PATTERNS):
            assert window // d == ATTN_BLOCK and S % (d * ATTN_BLOCK) == 0
            qkv = qkv_proj(h, w_in_b, cosf, sinf, B, S, U, g, d)
            o, st = dilated_attention(qkv, d)
            outs.append(o)
            stats.append(st)
        am = merge_groups(outs, stats, tuple(d for _, d in ATTN_PATTERNS))
        proj = matmul_cols(h, w_in_b, U, 9, "rest_proj")
        cn = conv_branch(proj, B, S, U, conv_dw[layer], conv_dw_bias[layer], conv_ln_gain[layer], conv_ln_bias[layer])
        z = gate_proj(cn, am, w_conv_out[layer].astype(BF16), w_attn_out[layer].astype(BF16), proj, U)
        y = matmul_cols(z, w_out[layer].astype(BF16), U, 0, "out_proj")
        rcat = jnp.zeros((D, LANES), F32)
        rcat = rcat.at[:, :N_EXPERT_GROUPS].set(router_group[layer])
        rcat = rcat.at[:, N_EXPERT_GROUPS:N_EXPERT_GROUPS + N_EXPERTS].set(router_expert[layer]).astype(BF16)
        x1, hp, eid, wts, rank, cnt = residual_prenorm_router(
            y, xc, mod, mix_norm_post[layer], ffn_norm_pre[layer], rcat, S)
        TM = MOE_ROWS
        P = (2 * N + N_EXPERTS * (TM - 1)) // TM * TM
        seg_start, seg_chunks = _moe_layout(cnt[0, :N_EXPERTS])
        start_row = jnp.zeros((8, LANES), F32).at[:, :N_EXPERTS].set(seg_start.astype(F32)[None, :])
        dest = dest_rows(eid, rank, start_row)
        xs = scatter_rows(hp, dest, P)
        yp = expert_ffn(xs, seg_start, seg_chunks, expert_w1[layer], expert_w3[layer], expert_w2[layer])
        xc = combine(yp, dest, wts, x1, mod, ffn_norm_post[layer], S)
    return xc.reshape(B, S, D)
```

```python
import functools

import jax
import jax.numpy as jnp
from jax import lax
from jax.experimental import pallas as pl
from jax.experimental.pallas import tpu as pltpu

F32 = jnp.float32
BF16 = jnp.bfloat16
I32 = jnp.int32
U32 = jnp.uint32

HEAD_DIM = 128
LANES = 128
ATTN_BLOCK = 128
ATTN_PATTERNS = ((128, 1), (512, 4), (2048, 16))
ROPE_THETA = 10000.0
CONV_WIDTH = 31
CONV_HALO = 32
N_EXPERT_GROUPS = 8
EXPERTS_PER_GROUP = 8
N_EXPERTS = 64
NORM_EPS = 1e-6
NEG_BIG = -1e30
MOE_ROWS = 256
V7X_VMEM_LIMIT = 60 * 1024 * 1024


def _cparams(sem):
    return pltpu.CompilerParams(dimension_semantics=sem, vmem_limit_bytes=V7X_VMEM_LIMIT)


def _sigmoid(x):
    return 1.0 / (1.0 + jnp.exp(-x))


def _pack_halves(lo, hi):
    lo_b = lax.bitcast_convert_type(lo.astype(BF16).astype(F32), U32) >> 16
    hi_b = lax.bitcast_convert_type(hi.astype(BF16).astype(F32), U32) & jnp.uint32(0xFFFF0000)
    return hi_b | lo_b


def _unpack_halves(w):
    lo = lax.bitcast_convert_type(w << 16, F32)
    hi = lax.bitcast_convert_type(w & jnp.uint32(0xFFFF0000), F32)
    return lo, hi


def _ada_kernel(c_ref, w_ref, b_ref, o_ref):
    c = c_ref[...]
    cact = (c * _sigmoid(c)).astype(BF16)
    o_ref[...] = jnp.dot(cact, w_ref[...].astype(BF16), preferred_element_type=F32) + b_ref[...]


def ada_mod(c, ada_w, ada_b):
    B, D = c.shape
    W = ada_w.shape[1]
    rows = 8
    cp = jnp.zeros((rows, D), F32).at[:B].set(c)
    tn = min(512, W)
    out = pl.pallas_call(
        _ada_kernel,
        grid=(W // tn,),
        in_specs=[pl.BlockSpec((rows, D), lambda j: (0, 0)),
                  pl.BlockSpec((D, tn), lambda j: (0, j)),
                  pl.BlockSpec((1, tn), lambda j: (0, j))],
        out_specs=pl.BlockSpec((rows, tn), lambda j: (0, j)),
        out_shape=jax.ShapeDtypeStruct((rows, W), F32),
        compiler_params=_cparams(("parallel",)),
        name="ada_mod",
    )(cp, ada_w, ada_b.reshape(1, W))
    return out[:B]


def _prenorm_kernel(x_ref, mod_ref, gain_ref, pos_ref, invf_ref, h_ref, cos_ref, sin_ref):
    x = x_ref[...]
    ms = jnp.mean(x * x, axis=-1, keepdims=True)
    y = x * lax.rsqrt(ms + NORM_EPS) * gain_ref[...]
    y = y * (1.0 + mod_ref[1:2, :]) + mod_ref[0:1, :]
    h_ref[...] = y.astype(BF16)
    ang = pos_ref[...].astype(F32) * invf_ref[...]
    lane = lax.broadcasted_iota(I32, ang.shape, 1)
    sn = jnp.sin(ang)
    cos_ref[...] = jnp.cos(ang)
    sin_ref[...] = jnp.where(lane < HEAD_DIM // 2, -sn, sn)


def prenorm_rope(x2d, mod, gain, positions, S):
    N, D = x2d.shape
    tr = min(256, S)
    tiles_per_b = S // tr
    half = HEAD_DIM // 2
    inv = ROPE_THETA ** (-jnp.arange(0, HEAD_DIM, 2, dtype=F32) / HEAD_DIM)
    invf = jnp.concatenate([inv, inv]).reshape(1, HEAD_DIM)
    del half
    return pl.pallas_call(
        _prenorm_kernel,
        grid=(N // tr,),
        in_specs=[pl.BlockSpec((tr, D), lambda i: (i, 0)),
                  pl.BlockSpec((None, 6, D), lambda i: (i // tiles_per_b, 0, 0)),
                  pl.BlockSpec((1, D), lambda i: (0, 0)),
                  pl.BlockSpec((tr, 1), lambda i: (i, 0)),
                  pl.BlockSpec((1, HEAD_DIM), lambda i: (0, 0))],
        out_specs=[pl.BlockSpec((tr, D), lambda i: (i, 0)),
                   pl.BlockSpec((tr, HEAD_DIM), lambda i: (i, 0)),
                   pl.BlockSpec((tr, HEAD_DIM), lambda i: (i, 0))],
        out_shape=[jax.ShapeDtypeStruct((N, D), BF16),
                   jax.ShapeDtypeStruct((N, HEAD_DIM), F32),
                   jax.ShapeDtypeStruct((N, HEAD_DIM), F32)],
        compiler_params=_cparams(("parallel",)),
        name="prenorm_rope",
    )(x2d, mod, gain.reshape(1, D), positions.reshape(N, 1), invf)


def _residue_rows(ref, r, d):
    if d == 1:
        return ref[...]
    return ref[pl.ds(r, ref.shape[0] // d, stride=d), :]


def _qkv_kernel(a_ref, w_ref, cos_ref, sin_ref, o_ref, acc_ref, *, d, heads, scale, n_i):
    s = pl.program_id(0)

    @pl.when(s == 0)
    def _():
        acc_ref[...] = jnp.zeros_like(acc_ref)

    kind = jnp.maximum(s - 1, 0) // n_i
    sc = jnp.where(kind == 0, scale, 1.0).astype(F32)
    is_v = kind == 2
    for r in range(d):
        c = jnp.where(is_v, 1.0, _residue_rows(cos_ref, r, d) * sc)
        sn = jnp.where(is_v, 0.0, _residue_rows(sin_ref, r, d) * sc)
        for h in range(heads):
            t = _residue_rows(acc_ref.at[h], r, d)
            o_ref[r, :, h * HEAD_DIM:(h + 1) * HEAD_DIM] = (
                t * c + pltpu.roll(t, HEAD_DIM // 2, 1) * sn).astype(o_ref.dtype)
    acc = jnp.dot(a_ref[...], w_ref[...], preferred_element_type=F32)
    for h in range(heads):
        acc_ref[h] = acc[:, h * HEAD_DIM:(h + 1) * HEAD_DIM]


def qkv_proj(h, w_bf16, cosf, sinf, B, S, U, g, d):
    N, D = h.shape
    tm = min(1024, S)
    tiles_per_b = S // tm
    n_i = N // tm
    last = 3 * n_i - 1
    kern = functools.partial(_qkv_kernel, d=d, heads=U // HEAD_DIM, scale=HEAD_DIM ** -0.5, n_i=n_i)

    def cur(s):
        return jnp.minimum(s, last)

    def fin(s):
        return jnp.maximum(s - 1, 0)

    return pl.pallas_call(
        kern,
        grid=(3 * n_i + 1,),
        in_specs=[pl.BlockSpec((tm, D), lambda s: (cur(s) % n_i, 0)),
                  pl.BlockSpec((D, U), lambda s: (0, (cur(s) // n_i) * 3 + g)),
                  pl.BlockSpec((tm, HEAD_DIM), lambda s: (fin(s) % n_i, 0)),
                  pl.BlockSpec((tm, HEAD_DIM), lambda s: (fin(s) % n_i, 0))],
        out_specs=pl.BlockSpec((None, None, d, tm // d, U),
                               lambda s: (fin(s) // n_i, (fin(s) % n_i) // tiles_per_b, 0,
                                          (fin(s) % n_i) % tiles_per_b, 0)),
        out_shape=jax.ShapeDtypeStruct((3, B, d, S // d, U), BF16),
        scratch_shapes=[pltpu.VMEM((U // HEAD_DIM, tm, HEAD_DIM), F32)],
        compiler_params=_cparams(("arbitrary",)),
        name=f"qkv_proj_d{d}",
    )(h, w_bf16, cosf, sinf)


def _matmul_kernel(a_ref, w_ref, o_ref):
    o_ref[...] = jnp.dot(a_ref[...], w_ref[...], preferred_element_type=F32).astype(o_ref.dtype)


def matmul_cols(a, w_bf16, U, first_tile, name):
    N, K = a.shape
    n_tiles = w_bf16.shape[1] // U - first_tile
    tm = min(1024, N)
    return pl.pallas_call(
        _matmul_kernel,
        grid=(n_tiles, N // tm),
        in_specs=[pl.BlockSpec((tm, K), lambda j, i: (i, 0)),
                  pl.BlockSpec((K, U), lambda j, i: (0, first_tile + j))],
        out_specs=pl.BlockSpec((tm, U), lambda j, i: (i, j)),
        out_shape=jax.ShapeDtypeStruct((N, n_tiles * U), BF16),
        compiler_params=_cparams(("parallel", "parallel")),
        name=name,
    )(a, w_bf16)


def _attn_kernel(q_ref, kc_ref, kp_ref, vc_ref, vp_ref, o_ref, st_ref, kx_ref, vx_ref, *, QB, H):
    i = pl.program_id(2)
    blk = ATTN_BLOCK
    kx_ref[0:blk, :] = kp_ref[...]
    kx_ref[blk:, :] = kc_ref[...]
    ones = jnp.ones((vx_ref.shape[0], HEAD_DIM), vx_ref.dtype)
    for h in range(H):
        hs = slice(h * HEAD_DIM, (h + 1) * HEAD_DIM)
        vx_ref[0:blk, 2 * h * HEAD_DIM:(2 * h + 1) * HEAD_DIM] = vp_ref[:, hs]
        vx_ref[blk:, 2 * h * HEAD_DIM:(2 * h + 1) * HEAD_DIM] = vc_ref[:, hs]
        vx_ref[:, (2 * h + 1) * HEAD_DIM:(2 * h + 2) * HEAD_DIM] = ones
    rows = lax.broadcasted_iota(I32, (blk, 2 * blk), 0)
    keys = lax.broadcasted_iota(I32, (blk, 2 * blk), 1)
    band = (keys >= rows) & (keys - blk <= rows)
    lane = lax.broadcasted_iota(I32, (blk, LANES), 1)
    dn = (((1,), (1,)), ((), ()))

    def body(qb, carry):
        r0 = pl.multiple_of(qb * blk, blk)
        ok = band & ((keys >= blk) | (i * QB + qb > 0))
        q = [q_ref[pl.ds(r0, blk), h * HEAD_DIM:(h + 1) * HEAD_DIM] for h in range(H)]
        k = [kx_ref[pl.ds(r0, 2 * blk), h * HEAD_DIM:(h + 1) * HEAD_DIM] for h in range(H)]
        s = [jnp.where(ok, lax.dot_general(q[h], k[h], dn, preferred_element_type=F32), NEG_BIG) for h in range(H)]
        m = [jnp.max(s[h], axis=-1, keepdims=True) for h in range(H)]
        p = [jnp.exp(s[h] - m[h]).astype(BF16) for h in range(H)]
        st = jnp.zeros((blk, LANES), F32)
        for h in range(H):
            v1 = vx_ref[pl.ds(r0, 2 * blk), 2 * h * HEAD_DIM:(2 * h + 2) * HEAD_DIM]
            acc = jnp.dot(p[h], v1, preferred_element_type=F32)
            l = acc[:, HEAD_DIM:]
            o_ref[pl.ds(r0, blk), h * HEAD_DIM:(h + 1) * HEAD_DIM] = (acc[:, :HEAD_DIM] / l).astype(o_ref.dtype)
            st = jnp.where(lane == h, m[h], st)
            st = jnp.where(lane == H + h, l, st)
        st_ref[pl.ds(r0, blk), :] = st
        return carry

    lax.fori_loop(0, QB, body, 0)


def dilated_attention(qkv, d):
    _, B, _, L, U = qkv.shape
    H = U // HEAD_DIM
    R = min(512, L)
    QB = R // ATTN_BLOCK

    def cur(kind):
        return pl.BlockSpec((None, None, None, R, U), lambda b, r, i: (kind, b, r, i, 0))

    def prev(kind):
        return pl.BlockSpec((None, None, None, ATTN_BLOCK, U),
                            lambda b, r, i: (kind, b, r, jnp.maximum(i * QB - 1, 0), 0))

    kern = functools.partial(_attn_kernel, QB=QB, H=H)
    return pl.pallas_call(
        kern,
        grid=(B, d, L // R),
        in_specs=[cur(0), cur(1), prev(1), cur(2), prev(2)],
        out_specs=[pl.BlockSpec((None, None, R, U), lambda b, r, i: (b, r, i, 0)),
                   pl.BlockSpec((None, None, R, LANES), lambda b, r, i: (b, r, i, 0))],
        out_shape=[jax.ShapeDtypeStruct((B, d, L, U), BF16),
                   jax.ShapeDtypeStruct((B, d, L, LANES), F32)],
        scratch_shapes=[pltpu.VMEM((R + ATTN_BLOCK, U), BF16), pltpu.VMEM((R + ATTN_BLOCK, 2 * U), BF16)],
        compiler_params=_cparams(("parallel", "parallel", "parallel")),
        name=f"dilated_attn_d{d}",
    )(qkv, qkv, qkv, qkv, qkv)


def _merge_kernel(o0_ref, o1_ref, o2_ref, s0_ref, s1_ref, s2_ref, out_ref, of_ref, sf_ref, *, H, dils):
    for g, (o_ref, s_ref) in enumerate(((o0_ref, s0_ref), (o1_ref, s1_ref), (o2_ref, s2_ref))):
        d = dils[g]
        n = sf_ref.shape[1] // d
        for r in range(d):
            rows = slice(None) if d == 1 else pl.ds(r, n, stride=d)
            sf_ref[g, rows, :] = s_ref[r]
            for h in range(H):
                of_ref[g, h, rows, :] = o_ref[r, :, h * HEAD_DIM:(h + 1) * HEAD_DIM].astype(F32)
    st = [sf_ref[0], sf_ref[1], sf_ref[2]]
    mx = jnp.maximum(jnp.maximum(st[0], st[1]), st[2])
    w = [pltpu.roll(s, LANES - H, 1) * jnp.exp(s - mx) for s in st]
    tot = w[0] + w[1] + w[2]
    coef = [x / tot for x in w]
    for h in range(H):
        hs = slice(h * HEAD_DIM, (h + 1) * HEAD_DIM)
        acc = coef[0][:, h:h + 1] * of_ref[0, h]
        acc += coef[1][:, h:h + 1] * of_ref[1, h]
        acc += coef[2][:, h:h + 1] * of_ref[2, h]
        out_ref[:, hs] = acc.astype(out_ref.dtype)


def merge_groups(outs, stats, dils):
    B, d0, L0, U = outs[0].shape
    S = d0 * L0
    H = U // HEAD_DIM
    tm = min(512, S)
    tiles_per_b = S // tm

    def ospec(d, w):
        return pl.BlockSpec((None, d, tm // d, w), lambda i: (i // tiles_per_b, 0, i % tiles_per_b, 0))

    return pl.pallas_call(
        functools.partial(_merge_kernel, H=H, dils=dils),
        grid=(B * S // tm,),
        in_specs=[ospec(d, U) for d in dils] + [ospec(d, LANES) for d in dils],
        out_specs=pl.BlockSpec((tm, U), lambda i: (i, 0)),
        out_shape=jax.ShapeDtypeStruct((B * S, U), BF16),
        scratch_shapes=[pltpu.VMEM((3, H, tm, HEAD_DIM), F32), pltpu.VMEM((3, tm, LANES), F32)],
        compiler_params=_cparams(("parallel",)),
        name="merge_groups",
    )(*outs, *stats)


def _conv_kernel(a0_ref, a1_ref, b0_ref, b1_ref, ha0_ref, ha1_ref, hb0_ref, hb1_ref,
                 w_ref, bias_ref, g_ref, be_ref, o_ref, u_ref, c_ref, sh_ref, *, ts, U):
    i = pl.program_id(1)
    halo = CONV_HALO
    for half, (a_ref, b_ref, ha_ref, hb_ref) in enumerate(((a0_ref, b0_ref, ha0_ref, hb0_ref),
                                                            (a1_ref, b1_ref, ha1_ref, hb1_ref))):
        cs = slice(half * U, (half + 1) * U)
        u_ref[halo:halo + ts, cs] = a_ref[...].astype(F32) * _sigmoid(b_ref[...].astype(F32))
        hu = ha_ref[...].astype(F32) * _sigmoid(hb_ref[...].astype(F32))
        u_ref[0:halo, cs] = jnp.where(i > 0, hu, 0.0)
    C = 2 * U
    rc = 64
    off = halo - (CONV_WIDTH - 1)

    sub = 8
    n_al = ts + halo - sub

    def chan_body(cc, carry):
        c0 = pl.multiple_of(cc * LANES, LANES)
        sh_ref[0] = u_ref[:, pl.ds(c0, LANES)]
        for b in range(1, sub):
            sh_ref[b, 0:n_al, :] = u_ref[b:b + n_al, pl.ds(c0, LANES)]
        for rb in range(ts // rc):
            acc = jnp.zeros((rc, LANES), F32) + bias_ref[:, pl.ds(c0, LANES)]
            for j in range(CONV_WIDTH):
                a, b = divmod(off + j, sub)
                r0 = rb * rc + a * sub
                acc += w_ref[j:j + 1, pl.ds(c0, LANES)] * sh_ref[b, r0:r0 + rc, :]
            c_ref[rb * rc:(rb + 1) * rc, pl.ds(c0, LANES)] = acc
        return carry

    lax.fori_loop(0, C // LANES, chan_body, 0)

    rn = 32

    def norm_body(rb, carry):
        r0 = pl.multiple_of(rb * rn, rn)
        v = c_ref[pl.ds(r0, rn), :]
        mu = jnp.mean(v, axis=-1, keepdims=True)
        dv = v - mu
        var = jnp.mean(dv * dv, axis=-1, keepdims=True)
        y = dv * lax.rsqrt(var + NORM_EPS) * g_ref[...] + be_ref[...]
        o_ref[pl.ds(r0, rn), :] = (y * _sigmoid(y)).astype(o_ref.dtype)
        return carry

    lax.fori_loop(0, ts // rn, norm_body, 0)


def conv_branch(proj, B, S, U, conv_dw, conv_bias, ln_gain, ln_bias):
    IN = proj.shape[1]
    C = 2 * U
    ts = min(256, S)
    pv = proj.reshape(B, S, IN)
    hb = ts // CONV_HALO
    cur = lambda blk: pl.BlockSpec((None, ts, U), lambda b, i, blk=blk: (b, i, blk))
    prv = lambda blk: pl.BlockSpec((None, CONV_HALO, U), lambda b, i, blk=blk: (b, jnp.maximum(i * hb - 1, 0), blk))
    vec = pl.BlockSpec((1, C), lambda b, i: (0, 0))
    out = pl.pallas_call(
        functools.partial(_conv_kernel, ts=ts, U=U),
        grid=(B, S // ts),
        in_specs=[cur(0), cur(1), cur(2), cur(3), prv(0), prv(1), prv(2), prv(3),
                  pl.BlockSpec((CONV_WIDTH, C), lambda b, i: (0, 0)), vec, vec, vec],
        out_specs=pl.BlockSpec((None, ts, C), lambda b, i: (b, i, 0)),
        out_shape=jax.ShapeDtypeStruct((B, S, C), BF16),
        scratch_shapes=[pltpu.VMEM((ts + CONV_HALO, C), F32), pltpu.VMEM((ts, C), F32),
                        pltpu.VMEM((8, ts + CONV_HALO, LANES), F32)],
        compiler_params=_cparams(("parallel", "parallel")),
        name="conv_branch",
    )(pv, pv, pv, pv, pv, pv, pv, pv, conv_dw, conv_bias.reshape(1, C), ln_gain.reshape(1, C), ln_bias.reshape(1, C))
    return out.reshape(B * S, C)


def _gateproj_kernel(cn_ref, am_ref, wc_ref, wa_ref, gc_ref, ga_ref, o_ref):
    conv = jnp.dot(cn_ref[...], wc_ref[...], preferred_element_type=F32)
    z = _sigmoid(gc_ref[...].astype(F32)) * conv
    attn = jnp.dot(am_ref[...], wa_ref[...], preferred_element_type=F32)
    z += _sigmoid(ga_ref[...].astype(F32)) * attn
    o_ref[...] = z.astype(o_ref.dtype)


def gate_proj(cn, am, wc, wa, proj, U):
    N, C = cn.shape
    D = wc.shape[1]
    tn = U
    tm = min(512, N)
    return pl.pallas_call(
        _gateproj_kernel,
        grid=(D // tn, N // tm),
        in_specs=[pl.BlockSpec((tm, C), lambda j, i: (i, 0)),
                  pl.BlockSpec((tm, U), lambda j, i: (i, 0)),
                  pl.BlockSpec((C, tn), lambda j, i: (0, j)),
                  pl.BlockSpec((U, tn), lambda j, i: (0, j)),
                  pl.BlockSpec((tm, tn), lambda j, i: (i, 4 + j)),
                  pl.BlockSpec((tm, tn), lambda j, i: (i, 8 + j))],
        out_specs=pl.BlockSpec((tm, tn), lambda j, i: (i, j)),
        out_shape=jax.ShapeDtypeStruct((N, D), BF16),
        compiler_params=_cparams(("parallel", "parallel")),
        name="gate_proj",
    )(cn, am, wc, wa, proj, proj)


def _router_kernel(y_ref, xin_ref, mod_ref, gpost_ref, gain_ref, r_ref,
                   x1_ref, hp_ref, eid_ref, wt_ref, rank_ref, cnt_ref, carry_ref, *, tr):
    step = pl.program_id(0)

    @pl.when(step == 0)
    def _():
        carry_ref[...] = jnp.zeros_like(carry_ref)

    y = y_ref[...].astype(F32)
    yms = jnp.mean(y * y, axis=-1, keepdims=True)
    x = xin_ref[...] + mod_ref[2:3, :] * (y * lax.rsqrt(yms + NORM_EPS) * gpost_ref[...])
    x1_ref[...] = x
    D = x.shape[1]
    ms = jnp.mean(x * x, axis=-1, keepdims=True)
    h = x * lax.rsqrt(ms + NORM_EPS) * gain_ref[...]
    h = h * (1.0 + mod_ref[4:5, :]) + mod_ref[3:4, :]
    lo = h[:, :D // 2]
    hi = h[:, D // 2:]
    hp_ref[...] = _pack_halves(lo, hi)
    logits = (jnp.dot(lo.astype(BF16), r_ref[:D // 2, :], preferred_element_type=F32)
              + jnp.dot(hi.astype(BF16), r_ref[D // 2:, :], preferred_element_type=F32))
    lane = lax.broadcasted_iota(I32, logits.shape, 1)
    G = N_EXPERT_GROUPS
    is_g = lane < G
    gl = jnp.where(is_g, logits, NEG_BIG)
    gmax = jnp.max(gl, axis=-1, keepdims=True)
    grp = jnp.min(jnp.where(gl == gmax, lane, LANES), axis=-1, keepdims=True)
    p_grp = 1.0 / jnp.sum(jnp.where(is_g, jnp.exp(gl - gmax), 0.0), axis=-1, keepdims=True)
    in_grp = (lane >= G) & (lane < G + N_EXPERTS) & (((lane - G) // EXPERTS_PER_GROUP) == grp)
    el = jnp.where(in_grp, logits, NEG_BIG)
    v0 = jnp.max(el, axis=-1, keepdims=True)
    i0 = jnp.min(jnp.where(in_grp & (el == v0), lane, LANES), axis=-1, keepdims=True)
    in2 = in_grp & (lane != i0)
    el2 = jnp.where(in2, logits, NEG_BIG)
    v1 = jnp.max(el2, axis=-1, keepdims=True)
    i1 = jnp.min(jnp.where(in2 & (el2 == v1), lane, LANES), axis=-1, keepdims=True)
    e1 = jnp.exp(v1 - v0)
    w0 = p_grp / (1.0 + e1)
    w1 = p_grp * e1 / (1.0 + e1)
    ex0 = i0 - G
    ex1 = i1 - G
    eid_ref[...] = jnp.where(lane == 0, ex0, jnp.where(lane == 1, ex1, 0))
    wt_ref[...] = jnp.where(lane == 0, w0, jnp.where(lane == 1, w1, 0.0))
    oh0 = (lane == ex0).astype(F32)
    oh1 = (lane == ex1).astype(F32)
    both = oh0 + oh1
    rr = lax.broadcasted_iota(I32, (tr, tr), 0)
    cc = lax.broadcasted_iota(I32, (tr, tr), 1)
    tril = (cc < rr).astype(BF16)
    before = jnp.dot(tril, both.astype(BF16), preferred_element_type=F32) + carry_ref[0:1, :]
    rk0 = jnp.sum(before * oh0, axis=-1, keepdims=True)
    rk1 = jnp.sum(before * oh1, axis=-1, keepdims=True)
    rank_ref[...] = jnp.where(lane == 0, rk0, jnp.where(lane == 1, rk1, 0.0))
    newc = carry_ref[0:1, :] + jnp.sum(both, axis=0, keepdims=True)
    carry_ref[...] = jnp.broadcast_to(newc, carry_ref.shape)
    cnt_ref[...] = jnp.broadcast_to(newc, cnt_ref.shape)


def residual_prenorm_router(y, x2d, mod, gain_post, gain, rcat, S):
    N, D = x2d.shape
    tr = min(256, S)
    tiles_per_b = S // tr
    lane_spec = pl.BlockSpec((tr, LANES), lambda i: (i, 0))
    row_spec = pl.BlockSpec((tr, D), lambda i: (i, 0))
    vec_spec = pl.BlockSpec((1, D), lambda i: (0, 0))
    return pl.pallas_call(
        functools.partial(_router_kernel, tr=tr),
        grid=(N // tr,),
        in_specs=[row_spec, row_spec,
                  pl.BlockSpec((None, 6, D), lambda i: (i // tiles_per_b, 0, 0)),
                  vec_spec, vec_spec,
                  pl.BlockSpec((D, LANES), lambda i: (0, 0))],
        out_specs=[row_spec, pl.BlockSpec((tr, D // 2), lambda i: (i, 0)), lane_spec, lane_spec, lane_spec,
                   pl.BlockSpec((8, LANES), lambda i: (0, 0))],
        out_shape=[jax.ShapeDtypeStruct((N, D), F32),
                   jax.ShapeDtypeStruct((N, D // 2), U32),
                   jax.ShapeDtypeStruct((N, LANES), I32),
                   jax.ShapeDtypeStruct((N, LANES), F32),
                   jax.ShapeDtypeStruct((N, LANES), F32),
                   jax.ShapeDtypeStruct((8, LANES), F32)],
        scratch_shapes=[pltpu.VMEM((8, LANES), F32)],
        compiler_params=_cparams(("arbitrary",)),
        name="residual_prenorm_router",
    )(y, x2d, mod, gain_post.reshape(1, D), gain.reshape(1, D), rcat)


def _dest_kernel(eid_ref, rank_ref, start_ref, o_ref):
    lane = lax.broadcasted_iota(I32, eid_ref.shape, 1)
    eid = eid_ref[...]
    rank = rank_ref[...]
    start = start_ref[0:1, :]
    d = []
    for k in range(2):
        oh = (lane == eid[:, k:k + 1]).astype(F32)
        d.append(jnp.sum(oh * start, axis=-1, keepdims=True) + rank[:, k:k + 1])
    o_ref[...] = jnp.where(lane == 0, d[0], jnp.where(lane == 1, d[1], 0.0)).astype(I32)


def dest_rows(eid, rank, pad_start):
    N = eid.shape[0]
    tr = min(1024, N)
    spec = pl.BlockSpec((tr, LANES), lambda i: (i, 0))
    return pl.pallas_call(
        _dest_kernel,
        grid=(N // tr,),
        in_specs=[spec, spec, pl.BlockSpec((8, LANES), lambda i: (0, 0))],
        out_specs=spec,
        out_shape=jax.ShapeDtypeStruct((N, LANES), I32),
        compiler_params=_cparams(("parallel",)),
        name="dest_rows",
    )(eid, rank, pad_start)


def _scatter_kernel(dest_ref, h_ref, xs_in_ref, xs_ref, sem, *, T):
    del xs_in_ref

    def issue(t, carry):
        for k in range(2):
            d = dest_ref[0, 0, 2 * t + k]
            pltpu.make_async_copy(h_ref.at[pl.ds(t, 1)], xs_ref.at[pl.ds(d, 1)], sem).start()
        return carry

    lax.fori_loop(0, T, issue, 0, unroll=4)
    for k in range(2):
        pltpu.make_async_copy(h_ref, xs_ref.at[pl.ds(0, T)], sem).wait()


def scatter_rows(hp, dest, P):
    N, W = hp.shape
    T = min(256, N)
    dest_s = dest[:, :2].reshape(N // T, 1, 2 * T)
    xs0 = jnp.zeros((P, W), hp.dtype)
    return pl.pallas_call(
        functools.partial(_scatter_kernel, T=T),
        grid=(N // T,),
        in_specs=[pl.BlockSpec((1, 1, 2 * T), lambda i: (i, 0, 0), memory_space=pltpu.SMEM),
                  pl.BlockSpec((T, W), lambda i: (i, 0)),
                  pl.BlockSpec(memory_space=pl.ANY)],
        out_specs=pl.BlockSpec(memory_space=pl.ANY),
        out_shape=jax.ShapeDtypeStruct((P, W), hp.dtype),
        scratch_shapes=[pltpu.SemaphoreType.DMA(())],
        input_output_aliases={2: 0},
        compiler_params=_cparams(("arbitrary",)),
        name="scatter_rows",
    )(dest_s, hp, xs0)


CHUNK_DMA_PRIORITY = 1


def _expert_chunks(st_ref, nc_ref, in_hbm, out_hbm, ibuf, obuf, isem, osem, compute):
    e = pl.program_id(0)
    n_e = pl.num_programs(0)
    n = nc_ref[e]
    TM = ibuf.shape[1]
    base = pl.multiple_of(st_ref[e], TM)

    def icopy(row, slot):
        return pltpu.make_async_copy(in_hbm.at[pl.ds(row, TM)], ibuf.at[slot], isem.at[slot])

    def ocopy(row, slot):
        return pltpu.make_async_copy(obuf.at[slot], out_hbm.at[pl.ds(row, TM)], osem.at[slot])

    @pl.when((e == 0) & (n > 0))
    def _():
        icopy(base, 0).start(priority=CHUNK_DMA_PRIORITY)

    def body(c, carry):
        slot = c % 2
        row = pl.multiple_of(base + c * TM, TM)

        @pl.when(c + 1 < n)
        def _():
            icopy(row + TM, 1 - slot).start(priority=CHUNK_DMA_PRIORITY)

        icopy(row, slot).wait()

        @pl.when(c >= 2)
        def _():
            ocopy(row, slot).wait()

        compute(ibuf.at[slot], obuf.at[slot])
        ocopy(row, slot).start(priority=CHUNK_DMA_PRIORITY)
        return carry

    lax.fori_loop(0, n, body, 0)

    @pl.when(n >= 2)
    def _():
        ocopy(base, n % 2).wait()

    @pl.when(n >= 1)
    def _():
        ocopy(base, (n + 1) % 2).wait()

    @pl.when(e + 1 < n_e)
    def _():
        @pl.when(nc_ref[e + 1] > 0)
        def _():
            icopy(pl.multiple_of(st_ref[e + 1], TM), 0).start(priority=CHUNK_DMA_PRIORITY)

    @pl.when(e == n_e - 1)
    def _():
        used = pl.multiple_of(base + n * TM, TM)
        n_tail = (out_hbm.shape[0] - used) // TM
        obuf[0] = jnp.zeros(obuf.shape[1:], obuf.dtype)

        def zstart(c, carry):
            ocopy(pl.multiple_of(used + c * TM, TM), 0).start(priority=CHUNK_DMA_PRIORITY)
            return carry

        def zwait(c, carry):
            ocopy(used, 0).wait()
            return carry

        lax.fori_loop(0, n_tail, zstart, 0)
        lax.fori_loop(0, n_tail, zwait, 0)


def _moe_up_kernel(st_ref, nc_ref, xs_hbm, w1_ref, w3_ref, o_hbm, wb_ref, ibuf, obuf, isem, osem, *, F):
    @pl.when(nc_ref[pl.program_id(0)] > 0)
    def _():
        wb_ref[:, :F] = w1_ref[...].astype(BF16)
        wb_ref[:, F:] = w3_ref[...].astype(BF16)

    def compute(x_ref, o_ref):
        lo, hi = _unpack_halves(x_ref[...])
        half = wb_ref.shape[0] // 2
        hcat = (jnp.dot(lo.astype(BF16), wb_ref[:half, :], preferred_element_type=F32)
                + jnp.dot(hi.astype(BF16), wb_ref[half:, :], preferred_element_type=F32))
        a = hcat[:, :F]
        o_ref[...] = (a * _sigmoid(a) * hcat[:, F:]).astype(o_ref.dtype)

    _expert_chunks(st_ref, nc_ref, xs_hbm, o_hbm, ibuf, obuf, isem, osem, compute)


def _moe_down_kernel(st_ref, nc_ref, h_hbm, w2_ref, o_hbm, wb_ref, ibuf, obuf, isem, osem):
    @pl.when(nc_ref[pl.program_id(0)] > 0)
    def _():
        wb_ref[...] = w2_ref[...].astype(BF16)

    def compute(h_ref, o_ref):
        y = jnp.dot(h_ref[...], wb_ref[...], preferred_element_type=F32)
        half = y.shape[1] // 2
        o_ref[...] = _pack_halves(y[:, :half], y[:, half:])

    _expert_chunks(st_ref, nc_ref, h_hbm, o_hbm, ibuf, obuf, isem, osem, compute)


def expert_ffn(xs, seg_start, seg_chunks, w1, w3, w2):
    P, Wp = xs.shape
    E, D, F = w1.shape
    TM = MOE_ROWS
    any_spec = pl.BlockSpec(memory_space=pl.ANY)
    dma2 = pltpu.SemaphoreType.DMA((2,))
    hmid = pl.pallas_call(
        functools.partial(_moe_up_kernel, F=F),
        grid_spec=pltpu.PrefetchScalarGridSpec(
            num_scalar_prefetch=2,
            grid=(E,),
            in_specs=[any_spec,
                      pl.BlockSpec((None, D, F), lambda e, st, nc: (e, 0, 0)),
                      pl.BlockSpec((None, D, F), lambda e, st, nc: (e, 0, 0))],
            out_specs=any_spec,
            scratch_shapes=[pltpu.VMEM((D, 2 * F), BF16), pltpu.VMEM((2, TM, Wp), U32),
                            pltpu.VMEM((2, TM, F), BF16), dma2, dma2]),
        out_shape=jax.ShapeDtypeStruct((P, F), BF16),
        compiler_params=_cparams(("arbitrary",)),
        name="moe_up",
    )(seg_start, seg_chunks, xs, w1, w3)
    return pl.pallas_call(
        _moe_down_kernel,
        grid_spec=pltpu.PrefetchScalarGridSpec(
            num_scalar_prefetch=2,
            grid=(E,),
            in_specs=[any_spec,
                      pl.BlockSpec((None, F, D), lambda e, st, nc: (e, 0, 0))],
            out_specs=any_spec,
            scratch_shapes=[pltpu.VMEM((F, D), BF16), pltpu.VMEM((2, TM, F), BF16),
                            pltpu.VMEM((2, TM, D // 2), U32), dma2, dma2]),
        out_shape=jax.ShapeDtypeStruct((P, D // 2), U32),
        compiler_params=_cparams(("arbitrary",)),
        name="moe_down",
    )(seg_start, seg_chunks, hmid, w2)


def _combine_kernel(pos_ref, nxt_ref, wt_ref, x_ref, mod_ref, gain_ref, y_hbm, o_ref, buf_ref, sem, *, T):
    i = pl.program_id(0)
    n = pl.num_programs(0)
    slot = i % 2

    def gather(p_ref, s):
        def issue(t, carry):
            for k in range(2):
                p = p_ref[0, 0, 2 * t + k]
                pltpu.make_async_copy(y_hbm.at[pl.ds(p, 1)], buf_ref.at[s, k, pl.ds(t, 1)], sem.at[s]).start()
            return carry

        lax.fori_loop(0, T, issue, 0, unroll=4)

    @pl.when(i == 0)
    def _():
        gather(pos_ref, 0)

    @pl.when(i + 1 < n)
    def _():
        gather(nxt_ref, 1 - slot)

    for k in range(2):
        pltpu.make_async_copy(y_hbm.at[pl.ds(0, T)], buf_ref.at[slot, k], sem.at[slot]).wait()

    wt = wt_ref[...]
    w0 = wt[:, 0:1]
    w1 = wt[:, 1:2]
    lo0, hi0 = _unpack_halves(buf_ref[slot, 0])
    lo1, hi1 = _unpack_halves(buf_ref[slot, 1])
    ylo = w0 * lo0 + w1 * lo1
    yhi = w0 * hi0 + w1 * hi1
    D = x_ref.shape[1]
    half = D // 2
    ms = (jnp.sum(ylo * ylo, axis=-1, keepdims=True) + jnp.sum(yhi * yhi, axis=-1, keepdims=True)) / D
    inv = lax.rsqrt(ms + NORM_EPS)
    o_ref[:, :half] = x_ref[:, :half] + mod_ref[5:6, :half] * (ylo * inv * gain_ref[:, :half])
    o_ref[:, half:] = x_ref[:, half:] + mod_ref[5:6, half:] * (yhi * inv * gain_ref[:, half:])


def combine(yp, dest, wts, x1, mod, gain, S):
    N, D = x1.shape
    T = min(128, S)
    tiles_per_b = S // T
    pos_s = dest[:, :2].reshape(N // T, 1, 2 * T)
    n_tiles = N // T
    return pl.pallas_call(
        functools.partial(_combine_kernel, T=T),
        grid=(n_tiles,),
        in_specs=[pl.BlockSpec((1, 1, 2 * T), lambda i: (i, 0, 0), memory_space=pltpu.SMEM),
                  pl.BlockSpec((1, 1, 2 * T), lambda i: (jnp.minimum(i + 1, n_tiles - 1), 0, 0),
                               memory_space=pltpu.SMEM),
                  pl.BlockSpec((T, LANES), lambda i: (i, 0)),
                  pl.BlockSpec((T, D), lambda i: (i, 0)),
                  pl.BlockSpec((None, 6, D), lambda i: (i // tiles_per_b, 0, 0)),
                  pl.BlockSpec((1, D), lambda i: (0, 0)),
                  pl.BlockSpec(memory_space=pl.ANY)],
        out_specs=pl.BlockSpec((T, D), lambda i: (i, 0)),
        out_shape=jax.ShapeDtypeStruct((N, D), F32),
        scratch_shapes=[pltpu.VMEM((2, 2, T, D // 2), U32), pltpu.SemaphoreType.DMA((2,))],
        compiler_params=_cparams(("arbitrary",)),
        name="combine",
    )(pos_s, pos_s, wts, x1, mod, gain.reshape(1, D), yp)


def _moe_layout(counts):
    TM = MOE_ROWS
    chunks = (counts.astype(I32) + TM - 1) // TM
    start = (jnp.cumsum(chunks) - chunks) * TM
    return start, chunks


def kernel(x, c, positions, ada_w, ada_b, mix_norm_pre, mix_norm_post, w_in, conv_dw, conv_dw_bias, conv_ln_gain, conv_ln_bias, w_conv_out, w_attn_out, w_out, ffn_norm_pre, ffn_norm_post, router_group, router_expert, expert_w1, expert_w3, expert_w2):
    B, S, D = x.shape
    N = B * S
    U = D // 4
    depth = ada_w.shape[0]
    xc = x.reshape(N, D)
    for layer in range(depth):
        mod = ada_mod(c, ada_w[layer], ada_b[layer]).reshape(B, 6, D)
        h, cosf, sinf = prenorm_rope(xc, mod, mix_norm_pre[layer], positions, S)
        w_in_b = w_in[layer].astype(BF16)
        outs, stats = [], []
        for g, (window, d) in enumerate(ATTN_PATTERNS):
            assert window // d == ATTN_BLOCK and S % (d * ATTN_BLOCK) == 0
            qkv = qkv_proj(h, w_in_b, cosf, sinf, B, S, U, g, d)
            o, st = dilated_attention(qkv, d)
            outs.append(o)
            stats.append(st)
        am = merge_groups(outs, stats, tuple(d for _, d in ATTN_PATTERNS))
        proj = matmul_cols(h, w_in_b, U, 9, "rest_proj")
        cn = conv_branch(proj, B, S, U, conv_dw[layer], conv_dw_bias[layer], conv_ln_gain[layer], conv_ln_bias[layer])
        z = gate_proj(cn, am, w_conv_out[layer].astype(BF16), w_attn_out[layer].astype(BF16), proj, U)
        y = matmul_cols(z, w_out[layer].astype(BF16), U, 0, "out_proj")
        rcat = jnp.zeros((D, LANES), F32)
        rcat = rcat.at[:, :N_EXPERT_GROUPS].set(router_group[layer])
        rcat = rcat.at[:, N_EXPERT_GROUPS:N_EXPERT_GROUPS + N_EXPERTS].set(router_expert[layer]).astype(BF16)
        x1, hp, eid, wts, rank, cnt = residual_prenorm_router(
            y, xc, mod, mix_norm_post[layer], ffn_norm_pre[layer], rcat, S)
        TM = MOE_ROWS
        P = (2 * N + N_EXPERTS * (TM - 1)) // TM * TM
        seg_start, seg_chunks = _moe_layout(cnt[0, :N_EXPERTS])
        start_row = jnp.zeros((8, LANES), F32).at[:, :N_EXPERTS].set(seg_start.astype(F32)[None, :])
        dest = dest_rows(eid, rank, start_row)
        xs = scatter_rows(hp, dest, P)
        yp = expert_ffn(xs, seg_start, seg_chunks, expert_w1[layer], expert_w3[layer], expert_w2[layer])
        xc = combine(yp, dest, wts, x1, mod, ffn_norm_post[layer], S)
    return xc.reshape(B, S, D)
```

```python
import functools

import jax
import jax.numpy as jnp
from jax import lax
from jax.experimental import pallas as pl
from jax.experimental.pallas import tpu as pltpu

F32 = jnp.float32
BF16 = jnp.bfloat16
I32 = jnp.int32
U32 = jnp.uint32

HEAD_DIM = 128
LANES = 128
ATTN_BLOCK = 128
ATTN_PATTERNS = ((128, 1), (512, 4), (2048, 16))
ROPE_THETA = 10000.0
CONV_WIDTH = 31
CONV_HALO = 32
N_EXPERT_GROUPS = 8
EXPERTS_PER_GROUP = 8
N_EXPERTS = 64
NORM_EPS = 1e-6
NEG_BIG = -1e30
MOE_ROWS = 256
V7X_VMEM_LIMIT = 60 * 1024 * 1024


def _cparams(sem):
    return pltpu.CompilerParams(dimension_semantics=sem, vmem_limit_bytes=V7X_VMEM_LIMIT)


def _sigmoid(x):
    return 1.0 / (1.0 + jnp.exp(-x))


def _pack_halves(lo, hi):
    lo_b = lax.bitcast_convert_type(lo.astype(BF16).astype(F32), U32) >> 16
    hi_b = lax.bitcast_convert_type(hi.astype(BF16).astype(F32), U32) & jnp.uint32(0xFFFF0000)
    return hi_b | lo_b


def _unpack_halves(w):
    lo = lax.bitcast_convert_type(w << 16, F32)
    hi = lax.bitcast_convert_type(w & jnp.uint32(0xFFFF0000), F32)
    return lo, hi


def _ada_kernel(c_ref, w_ref, b_ref, o_ref):
    c = c_ref[...]
    cact = (c * _sigmoid(c)).astype(BF16)
    o_ref[...] = jnp.dot(cact, w_ref[...].astype(BF16), preferred_element_type=F32) + b_ref[...]


def ada_mod(c, ada_w, ada_b):
    B, D = c.shape
    W = ada_w.shape[1]
    rows = 8
    cp = jnp.zeros((rows, D), F32).at[:B].set(c)
    tn = min(512, W)
    out = pl.pallas_call(
        _ada_kernel,
        grid=(W // tn,),
        in_specs=[pl.BlockSpec((rows, D), lambda j: (0, 0)),
                  pl.BlockSpec((D, tn), lambda j: (0, j)),
                  pl.BlockSpec((1, tn), lambda j: (0, j))],
        out_specs=pl.BlockSpec((rows, tn), lambda j: (0, j)),
        out_shape=jax.ShapeDtypeStruct((rows, W), F32),
        compiler_params=_cparams(("parallel",)),
        name="ada_mod",
    )(cp, ada_w, ada_b.reshape(1, W))
    return out[:B]


def _prenorm_kernel(x_ref, mod_ref, gain_ref, pos_ref, invf_ref, h_ref, cos_ref, sin_ref):
    x = x_ref[...]
    ms = jnp.mean(x * x, axis=-1, keepdims=True)
    y = x * lax.rsqrt(ms + NORM_EPS) * gain_ref[...]
    y = y * (1.0 + mod_ref[1:2, :]) + mod_ref[0:1, :]
    h_ref[...] = y.astype(BF16)
    ang = pos_ref[...].astype(F32) * invf_ref[...]
    lane = lax.broadcasted_iota(I32, ang.shape, 1)
    sn = jnp.sin(ang)
    cos_ref[...] = jnp.cos(ang)
    sin_ref[...] = jnp.where(lane < HEAD_DIM // 2, -sn, sn)


def prenorm_rope(x2d, mod, gain, positions, S):
    N, D = x2d.shape
    tr = min(256, S)
    tiles_per_b = S // tr
    half = HEAD_DIM // 2
    inv = ROPE_THETA ** (-jnp.arange(0, HEAD_DIM, 2, dtype=F32) / HEAD_DIM)
    invf = jnp.concatenate([inv, inv]).reshape(1, HEAD_DIM)
    del half
    return pl.pallas_call(
        _prenorm_kernel,
        grid=(N // tr,),
        in_specs=[pl.BlockSpec((tr, D), lambda i: (i, 0)),
                  pl.BlockSpec((None, 6, D), lambda i: (i // tiles_per_b, 0, 0)),
                  pl.BlockSpec((1, D), lambda i: (0, 0)),
                  pl.BlockSpec((tr, 1), lambda i: (i, 0)),
                  pl.BlockSpec((1, HEAD_DIM), lambda i: (0, 0))],
        out_specs=[pl.BlockSpec((tr, D), lambda i: (i, 0)),
                   pl.BlockSpec((tr, HEAD_DIM), lambda i: (i, 0)),
                   pl.BlockSpec((tr, HEAD_DIM), lambda i: (i, 0))],
        out_shape=[jax.ShapeDtypeStruct((N, D), BF16),
                   jax.ShapeDtypeStruct((N, HEAD_DIM), F32),
                   jax.ShapeDtypeStruct((N, HEAD_DIM), F32)],
        compiler_params=_cparams(("parallel",)),
        name="prenorm_rope",
    )(x2d, mod, gain.reshape(1, D), positions.reshape(N, 1), invf)


def _residue_rows(ref, r, d):
    if d == 1:
        return ref[...]
    return ref[pl.ds(r, ref.shape[0] // d, stride=d), :]


def _qkv_kernel(a_ref, w_ref, cos_ref, sin_ref, o_ref, acc_ref, *, d, heads, scale, n_i):
    s = pl.program_id(0)

    @pl.when(s == 0)
    def _():
        acc_ref[...] = jnp.zeros_like(acc_ref)

    kind = jnp.maximum(s - 1, 0) // n_i
    sc = jnp.where(kind == 0, scale, 1.0).astype(F32)
    is_v = kind == 2
    for r in range(d):
        c = jnp.where(is_v, 1.0, _residue_rows(cos_ref, r, d) * sc)
        sn = jnp.where(is_v, 0.0, _residue_rows(sin_ref, r, d) * sc)
        for h in range(heads):
            t = _residue_rows(acc_ref.at[h], r, d)
            o_ref[r, :, h * HEAD_DIM:(h + 1) * HEAD_DIM] = (
                t * c + pltpu.roll(t, HEAD_DIM // 2, 1) * sn).astype(o_ref.dtype)
    acc = jnp.dot(a_ref[...], w_ref[...], preferred_element_type=F32)
    for h in range(heads):
        acc_ref[h] = acc[:, h * HEAD_DIM:(h + 1) * HEAD_DIM]


def qkv_proj(h, w_bf16, cosf, sinf, B, S, U, g, d):
    N, D = h.shape
    tm = min(1024, S)
    tiles_per_b = S // tm
    n_i = N // tm
    last = 3 * n_i - 1
    kern = functools.partial(_qkv_kernel, d=d, heads=U // HEAD_DIM, scale=HEAD_DIM ** -0.5, n_i=n_i)

    def cur(s):
        return jnp.minimum(s, last)

    def fin(s):
        return jnp.maximum(s - 1, 0)

    return pl.pallas_call(
        kern,
        grid=(3 * n_i + 1,),
        in_specs=[pl.BlockSpec((tm, D), lambda s: (cur(s) % n_i, 0)),
                  pl.BlockSpec((D, U), lambda s: (0, (cur(s) // n_i) * 3 + g)),
                  pl.BlockSpec((tm, HEAD_DIM), lambda s: (fin(s) % n_i, 0)),
                  pl.BlockSpec((tm, HEAD_DIM), lambda s: (fin(s) % n_i, 0))],
        out_specs=pl.BlockSpec((None, None, d, tm // d, U),
                               lambda s: (fin(s) // n_i, (fin(s) % n_i) // tiles_per_b, 0,
                                          (fin(s) % n_i) % tiles_per_b, 0)),
        out_shape=jax.ShapeDtypeStruct((3, B, d, S // d, U), BF16),
        scratch_shapes=[pltpu.VMEM((U // HEAD_DIM, tm, HEAD_DIM), F32)],
        compiler_params=_cparams(("arbitrary",)),
        name=f"qkv_proj_d{d}",
    )(h, w_bf16, cosf, sinf)


def _matmul_kernel(a_ref, w_ref, o_ref):
    o_ref[...] = jnp.dot(a_ref[...], w_ref[...], preferred_element_type=F32).astype(o_ref.dtype)


def matmul_cols(a, w_bf16, U, first_tile, name):
    N, K = a.shape
    n_tiles = w_bf16.shape[1] // U - first_tile
    tm = min(1024, N)
    return pl.pallas_call(
        _matmul_kernel,
        grid=(n_tiles, N // tm),
        in_specs=[pl.BlockSpec((tm, K), lambda j, i: (i, 0)),
                  pl.BlockSpec((K, U), lambda j, i: (0, first_tile + j))],
        out_specs=pl.BlockSpec((tm, U), lambda j, i: (i, j)),
        out_shape=jax.ShapeDtypeStruct((N, n_tiles * U), BF16),
        compiler_params=_cparams(("parallel", "parallel")),
        name=name,
    )(a, w_bf16)


def _attn_kernel(q_ref, kc_ref, kp_ref, vc_ref, vp_ref, o_ref, st_ref, kx_ref, vx_ref, *, QB, H):
    i = pl.program_id(2)
    blk = ATTN_BLOCK
    kx_ref[0:blk, :] = kp_ref[...]
    kx_ref[blk:, :] = kc_ref[...]
    ones = jnp.ones((vx_ref.shape[0], HEAD_DIM), vx_ref.dtype)
    for h in range(H):
        hs = slice(h * HEAD_DIM, (h + 1) * HEAD_DIM)
        vx_ref[0:blk, 2 * h * HEAD_DIM:(2 * h + 1) * HEAD_DIM] = vp_ref[:, hs]
        vx_ref[blk:, 2 * h * HEAD_DIM:(2 * h + 1) * HEAD_DIM] = vc_ref[:, hs]
        vx_ref[:, (2 * h + 1) * HEAD_DIM:(2 * h + 2) * HEAD_DIM] = ones
    rows = lax.broadcasted_iota(I32, (blk, 2 * blk), 0)
    keys = lax.broadcasted_iota(I32, (blk, 2 * blk), 1)
    band = (keys >= rows) & (keys - blk <= rows)
    lane = lax.broadcasted_iota(I32, (blk, LANES), 1)
    dn = (((1,), (1,)), ((), ()))

    def body(qb, carry):
        r0 = pl.multiple_of(qb * blk, blk)
        ok = band & ((keys >= blk) | (i * QB + qb > 0))
        q = [q_ref[pl.ds(r0, blk), h * HEAD_DIM:(h + 1) * HEAD_DIM] for h in range(H)]
        k = [kx_ref[pl.ds(r0, 2 * blk), h * HEAD_DIM:(h + 1) * HEAD_DIM] for h in range(H)]
        s = [jnp.where(ok, lax.dot_general(q[h], k[h], dn, preferred_element_type=F32), NEG_BIG) for h in range(H)]
        m = [jnp.max(s[h], axis=-1, keepdims=True) for h in range(H)]
        p = [jnp.exp(s[h] - m[h]).astype(BF16) for h in range(H)]
        st = jnp.zeros((blk, LANES), F32)
        for h in range(H):
            v1 = vx_ref[pl.ds(r0, 2 * blk), 2 * h * HEAD_DIM:(2 * h + 2) * HEAD_DIM]
            acc = jnp.dot(p[h], v1, preferred_element_type=F32)
            l = acc[:, HEAD_DIM:]
            o_ref[pl.ds(r0, blk), h * HEAD_DIM:(h + 1) * HEAD_DIM] = (acc[:, :HEAD_DIM] / l).astype(o_ref.dtype)
            st = jnp.where(lane == h, m[h], st)
            st = jnp.where(lane == H + h, l, st)
        st_ref[pl.ds(r0, blk), :] = st
        return carry

    lax.fori_loop(0, QB, body, 0)


def dilated_attention(qkv, d):
    _, B, _, L, U = qkv.shape
    H = U // HEAD_DIM
    R = min(512, L)
    QB = R // ATTN_BLOCK

    def cur(kind):
        return pl.BlockSpec((None, None, None, R, U), lambda b, r, i: (kind, b, r, i, 0))

    def prev(kind):
        return pl.BlockSpec((None, None, None, ATTN_BLOCK, U),
                            lambda b, r, i: (kind, b, r, jnp.maximum(i * QB - 1, 0), 0))

    kern = functools.partial(_attn_kernel, QB=QB, H=H)
    return pl.pallas_call(
        kern,
        grid=(B, d, L // R),
        in_specs=[cur(0), cur(1), prev(1), cur(2), prev(2)],
        out_specs=[pl.BlockSpec((None, None, R, U), lambda b, r, i: (b, r, i, 0)),
                   pl.BlockSpec((None, None, R, LANES), lambda b, r, i: (b, r, i, 0))],
        out_shape=[jax.ShapeDtypeStruct((B, d, L, U), BF16),
                   jax.ShapeDtypeStruct((B, d, L, LANES), F32)],
        scratch_shapes=[pltpu.VMEM((R + ATTN_BLOCK, U), BF16), pltpu.VMEM((R + ATTN_BLOCK, 2 * U), BF16)],
        compiler_params=_cparams(("parallel", "parallel", "parallel")),
        name=f"dilated_attn_d{d}",
    )(qkv, qkv, qkv, qkv, qkv)


def _merge_kernel(o0_ref, o1_ref, o2_ref, s0_ref, s1_ref, s2_ref, out_ref, of_ref, sf_ref, *, H, dils):
    for g, (o_ref, s_ref) in enumerate(((o0_ref, s0_ref), (o1_ref, s1_ref), (o2_ref, s2_ref))):
        d = dils[g]
        n = sf_ref.shape[1] // d
        for r in range(d):
            rows = slice(None) if d == 1 else pl.ds(r, n, stride=d)
            sf_ref[g, rows, :] = s_ref[r]
            for h in range(H):
                of_ref[g, h, rows, :] = o_ref[r, :, h * HEAD_DIM:(h + 1) * HEAD_DIM].astype(F32)
    st = [sf_ref[0], sf_ref[1], sf_ref[2]]
    mx = jnp.maximum(jnp.maximum(st[0], st[1]), st[2])
    w = [pltpu.roll(s, LANES - H, 1) * jnp.exp(s - mx) for s in st]
    tot = w[0] + w[1] + w[2]
    coef = [x / tot for x in w]
    for h in range(H):
        hs = slice(h * HEAD_DIM, (h + 1) * HEAD_DIM)
        acc = coef[0][:, h:h + 1] * of_ref[0, h]
        acc += coef[1][:, h:h + 1] * of_ref[1, h]
        acc += coef[2][:, h:h + 1] * of_ref[2, h]
        out_ref[:, hs] = acc.astype(out_ref.dtype)


def merge_groups(outs, stats, dils):
    B, d0, L0, U = outs[0].shape
    S = d0 * L0
    H = U // HEAD_DIM
    tm = min(512, S)
    tiles_per_b = S // tm

    def ospec(d, w):
        return pl.BlockSpec((None, d, tm // d, w), lambda i: (i // tiles_per_b, 0, i % tiles_per_b, 0))

    return pl.pallas_call(
        functools.partial(_merge_kernel, H=H, dils=dils),
        grid=(B * S // tm,),
        in_specs=[ospec(d, U) for d in dils] + [ospec(d, LANES) for d in dils],
        out_specs=pl.BlockSpec((tm, U), lambda i: (i, 0)),
        out_shape=jax.ShapeDtypeStruct((B * S, U), BF16),
        scratch_shapes=[pltpu.VMEM((3, H, tm, HEAD_DIM), F32), pltpu.VMEM((3, tm, LANES), F32)],
        compiler_params=_cparams(("parallel",)),
        name="merge_groups",
    )(*outs, *stats)


def _conv_gate_kernel(a0_ref, a1_ref, b0_ref, b1_ref, ha0_ref, ha1_ref, hb0_ref, hb1_ref,
                      w_ref, bias_ref, g_ref, be_ref, am_ref, gc_ref, ga_ref, wc_hbm, wa_hbm,
                      o_ref, u_ref, c_ref, sh_ref, cn_ref, z_ref, wc_ref, wa_ref, sem,
                      *, ts, U, n_t, tiles_per_b):
    s = pl.program_id(0)

    @pl.when(s == 0)
    def _():
        cw = pltpu.make_async_copy(wc_hbm, wc_ref, sem.at[0])
        ca = pltpu.make_async_copy(wa_hbm, wa_ref, sem.at[1])
        cw.start()
        ca.start()
        cn_ref[...] = jnp.zeros_like(cn_ref)
        cw.wait()
        ca.wait()

    i = jnp.minimum(s, n_t - 1) % tiles_per_b
    halo = CONV_HALO
    for half, (a_ref, b_ref, ha_ref, hb_ref) in enumerate(((a0_ref, b0_ref, ha0_ref, hb0_ref),
                                                            (a1_ref, b1_ref, ha1_ref, hb1_ref))):
        cs = slice(half * U, (half + 1) * U)
        u_ref[halo:halo + ts, cs] = a_ref[...].astype(F32) * _sigmoid(b_ref[...].astype(F32))
        hu = ha_ref[...].astype(F32) * _sigmoid(hb_ref[...].astype(F32))
        u_ref[0:halo, cs] = jnp.where(i > 0, hu, 0.0)
    C = 2 * U
    rc = 64
    off = halo - (CONV_WIDTH - 1)
    D = z_ref.shape[1]
    n_cc = C // LANES
    wcol = D // n_cc

    sub = 8
    n_al = ts + halo - sub

    def chan_body(cc, carry):
        w0 = pl.multiple_of(cc * wcol, wcol)
        conv_o = jnp.dot(cn_ref[...], wc_ref[:, pl.ds(w0, wcol)], preferred_element_type=F32)
        z_ref[:, pl.ds(w0, wcol)] = _sigmoid(gc_ref[:, pl.ds(w0, wcol)].astype(F32)) * conv_o
        c0 = pl.multiple_of(cc * LANES, LANES)
        sh_ref[0] = u_ref[:, pl.ds(c0, LANES)]
        for b in range(1, sub):
            sh_ref[b, 0:n_al, :] = u_ref[b:b + n_al, pl.ds(c0, LANES)]
        for rb in range(ts // rc):
            acc = jnp.zeros((rc, LANES), F32) + bias_ref[:, pl.ds(c0, LANES)]
            for j in range(CONV_WIDTH):
                a, b = divmod(off + j, sub)
                r0 = rb * rc + a * sub
                acc += w_ref[j:j + 1, pl.ds(c0, LANES)] * sh_ref[b, r0:r0 + rc, :]
            c_ref[rb * rc:(rb + 1) * rc, pl.ds(c0, LANES)] = acc
        return carry

    lax.fori_loop(0, n_cc, chan_body, 0, unroll=2)

    rn = 32
    n_rb = ts // rn
    acol = D // n_rb

    def norm_body(rb, carry):
        w0 = pl.multiple_of(rb * acol, acol)
        attn_o = jnp.dot(am_ref[...], wa_ref[:, pl.ds(w0, acol)], preferred_element_type=F32)
        z = z_ref[:, pl.ds(w0, acol)] + _sigmoid(ga_ref[:, pl.ds(w0, acol)].astype(F32)) * attn_o
        o_ref[:, pl.ds(w0, acol)] = z.astype(o_ref.dtype)
        r0 = pl.multiple_of(rb * rn, rn)
        v = c_ref[pl.ds(r0, rn), :]
        mu = jnp.mean(v, axis=-1, keepdims=True)
        dv = v - mu
        var = jnp.mean(dv * dv, axis=-1, keepdims=True)
        y = dv * lax.rsqrt(var + NORM_EPS) * g_ref[...] + be_ref[...]
        cn_ref[pl.ds(r0, rn), :] = (y * _sigmoid(y)).astype(cn_ref.dtype)
        return carry

    lax.fori_loop(0, n_rb, norm_body, 0, unroll=4)


def conv_gate(proj, am, wc, wa, B, S, U, conv_dw, conv_bias, ln_gain, ln_bias):
    N, IN = proj.shape
    C = 2 * U
    D = wc.shape[1]
    ts = min(256, S)
    tiles_per_b = S // ts
    n_t = N // ts
    assert D % (C // LANES * LANES) == 0 and D % (ts // 32 * LANES) == 0
    pv = proj.reshape(B, S, IN)
    hb = ts // CONV_HALO

    def conv_tile(s):
        t = jnp.minimum(s, n_t - 1)
        return t // tiles_per_b, t % tiles_per_b

    def proj_tile(s):
        return jnp.maximum(s - 1, 0)

    def cur(blk):
        return pl.BlockSpec((None, ts, U), lambda s: (*conv_tile(s), blk))

    def prv(blk):
        return pl.BlockSpec((None, CONV_HALO, U),
                            lambda s: (conv_tile(s)[0], jnp.maximum(conv_tile(s)[1] * hb - 1, 0), blk))

    vec = pl.BlockSpec((1, C), lambda s: (0, 0))
    any_spec = pl.BlockSpec(memory_space=pl.ANY)
    return pl.pallas_call(
        functools.partial(_conv_gate_kernel, ts=ts, U=U, n_t=n_t, tiles_per_b=tiles_per_b),
        grid=(n_t + 1,),
        in_specs=[cur(0), cur(1), cur(2), cur(3), prv(0), prv(1), prv(2), prv(3),
                  pl.BlockSpec((CONV_WIDTH, C), lambda s: (0, 0)), vec, vec, vec,
                  pl.BlockSpec((ts, U), lambda s: (proj_tile(s), 0)),
                  pl.BlockSpec((ts, D), lambda s: (proj_tile(s), 1)),
                  pl.BlockSpec((ts, D), lambda s: (proj_tile(s), 2)),
                  any_spec, any_spec],
        out_specs=pl.BlockSpec((ts, D), lambda s: (proj_tile(s), 0)),
        out_shape=jax.ShapeDtypeStruct((N, D), BF16),
        scratch_shapes=[pltpu.VMEM((ts + CONV_HALO, C), F32), pltpu.VMEM((ts, C), F32),
                        pltpu.VMEM((8, ts + CONV_HALO, LANES), F32), pltpu.VMEM((ts, C), BF16),
                        pltpu.VMEM((ts, D), F32), pltpu.VMEM((C, D), BF16), pltpu.VMEM((U, D), BF16),
                        pltpu.SemaphoreType.DMA((2,))],
        compiler_params=_cparams(("arbitrary",)),
        name="conv_gate",
    )(pv, pv, pv, pv, pv, pv, pv, pv, conv_dw, conv_bias.reshape(1, C), ln_gain.reshape(1, C),
      ln_bias.reshape(1, C), am, proj, proj, wc, wa)


def _router_kernel(y_ref, xin_ref, mod_ref, gpost_ref, gain_ref, r_ref,
                   x1_ref, hp_ref, eid_ref, wt_ref, rank_ref, cnt_ref, carry_ref, *, tr):
    step = pl.program_id(0)

    @pl.when(step == 0)
    def _():
        carry_ref[...] = jnp.zeros_like(carry_ref)

    y = y_ref[...].astype(F32)
    yms = jnp.mean(y * y, axis=-1, keepdims=True)
    x = xin_ref[...] + mod_ref[2:3, :] * (y * lax.rsqrt(yms + NORM_EPS) * gpost_ref[...])
    x1_ref[...] = x
    D = x.shape[1]
    ms = jnp.mean(x * x, axis=-1, keepdims=True)
    h = x * lax.rsqrt(ms + NORM_EPS) * gain_ref[...]
    h = h * (1.0 + mod_ref[4:5, :]) + mod_ref[3:4, :]
    lo = h[:, :D // 2]
    hi = h[:, D // 2:]
    hp_ref[...] = _pack_halves(lo, hi)
    logits = (jnp.dot(lo.astype(BF16), r_ref[:D // 2, :], preferred_element_type=F32)
              + jnp.dot(hi.astype(BF16), r_ref[D // 2:, :], preferred_element_type=F32))
    lane = lax.broadcasted_iota(I32, logits.shape, 1)
    G = N_EXPERT_GROUPS
    is_g = lane < G
    gl = jnp.where(is_g, logits, NEG_BIG)
    gmax = jnp.max(gl, axis=-1, keepdims=True)
    grp = jnp.min(jnp.where(gl == gmax, lane, LANES), axis=-1, keepdims=True)
    p_grp = 1.0 / jnp.sum(jnp.where(is_g, jnp.exp(gl - gmax), 0.0), axis=-1, keepdims=True)
    in_grp = (lane >= G) & (lane < G + N_EXPERTS) & (((lane - G) // EXPERTS_PER_GROUP) == grp)
    el = jnp.where(in_grp, logits, NEG_BIG)
    v0 = jnp.max(el, axis=-1, keepdims=True)
    i0 = jnp.min(jnp.where(in_grp & (el == v0), lane, LANES), axis=-1, keepdims=True)
    in2 = in_grp & (lane != i0)
    el2 = jnp.where(in2, logits, NEG_BIG)
    v1 = jnp.max(el2, axis=-1, keepdims=True)
    i1 = jnp.min(jnp.where(in2 & (el2 == v1), lane, LANES), axis=-1, keepdims=True)
    e1 = jnp.exp(v1 - v0)
    w0 = p_grp / (1.0 + e1)
    w1 = p_grp * e1 / (1.0 + e1)
    ex0 = i0 - G
    ex1 = i1 - G
    eid_ref[...] = jnp.where(lane == 0, ex0, jnp.where(lane == 1, ex1, 0))
    wt_ref[...] = jnp.where(lane == 0, w0, jnp.where(lane == 1, w1, 0.0))
    oh0 = (lane == ex0).astype(F32)
    oh1 = (lane == ex1).astype(F32)
    both = oh0 + oh1
    rr = lax.broadcasted_iota(I32, (tr, tr), 0)
    cc = lax.broadcasted_iota(I32, (tr, tr), 1)
    tril = (cc < rr).astype(BF16)
    before = jnp.dot(tril, both.astype(BF16), preferred_element_type=F32) + carry_ref[0:1, :]
    rk0 = jnp.sum(before * oh0, axis=-1, keepdims=True)
    rk1 = jnp.sum(before * oh1, axis=-1, keepdims=True)
    rank_ref[...] = jnp.where(lane == 0, rk0, jnp.where(lane == 1, rk1, 0.0))
    newc = carry_ref[0:1, :] + jnp.sum(both, axis=0, keepdims=True)
    carry_ref[...] = jnp.broadcast_to(newc, carry_ref.shape)
    cnt_ref[...] = jnp.broadcast_to(newc, cnt_ref.shape)


def residual_prenorm_router(y, x2d, mod, gain_post, gain, rcat, S):
    N, D = x2d.shape
    tr = min(256, S)
    tiles_per_b = S // tr
    lane_spec = pl.BlockSpec((tr, LANES), lambda i: (i, 0))
    row_spec = pl.BlockSpec((tr, D), lambda i: (i, 0))
    vec_spec = pl.BlockSpec((1, D), lambda i: (0, 0))
    return pl.pallas_call(
        functools.partial(_router_kernel, tr=tr),
        grid=(N // tr,),
        in_specs=[row_spec, row_spec,
                  pl.BlockSpec((None, 6, D), lambda i: (i // tiles_per_b, 0, 0)),
                  vec_spec, vec_spec,
                  pl.BlockSpec((D, LANES), lambda i: (0, 0))],
        out_specs=[row_spec, pl.BlockSpec((tr, D // 2), lambda i: (i, 0)), lane_spec, lane_spec, lane_spec,
                   pl.BlockSpec((8, LANES), lambda i: (0, 0))],
        out_shape=[jax.ShapeDtypeStruct((N, D), F32),
                   jax.ShapeDtypeStruct((N, D // 2), U32),
                   jax.ShapeDtypeStruct((N, LANES), I32),
                   jax.ShapeDtypeStruct((N, LANES), F32),
                   jax.ShapeDtypeStruct((N, LANES), F32),
                   jax.ShapeDtypeStruct((8, LANES), F32)],
        scratch_shapes=[pltpu.VMEM((8, LANES), F32)],
        compiler_params=_cparams(("arbitrary",)),
        name="residual_prenorm_router",
    )(y, x2d, mod, gain_post.reshape(1, D), gain.reshape(1, D), rcat)


def _dest_kernel(eid_ref, rank_ref, start_ref, o_ref):
    lane = lax.broadcasted_iota(I32, eid_ref.shape, 1)
    eid = eid_ref[...]
    rank = rank_ref[...]
    start = start_ref[0:1, :]
    d = []
    for k in range(2):
        oh = (lane == eid[:, k:k + 1]).astype(F32)
        d.append(jnp.sum(oh * start, axis=-1, keepdims=True) + rank[:, k:k + 1])
    o_ref[...] = jnp.where(lane == 0, d[0], jnp.where(lane == 1, d[1], 0.0)).astype(I32)


def dest_rows(eid, rank, pad_start):
    N = eid.shape[0]
    tr = min(1024, N)
    spec = pl.BlockSpec((tr, LANES), lambda i: (i, 0))
    return pl.pallas_call(
        _dest_kernel,
        grid=(N // tr,),
        in_specs=[spec, spec, pl.BlockSpec((8, LANES), lambda i: (0, 0))],
        out_specs=spec,
        out_shape=jax.ShapeDtypeStruct((N, LANES), I32),
        compiler_params=_cparams(("parallel",)),
        name="dest_rows",
    )(eid, rank, pad_start)


def _scatter_kernel(dest_ref, h_ref, xs_in_ref, xs_ref, sem, *, T):
    del xs_in_ref

    def issue(t, carry):
        for k in range(2):
            d = dest_ref[0, 0, 2 * t + k]
            pltpu.make_async_copy(h_ref.at[pl.ds(t, 1)], xs_ref.at[pl.ds(d, 1)], sem).start()
        return carry

    lax.fori_loop(0, T, issue, 0, unroll=4)
    for k in range(2):
        pltpu.make_async_copy(h_ref, xs_ref.at[pl.ds(0, T)], sem).wait()


def scatter_rows(hp, dest, P):
    N, W = hp.shape
    T = min(256, N)
    dest_s = dest[:, :2].reshape(N // T, 1, 2 * T)
    xs0 = jnp.zeros((P, W), hp.dtype)
    return pl.pallas_call(
        functools.partial(_scatter_kernel, T=T),
        grid=(N // T,),
        in_specs=[pl.BlockSpec((1, 1, 2 * T), lambda i: (i, 0, 0), memory_space=pltpu.SMEM),
                  pl.BlockSpec((T, W), lambda i: (i, 0)),
                  pl.BlockSpec(memory_space=pl.ANY)],
        out_specs=pl.BlockSpec(memory_space=pl.ANY),
        out_shape=jax.ShapeDtypeStruct((P, W), hp.dtype),
        scratch_shapes=[pltpu.SemaphoreType.DMA(())],
        input_output_aliases={2: 0},
        compiler_params=_cparams(("arbitrary",)),
        name="scatter_rows",
    )(dest_s, hp, xs0)


def _expert_chunks(st_ref, nc_ref, in_hbm, out_hbm, ibuf, obuf, isem, osem, compute):
    e = pl.program_id(0)
    n_e = pl.num_programs(0)
    n = nc_ref[e]
    TM = ibuf.shape[1]
    base = pl.multiple_of(st_ref[e], TM)

    def icopy(row, slot):
        return pltpu.make_async_copy(in_hbm.at[pl.ds(row, TM)], ibuf.at[slot], isem.at[slot])

    def ocopy(row, slot):
        return pltpu.make_async_copy(obuf.at[slot], out_hbm.at[pl.ds(row, TM)], osem.at[slot])

    @pl.when((e == 0) & (n > 0))
    def _():
        icopy(base, 0).start()

    def body(c, carry):
        slot = c % 2
        row = pl.multiple_of(base + c * TM, TM)

        @pl.when(c + 1 < n)
        def _():
            icopy(row + TM, 1 - slot).start()

        icopy(row, slot).wait()

        @pl.when(c >= 2)
        def _():
            ocopy(row, slot).wait()

        compute(ibuf.at[slot], obuf.at[slot])
        ocopy(row, slot).start()
        return carry

    lax.fori_loop(0, n, body, 0)

    @pl.when(n >= 2)
    def _():
        ocopy(base, n % 2).wait()

    @pl.when(n >= 1)
    def _():
        ocopy(base, (n + 1) % 2).wait()

    @pl.when(e + 1 < n_e)
    def _():
        @pl.when(nc_ref[e + 1] > 0)
        def _():
            icopy(pl.multiple_of(st_ref[e + 1], TM), 0).start()

    @pl.when(e == n_e - 1)
    def _():
        used = pl.multiple_of(base + n * TM, TM)
        n_tail = (out_hbm.shape[0] - used) // TM
        obuf[0] = jnp.zeros(obuf.shape[1:], obuf.dtype)

        def zstart(c, carry):
            ocopy(pl.multiple_of(used + c * TM, TM), 0).start()
            return carry

        def zwait(c, carry):
            ocopy(used, 0).wait()
            return carry

        lax.fori_loop(0, n_tail, zstart, 0)
        lax.fori_loop(0, n_tail, zwait, 0)


def _moe_up_kernel(st_ref, nc_ref, xs_hbm, w1_ref, w3_ref, o_hbm, wb_ref, ibuf, obuf, isem, osem, *, F):
    @pl.when(nc_ref[pl.program_id(0)] > 0)
    def _():
        wb_ref[:, :F] = w1_ref[...].astype(BF16)
        wb_ref[:, F:] = w3_ref[...].astype(BF16)

    def compute(x_ref, o_ref):
        lo, hi = _unpack_halves(x_ref[...])
        half = wb_ref.shape[0] // 2
        hcat = (jnp.dot(lo.astype(BF16), wb_ref[:half, :], preferred_element_type=F32)
                + jnp.dot(hi.astype(BF16), wb_ref[half:, :], preferred_element_type=F32))
        a = hcat[:, :F]
        o_ref[...] = (a * _sigmoid(a) * hcat[:, F:]).astype(o_ref.dtype)

    _expert_chunks(st_ref, nc_ref, xs_hbm, o_hbm, ibuf, obuf, isem, osem, compute)


def _moe_down_kernel(st_ref, nc_ref, h_hbm, w2_ref, o_hbm, wb_ref, ibuf, obuf, isem, osem):
    @pl.when(nc_ref[pl.program_id(0)] > 0)
    def _():
        wb_ref[...] = w2_ref[...].astype(BF16)

    def compute(h_ref, o_ref):
        y = jnp.dot(h_ref[...], wb_ref[...], preferred_element_type=F32)
        half = y.shape[1] // 2
        o_ref[...] = _pack_halves(y[:, :half], y[:, half:])

    _expert_chunks(st_ref, nc_ref, h_hbm, o_hbm, ibuf, obuf, isem, osem, compute)


def expert_ffn(xs, seg_start, seg_chunks, w1, w3, w2):
    P, Wp = xs.shape
    E, D, F = w1.shape
    TM = MOE_ROWS
    any_spec = pl.BlockSpec(memory_space=pl.ANY)
    dma2 = pltpu.SemaphoreType.DMA((2,))
    hmid = pl.pallas_call(
        functools.partial(_moe_up_kernel, F=F),
        grid_spec=pltpu.PrefetchScalarGridSpec(
            num_scalar_prefetch=2,
            grid=(E,),
            in_specs=[any_spec,
                      pl.BlockSpec((None, D, F), lambda e, st, nc: (e, 0, 0)),
                      pl.BlockSpec((None, D, F), lambda e, st, nc: (e, 0, 0))],
            out_specs=any_spec,
            scratch_shapes=[pltpu.VMEM((D, 2 * F), BF16), pltpu.VMEM((2, TM, Wp), U32),
                            pltpu.VMEM((2, TM, F), BF16), dma2, dma2]),
        out_shape=jax.ShapeDtypeStruct((P, F), BF16),
        compiler_params=_cparams(("arbitrary",)),
        name="moe_up",
    )(seg_start, seg_chunks, xs, w1, w3)
    return pl.pallas_call(
        _moe_down_kernel,
        grid_spec=pltpu.PrefetchScalarGridSpec(
            num_scalar_prefetch=2,
            grid=(E,),
            in_specs=[any_spec,
                      pl.BlockSpec((None, F, D), lambda e, st, nc: (e, 0, 0))],
            out_specs=any_spec,
            scratch_shapes=[pltpu.VMEM((F, D), BF16), pltpu.VMEM((2, TM, F), BF16),
                            pltpu.VMEM((2, TM, D // 2), U32), dma2, dma2]),
        out_shape=jax.ShapeDtypeStruct((P, D // 2), U32),
        compiler_params=_cparams(("arbitrary",)),
        name="moe_down",
    )(seg_start, seg_chunks, hmid, w2)


def _combine_kernel(pos_ref, nxt_ref, wt_ref, x_ref, mod_ref, gain_ref, y_hbm, o_ref, buf_ref, sem, *, T):
    i = pl.program_id(0)
    n = pl.num_programs(0)
    slot = i % 2

    def gather(p_ref, s):
        def issue(t, carry):
            for k in range(2):
                p = p_ref[0, 0, 2 * t + k]
                pltpu.make_async_copy(y_hbm.at[pl.ds(p, 1)], buf_ref.at[s, k, pl.ds(t, 1)], sem.at[s]).start()
            return carry

        lax.fori_loop(0, T, issue, 0, unroll=4)

    @pl.when(i == 0)
    def _():
        gather(pos_ref, 0)

    @pl.when(i + 1 < n)
    def _():
        gather(nxt_ref, 1 - slot)

    for k in range(2):
        pltpu.make_async_copy(y_hbm.at[pl.ds(0, T)], buf_ref.at[slot, k], sem.at[slot]).wait()

    wt = wt_ref[...]
    w0 = wt[:, 0:1]
    w1 = wt[:, 1:2]
    lo0, hi0 = _unpack_halves(buf_ref[slot, 0])
    lo1, hi1 = _unpack_halves(buf_ref[slot, 1])
    ylo = w0 * lo0 + w1 * lo1
    yhi = w0 * hi0 + w1 * hi1
    D = x_ref.shape[1]
    half = D // 2
    ms = (jnp.sum(ylo * ylo, axis=-1, keepdims=True) + jnp.sum(yhi * yhi, axis=-1, keepdims=True)) / D
    inv = lax.rsqrt(ms + NORM_EPS)
    o_ref[:, :half] = x_ref[:, :half] + mod_ref[5:6, :half] * (ylo * inv * gain_ref[:, :half])
    o_ref[:, half:] = x_ref[:, half:] + mod_ref[5:6, half:] * (yhi * inv * gain_ref[:, half:])


def combine(yp, dest, wts, x1, mod, gain, S):
    N, D = x1.shape
    T = min(128, S)
    tiles_per_b = S // T
    pos_s = dest[:, :2].reshape(N // T, 1, 2 * T)
    n_tiles = N // T
    return pl.pallas_call(
        functools.partial(_combine_kernel, T=T),
        grid=(n_tiles,),
        in_specs=[pl.BlockSpec((1, 1, 2 * T), lambda i: (i, 0, 0), memory_space=pltpu.SMEM),
                  pl.BlockSpec((1, 1, 2 * T), lambda i: (jnp.minimum(i + 1, n_tiles - 1), 0, 0),
                               memory_space=pltpu.SMEM),
                  pl.BlockSpec((T, LANES), lambda i: (i, 0)),
                  pl.BlockSpec((T, D), lambda i: (i, 0)),
                  pl.BlockSpec((None, 6, D), lambda i: (i // tiles_per_b, 0, 0)),
                  pl.BlockSpec((1, D), lambda i: (0, 0)),
                  pl.BlockSpec(memory_space=pl.ANY)],
        out_specs=pl.BlockSpec((T, D), lambda i: (i, 0)),
        out_shape=jax.ShapeDtypeStruct((N, D), F32),
        scratch_shapes=[pltpu.VMEM((2, 2, T, D // 2), U32), pltpu.SemaphoreType.DMA((2,))],
        compiler_params=_cparams(("arbitrary",)),
        name="combine",
    )(pos_s, pos_s, wts, x1, mod, gain.reshape(1, D), yp)


def _moe_layout(counts):
    TM = MOE_ROWS
    chunks = (counts.astype(I32) + TM - 1) // TM
    start = (jnp.cumsum(chunks) - chunks) * TM
    return start, chunks


def kernel(x, c, positions, ada_w, ada_b, mix_norm_pre, mix_norm_post, w_in, conv_dw, conv_dw_bias, conv_ln_gain, conv_ln_bias, w_conv_out, w_attn_out, w_out, ffn_norm_pre, ffn_norm_post, router_group, router_expert, expert_w1, expert_w3, expert_w2):
    B, S, D = x.shape
    N = B * S
    U = D // 4
    depth = ada_w.shape[0]
    xc = x.reshape(N, D)
    for layer in range(depth):
        mod = ada_mod(c, ada_w[layer], ada_b[layer]).reshape(B, 6, D)
        h, cosf, sinf = prenorm_rope(xc, mod, mix_norm_pre[layer], positions, S)
        w_in_b = w_in[layer].astype(BF16)
        outs, stats = [], []
        for g, (window, d) in enumerate(ATTN_PATTERNS):
            assert window // d == ATTN_BLOCK and S % (d * ATTN_BLOCK) == 0
            qkv = qkv_proj(h, w_in_b, cosf, sinf, B, S, U, g, d)
            o, st = dilated_attention(qkv, d)
            outs.append(o)
            stats.append(st)
        am = merge_groups(outs, stats, tuple(d for _, d in ATTN_PATTERNS))
        proj = matmul_cols(h, w_in_b, U, 9, "rest_proj")
        z = conv_gate(proj, am, w_conv_out[layer].astype(BF16), w_attn_out[layer].astype(BF16), B, S, U,
                      conv_dw[layer], conv_dw_bias[layer], conv_ln_gain[layer], conv_ln_bias[layer])
        y = matmul_cols(z, w_out[layer].astype(BF16), U, 0, "out_proj")
        rcat = jnp.zeros((D, LANES), F32)
        rcat = rcat.at[:, :N_EXPERT_GROUPS].set(router_group[layer])
        rcat = rcat.at[:, N_EXPERT_GROUPS:N_EXPERT_GROUPS + N_EXPERTS].set(router_expert[layer]).astype(BF16)
        x1, hp, eid, wts, rank, cnt = residual_prenorm_router(
            y, xc, mod, mix_norm_post[layer], ffn_norm_pre[layer], rcat, S)
        TM = MOE_ROWS
        P = (2 * N + N_EXPERTS * (TM - 1)) // TM * TM
        seg_start, seg_chunks = _moe_layout(cnt[0, :N_EXPERTS])
        start_row = jnp.zeros((8, LANES), F32).at[:, :N_EXPERTS].set(seg_start.astype(F32)[None, :])
        dest = dest_rows(eid, rank, start_row)
        xs = scatter_rows(hp, dest, P)
        yp = expert_ffn(xs, seg_start, seg_chunks, expert_w1[layer], expert_w3[layer], expert_w2[layer])
        xc = combine(yp, dest, wts, x1, mod, ffn_norm_post[layer], S)
    return xc.reshape(B, S, D)
```

```python
import functools

import jax
import jax.numpy as jnp
from jax import lax
from jax.experimental import pallas as pl
from jax.experimental.pallas import tpu as pltpu

F32 = jnp.float32
BF16 = jnp.bfloat16
I32 = jnp.int32
U32 = jnp.uint32

HEAD_DIM = 128
LANES = 128
ATTN_BLOCK = 128
ATTN_PATTERNS = ((128, 1), (512, 4), (2048, 16))
ROPE_THETA = 10000.0
CONV_WIDTH = 31
CONV_HALO = 32
N_EXPERT_GROUPS = 8
EXPERTS_PER_GROUP = 8
N_EXPERTS = 64
NORM_EPS = 1e-6
NEG_BIG = -1e30
MOE_ROWS = 256
V7X_VMEM_LIMIT = 60 * 1024 * 1024


def _cparams(sem):
    return pltpu.CompilerParams(dimension_semantics=sem, vmem_limit_bytes=V7X_VMEM_LIMIT)


def _sigmoid(x):
    return 1.0 / (1.0 + jnp.exp(-x))


def _pack_halves(lo, hi):
    lo_b = lax.bitcast_convert_type(lo.astype(BF16).astype(F32), U32) >> 16
    hi_b = lax.bitcast_convert_type(hi.astype(BF16).astype(F32), U32) & jnp.uint32(0xFFFF0000)
    return hi_b | lo_b


def _unpack_halves(w):
    lo = lax.bitcast_convert_type(w << 16, F32)
    hi = lax.bitcast_convert_type(w & jnp.uint32(0xFFFF0000), F32)
    return lo, hi


def _ada_kernel(c_ref, w_ref, b_ref, o_ref):
    c = c_ref[...]
    cact = (c * _sigmoid(c)).astype(BF16)
    o_ref[...] = jnp.dot(cact, w_ref[...].astype(BF16), preferred_element_type=F32) + b_ref[...]


def ada_mod(c, ada_w, ada_b):
    B, D = c.shape
    W = ada_w.shape[1]
    rows = 8
    cp = jnp.zeros((rows, D), F32).at[:B].set(c)
    tn = min(512, W)
    out = pl.pallas_call(
        _ada_kernel,
        grid=(W // tn,),
        in_specs=[pl.BlockSpec((rows, D), lambda j: (0, 0)),
                  pl.BlockSpec((D, tn), lambda j: (0, j)),
                  pl.BlockSpec((1, tn), lambda j: (0, j))],
        out_specs=pl.BlockSpec((rows, tn), lambda j: (0, j)),
        out_shape=jax.ShapeDtypeStruct((rows, W), F32),
        compiler_params=_cparams(("parallel",)),
        name="ada_mod",
    )(cp, ada_w, ada_b.reshape(1, W))
    return out[:B]


def _prenorm_kernel(x_ref, mod_ref, gain_ref, pos_ref, invf_ref, h_ref, cos_ref, sin_ref):
    x = x_ref[...]
    ms = jnp.mean(x * x, axis=-1, keepdims=True)
    y = x * lax.rsqrt(ms + NORM_EPS) * gain_ref[...]
    y = y * (1.0 + mod_ref[1:2, :]) + mod_ref[0:1, :]
    h_ref[...] = y.astype(BF16)
    ang = pos_ref[...].astype(F32) * invf_ref[...]
    lane = lax.broadcasted_iota(I32, ang.shape, 1)
    sn = jnp.sin(ang)
    cos_ref[...] = jnp.cos(ang)
    sin_ref[...] = jnp.where(lane < HEAD_DIM // 2, -sn, sn)


def prenorm_rope(x2d, mod, gain, positions, S):
    N, D = x2d.shape
    tr = min(256, S)
    tiles_per_b = S // tr
    half = HEAD_DIM // 2
    inv = ROPE_THETA ** (-jnp.arange(0, HEAD_DIM, 2, dtype=F32) / HEAD_DIM)
    invf = jnp.concatenate([inv, inv]).reshape(1, HEAD_DIM)
    del half
    return pl.pallas_call(
        _prenorm_kernel,
        grid=(N // tr,),
        in_specs=[pl.BlockSpec((tr, D), lambda i: (i, 0)),
                  pl.BlockSpec((None, 6, D), lambda i: (i // tiles_per_b, 0, 0)),
                  pl.BlockSpec((1, D), lambda i: (0, 0)),
                  pl.BlockSpec((tr, 1), lambda i: (i, 0)),
                  pl.BlockSpec((1, HEAD_DIM), lambda i: (0, 0))],
        out_specs=[pl.BlockSpec((tr, D), lambda i: (i, 0)),
                   pl.BlockSpec((tr, HEAD_DIM), lambda i: (i, 0)),
                   pl.BlockSpec((tr, HEAD_DIM), lambda i: (i, 0))],
        out_shape=[jax.ShapeDtypeStruct((N, D), BF16),
                   jax.ShapeDtypeStruct((N, HEAD_DIM), F32),
                   jax.ShapeDtypeStruct((N, HEAD_DIM), F32)],
        compiler_params=_cparams(("parallel",)),
        name="prenorm_rope",
    )(x2d, mod, gain.reshape(1, D), positions.reshape(N, 1), invf)


def _residue_rows(ref, r, d):
    if d == 1:
        return ref[...]
    return ref[pl.ds(r, ref.shape[0] // d, stride=d), :]


def _qkv_kernel(a_ref, w_ref, cos_ref, sin_ref, o_ref, acc_ref, *, d, heads, scale, n_i):
    s = pl.program_id(0)

    @pl.when(s == 0)
    def _():
        acc_ref[...] = jnp.zeros_like(acc_ref)

    kind = jnp.maximum(s - 1, 0) // n_i
    sc = jnp.where(kind == 0, scale, 1.0).astype(F32)
    is_v = kind == 2
    for r in range(d):
        c = jnp.where(is_v, 1.0, _residue_rows(cos_ref, r, d) * sc)
        sn = jnp.where(is_v, 0.0, _residue_rows(sin_ref, r, d) * sc)
        for h in range(heads):
            t = _residue_rows(acc_ref.at[h], r, d)
            o_ref[r, :, h * HEAD_DIM:(h + 1) * HEAD_DIM] = (
                t * c + pltpu.roll(t, HEAD_DIM // 2, 1) * sn).astype(o_ref.dtype)
    acc = jnp.dot(a_ref[...], w_ref[...], preferred_element_type=F32)
    for h in range(heads):
        acc_ref[h] = acc[:, h * HEAD_DIM:(h + 1) * HEAD_DIM]


def qkv_proj(h, w_bf16, cosf, sinf, B, S, U, g, d):
    N, D = h.shape
    tm = min(1024, S)
    tiles_per_b = S // tm
    n_i = N // tm
    last = 3 * n_i - 1
    kern = functools.partial(_qkv_kernel, d=d, heads=U // HEAD_DIM, scale=HEAD_DIM ** -0.5, n_i=n_i)

    def cur(s):
        return jnp.minimum(s, last)

    def fin(s):
        return jnp.maximum(s - 1, 0)

    return pl.pallas_call(
        kern,
        grid=(3 * n_i + 1,),
        in_specs=[pl.BlockSpec((tm, D), lambda s: (cur(s) % n_i, 0)),
                  pl.BlockSpec((D, U), lambda s: (0, (cur(s) // n_i) * 3 + g)),
                  pl.BlockSpec((tm, HEAD_DIM), lambda s: (fin(s) % n_i, 0)),
                  pl.BlockSpec((tm, HEAD_DIM), lambda s: (fin(s) % n_i, 0))],
        out_specs=pl.BlockSpec((None, None, d, tm // d, U),
                               lambda s: (fin(s) // n_i, (fin(s) % n_i) // tiles_per_b, 0,
                                          (fin(s) % n_i) % tiles_per_b, 0)),
        out_shape=jax.ShapeDtypeStruct((3, B, d, S // d, U), BF16),
        scratch_shapes=[pltpu.VMEM((U // HEAD_DIM, tm, HEAD_DIM), F32)],
        compiler_params=_cparams(("arbitrary",)),
        name=f"qkv_proj_d{d}",
    )(h, w_bf16, cosf, sinf)


def _matmul_kernel(a_ref, w_ref, o_ref):
    o_ref[...] = jnp.dot(a_ref[...], w_ref[...], preferred_element_type=F32).astype(o_ref.dtype)


def matmul_cols(a, w_bf16, U, first_tile, name):
    N, K = a.shape
    n_tiles = w_bf16.shape[1] // U - first_tile
    tm = min(1024, N)
    return pl.pallas_call(
        _matmul_kernel,
        grid=(n_tiles, N // tm),
        in_specs=[pl.BlockSpec((tm, K), lambda j, i: (i, 0)),
                  pl.BlockSpec((K, U), lambda j, i: (0, first_tile + j))],
        out_specs=pl.BlockSpec((tm, U), lambda j, i: (i, j)),
        out_shape=jax.ShapeDtypeStruct((N, n_tiles * U), BF16),
        compiler_params=_cparams(("parallel", "parallel")),
        name=name,
    )(a, w_bf16)


def _attn_kernel(q_ref, kc_ref, kp_ref, vc_ref, vp_ref, o_ref, st_ref, kx_ref, vx_ref, *, QB, H):
    i = pl.program_id(2)
    blk = ATTN_BLOCK
    kx_ref[0:blk, :] = kp_ref[...]
    kx_ref[blk:, :] = kc_ref[...]
    ones = jnp.ones((vx_ref.shape[0], HEAD_DIM), vx_ref.dtype)
    for h in range(H):
        hs = slice(h * HEAD_DIM, (h + 1) * HEAD_DIM)
        vx_ref[0:blk, 2 * h * HEAD_DIM:(2 * h + 1) * HEAD_DIM] = vp_ref[:, hs]
        vx_ref[blk:, 2 * h * HEAD_DIM:(2 * h + 1) * HEAD_DIM] = vc_ref[:, hs]
        vx_ref[:, (2 * h + 1) * HEAD_DIM:(2 * h + 2) * HEAD_DIM] = ones
    rows = lax.broadcasted_iota(I32, (blk, 2 * blk), 0)
    keys = lax.broadcasted_iota(I32, (blk, 2 * blk), 1)
    band = (keys >= rows) & (keys - blk <= rows)
    lane = lax.broadcasted_iota(I32, (blk, LANES), 1)
    dn = (((1,), (1,)), ((), ()))

    def body(qb, carry):
        r0 = pl.multiple_of(qb * blk, blk)
        ok = band & ((keys >= blk) | (i * QB + qb > 0))
        q = [q_ref[pl.ds(r0, blk), h * HEAD_DIM:(h + 1) * HEAD_DIM] for h in range(H)]
        k = [kx_ref[pl.ds(r0, 2 * blk), h * HEAD_DIM:(h + 1) * HEAD_DIM] for h in range(H)]
        s = [jnp.where(ok, lax.dot_general(q[h], k[h], dn, preferred_element_type=F32), NEG_BIG) for h in range(H)]
        m = [jnp.max(s[h], axis=-1, keepdims=True) for h in range(H)]
        p = [jnp.exp(s[h] - m[h]).astype(BF16) for h in range(H)]
        st = jnp.zeros((blk, LANES), F32)
        for h in range(H):
            v1 = vx_ref[pl.ds(r0, 2 * blk), 2 * h * HEAD_DIM:(2 * h + 2) * HEAD_DIM]
            acc = jnp.dot(p[h], v1, preferred_element_type=F32)
            l = acc[:, HEAD_DIM:]
            o_ref[pl.ds(r0, blk), h * HEAD_DIM:(h + 1) * HEAD_DIM] = (acc[:, :HEAD_DIM] / l).astype(o_ref.dtype)
            st = jnp.where(lane == h, m[h], st)
            st = jnp.where(lane == H + h, l, st)
        st_ref[pl.ds(r0, blk), :] = st
        return carry

    lax.fori_loop(0, QB, body, 0)


def dilated_attention(qkv, d):
    _, B, _, L, U = qkv.shape
    H = U // HEAD_DIM
    R = min(512, L)
    QB = R // ATTN_BLOCK

    def cur(kind):
        return pl.BlockSpec((None, None, None, R, U), lambda b, r, i: (kind, b, r, i, 0))

    def prev(kind):
        return pl.BlockSpec((None, None, None, ATTN_BLOCK, U),
                            lambda b, r, i: (kind, b, r, jnp.maximum(i * QB - 1, 0), 0))

    kern = functools.partial(_attn_kernel, QB=QB, H=H)
    return pl.pallas_call(
        kern,
        grid=(B, d, L // R),
        in_specs=[cur(0), cur(1), prev(1), cur(2), prev(2)],
        out_specs=[pl.BlockSpec((None, None, R, U), lambda b, r, i: (b, r, i, 0)),
                   pl.BlockSpec((None, None, R, LANES), lambda b, r, i: (b, r, i, 0))],
        out_shape=[jax.ShapeDtypeStruct((B, d, L, U), BF16),
                   jax.ShapeDtypeStruct((B, d, L, LANES), F32)],
        scratch_shapes=[pltpu.VMEM((R + ATTN_BLOCK, U), BF16), pltpu.VMEM((R + ATTN_BLOCK, 2 * U), BF16)],
        compiler_params=_cparams(("parallel", "parallel", "parallel")),
        name=f"dilated_attn_d{d}",
    )(qkv, qkv, qkv, qkv, qkv)


def _merge_kernel(o0_ref, o1_ref, o2_ref, s0_ref, s1_ref, s2_ref, out_ref, of_ref, sf_ref, *, H, dils):
    for g, (o_ref, s_ref) in enumerate(((o0_ref, s0_ref), (o1_ref, s1_ref), (o2_ref, s2_ref))):
        d = dils[g]
        n = sf_ref.shape[1] // d
        for r in range(d):
            rows = slice(None) if d == 1 else pl.ds(r, n, stride=d)
            sf_ref[g, rows, :] = s_ref[r]
            for h in range(H):
                of_ref[g, h, rows, :] = o_ref[r, :, h * HEAD_DIM:(h + 1) * HEAD_DIM].astype(F32)
    st = [sf_ref[0], sf_ref[1], sf_ref[2]]
    mx = jnp.maximum(jnp.maximum(st[0], st[1]), st[2])
    w = [pltpu.roll(s, LANES - H, 1) * jnp.exp(s - mx) for s in st]
    tot = w[0] + w[1] + w[2]
    coef = [x / tot for x in w]
    for h in range(H):
        hs = slice(h * HEAD_DIM, (h + 1) * HEAD_DIM)
        acc = coef[0][:, h:h + 1] * of_ref[0, h]
        acc += coef[1][:, h:h + 1] * of_ref[1, h]
        acc += coef[2][:, h:h + 1] * of_ref[2, h]
        out_ref[:, hs] = acc.astype(out_ref.dtype)


def merge_groups(outs, stats, dils):
    B, d0, L0, U = outs[0].shape
    S = d0 * L0
    H = U // HEAD_DIM
    tm = min(512, S)
    tiles_per_b = S // tm

    def ospec(d, w):
        return pl.BlockSpec((None, d, tm // d, w), lambda i: (i // tiles_per_b, 0, i % tiles_per_b, 0))

    return pl.pallas_call(
        functools.partial(_merge_kernel, H=H, dils=dils),
        grid=(B * S // tm,),
        in_specs=[ospec(d, U) for d in dils] + [ospec(d, LANES) for d in dils],
        out_specs=pl.BlockSpec((tm, U), lambda i: (i, 0)),
        out_shape=jax.ShapeDtypeStruct((B * S, U), BF16),
        scratch_shapes=[pltpu.VMEM((3, H, tm, HEAD_DIM), F32), pltpu.VMEM((3, tm, LANES), F32)],
        compiler_params=_cparams(("parallel",)),
        name="merge_groups",
    )(*outs, *stats)


def _conv_kernel(a0_ref, a1_ref, b0_ref, b1_ref, ha0_ref, ha1_ref, hb0_ref, hb1_ref,
                 w_ref, bias_ref, g_ref, be_ref, o_ref, u_ref, c_ref, sh_ref, *, ts, U):
    i = pl.program_id(1)
    halo = CONV_HALO
    for half, (a_ref, b_ref, ha_ref, hb_ref) in enumerate(((a0_ref, b0_ref, ha0_ref, hb0_ref),
                                                            (a1_ref, b1_ref, ha1_ref, hb1_ref))):
        cs = slice(half * U, (half + 1) * U)
        u_ref[halo:halo + ts, cs] = a_ref[...].astype(F32) * _sigmoid(b_ref[...].astype(F32))
        hu = ha_ref[...].astype(F32) * _sigmoid(hb_ref[...].astype(F32))
        u_ref[0:halo, cs] = jnp.where(i > 0, hu, 0.0)
    C = 2 * U
    rc = 64
    off = halo - (CONV_WIDTH - 1)

    sub = 8
    n_al = ts + halo - sub

    def chan_body(cc, carry):
        c0 = pl.multiple_of(cc * LANES, LANES)
        sh_ref[0] = u_ref[:, pl.ds(c0, LANES)]
        for b in range(1, sub):
            sh_ref[b, 0:n_al, :] = u_ref[b:b + n_al, pl.ds(c0, LANES)]
        for rb in range(ts // rc):
            acc = jnp.zeros((rc, LANES), F32) + bias_ref[:, pl.ds(c0, LANES)]
            for j in range(CONV_WIDTH):
                a, b = divmod(off + j, sub)
                r0 = rb * rc + a * sub
                acc += w_ref[j:j + 1, pl.ds(c0, LANES)] * sh_ref[b, r0:r0 + rc, :]
            c_ref[rb * rc:(rb + 1) * rc, pl.ds(c0, LANES)] = acc
        return carry

    lax.fori_loop(0, C // LANES, chan_body, 0)

    rn = 32

    def norm_body(rb, carry):
        r0 = pl.multiple_of(rb * rn, rn)
        v = c_ref[pl.ds(r0, rn), :]
        mu = jnp.mean(v, axis=-1, keepdims=True)
        dv = v - mu
        var = jnp.mean(dv * dv, axis=-1, keepdims=True)
        y = dv * lax.rsqrt(var + NORM_EPS) * g_ref[...] + be_ref[...]
        o_ref[pl.ds(r0, rn), :] = (y * _sigmoid(y)).astype(o_ref.dtype)
        return carry

    lax.fori_loop(0, ts // rn, norm_body, 0)


def conv_branch(proj, B, S, U, conv_dw, conv_bias, ln_gain, ln_bias):
    IN = proj.shape[1]
    C = 2 * U
    ts = min(256, S)
    pv = proj.reshape(B, S, IN)
    hb = ts // CONV_HALO
    cur = lambda blk: pl.BlockSpec((None, ts, U), lambda b, i, blk=blk: (b, i, blk))
    prv = lambda blk: pl.BlockSpec((None, CONV_HALO, U), lambda b, i, blk=blk: (b, jnp.maximum(i * hb - 1, 0), blk))
    vec = pl.BlockSpec((1, C), lambda b, i: (0, 0))
    out = pl.pallas_call(
        functools.partial(_conv_kernel, ts=ts, U=U),
        grid=(B, S // ts),
        in_specs=[cur(0), cur(1), cur(2), cur(3), prv(0), prv(1), prv(2), prv(3),
                  pl.BlockSpec((CONV_WIDTH, C), lambda b, i: (0, 0)), vec, vec, vec],
        out_specs=pl.BlockSpec((None, ts, C), lambda b, i: (b, i, 0)),
        out_shape=jax.ShapeDtypeStruct((B, S, C), BF16),
        scratch_shapes=[pltpu.VMEM((ts + CONV_HALO, C), F32), pltpu.VMEM((ts, C), F32),
                        pltpu.VMEM((8, ts + CONV_HALO, LANES), F32)],
        compiler_params=_cparams(("parallel", "parallel")),
        name="conv_branch",
    )(pv, pv, pv, pv, pv, pv, pv, pv, conv_dw, conv_bias.reshape(1, C), ln_gain.reshape(1, C), ln_bias.reshape(1, C))
    return out.reshape(B * S, C)


def _gateproj_kernel(cn_ref, am_ref, wc_ref, wa_ref, gc_ref, ga_ref, o_ref):
    conv = jnp.dot(cn_ref[...], wc_ref[...], preferred_element_type=F32)
    z = _sigmoid(gc_ref[...].astype(F32)) * conv
    attn = jnp.dot(am_ref[...], wa_ref[...], preferred_element_type=F32)
    z += _sigmoid(ga_ref[...].astype(F32)) * attn
    o_ref[...] = z.astype(o_ref.dtype)


def gate_proj(cn, am, wc, wa, proj, U):
    N, C = cn.shape
    D = wc.shape[1]
    tn = U
    tm = min(512, N)
    return pl.pallas_call(
        _gateproj_kernel,
        grid=(D // tn, N // tm),
        in_specs=[pl.BlockSpec((tm, C), lambda j, i: (i, 0)),
                  pl.BlockSpec((tm, U), lambda j, i: (i, 0)),
                  pl.BlockSpec((C, tn), lambda j, i: (0, j)),
                  pl.BlockSpec((U, tn), lambda j, i: (0, j)),
                  pl.BlockSpec((tm, tn), lambda j, i: (i, 4 + j)),
                  pl.BlockSpec((tm, tn), lambda j, i: (i, 8 + j))],
        out_specs=pl.BlockSpec((tm, tn), lambda j, i: (i, j)),
        out_shape=jax.ShapeDtypeStruct((N, D), BF16),
        compiler_params=_cparams(("parallel", "parallel")),
        name="gate_proj",
    )(cn, am, wc, wa, proj, proj)


def _router_kernel(y_ref, xin_ref, mod_ref, gpost_ref, gain_ref, r_ref,
                   x1_ref, hp_ref, eid_ref, wt_ref, rank_ref, cnt_ref, carry_ref, *, tr):
    step = pl.program_id(0)

    @pl.when(step == 0)
    def _():
        carry_ref[...] = jnp.zeros_like(carry_ref)

    y = y_ref[...].astype(F32)
    yms = jnp.mean(y * y, axis=-1, keepdims=True)
    x = xin_ref[...] + mod_ref[2:3, :] * (y * lax.rsqrt(yms + NORM_EPS) * gpost_ref[...])
    x1_ref[...] = x
    D = x.shape[1]
    ms = jnp.mean(x * x, axis=-1, keepdims=True)
    h = x * lax.rsqrt(ms + NORM_EPS) * gain_ref[...]
    h = h * (1.0 + mod_ref[4:5, :]) + mod_ref[3:4, :]
    lo = h[:, :D // 2]
    hi = h[:, D // 2:]
    hp_ref[...] = _pack_halves(lo, hi)
    logits = (jnp.dot(lo.astype(BF16), r_ref[:D // 2, :], preferred_element_type=F32)
              + jnp.dot(hi.astype(BF16), r_ref[D // 2:, :], preferred_element_type=F32))
    lane = lax.broadcasted_iota(I32, logits.shape, 1)
    G = N_EXPERT_GROUPS
    is_g = lane < G
    gl = jnp.where(is_g, logits, NEG_BIG)
    gmax = jnp.max(gl, axis=-1, keepdims=True)
    grp = jnp.min(jnp.where(gl == gmax, lane, LANES), axis=-1, keepdims=True)
    p_grp = 1.0 / jnp.sum(jnp.where(is_g, jnp.exp(gl - gmax), 0.0), axis=-1, keepdims=True)
    in_grp = (lane >= G) & (lane < G + N_EXPERTS) & (((lane - G) // EXPERTS_PER_GROUP) == grp)
    el = jnp.where(in_grp, logits, NEG_BIG)
    v0 = jnp.max(el, axis=-1, keepdims=True)
    i0 = jnp.min(jnp.where(in_grp & (el == v0), lane, LANES), axis=-1, keepdims=True)
    in2 = in_grp & (lane != i0)
    el2 = jnp.where(in2, logits, NEG_BIG)
    v1 = jnp.max(el2, axis=-1, keepdims=True)
    i1 = jnp.min(jnp.where(in2 & (el2 == v1), lane, LANES), axis=-1, keepdims=True)
    e1 = jnp.exp(v1 - v0)
    w0 = p_grp / (1.0 + e1)
    w1 = p_grp * e1 / (1.0 + e1)
    ex0 = i0 - G
    ex1 = i1 - G
    eid_ref[...] = jnp.where(lane == 0, ex0, jnp.where(lane == 1, ex1, 0))
    wt_ref[...] = jnp.where(lane == 0, w0, jnp.where(lane == 1, w1, 0.0))
    oh0 = (lane == ex0).astype(F32)
    oh1 = (lane == ex1).astype(F32)
    both = oh0 + oh1
    rr = lax.broadcasted_iota(I32, (tr, tr), 0)
    cc = lax.broadcasted_iota(I32, (tr, tr), 1)
    tril = (cc < rr).astype(BF16)
    before = jnp.dot(tril, both.astype(BF16), preferred_element_type=F32) + carry_ref[0:1, :]
    rk0 = jnp.sum(before * oh0, axis=-1, keepdims=True)
    rk1 = jnp.sum(before * oh1, axis=-1, keepdims=True)
    rank_ref[...] = jnp.where(lane == 0, rk0, jnp.where(lane == 1, rk1, 0.0))
    newc = carry_ref[0:1, :] + jnp.sum(both, axis=0, keepdims=True)
    carry_ref[...] = jnp.broadcast_to(newc, carry_ref.shape)
    cnt_ref[...] = jnp.broadcast_to(newc, cnt_ref.shape)


def residual_prenorm_router(y, x2d, mod, gain_post, gain, rcat, S):
    N, D = x2d.shape
    tr = min(256, S)
    tiles_per_b = S // tr
    lane_spec = pl.BlockSpec((tr, LANES), lambda i: (i, 0))
    row_spec = pl.BlockSpec((tr, D), lambda i: (i, 0))
    vec_spec = pl.BlockSpec((1, D), lambda i: (0, 0))
    return pl.pallas_call(
        functools.partial(_router_kernel, tr=tr),
        grid=(N // tr,),
        in_specs=[row_spec, row_spec,
                  pl.BlockSpec((None, 6, D), lambda i: (i // tiles_per_b, 0, 0)),
                  vec_spec, vec_spec,
                  pl.BlockSpec((D, LANES), lambda i: (0, 0))],
        out_specs=[row_spec, pl.BlockSpec((tr, D // 2), lambda i: (i, 0)), lane_spec, lane_spec, lane_spec,
                   pl.BlockSpec((8, LANES), lambda i: (0, 0))],
        out_shape=[jax.ShapeDtypeStruct((N, D), F32),
                   jax.ShapeDtypeStruct((N, D // 2), U32),
                   jax.ShapeDtypeStruct((N, LANES), I32),
                   jax.ShapeDtypeStruct((N, LANES), F32),
                   jax.ShapeDtypeStruct((N, LANES), F32),
                   jax.ShapeDtypeStruct((8, LANES), F32)],
        scratch_shapes=[pltpu.VMEM((8, LANES), F32)],
        compiler_params=_cparams(("arbitrary",)),
        name="residual_prenorm_router",
    )(y, x2d, mod, gain_post.reshape(1, D), gain.reshape(1, D), rcat)


def _dest_kernel(eid_ref, rank_ref, start_ref, o_ref):
    lane = lax.broadcasted_iota(I32, eid_ref.shape, 1)
    eid = eid_ref[...]
    rank = rank_ref[...]
    start = start_ref[0:1, :]
    d = []
    for k in range(2):
        oh = (lane == eid[:, k:k + 1]).astype(F32)
        d.append(jnp.sum(oh * start, axis=-1, keepdims=True) + rank[:, k:k + 1])
    o_ref[...] = jnp.where(lane == 0, d[0], jnp.where(lane == 1, d[1], 0.0)).astype(I32)


def dest_rows(eid, rank, pad_start):
    N = eid.shape[0]
    tr = min(1024, N)
    spec = pl.BlockSpec((tr, LANES), lambda i: (i, 0))
    return pl.pallas_call(
        _dest_kernel,
        grid=(N // tr,),
        in_specs=[spec, spec, pl.BlockSpec((8, LANES), lambda i: (0, 0))],
        out_specs=spec,
        out_shape=jax.ShapeDtypeStruct((N, LANES), I32),
        compiler_params=_cparams(("parallel",)),
        name="dest_rows",
    )(eid, rank, pad_start)


def _scatter_kernel(dest_ref, h_ref, xs_in_ref, xs_ref, sem, *, T):
    del xs_in_ref

    def issue(t, carry):
        for k in range(2):
            d = dest_ref[0, 0, 2 * t + k]
            pltpu.make_async_copy(h_ref.at[pl.ds(t, 1)], xs_ref.at[pl.ds(d, 1)], sem).start()
        return carry

    lax.fori_loop(0, T, issue, 0, unroll=4)
    for k in range(2):
        pltpu.make_async_copy(h_ref, xs_ref.at[pl.ds(0, T)], sem).wait()


def scatter_rows(hp, dest, P):
    N, W = hp.shape
    T = min(256, N)
    dest_s = dest[:, :2].reshape(N // T, 1, 2 * T)
    xs0 = jnp.zeros((P, W), hp.dtype)
    return pl.pallas_call(
        functools.partial(_scatter_kernel, T=T),
        grid=(N // T,),
        in_specs=[pl.BlockSpec((1, 1, 2 * T), lambda i: (i, 0, 0), memory_space=pltpu.SMEM),
                  pl.BlockSpec((T, W), lambda i: (i, 0)),
                  pl.BlockSpec(memory_space=pl.ANY)],
        out_specs=pl.BlockSpec(memory_space=pl.ANY),
        out_shape=jax.ShapeDtypeStruct((P, W), hp.dtype),
        scratch_shapes=[pltpu.SemaphoreType.DMA(())],
        input_output_aliases={2: 0},
        compiler_params=_cparams(("arbitrary",)),
        name="scatter_rows",
    )(dest_s, hp, xs0)


def _expert_chunks(st_ref, nc_ref, in_hbm, out_hbm, ibuf, obuf, isem, osem, compute):
    e = pl.program_id(0)
    n_e = pl.num_programs(0)
    n = nc_ref[e]
    TM = ibuf.shape[1]
    base = pl.multiple_of(st_ref[e], TM)

    def icopy(row, slot):
        return pltpu.make_async_copy(in_hbm.at[pl.ds(row, TM)], ibuf.at[slot], isem.at[slot])

    def ocopy(row, slot):
        return pltpu.make_async_copy(obuf.at[slot], out_hbm.at[pl.ds(row, TM)], osem.at[slot])

    @pl.when((e == 0) & (n > 0))
    def _():
        icopy(base, 0).start()

    def body(c, carry):
        slot = c % 2
        row = pl.multiple_of(base + c * TM, TM)

        @pl.when(c + 1 < n)
        def _():
            icopy(row + TM, 1 - slot).start()

        icopy(row, slot).wait()

        @pl.when(c >= 2)
        def _():
            ocopy(row, slot).wait()

        compute(ibuf.at[slot], obuf.at[slot])
        ocopy(row, slot).start()
        return carry

    lax.fori_loop(0, n, body, 0)

    @pl.when(n >= 2)
    def _():
        ocopy(base, n % 2).wait()

    @pl.when(n >= 1)
    def _():
        ocopy(base, (n + 1) % 2).wait()

    @pl.when(e + 1 < n_e)
    def _():
        @pl.when(nc_ref[e + 1] > 0)
        def _():
            icopy(pl.multiple_of(st_ref[e + 1], TM), 0).start()

    @pl.when(e == n_e - 1)
    def _():
        used = pl.multiple_of(base + n * TM, TM)
        n_tail = (out_hbm.shape[0] - used) // TM
        obuf[0] = jnp.zeros(obuf.shape[1:], obuf.dtype)

        def zstart(c, carry):
            ocopy(pl.multiple_of(used + c * TM, TM), 0).start()
            return carry

        def zwait(c, carry):
            ocopy(used, 0).wait()
            return carry

        lax.fori_loop(0, n_tail, zstart, 0)
        lax.fori_loop(0, n_tail, zwait, 0)


WEIGHT_DMA_PRIORITY = 1


def _stream_expert_weights(w_hbms, w_bufs, wsem):
    e = pl.program_id(0)
    n_e = pl.num_programs(0)
    slot = e % 2

    def copy(k, ei, s):
        return pltpu.make_async_copy(w_hbms[k].at[ei], w_bufs[k].at[s], wsem.at[k, s])

    @pl.when(e == 0)
    def _():
        for k in range(len(w_hbms)):
            copy(k, 0, 0).start(priority=WEIGHT_DMA_PRIORITY)

    @pl.when(e + 1 < n_e)
    def _():
        for k in range(len(w_hbms)):
            copy(k, e + 1, 1 - slot).start(priority=WEIGHT_DMA_PRIORITY)

    for k in range(len(w_hbms)):
        copy(k, e, slot).wait()
    return slot


def _moe_up_kernel(st_ref, nc_ref, xs_hbm, w1_hbm, w3_hbm, o_hbm, wb_ref, w1_buf, w3_buf, ibuf, obuf,
                   wsem, isem, osem, *, F):
    slot = _stream_expert_weights((w1_hbm, w3_hbm), (w1_buf, w3_buf), wsem)

    @pl.when(nc_ref[pl.program_id(0)] > 0)
    def _():
        wb_ref[:, :F] = w1_buf[slot].astype(BF16)
        wb_ref[:, F:] = w3_buf[slot].astype(BF16)

    def compute(x_ref, o_ref):
        lo, hi = _unpack_halves(x_ref[...])
        half = wb_ref.shape[0] // 2
        hcat = (jnp.dot(lo.astype(BF16), wb_ref[:half, :], preferred_element_type=F32)
                + jnp.dot(hi.astype(BF16), wb_ref[half:, :], preferred_element_type=F32))
        a = hcat[:, :F]
        o_ref[...] = (a * _sigmoid(a) * hcat[:, F:]).astype(o_ref.dtype)

    _expert_chunks(st_ref, nc_ref, xs_hbm, o_hbm, ibuf, obuf, isem, osem, compute)


def _moe_down_kernel(st_ref, nc_ref, h_hbm, w2_hbm, o_hbm, wb_ref, w2_buf, ibuf, obuf, wsem, isem, osem):
    slot = _stream_expert_weights((w2_hbm,), (w2_buf,), wsem)

    @pl.when(nc_ref[pl.program_id(0)] > 0)
    def _():
        wb_ref[...] = w2_buf[slot].astype(BF16)

    def compute(h_ref, o_ref):
        y = jnp.dot(h_ref[...], wb_ref[...], preferred_element_type=F32)
        half = y.shape[1] // 2
        o_ref[...] = _pack_halves(y[:, :half], y[:, half:])

    _expert_chunks(st_ref, nc_ref, h_hbm, o_hbm, ibuf, obuf, isem, osem, compute)


def expert_ffn(xs, seg_start, seg_chunks, w1, w3, w2):
    P, Wp = xs.shape
    E, D, F = w1.shape
    TM = MOE_ROWS
    any_spec = pl.BlockSpec(memory_space=pl.ANY)
    dma2 = pltpu.SemaphoreType.DMA((2,))
    hmid = pl.pallas_call(
        functools.partial(_moe_up_kernel, F=F),
        grid_spec=pltpu.PrefetchScalarGridSpec(
            num_scalar_prefetch=2,
            grid=(E,),
            in_specs=[any_spec, any_spec, any_spec],
            out_specs=any_spec,
            scratch_shapes=[pltpu.VMEM((D, 2 * F), BF16), pltpu.VMEM((2, D, F), F32), pltpu.VMEM((2, D, F), F32),
                            pltpu.VMEM((2, TM, Wp), U32), pltpu.VMEM((2, TM, F), BF16),
                            pltpu.SemaphoreType.DMA((2, 2)), dma2, dma2]),
        out_shape=jax.ShapeDtypeStruct((P, F), BF16),
        compiler_params=_cparams(("arbitrary",)),
        name="moe_up",
    )(seg_start, seg_chunks, xs, w1, w3)
    return pl.pallas_call(
        _moe_down_kernel,
        grid_spec=pltpu.PrefetchScalarGridSpec(
            num_scalar_prefetch=2,
            grid=(E,),
            in_specs=[any_spec, any_spec],
            out_specs=any_spec,
            scratch_shapes=[pltpu.VMEM((F, D), BF16), pltpu.VMEM((2, F, D), F32), pltpu.VMEM((2, TM, F), BF16),
                            pltpu.VMEM((2, TM, D // 2), U32), pltpu.SemaphoreType.DMA((1, 2)), dma2, dma2]),
        out_shape=jax.ShapeDtypeStruct((P, D // 2), U32),
        compiler_params=_cparams(("arbitrary",)),
        name="moe_down",
    )(seg_start, seg_chunks, hmid, w2)


def _combine_kernel(pos_ref, nxt_ref, wt_ref, x_ref, mod_ref, gain_ref, y_hbm, o_ref, buf_ref, sem, *, T):
    i = pl.program_id(0)
    n = pl.num_programs(0)
    slot = i % 2

    def gather(p_ref, s):
        def issue(t, carry):
            for k in range(2):
                p = p_ref[0, 0, 2 * t + k]
                pltpu.make_async_copy(y_hbm.at[pl.ds(p, 1)], buf_ref.at[s, k, pl.ds(t, 1)], sem.at[s]).start()
            return carry

        lax.fori_loop(0, T, issue, 0, unroll=4)

    @pl.when(i == 0)
    def _():
        gather(pos_ref, 0)

    @pl.when(i + 1 < n)
    def _():
        gather(nxt_ref, 1 - slot)

    for k in range(2):
        pltpu.make_async_copy(y_hbm.at[pl.ds(0, T)], buf_ref.at[slot, k], sem.at[slot]).wait()

    wt = wt_ref[...]
    w0 = wt[:, 0:1]
    w1 = wt[:, 1:2]
    lo0, hi0 = _unpack_halves(buf_ref[slot, 0])
    lo1, hi1 = _unpack_halves(buf_ref[slot, 1])
    ylo = w0 * lo0 + w1 * lo1
    yhi = w0 * hi0 + w1 * hi1
    D = x_ref.shape[1]
    half = D // 2
    ms = (jnp.sum(ylo * ylo, axis=-1, keepdims=True) + jnp.sum(yhi * yhi, axis=-1, keepdims=True)) / D
    inv = lax.rsqrt(ms + NORM_EPS)
    o_ref[:, :half] = x_ref[:, :half] + mod_ref[5:6, :half] * (ylo * inv * gain_ref[:, :half])
    o_ref[:, half:] = x_ref[:, half:] + mod_ref[5:6, half:] * (yhi * inv * gain_ref[:, half:])


def combine(yp, dest, wts, x1, mod, gain, S):
    N, D = x1.shape
    T = min(128, S)
    tiles_per_b = S // T
    pos_s = dest[:, :2].reshape(N // T, 1, 2 * T)
    n_tiles = N // T
    return pl.pallas_call(
        functools.partial(_combine_kernel, T=T),
        grid=(n_tiles,),
        in_specs=[pl.BlockSpec((1, 1, 2 * T), lambda i: (i, 0, 0), memory_space=pltpu.SMEM),
                  pl.BlockSpec((1, 1, 2 * T), lambda i: (jnp.minimum(i + 1, n_tiles - 1), 0, 0),
                               memory_space=pltpu.SMEM),
                  pl.BlockSpec((T, LANES), lambda i: (i, 0)),
                  pl.BlockSpec((T, D), lambda i: (i, 0)),
                  pl.BlockSpec((None, 6, D), lambda i: (i // tiles_per_b, 0, 0)),
                  pl.BlockSpec((1, D), lambda i: (0, 0)),
                  pl.BlockSpec(memory_space=pl.ANY)],
        out_specs=pl.BlockSpec((T, D), lambda i: (i, 0)),
        out_shape=jax.ShapeDtypeStruct((N, D), F32),
        scratch_shapes=[pltpu.VMEM((2, 2, T, D // 2), U32), pltpu.SemaphoreType.DMA((2,))],
        compiler_params=_cparams(("arbitrary",)),
        name="combine",
    )(pos_s, pos_s, wts, x1, mod, gain.reshape(1, D), yp)


def _moe_layout(counts):
    TM = MOE_ROWS
    chunks = (counts.astype(I32) + TM - 1) // TM
    start = (jnp.cumsum(chunks) - chunks) * TM
    return start, chunks


def kernel(x, c, positions, ada_w, ada_b, mix_norm_pre, mix_norm_post, w_in, conv_dw, conv_dw_bias, conv_ln_gain, conv_ln_bias, w_conv_out, w_attn_out, w_out, ffn_norm_pre, ffn_norm_post, router_group, router_expert, expert_w1, expert_w3, expert_w2):
    B, S, D = x.shape
    N = B * S
    U = D // 4
    depth = ada_w.shape[0]
    xc = x.reshape(N, D)
    for layer in range(depth):
        mod = ada_mod(c, ada_w[layer], ada_b[layer]).reshape(B, 6, D)
        h, cosf, sinf = prenorm_rope(xc, mod, mix_norm_pre[layer], positions, S)
        w_in_b = w_in[layer].astype(BF16)
        outs, stats = [], []
        for g, (window, d) in enumerate(ATTN_PATTERNS):
            assert window // d == ATTN_BLOCK and S % (d * ATTN_BLOCK) == 0
            qkv = qkv_proj(h, w_in_b, cosf, sinf, B, S, U, g, d)
            o, st = dilated_attention(qkv, d)
            outs.append(o)
            stats.append(st)
        am = merge_groups(outs, stats, tuple(d for _, d in ATTN_PATTERNS))
        proj = matmul_cols(h, w_in_b, U, 9, "rest_proj")
        cn = conv_branch(proj, B, S, U, conv_dw[layer], conv_dw_bias[layer], conv_ln_gain[layer], conv_ln_bias[layer])
        z = gate_proj(cn, am, w_conv_out[layer].astype(BF16), w_attn_out[layer].astype(BF16), proj, U)
        y = matmul_cols(z, w_out[layer].astype(BF16), U, 0, "out_proj")
        rcat = jnp.zeros((D, LANES), F32)
        rcat = rcat.at[:, :N_EXPERT_GROUPS].set(router_group[layer])
        rcat = rcat.at[:, N_EXPERT_GROUPS:N_EXPERT_GROUPS + N_EXPERTS].set(router_expert[layer]).astype(BF16)
        x1, hp, eid, wts, rank, cnt = residual_prenorm_router(
            y, xc, mod, mix_norm_post[layer], ffn_norm_pre[layer], rcat, S)
        TM = MOE_ROWS
        P = (2 * N + N_EXPERTS * (TM - 1)) // TM * TM
        seg_start, seg_chunks = _moe_layout(cnt[0, :N_EXPERTS])
        start_row = jnp.zeros((8, LANES), F32).at[:, :N_EXPERTS].set(seg_start.astype(F32)[None, :])
        dest = dest_rows(eid, rank, start_row)
        xs = scatter_rows(hp, dest, P)
        yp = expert_ffn(xs, seg_start, seg_chunks, expert_w1[layer], expert_w3[layer], expert_w2[layer])
        xc = combine(yp, dest, wts, x1, mod, ffn_norm_post[layer], S)
    return xc.reshape(B, S, D)
```

```python
import functools

import jax
import jax.numpy as jnp
from jax import lax
from jax.experimental import pallas as pl
from jax.experimental.pallas import tpu as pltpu

F32 = jnp.float32
BF16 = jnp.bfloat16
I32 = jnp.int32
U32 = jnp.uint32

HEAD_DIM = 128
LANES = 128
ATTN_BLOCK = 128
ATTN_PATTERNS = ((128, 1), (512, 4), (2048, 16))
ROPE_THETA = 10000.0
CONV_WIDTH = 31
CONV_HALO = 32
N_EXPERT_GROUPS = 8
EXPERTS_PER_GROUP = 8
N_EXPERTS = 64
NORM_EPS = 1e-6
NEG_BIG = -1e30
MOE_ROWS = 256
V7X_VMEM_LIMIT = 60 * 1024 * 1024


def _cparams(sem):
    return pltpu.CompilerParams(dimension_semantics=sem, vmem_limit_bytes=V7X_VMEM_LIMIT)


def _sigmoid(x):
    return 1.0 / (1.0 + jnp.exp(-x))


def _pack_halves(lo, hi):
    lo_b = lax.bitcast_convert_type(lo.astype(BF16).astype(F32), U32) >> 16
    hi_b = lax.bitcast_convert_type(hi.astype(BF16).astype(F32), U32) & jnp.uint32(0xFFFF0000)
    return hi_b | lo_b


def _unpack_halves(w):
    lo = lax.bitcast_convert_type(w << 16, F32)
    hi = lax.bitcast_convert_type(w & jnp.uint32(0xFFFF0000), F32)
    return lo, hi


def _ada_kernel(c_ref, w_ref, b_ref, o_ref):
    c = c_ref[...]
    cact = (c * _sigmoid(c)).astype(BF16)
    o_ref[...] = jnp.dot(cact, w_ref[...].astype(BF16), preferred_element_type=F32) + b_ref[...]


def ada_mod(c, ada_w, ada_b):
    B, D = c.shape
    W = ada_w.shape[1]
    rows = 8
    cp = jnp.zeros((rows, D), F32).at[:B].set(c)
    tn = min(512, W)
    out = pl.pallas_call(
        _ada_kernel,
        grid=(W // tn,),
        in_specs=[pl.BlockSpec((rows, D), lambda j: (0, 0)),
                  pl.BlockSpec((D, tn), lambda j: (0, j)),
                  pl.BlockSpec((1, tn), lambda j: (0, j))],
        out_specs=pl.BlockSpec((rows, tn), lambda j: (0, j)),
        out_shape=jax.ShapeDtypeStruct((rows, W), F32),
        compiler_params=_cparams(("parallel",)),
        name="ada_mod",
    )(cp, ada_w, ada_b.reshape(1, W))
    return out[:B]


def _prenorm_kernel(x_ref, mod_ref, gain_ref, pos_ref, invf_ref, h_ref, cos_ref, sin_ref):
    x = x_ref[...]
    ms = jnp.mean(x * x, axis=-1, keepdims=True)
    y = x * lax.rsqrt(ms + NORM_EPS) * gain_ref[...]
    y = y * (1.0 + mod_ref[1:2, :]) + mod_ref[0:1, :]
    h_ref[...] = y.astype(BF16)
    ang = pos_ref[...].astype(F32) * invf_ref[...]
    lane = lax.broadcasted_iota(I32, ang.shape, 1)
    sn = jnp.sin(ang)
    cos_ref[...] = jnp.cos(ang)
    sin_ref[...] = jnp.where(lane < HEAD_DIM // 2, -sn, sn)


def prenorm_rope(x2d, mod, gain, positions, S):
    N, D = x2d.shape
    tr = min(256, S)
    tiles_per_b = S // tr
    half = HEAD_DIM // 2
    inv = ROPE_THETA ** (-jnp.arange(0, HEAD_DIM, 2, dtype=F32) / HEAD_DIM)
    invf = jnp.concatenate([inv, inv]).reshape(1, HEAD_DIM)
    del half
    return pl.pallas_call(
        _prenorm_kernel,
        grid=(N // tr,),
        in_specs=[pl.BlockSpec((tr, D), lambda i: (i, 0)),
                  pl.BlockSpec((None, 6, D), lambda i: (i // tiles_per_b, 0, 0)),
                  pl.BlockSpec((1, D), lambda i: (0, 0)),
                  pl.BlockSpec((tr, 1), lambda i: (i, 0)),
                  pl.BlockSpec((1, HEAD_DIM), lambda i: (0, 0))],
        out_specs=[pl.BlockSpec((tr, D), lambda i: (i, 0)),
                   pl.BlockSpec((tr, HEAD_DIM), lambda i: (i, 0)),
                   pl.BlockSpec((tr, HEAD_DIM), lambda i: (i, 0))],
        out_shape=[jax.ShapeDtypeStruct((N, D), BF16),
                   jax.ShapeDtypeStruct((N, HEAD_DIM), F32),
                   jax.ShapeDtypeStruct((N, HEAD_DIM), F32)],
        compiler_params=_cparams(("parallel",)),
        name="prenorm_rope",
    )(x2d, mod, gain.reshape(1, D), positions.reshape(N, 1), invf)


def _residue_rows(ref, r, d):
    if d == 1:
        return ref[...]
    return ref[pl.ds(r, ref.shape[0] // d, stride=d), :]


def _qkv_kernel(a_ref, w_ref, cos_ref, sin_ref, o_ref, acc_ref, *, d, heads, scale, n_i):
    s = pl.program_id(0)

    @pl.when(s == 0)
    def _():
        acc_ref[...] = jnp.zeros_like(acc_ref)

    kind = jnp.maximum(s - 1, 0) // n_i
    sc = jnp.where(kind == 0, scale, 1.0).astype(F32)
    is_v = kind == 2
    for r in range(d):
        c = jnp.where(is_v, 1.0, _residue_rows(cos_ref, r, d) * sc)
        sn = jnp.where(is_v, 0.0, _residue_rows(sin_ref, r, d) * sc)
        for h in range(heads):
            t = _residue_rows(acc_ref.at[h], r, d)
            o_ref[r, :, h * HEAD_DIM:(h + 1) * HEAD_DIM] = (
                t * c + pltpu.roll(t, HEAD_DIM // 2, 1) * sn).astype(o_ref.dtype)
    acc = jnp.dot(a_ref[...], w_ref[...], preferred_element_type=F32)
    for h in range(heads):
        acc_ref[h] = acc[:, h * HEAD_DIM:(h + 1) * HEAD_DIM]


def qkv_proj(h, w_bf16, cosf, sinf, B, S, U, g, d):
    N, D = h.shape
    tm = min(1024, S)
    tiles_per_b = S // tm
    n_i = N // tm
    last = 3 * n_i - 1
    kern = functools.partial(_qkv_kernel, d=d, heads=U // HEAD_DIM, scale=HEAD_DIM ** -0.5, n_i=n_i)

    def cur(s):
        return jnp.minimum(s, last)

    def fin(s):
        return jnp.maximum(s - 1, 0)

    return pl.pallas_call(
        kern,
        grid=(3 * n_i + 1,),
        in_specs=[pl.BlockSpec((tm, D), lambda s: (cur(s) % n_i, 0)),
                  pl.BlockSpec((D, U), lambda s: (0, (cur(s) // n_i) * 3 + g)),
                  pl.BlockSpec((tm, HEAD_DIM), lambda s: (fin(s) % n_i, 0)),
                  pl.BlockSpec((tm, HEAD_DIM), lambda s: (fin(s) % n_i, 0))],
        out_specs=pl.BlockSpec((None, None, d, tm // d, U),
                               lambda s: (fin(s) // n_i, (fin(s) % n_i) // tiles_per_b, 0,
                                          (fin(s) % n_i) % tiles_per_b, 0)),
        out_shape=jax.ShapeDtypeStruct((3, B, d, S // d, U), BF16),
        scratch_shapes=[pltpu.VMEM((U // HEAD_DIM, tm, HEAD_DIM), F32)],
        compiler_params=_cparams(("arbitrary",)),
        name=f"qkv_proj_d{d}",
    )(h, w_bf16, cosf, sinf)


def _matmul_kernel(a_ref, w_ref, o_ref):
    o_ref[...] = jnp.dot(a_ref[...], w_ref[...], preferred_element_type=F32).astype(o_ref.dtype)


def matmul_cols(a, w_bf16, U, first_tile, name):
    N, K = a.shape
    n_tiles = w_bf16.shape[1] // U - first_tile
    tm = min(1024, N)
    return pl.pallas_call(
        _matmul_kernel,
        grid=(n_tiles, N // tm),
        in_specs=[pl.BlockSpec((tm, K), lambda j, i: (i, 0)),
                  pl.BlockSpec((K, U), lambda j, i: (0, first_tile + j))],
        out_specs=pl.BlockSpec((tm, U), lambda j, i: (i, j)),
        out_shape=jax.ShapeDtypeStruct((N, n_tiles * U), BF16),
        compiler_params=_cparams(("parallel", "parallel")),
        name=name,
    )(a, w_bf16)


def _attn_kernel(q_ref, kc_ref, kp_ref, vc_ref, vp_ref, o_ref, st_ref, kx_ref, vx_ref, *, QB, H):
    i = pl.program_id(2)
    blk = ATTN_BLOCK
    kx_ref[0:blk, :] = kp_ref[...]
    kx_ref[blk:, :] = kc_ref[...]
    ones = jnp.ones((vx_ref.shape[0], HEAD_DIM), vx_ref.dtype)
    for h in range(H):
        hs = slice(h * HEAD_DIM, (h + 1) * HEAD_DIM)
        vx_ref[0:blk, 2 * h * HEAD_DIM:(2 * h + 1) * HEAD_DIM] = vp_ref[:, hs]
        vx_ref[blk:, 2 * h * HEAD_DIM:(2 * h + 1) * HEAD_DIM] = vc_ref[:, hs]
        vx_ref[:, (2 * h + 1) * HEAD_DIM:(2 * h + 2) * HEAD_DIM] = ones
    rows = lax.broadcasted_iota(I32, (blk, 2 * blk), 0)
    keys = lax.broadcasted_iota(I32, (blk, 2 * blk), 1)
    band = (keys >= rows) & (keys - blk <= rows)
    lane = lax.broadcasted_iota(I32, (blk, LANES), 1)
    dn = (((1,), (1,)), ((), ()))

    def body(qb, carry):
        r0 = pl.multiple_of(qb * blk, blk)
        ok = band & ((keys >= blk) | (i * QB + qb > 0))
        q = [q_ref[pl.ds(r0, blk), h * HEAD_DIM:(h + 1) * HEAD_DIM] for h in range(H)]
        k = [kx_ref[pl.ds(r0, 2 * blk), h * HEAD_DIM:(h + 1) * HEAD_DIM] for h in range(H)]
        s = [jnp.where(ok, lax.dot_general(q[h], k[h], dn, preferred_element_type=F32), NEG_BIG) for h in range(H)]
        m = [jnp.max(s[h], axis=-1, keepdims=True) for h in range(H)]
        p = [jnp.exp(s[h] - m[h]).astype(BF16) for h in range(H)]
        st = jnp.zeros((blk, LANES), F32)
        for h in range(H):
            v1 = vx_ref[pl.ds(r0, 2 * blk), 2 * h * HEAD_DIM:(2 * h + 2) * HEAD_DIM]
            acc = jnp.dot(p[h], v1, preferred_element_type=F32)
            l = acc[:, HEAD_DIM:]
            o_ref[pl.ds(r0, blk), h * HEAD_DIM:(h + 1) * HEAD_DIM] = (acc[:, :HEAD_DIM] / l).astype(o_ref.dtype)
            st = jnp.where(lane == h, m[h], st)
            st = jnp.where(lane == H + h, l, st)
        st_ref[pl.ds(r0, blk), :] = st
        return carry

    lax.fori_loop(0, QB, body, 0)


def dilated_attention(qkv, d):
    _, B, _, L, U = qkv.shape
    H = U // HEAD_DIM
    R = min(512, L)
    QB = R // ATTN_BLOCK

    def cur(kind):
        return pl.BlockSpec((None, None, None, R, U), lambda b, r, i: (kind, b, r, i, 0))

    def prev(kind):
        return pl.BlockSpec((None, None, None, ATTN_BLOCK, U),
                            lambda b, r, i: (kind, b, r, jnp.maximum(i * QB - 1, 0), 0))

    kern = functools.partial(_attn_kernel, QB=QB, H=H)
    return pl.pallas_call(
        kern,
        grid=(B, d, L // R),
        in_specs=[cur(0), cur(1), prev(1), cur(2), prev(2)],
        out_specs=[pl.BlockSpec((None, None, R, U), lambda b, r, i: (b, r, i, 0)),
                   pl.BlockSpec((None, None, R, LANES), lambda b, r, i: (b, r, i, 0))],
        out_shape=[jax.ShapeDtypeStruct((B, d, L, U), BF16),
                   jax.ShapeDtypeStruct((B, d, L, LANES), F32)],
        scratch_shapes=[pltpu.VMEM((R + ATTN_BLOCK, U), BF16), pltpu.VMEM((R + ATTN_BLOCK, 2 * U), BF16)],
        compiler_params=_cparams(("parallel", "parallel", "parallel")),
        name=f"dilated_attn_d{d}",
    )(qkv, qkv, qkv, qkv, qkv)


def _merge_kernel(o0_ref, o1_ref, o2_ref, s0_ref, s1_ref, s2_ref, out_ref, of_ref, sf_ref, *, H, dils):
    for g, (o_ref, s_ref) in enumerate(((o0_ref, s0_ref), (o1_ref, s1_ref), (o2_ref, s2_ref))):
        d = dils[g]
        n = sf_ref.shape[1] // d
        for r in range(d):
            rows = slice(None) if d == 1 else pl.ds(r, n, stride=d)
            sf_ref[g, rows, :] = s_ref[r]
            for h in range(H):
                of_ref[g, h, rows, :] = o_ref[r, :, h * HEAD_DIM:(h + 1) * HEAD_DIM].astype(F32)
    st = [sf_ref[0], sf_ref[1], sf_ref[2]]
    mx = jnp.maximum(jnp.maximum(st[0], st[1]), st[2])
    w = [pltpu.roll(s, LANES - H, 1) * jnp.exp(s - mx) for s in st]
    tot = w[0] + w[1] + w[2]
    coef = [x / tot for x in w]
    for h in range(H):
        hs = slice(h * HEAD_DIM, (h + 1) * HEAD_DIM)
        acc = coef[0][:, h:h + 1] * of_ref[0, h]
        acc += coef[1][:, h:h + 1] * of_ref[1, h]
        acc += coef[2][:, h:h + 1] * of_ref[2, h]
        out_ref[:, hs] = acc.astype(out_ref.dtype)


def merge_groups(outs, stats, dils):
    B, d0, L0, U = outs[0].shape
    S = d0 * L0
    H = U // HEAD_DIM
    tm = min(512, S)
    tiles_per_b = S // tm

    def ospec(d, w):
        return pl.BlockSpec((None, d, tm // d, w), lambda i: (i // tiles_per_b, 0, i % tiles_per_b, 0))

    return pl.pallas_call(
        functools.partial(_merge_kernel, H=H, dils=dils),
        grid=(B * S // tm,),
        in_specs=[ospec(d, U) for d in dils] + [ospec(d, LANES) for d in dils],
        out_specs=pl.BlockSpec((tm, U), lambda i: (i, 0)),
        out_shape=jax.ShapeDtypeStruct((B * S, U), BF16),
        scratch_shapes=[pltpu.VMEM((3, H, tm, HEAD_DIM), F32), pltpu.VMEM((3, tm, LANES), F32)],
        compiler_params=_cparams(("parallel",)),
        name="merge_groups",
    )(*outs, *stats)


def _conv_kernel(a0_ref, a1_ref, b0_ref, b1_ref, ha0_ref, ha1_ref, hb0_ref, hb1_ref,
                 w_ref, bias_ref, g_ref, be_ref, o_ref, u_ref, c_ref, sh_ref, *, ts, U):
    i = pl.program_id(1)
    halo = CONV_HALO
    for half, (a_ref, b_ref, ha_ref, hb_ref) in enumerate(((a0_ref, b0_ref, ha0_ref, hb0_ref),
                                                            (a1_ref, b1_ref, ha1_ref, hb1_ref))):
        cs = slice(half * U, (half + 1) * U)
        u_ref[halo:halo + ts, cs] = a_ref[...].astype(F32) * _sigmoid(b_ref[...].astype(F32))
        hu = ha_ref[...].astype(F32) * _sigmoid(hb_ref[...].astype(F32))
        u_ref[0:halo, cs] = jnp.where(i > 0, hu, 0.0)
    C = 2 * U
    rc = 64
    off = halo - (CONV_WIDTH - 1)

    sub = 8
    n_al = ts + halo - sub

    def chan_body(cc, carry):
        c0 = pl.multiple_of(cc * LANES, LANES)
        sh_ref[0] = u_ref[:, pl.ds(c0, LANES)]
        for b in range(1, sub):
            sh_ref[b, 0:n_al, :] = u_ref[b:b + n_al, pl.ds(c0, LANES)]
        for rb in range(ts // rc):
            acc = jnp.zeros((rc, LANES), F32) + bias_ref[:, pl.ds(c0, LANES)]
            for j in range(CONV_WIDTH):
                a, b = divmod(off + j, sub)
                r0 = rb * rc + a * sub
                acc += w_ref[j:j + 1, pl.ds(c0, LANES)] * sh_ref[b, r0:r0 + rc, :]
            c_ref[rb * rc:(rb + 1) * rc, pl.ds(c0, LANES)] = acc
        return carry

    lax.fori_loop(0, C // LANES, chan_body, 0)

    rn = 32

    def norm_body(rb, carry):
        r0 = pl.multiple_of(rb * rn, rn)
        v = c_ref[pl.ds(r0, rn), :]
        mu = jnp.mean(v, axis=-1, keepdims=True)
        dv = v - mu
        var = jnp.mean(dv * dv, axis=-1, keepdims=True)
        y = dv * lax.rsqrt(var + NORM_EPS) * g_ref[...] + be_ref[...]
        o_ref[pl.ds(r0, rn), :] = (y * _sigmoid(y)).astype(o_ref.dtype)
        return carry

    lax.fori_loop(0, ts // rn, norm_body, 0)


def conv_branch(proj, B, S, U, conv_dw, conv_bias, ln_gain, ln_bias):
    IN = proj.shape[1]
    C = 2 * U
    ts = min(256, S)
    pv = proj.reshape(B, S, IN)
    hb = ts // CONV_HALO
    cur = lambda blk: pl.BlockSpec((None, ts, U), lambda b, i, blk=blk: (b, i, blk))
    prv = lambda blk: pl.BlockSpec((None, CONV_HALO, U), lambda b, i, blk=blk: (b, jnp.maximum(i * hb - 1, 0), blk))
    vec = pl.BlockSpec((1, C), lambda b, i: (0, 0))
    out = pl.pallas_call(
        functools.partial(_conv_kernel, ts=ts, U=U),
        grid=(B, S // ts),
        in_specs=[cur(0), cur(1), cur(2), cur(3), prv(0), prv(1), prv(2), prv(3),
                  pl.BlockSpec((CONV_WIDTH, C), lambda b, i: (0, 0)), vec, vec, vec],
        out_specs=pl.BlockSpec((None, ts, C), lambda b, i: (b, i, 0)),
        out_shape=jax.ShapeDtypeStruct((B, S, C), BF16),
        scratch_shapes=[pltpu.VMEM((ts + CONV_HALO, C), F32), pltpu.VMEM((ts, C), F32),
                        pltpu.VMEM((8, ts + CONV_HALO, LANES), F32)],
        compiler_params=_cparams(("parallel", "parallel")),
        name="conv_branch",
    )(pv, pv, pv, pv, pv, pv, pv, pv, conv_dw, conv_bias.reshape(1, C), ln_gain.reshape(1, C), ln_bias.reshape(1, C))
    return out.reshape(B * S, C)


def _gateproj_kernel(cn_ref, am_ref, wc_ref, wa_ref, gc_ref, ga_ref, o_ref, conv_ref, attn_ref):
    @pl.when(pl.program_id(0) == 0)
    def _():
        conv_ref[...] = jnp.zeros_like(conv_ref)
        attn_ref[...] = jnp.zeros_like(attn_ref)

    z = _sigmoid(gc_ref[...].astype(F32)) * conv_ref[...] + _sigmoid(ga_ref[...].astype(F32)) * attn_ref[...]
    o_ref[...] = z.astype(o_ref.dtype)
    conv_ref[...] = jnp.dot(cn_ref[...], wc_ref[...], preferred_element_type=F32)
    attn_ref[...] = jnp.dot(am_ref[...], wa_ref[...], preferred_element_type=F32)


def gate_proj(cn, am, wc, wa, proj, U):
    N, C = cn.shape
    D = wc.shape[1]
    tn = U
    tm = min(512, N)
    n_i = N // tm
    last = (D // tn) * n_i - 1

    def cur(s):
        p = jnp.minimum(s, last)
        return p // n_i, p % n_i

    def fin(s):
        p = jnp.maximum(s - 1, 0)
        return p // n_i, p % n_i

    return pl.pallas_call(
        _gateproj_kernel,
        grid=(last + 2,),
        in_specs=[pl.BlockSpec((tm, C), lambda s: (cur(s)[1], 0)),
                  pl.BlockSpec((tm, U), lambda s: (cur(s)[1], 0)),
                  pl.BlockSpec((C, tn), lambda s: (0, cur(s)[0])),
                  pl.BlockSpec((U, tn), lambda s: (0, cur(s)[0])),
                  pl.BlockSpec((tm, tn), lambda s: (fin(s)[1], 4 + fin(s)[0])),
                  pl.BlockSpec((tm, tn), lambda s: (fin(s)[1], 8 + fin(s)[0]))],
        out_specs=pl.BlockSpec((tm, tn), lambda s: (fin(s)[1], fin(s)[0])),
        out_shape=jax.ShapeDtypeStruct((N, D), BF16),
        scratch_shapes=[pltpu.VMEM((tm, tn), F32), pltpu.VMEM((tm, tn), F32)],
        compiler_params=_cparams(("arbitrary",)),
        name="gate_proj",
    )(cn, am, wc, wa, proj, proj)


def _router_kernel(y_ref, xin_ref, mod_ref, gpost_ref, gain_ref, r_ref,
                   x1_ref, hp_ref, eid_ref, wt_ref, rank_ref, cnt_ref, carry_ref, *, tr):
    step = pl.program_id(0)

    @pl.when(step == 0)
    def _():
        carry_ref[...] = jnp.zeros_like(carry_ref)

    y = y_ref[...].astype(F32)
    yms = jnp.mean(y * y, axis=-1, keepdims=True)
    x = xin_ref[...] + mod_ref[2:3, :] * (y * lax.rsqrt(yms + NORM_EPS) * gpost_ref[...])
    x1_ref[...] = x
    D = x.shape[1]
    ms = jnp.mean(x * x, axis=-1, keepdims=True)
    h = x * lax.rsqrt(ms + NORM_EPS) * gain_ref[...]
    h = h * (1.0 + mod_ref[4:5, :]) + mod_ref[3:4, :]
    lo = h[:, :D // 2]
    hi = h[:, D // 2:]
    hp_ref[...] = _pack_halves(lo, hi)
    logits = (jnp.dot(lo.astype(BF16), r_ref[:D // 2, :], preferred_element_type=F32)
              + jnp.dot(hi.astype(BF16), r_ref[D // 2:, :], preferred_element_type=F32))
    lane = lax.broadcasted_iota(I32, logits.shape, 1)
    G = N_EXPERT_GROUPS
    is_g = lane < G
    gl = jnp.where(is_g, logits, NEG_BIG)
    gmax = jnp.max(gl, axis=-1, keepdims=True)
    grp = jnp.min(jnp.where(gl == gmax, lane, LANES), axis=-1, keepdims=True)
    p_grp = 1.0 / jnp.sum(jnp.where(is_g, jnp.exp(gl - gmax), 0.0), axis=-1, keepdims=True)
    in_grp = (lane >= G) & (lane < G + N_EXPERTS) & (((lane - G) // EXPERTS_PER_GROUP) == grp)
    el = jnp.where(in_grp, logits, NEG_BIG)
    v0 = jnp.max(el, axis=-1, keepdims=True)
    i0 = jnp.min(jnp.where(in_grp & (el == v0), lane, LANES), axis=-1, keepdims=True)
    in2 = in_grp & (lane != i0)
    el2 = jnp.where(in2, logits, NEG_BIG)
    v1 = jnp.max(el2, axis=-1, keepdims=True)
    i1 = jnp.min(jnp.where(in2 & (el2 == v1), lane, LANES), axis=-1, keepdims=True)
    e1 = jnp.exp(v1 - v0)
    w0 = p_grp / (1.0 + e1)
    w1 = p_grp * e1 / (1.0 + e1)
    ex0 = i0 - G
    ex1 = i1 - G
    eid_ref[...] = jnp.where(lane == 0, ex0, jnp.where(lane == 1, ex1, 0))
    wt_ref[...] = jnp.where(lane == 0, w0, jnp.where(lane == 1, w1, 0.0))
    oh0 = (lane == ex0).astype(F32)
    oh1 = (lane == ex1).astype(F32)
    both = oh0 + oh1
    rr = lax.broadcasted_iota(I32, (tr, tr), 0)
    cc = lax.broadcasted_iota(I32, (tr, tr), 1)
    tril = (cc < rr).astype(BF16)
    before = jnp.dot(tril, both.astype(BF16), preferred_element_type=F32) + carry_ref[0:1, :]
    rk0 = jnp.sum(before * oh0, axis=-1, keepdims=True)
    rk1 = jnp.sum(before * oh1, axis=-1, keepdims=True)
    rank_ref[...] = jnp.where(lane == 0, rk0, jnp.where(lane == 1, rk1, 0.0))
    newc = carry_ref[0:1, :] + jnp.sum(both, axis=0, keepdims=True)
    carry_ref[...] = jnp.broadcast_to(newc, carry_ref.shape)
    cnt_ref[...] = jnp.broadcast_to(newc, cnt_ref.shape)


def residual_prenorm_router(y, x2d, mod, gain_post, gain, rcat, S):
    N, D = x2d.shape
    tr = min(256, S)
    tiles_per_b = S // tr
    lane_spec = pl.BlockSpec((tr, LANES), lambda i: (i, 0))
    row_spec = pl.BlockSpec((tr, D), lambda i: (i, 0))
    vec_spec = pl.BlockSpec((1, D), lambda i: (0, 0))
    return pl.pallas_call(
        functools.partial(_router_kernel, tr=tr),
        grid=(N // tr,),
        in_specs=[row_spec, row_spec,
                  pl.BlockSpec((None, 6, D), lambda i: (i // tiles_per_b, 0, 0)),
                  vec_spec, vec_spec,
                  pl.BlockSpec((D, LANES), lambda i: (0, 0))],
        out_specs=[row_spec, pl.BlockSpec((tr, D // 2), lambda i: (i, 0)), lane_spec, lane_spec, lane_spec,
                   pl.BlockSpec((8, LANES), lambda i: (0, 0))],
        out_shape=[jax.ShapeDtypeStruct((N, D), F32),
                   jax.ShapeDtypeStruct((N, D // 2), U32),
                   jax.ShapeDtypeStruct((N, LANES), I32),
                   jax.ShapeDtypeStruct((N, LANES), F32),
                   jax.ShapeDtypeStruct((N, LANES), F32),
                   jax.ShapeDtypeStruct((8, LANES), F32)],
        scratch_shapes=[pltpu.VMEM((8, LANES), F32)],
        compiler_params=_cparams(("arbitrary",)),
        name="residual_prenorm_router",
    )(y, x2d, mod, gain_post.reshape(1, D), gain.reshape(1, D), rcat)


def _dest_kernel(eid_ref, rank_ref, start_ref, o_ref):
    lane = lax.broadcasted_iota(I32, eid_ref.shape, 1)
    eid = eid_ref[...]
    rank = rank_ref[...]
    start = start_ref[0:1, :]
    d = []
    for k in range(2):
        oh = (lane == eid[:, k:k + 1]).astype(F32)
        d.append(jnp.sum(oh * start, axis=-1, keepdims=True) + rank[:, k:k + 1])
    o_ref[...] = jnp.where(lane == 0, d[0], jnp.where(lane == 1, d[1], 0.0)).astype(I32)


def dest_rows(eid, rank, pad_start):
    N = eid.shape[0]
    tr = min(1024, N)
    spec = pl.BlockSpec((tr, LANES), lambda i: (i, 0))
    return pl.pallas_call(
        _dest_kernel,
        grid=(N // tr,),
        in_specs=[spec, spec, pl.BlockSpec((8, LANES), lambda i: (0, 0))],
        out_specs=spec,
        out_shape=jax.ShapeDtypeStruct((N, LANES), I32),
        compiler_params=_cparams(("parallel",)),
        name="dest_rows",
    )(eid, rank, pad_start)


def _scatter_kernel(dest_ref, h_ref, xs_in_ref, xs_ref, sem, *, T):
    del xs_in_ref

    def issue(t, carry):
        for k in range(2):
            d = dest_ref[0, 0, 2 * t + k]
            pltpu.make_async_copy(h_ref.at[pl.ds(t, 1)], xs_ref.at[pl.ds(d, 1)], sem).start()
        return carry

    lax.fori_loop(0, T, issue, 0, unroll=4)
    for k in range(2):
        pltpu.make_async_copy(h_ref, xs_ref.at[pl.ds(0, T)], sem).wait()


def scatter_rows(hp, dest, P):
    N, W = hp.shape
    T = min(256, N)
    dest_s = dest[:, :2].reshape(N // T, 1, 2 * T)
    xs0 = jnp.zeros((P, W), hp.dtype)
    return pl.pallas_call(
        functools.partial(_scatter_kernel, T=T),
        grid=(N // T,),
        in_specs=[pl.BlockSpec((1, 1, 2 * T), lambda i: (i, 0, 0), memory_space=pltpu.SMEM),
                  pl.BlockSpec((T, W), lambda i: (i, 0)),
                  pl.BlockSpec(memory_space=pl.ANY)],
        out_specs=pl.BlockSpec(memory_space=pl.ANY),
        out_shape=jax.ShapeDtypeStruct((P, W), hp.dtype),
        scratch_shapes=[pltpu.SemaphoreType.DMA(())],
        input_output_aliases={2: 0},
        compiler_params=_cparams(("arbitrary",)),
        name="scatter_rows",
    )(dest_s, hp, xs0)


def _expert_chunks(st_ref, nc_ref, in_hbm, out_hbm, ibuf, obuf, isem, osem, compute):
    e = pl.program_id(0)
    n_e = pl.num_programs(0)
    n = nc_ref[e]
    TM = ibuf.shape[1]
    base = pl.multiple_of(st_ref[e], TM)
    g0 = base // TM

    def icopy(row, slot):
        return pltpu.make_async_copy(in_hbm.at[pl.ds(row, TM)], ibuf.at[slot], isem.at[slot])

    def ocopy(row, slot):
        return pltpu.make_async_copy(obuf.at[slot], out_hbm.at[pl.ds(row, TM)], osem.at[slot])

    @pl.when((e == 0) & (n > 0))
    def _():
        icopy(base, 0).start()

    def body(c, carry):
        g = g0 + c
        slot = g % 2
        row = pl.multiple_of(base + c * TM, TM)

        @pl.when(c + 1 < n)
        def _():
            icopy(row + TM, 1 - slot).start()

        icopy(row, slot).wait()

        @pl.when(g >= 2)
        def _():
            ocopy(row, slot).wait()

        compute(ibuf.at[slot], obuf.at[slot])
        ocopy(row, slot).start()
        return carry

    lax.fori_loop(0, n, body, 0)

    @pl.when(e + 1 < n_e)
    def _():
        @pl.when(nc_ref[e + 1] > 0)
        def _():
            icopy(pl.multiple_of(st_ref[e + 1], TM), (g0 + n) % 2).start()

    @pl.when(e == n_e - 1)
    def _():
        g_end = g0 + n

        @pl.when(g_end >= 2)
        def _():
            ocopy(base, g_end % 2).wait()

        @pl.when(g_end >= 1)
        def _():
            ocopy(base, (g_end + 1) % 2).wait()

        used = pl.multiple_of(base + n * TM, TM)
        n_tail = (out_hbm.shape[0] - used) // TM
        obuf[0] = jnp.zeros(obuf.shape[1:], obuf.dtype)

        def zstart(c, carry):
            ocopy(pl.multiple_of(used + c * TM, TM), 0).start()
            return carry

        def zwait(c, carry):
            ocopy(used, 0).wait()
            return carry

        lax.fori_loop(0, n_tail, zstart, 0)
        lax.fori_loop(0, n_tail, zwait, 0)


WEIGHT_DMA_PRIORITY = 1


def _stream_expert_weights(w_hbms, w_bufs, wsem):
    e = pl.program_id(0)
    n_e = pl.num_programs(0)
    slot = e % 2

    def copy(k, ei, s):
        return pltpu.make_async_copy(w_hbms[k].at[ei], w_bufs[k].at[s], wsem.at[k, s])

    @pl.when(e == 0)
    def _():
        for k in range(len(w_hbms)):
            copy(k, 0, 0).start(priority=WEIGHT_DMA_PRIORITY)

    @pl.when(e + 1 < n_e)
    def _():
        for k in range(len(w_hbms)):
            copy(k, e + 1, 1 - slot).start(priority=WEIGHT_DMA_PRIORITY)

    for k in range(len(w_hbms)):
        copy(k, e, slot).wait()
    return slot


def _moe_up_kernel(st_ref, nc_ref, xs_hbm, w1_hbm, w3_hbm, o_hbm, wb_ref, w1_buf, w3_buf, ibuf, obuf,
                   wsem, isem, osem, *, F):
    slot = _stream_expert_weights((w1_hbm, w3_hbm), (w1_buf, w3_buf), wsem)

    @pl.when(nc_ref[pl.program_id(0)] > 0)
    def _():
        wb_ref[:, :F] = w1_buf[slot].astype(BF16)
        wb_ref[:, F:] = w3_buf[slot].astype(BF16)

    def compute(x_ref, o_ref):
        lo, hi = _unpack_halves(x_ref[...])
        half = wb_ref.shape[0] // 2
        hcat = (jnp.dot(lo.astype(BF16), wb_ref[:half, :], preferred_element_type=F32)
                + jnp.dot(hi.astype(BF16), wb_ref[half:, :], preferred_element_type=F32))
        a = hcat[:, :F]
        o_ref[...] = (a * _sigmoid(a) * hcat[:, F:]).astype(o_ref.dtype)

    _expert_chunks(st_ref, nc_ref, xs_hbm, o_hbm, ibuf, obuf, isem, osem, compute)


def _moe_down_kernel(st_ref, nc_ref, h_hbm, w2_hbm, o_hbm, wb_ref, w2_buf, ibuf, obuf, wsem, isem, osem):
    slot = _stream_expert_weights((w2_hbm,), (w2_buf,), wsem)

    @pl.when(nc_ref[pl.program_id(0)] > 0)
    def _():
        wb_ref[...] = w2_buf[slot].astype(BF16)

    def compute(h_ref, o_ref):
        y = jnp.dot(h_ref[...], wb_ref[...], preferred_element_type=F32)
        half = y.shape[1] // 2
        o_ref[...] = _pack_halves(y[:, :half], y[:, half:])

    _expert_chunks(st_ref, nc_ref, h_hbm, o_hbm, ibuf, obuf, isem, osem, compute)


def expert_ffn(xs, seg_start, seg_chunks, w1, w3, w2):
    P, Wp = xs.shape
    E, D, F = w1.shape
    TM = MOE_ROWS
    any_spec = pl.BlockSpec(memory_space=pl.ANY)
    dma2 = pltpu.SemaphoreType.DMA((2,))
    hmid = pl.pallas_call(
        functools.partial(_moe_up_kernel, F=F),
        grid_spec=pltpu.PrefetchScalarGridSpec(
            num_scalar_prefetch=2,
            grid=(E,),
            in_specs=[any_spec, any_spec, any_spec],
            out_specs=any_spec,
            scratch_shapes=[pltpu.VMEM((D, 2 * F), BF16), pltpu.VMEM((2, D, F), F32), pltpu.VMEM((2, D, F), F32),
                            pltpu.VMEM((2, TM, Wp), U32), pltpu.VMEM((2, TM, F), BF16),
                            pltpu.SemaphoreType.DMA((2, 2)), dma2, dma2]),
        out_shape=jax.ShapeDtypeStruct((P, F), BF16),
        compiler_params=_cparams(("arbitrary",)),
        name="moe_up",
    )(seg_start, seg_chunks, xs, w1, w3)
    return pl.pallas_call(
        _moe_down_kernel,
        grid_spec=pltpu.PrefetchScalarGridSpec(
            num_scalar_prefetch=2,
            grid=(E,),
            in_specs=[any_spec, any_spec],
            out_specs=any_spec,
            scratch_shapes=[pltpu.VMEM((F, D), BF16), pltpu.VMEM((2, F, D), F32), pltpu.VMEM((2, TM, F), BF16),
                            pltpu.VMEM((2, TM, D // 2), U32), pltpu.SemaphoreType.DMA((1, 2)), dma2, dma2]),
        out_shape=jax.ShapeDtypeStruct((P, D // 2), U32),
        compiler_params=_cparams(("arbitrary",)),
        name="moe_down",
    )(seg_start, seg_chunks, hmid, w2)


def _combine_kernel(pos_ref, nxt_ref, wt_ref, x_ref, mod_ref, gain_ref, y_hbm, o_ref, buf_ref, sem, *, T):
    i = pl.program_id(0)
    n = pl.num_programs(0)
    slot = i % 2

    def gather(p_ref, s):
        def issue(t, carry):
            for k in range(2):
                p = p_ref[0, 0, 2 * t + k]
                pltpu.make_async_copy(y_hbm.at[pl.ds(p, 1)], buf_ref.at[s, k, pl.ds(t, 1)], sem.at[s]).start()
            return carry

        lax.fori_loop(0, T, issue, 0, unroll=4)

    @pl.when(i == 0)
    def _():
        gather(pos_ref, 0)

    @pl.when(i + 1 < n)
    def _():
        gather(nxt_ref, 1 - slot)

    for k in range(2):
        pltpu.make_async_copy(y_hbm.at[pl.ds(0, T)], buf_ref.at[slot, k], sem.at[slot]).wait()

    wt = wt_ref[...]
    w0 = wt[:, 0:1]
    w1 = wt[:, 1:2]
    lo0, hi0 = _unpack_halves(buf_ref[slot, 0])
    lo1, hi1 = _unpack_halves(buf_ref[slot, 1])
    ylo = w0 * lo0 + w1 * lo1
    yhi = w0 * hi0 + w1 * hi1
    D = x_ref.shape[1]
    half = D // 2
    ms = (jnp.sum(ylo * ylo, axis=-1, keepdims=True) + jnp.sum(yhi * yhi, axis=-1, keepdims=True)) / D
    inv = lax.rsqrt(ms + NORM_EPS)
    o_ref[:, :half] = x_ref[:, :half] + mod_ref[5:6, :half] * (ylo * inv * gain_ref[:, :half])
    o_ref[:, half:] = x_ref[:, half:] + mod_ref[5:6, half:] * (yhi * inv * gain_ref[:, half:])


def combine(yp, dest, wts, x1, mod, gain, S):
    N, D = x1.shape
    T = min(128, S)
    tiles_per_b = S // T
    pos_s = dest[:, :2].reshape(N // T, 1, 2 * T)
    n_tiles = N // T
    return pl.pallas_call(
        functools.partial(_combine_kernel, T=T),
        grid=(n_tiles,),
        in_specs=[pl.BlockSpec((1, 1, 2 * T), lambda i: (i, 0, 0), memory_space=pltpu.SMEM),
                  pl.BlockSpec((1, 1, 2 * T), lambda i: (jnp.minimum(i + 1, n_tiles - 1), 0, 0),
                               memory_space=pltpu.SMEM),
                  pl.BlockSpec((T, LANES), lambda i: (i, 0)),
                  pl.BlockSpec((T, D), lambda i: (i, 0)),
                  pl.BlockSpec((None, 6, D), lambda i: (i // tiles_per_b, 0, 0)),
                  pl.BlockSpec((1, D), lambda i: (0, 0)),
                  pl.BlockSpec(memory_space=pl.ANY)],
        out_specs=pl.BlockSpec((T, D), lambda i: (i, 0)),
        out_shape=jax.ShapeDtypeStruct((N, D), F32),
        scratch_shapes=[pltpu.VMEM((2, 2, T, D // 2), U32), pltpu.SemaphoreType.DMA((2,))],
        compiler_params=_cparams(("arbitrary",)),
        name="combine",
    )(pos_s, pos_s, wts, x1, mod, gain.reshape(1, D), yp)


def _moe_layout(counts):
    TM = MOE_ROWS
    chunks = (counts.astype(I32) + TM - 1) // TM
    start = (jnp.cumsum(chunks) - chunks) * TM
    return start, chunks


def kernel(x, c, positions, ada_w, ada_b, mix_norm_pre, mix_norm_post, w_in, conv_dw, conv_dw_bias, conv_ln_gain, conv_ln_bias, w_conv_out, w_attn_out, w_out, ffn_norm_pre, ffn_norm_post, router_group, router_expert, expert_w1, expert_w3, expert_w2):
    B, S, D = x.shape
    N = B * S
    U = D // 4
    depth = ada_w.shape[0]
    xc = x.reshape(N, D)
    for layer in range(depth):
        mod = ada_mod(c, ada_w[layer], ada_b[layer]).reshape(B, 6, D)
        h, cosf, sinf = prenorm_rope(xc, mod, mix_norm_pre[layer], positions, S)
        w_in_b = w_in[layer].astype(BF16)
        outs, stats = [], []
        for g, (window, d) in enumerate(ATTN_PATTERNS):
            assert window // d == ATTN_BLOCK and S % (d * ATTN_BLOCK) == 0
            qkv = qkv_proj(h, w_in_b, cosf, sinf, B, S, U, g, d)
            o, st = dilated_attention(qkv, d)
            outs.append(o)
            stats.append(st)
        am = merge_groups(outs, stats, tuple(d for _, d in ATTN_PATTERNS))
        proj = matmul_cols(h, w_in_b, U, 9, "rest_proj")
        cn = conv_branch(proj, B, S, U, conv_dw[layer], conv_dw_bias[layer], conv_ln_gain[layer], conv_ln_bias[layer])
        z = gate_proj(cn, am, w_conv_out[layer].astype(BF16), w_attn_out[layer].astype(BF16), proj, U)
        y = matmul_cols(z, w_out[layer].astype(BF16), U, 0, "out_proj")
        rcat = jnp.zeros((D, LANES), F32)
        rcat = rcat.at[:, :N_EXPERT_GROUPS].set(router_group[layer])
        rcat = rcat.at[:, N_EXPERT_GROUPS:N_EXPERT_GROUPS + N_EXPERTS].set(router_expert[layer]).astype(BF16)
        x1, hp, eid, wts, rank, cnt = residual_prenorm_router(
            y, xc, mod, mix_norm_post[layer], ffn_norm_pre[layer], rcat, S)
        TM = MOE_ROWS
        P = (2 * N + N_EXPERTS * (TM - 1)) // TM * TM
        seg_start, seg_chunks = _moe_layout(cnt[0, :N_EXPERTS])
        start_row = jnp.zeros((8, LANES), F32).at[:, :N_EXPERTS].set(seg_start.astype(F32)[None, :])
        dest = dest_rows(eid, rank, start_row)
        xs = scatter_rows(hp, dest, P)
        yp = expert_ffn(xs, seg_start, seg_chunks, expert_w1[layer], expert_w3[layer], expert_w2[layer])
        xc = combine(yp, dest, wts, x1, mod, ffn_norm_post[layer], S)
    return xc.reshape(B, S, D)
```

```python
import functools

import jax
import jax.numpy as jnp
from jax import lax
from jax.experimental import pallas as pl
from jax.experimental.pallas import tpu as pltpu

F32 = jnp.float32
BF16 = jnp.bfloat16
I32 = jnp.int32
U32 = jnp.uint32

HEAD_DIM = 128
LANES = 128
ATTN_BLOCK = 128
ATTN_PATTERNS = ((128, 1), (512, 4), (2048, 16))
ROPE_THETA = 10000.0
CONV_WIDTH = 31
CONV_HALO = 32
N_EXPERT_GROUPS = 8
EXPERTS_PER_GROUP = 8
N_EXPERTS = 64
NORM_EPS = 1e-6
NEG_BIG = -1e30
MOE_ROWS = 256
V7X_VMEM_LIMIT = 60 * 1024 * 1024


def _cparams(sem):
    return pltpu.CompilerParams(dimension_semantics=sem, vmem_limit_bytes=V7X_VMEM_LIMIT)


def _sigmoid(x):
    return 1.0 / (1.0 + jnp.exp(-x))


def _pack_halves(lo, hi):
    lo_b = lax.bitcast_convert_type(lo.astype(BF16).astype(F32), U32) >> 16
    hi_b = lax.bitcast_convert_type(hi.astype(BF16).astype(F32), U32) & jnp.uint32(0xFFFF0000)
    return hi_b | lo_b


def _unpack_halves(w):
    lo = lax.bitcast_convert_type(w << 16, F32)
    hi = lax.bitcast_convert_type(w & jnp.uint32(0xFFFF0000), F32)
    return lo, hi


def _ada_kernel(c_ref, w_ref, b_ref, o_ref):
    c = c_ref[...]
    cact = (c * _sigmoid(c)).astype(BF16)
    o_ref[...] = jnp.dot(cact, w_ref[...].astype(BF16), preferred_element_type=F32) + b_ref[...]


def ada_mod(c, ada_w, ada_b):
    B, D = c.shape
    W = ada_w.shape[1]
    rows = 8
    cp = jnp.zeros((rows, D), F32).at[:B].set(c)
    tn = min(512, W)
    out = pl.pallas_call(
        _ada_kernel,
        grid=(W // tn,),
        in_specs=[pl.BlockSpec((rows, D), lambda j: (0, 0)),
                  pl.BlockSpec((D, tn), lambda j: (0, j)),
                  pl.BlockSpec((1, tn), lambda j: (0, j))],
        out_specs=pl.BlockSpec((rows, tn), lambda j: (0, j)),
        out_shape=jax.ShapeDtypeStruct((rows, W), F32),
        compiler_params=_cparams(("parallel",)),
        name="ada_mod",
    )(cp, ada_w, ada_b.reshape(1, W))
    return out[:B]


def _prenorm_kernel(x_ref, mod_ref, gain_ref, pos_ref, invf_ref, h_ref, cos_ref, sin_ref):
    x = x_ref[...]
    ms = jnp.mean(x * x, axis=-1, keepdims=True)
    y = x * lax.rsqrt(ms + NORM_EPS) * gain_ref[...]
    y = y * (1.0 + mod_ref[1:2, :]) + mod_ref[0:1, :]
    h_ref[...] = y.astype(BF16)
    ang = pos_ref[...].astype(F32) * invf_ref[...]
    lane = lax.broadcasted_iota(I32, ang.shape, 1)
    sn = jnp.sin(ang)
    cos_ref[...] = jnp.cos(ang)
    sin_ref[...] = jnp.where(lane < HEAD_DIM // 2, -sn, sn)


def prenorm_rope(x2d, mod, gain, positions, S):
    N, D = x2d.shape
    tr = min(256, S)
    tiles_per_b = S // tr
    half = HEAD_DIM // 2
    inv = ROPE_THETA ** (-jnp.arange(0, HEAD_DIM, 2, dtype=F32) / HEAD_DIM)
    invf = jnp.concatenate([inv, inv]).reshape(1, HEAD_DIM)
    del half
    return pl.pallas_call(
        _prenorm_kernel,
        grid=(N // tr,),
        in_specs=[pl.BlockSpec((tr, D), lambda i: (i, 0)),
                  pl.BlockSpec((None, 6, D), lambda i: (i // tiles_per_b, 0, 0)),
                  pl.BlockSpec((1, D), lambda i: (0, 0)),
                  pl.BlockSpec((tr, 1), lambda i: (i, 0)),
                  pl.BlockSpec((1, HEAD_DIM), lambda i: (0, 0))],
        out_specs=[pl.BlockSpec((tr, D), lambda i: (i, 0)),
                   pl.BlockSpec((tr, HEAD_DIM), lambda i: (i, 0)),
                   pl.BlockSpec((tr, HEAD_DIM), lambda i: (i, 0))],
        out_shape=[jax.ShapeDtypeStruct((N, D), BF16),
                   jax.ShapeDtypeStruct((N, HEAD_DIM), F32),
                   jax.ShapeDtypeStruct((N, HEAD_DIM), F32)],
        compiler_params=_cparams(("parallel",)),
        name="prenorm_rope",
    )(x2d, mod, gain.reshape(1, D), positions.reshape(N, 1), invf)


def _residue_rows(ref, r, d):
    if d == 1:
        return ref[...]
    return ref[pl.ds(r, ref.shape[0] // d, stride=d), :]


def _qkv_kernel(a_ref, w_ref, cos_ref, sin_ref, o_ref, acc_ref, *, d, heads, scale, n_i):
    s = pl.program_id(0)

    @pl.when(s == 0)
    def _():
        acc_ref[...] = jnp.zeros_like(acc_ref)

    kind = jnp.maximum(s - 1, 0) // n_i
    sc = jnp.where(kind == 0, scale, 1.0).astype(F32)
    is_v = kind == 2
    for r in range(d):
        c = jnp.where(is_v, 1.0, _residue_rows(cos_ref, r, d) * sc)
        sn = jnp.where(is_v, 0.0, _residue_rows(sin_ref, r, d) * sc)
        for h in range(heads):
            t = _residue_rows(acc_ref.at[h], r, d)
            o_ref[r, :, h * HEAD_DIM:(h + 1) * HEAD_DIM] = (
                t * c + pltpu.roll(t, HEAD_DIM // 2, 1) * sn).astype(o_ref.dtype)
    acc = jnp.dot(a_ref[...], w_ref[...], preferred_element_type=F32)
    for h in range(heads):
        acc_ref[h] = acc[:, h * HEAD_DIM:(h + 1) * HEAD_DIM]


def qkv_proj(h, w_bf16, cosf, sinf, B, S, U, g, d):
    N, D = h.shape
    tm = min(1024, S)
    tiles_per_b = S // tm
    n_i = N // tm
    last = 3 * n_i - 1
    kern = functools.partial(_qkv_kernel, d=d, heads=U // HEAD_DIM, scale=HEAD_DIM ** -0.5, n_i=n_i)

    def cur(s):
        return jnp.minimum(s, last)

    def fin(s):
        return jnp.maximum(s - 1, 0)

    return pl.pallas_call(
        kern,
        grid=(3 * n_i + 1,),
        in_specs=[pl.BlockSpec((tm, D), lambda s: (cur(s) % n_i, 0)),
                  pl.BlockSpec((D, U), lambda s: (0, (cur(s) // n_i) * 3 + g)),
                  pl.BlockSpec((tm, HEAD_DIM), lambda s: (fin(s) % n_i, 0)),
                  pl.BlockSpec((tm, HEAD_DIM), lambda s: (fin(s) % n_i, 0))],
        out_specs=pl.BlockSpec((None, None, d, tm // d, U),
                               lambda s: (fin(s) // n_i, (fin(s) % n_i) // tiles_per_b, 0,
                                          (fin(s) % n_i) % tiles_per_b, 0)),
        out_shape=jax.ShapeDtypeStruct((3, B, d, S // d, U), BF16),
        scratch_shapes=[pltpu.VMEM((U // HEAD_DIM, tm, HEAD_DIM), F32)],
        compiler_params=_cparams(("arbitrary",)),
        name=f"qkv_proj_d{d}",
    )(h, w_bf16, cosf, sinf)


def _matmul_kernel(a_ref, w_ref, o_ref):
    o_ref[...] = jnp.dot(a_ref[...], w_ref[...], preferred_element_type=F32).astype(o_ref.dtype)


def matmul_cols(a, w_bf16, U, first_tile, name):
    N, K = a.shape
    n_tiles = w_bf16.shape[1] // U - first_tile
    tm = min(1024, N)
    return pl.pallas_call(
        _matmul_kernel,
        grid=(n_tiles, N // tm),
        in_specs=[pl.BlockSpec((tm, K), lambda j, i: (i, 0)),
                  pl.BlockSpec((K, U), lambda j, i: (0, first_tile + j))],
        out_specs=pl.BlockSpec((tm, U), lambda j, i: (i, j)),
        out_shape=jax.ShapeDtypeStruct((N, n_tiles * U), BF16),
        compiler_params=_cparams(("parallel", "parallel")),
        name=name,
    )(a, w_bf16)


def _attn_kernel(q_ref, kc_ref, kp_ref, vc_ref, vp_ref, o_ref, st_ref, kx_ref, vx_ref, *, QB, H):
    i = pl.program_id(2)
    blk = ATTN_BLOCK
    kx_ref[0:blk, :] = kp_ref[...]
    kx_ref[blk:, :] = kc_ref[...]
    ones = jnp.ones((vx_ref.shape[0], HEAD_DIM), vx_ref.dtype)
    for h in range(H):
        hs = slice(h * HEAD_DIM, (h + 1) * HEAD_DIM)
        vx_ref[0:blk, 2 * h * HEAD_DIM:(2 * h + 1) * HEAD_DIM] = vp_ref[:, hs]
        vx_ref[blk:, 2 * h * HEAD_DIM:(2 * h + 1) * HEAD_DIM] = vc_ref[:, hs]
        vx_ref[:, (2 * h + 1) * HEAD_DIM:(2 * h + 2) * HEAD_DIM] = ones
    rows = lax.broadcasted_iota(I32, (blk, 2 * blk), 0)
    keys = lax.broadcasted_iota(I32, (blk, 2 * blk), 1)
    band = (keys >= rows) & (keys - blk <= rows)
    lane = lax.broadcasted_iota(I32, (blk, LANES), 1)
    dn = (((1,), (1,)), ((), ()))

    def body(qb, carry):
        r0 = pl.multiple_of(qb * blk, blk)
        ok = band & ((keys >= blk) | (i * QB + qb > 0))
        q = [q_ref[pl.ds(r0, blk), h * HEAD_DIM:(h + 1) * HEAD_DIM] for h in range(H)]
        k = [kx_ref[pl.ds(r0, 2 * blk), h * HEAD_DIM:(h + 1) * HEAD_DIM] for h in range(H)]
        s = [jnp.where(ok, lax.dot_general(q[h], k[h], dn, preferred_element_type=F32), NEG_BIG) for h in range(H)]
        m = [jnp.max(s[h], axis=-1, keepdims=True) for h in range(H)]
        p = [jnp.exp(s[h] - m[h]).astype(BF16) for h in range(H)]
        st = jnp.zeros((blk, LANES), F32)
        for h in range(H):
            v1 = vx_ref[pl.ds(r0, 2 * blk), 2 * h * HEAD_DIM:(2 * h + 2) * HEAD_DIM]
            acc = jnp.dot(p[h], v1, preferred_element_type=F32)
            l = acc[:, HEAD_DIM:]
            o_ref[pl.ds(r0, blk), h * HEAD_DIM:(h + 1) * HEAD_DIM] = (acc[:, :HEAD_DIM] / l).astype(o_ref.dtype)
            st = jnp.where(lane == h, m[h], st)
            st = jnp.where(lane == H + h, l, st)
        st_ref[pl.ds(r0, blk), :] = st
        return carry

    lax.fori_loop(0, QB, body, 0)


def dilated_attention(qkv, d):
    _, B, _, L, U = qkv.shape
    H = U // HEAD_DIM
    R = min(512, L)
    QB = R // ATTN_BLOCK

    def cur(kind):
        return pl.BlockSpec((None, None, None, R, U), lambda b, r, i: (kind, b, r, i, 0))

    def prev(kind):
        return pl.BlockSpec((None, None, None, ATTN_BLOCK, U),
                            lambda b, r, i: (kind, b, r, jnp.maximum(i * QB - 1, 0), 0))

    kern = functools.partial(_attn_kernel, QB=QB, H=H)
    return pl.pallas_call(
        kern,
        grid=(B, d, L // R),
        in_specs=[cur(0), cur(1), prev(1), cur(2), prev(2)],
        out_specs=[pl.BlockSpec((None, None, R, U), lambda b, r, i: (b, r, i, 0)),
                   pl.BlockSpec((None, None, R, LANES), lambda b, r, i: (b, r, i, 0))],
        out_shape=[jax.ShapeDtypeStruct((B, d, L, U), BF16),
                   jax.ShapeDtypeStruct((B, d, L, LANES), F32)],
        scratch_shapes=[pltpu.VMEM((R + ATTN_BLOCK, U), BF16), pltpu.VMEM((R + ATTN_BLOCK, 2 * U), BF16)],
        compiler_params=_cparams(("parallel", "parallel", "parallel")),
        name=f"dilated_attn_d{d}",
    )(qkv, qkv, qkv, qkv, qkv)


def _merge_kernel(o0_ref, o1_ref, o2_ref, s0_ref, s1_ref, s2_ref, out_ref, of_ref, sf_ref, *, H, dils):
    for g, (o_ref, s_ref) in enumerate(((o0_ref, s0_ref), (o1_ref, s1_ref), (o2_ref, s2_ref))):
        d = dils[g]
        n = sf_ref.shape[1] // d
        for r in range(d):
            rows = slice(None) if d == 1 else pl.ds(r, n, stride=d)
            sf_ref[g, rows, :] = s_ref[r]
            for h in range(H):
                of_ref[g, h, rows, :] = o_ref[r, :, h * HEAD_DIM:(h + 1) * HEAD_DIM].astype(F32)
    st = [sf_ref[0], sf_ref[1], sf_ref[2]]
    mx = jnp.maximum(jnp.maximum(st[0], st[1]), st[2])
    w = [pltpu.roll(s, LANES - H, 1) * jnp.exp(s - mx) for s in st]
    tot = w[0] + w[1] + w[2]
    coef = [x / tot for x in w]
    for h in range(H):
        hs = slice(h * HEAD_DIM, (h + 1) * HEAD_DIM)
        acc = coef[0][:, h:h + 1] * of_ref[0, h]
        acc += coef[1][:, h:h + 1] * of_ref[1, h]
        acc += coef[2][:, h:h + 1] * of_ref[2, h]
        out_ref[:, hs] = acc.astype(out_ref.dtype)


def merge_groups(outs, stats, dils):
    B, d0, L0, U = outs[0].shape
    S = d0 * L0
    H = U // HEAD_DIM
    tm = min(512, S)
    tiles_per_b = S // tm

    def ospec(d, w):
        return pl.BlockSpec((None, d, tm // d, w), lambda i: (i // tiles_per_b, 0, i % tiles_per_b, 0))

    return pl.pallas_call(
        functools.partial(_merge_kernel, H=H, dils=dils),
        grid=(B * S // tm,),
        in_specs=[ospec(d, U) for d in dils] + [ospec(d, LANES) for d in dils],
        out_specs=pl.BlockSpec((tm, U), lambda i: (i, 0)),
        out_shape=jax.ShapeDtypeStruct((B * S, U), BF16),
        scratch_shapes=[pltpu.VMEM((3, H, tm, HEAD_DIM), F32), pltpu.VMEM((3, tm, LANES), F32)],
        compiler_params=_cparams(("parallel",)),
        name="merge_groups",
    )(*outs, *stats)


def _conv_kernel(a0_ref, a1_ref, b0_ref, b1_ref, ha0_ref, ha1_ref, hb0_ref, hb1_ref,
                 w_ref, bias_ref, g_ref, be_ref, o_ref, u_ref, c_ref, sh_ref, *, ts, U):
    i = pl.program_id(1)
    halo = CONV_HALO
    for half, (a_ref, b_ref, ha_ref, hb_ref) in enumerate(((a0_ref, b0_ref, ha0_ref, hb0_ref),
                                                            (a1_ref, b1_ref, ha1_ref, hb1_ref))):
        cs = slice(half * U, (half + 1) * U)
        u_ref[halo:halo + ts, cs] = a_ref[...].astype(F32) * _sigmoid(b_ref[...].astype(F32))
        hu = ha_ref[...].astype(F32) * _sigmoid(hb_ref[...].astype(F32))
        u_ref[0:halo, cs] = jnp.where(i > 0, hu, 0.0)
    C = 2 * U
    rc = 64
    off = halo - (CONV_WIDTH - 1)

    sub = 8
    n_al = ts + halo - sub

    def chan_body(cc, carry):
        c0 = pl.multiple_of(cc * LANES, LANES)
        sh_ref[0] = u_ref[:, pl.ds(c0, LANES)]
        for b in range(1, sub):
            sh_ref[b, 0:n_al, :] = u_ref[b:b + n_al, pl.ds(c0, LANES)]
        for rb in range(ts // rc):
            acc = jnp.zeros((rc, LANES), F32) + bias_ref[:, pl.ds(c0, LANES)]
            for j in range(CONV_WIDTH):
                a, b = divmod(off + j, sub)
                r0 = rb * rc + a * sub
                acc += w_ref[j:j + 1, pl.ds(c0, LANES)] * sh_ref[b, r0:r0 + rc, :]
            c_ref[rb * rc:(rb + 1) * rc, pl.ds(c0, LANES)] = acc
        return carry

    lax.fori_loop(0, C // LANES, chan_body, 0)

    rn = 16

    def norm_body(rb, carry):
        r0 = pl.multiple_of(rb * rn, rn)
        v = c_ref[pl.ds(r0, rn), :]
        mu = jnp.mean(v, axis=-1, keepdims=True)
        dv = v - mu
        var = jnp.mean(dv * dv, axis=-1, keepdims=True)
        y = dv * lax.rsqrt(var + NORM_EPS) * g_ref[...] + be_ref[...]
        o_ref[pl.ds(r0, rn), :] = (y * _sigmoid(y)).astype(o_ref.dtype)
        return carry

    lax.fori_loop(0, ts // rn, norm_body, 0, unroll=4)


def conv_branch(proj, B, S, U, conv_dw, conv_bias, ln_gain, ln_bias):
    IN = proj.shape[1]
    C = 2 * U
    ts = min(256, S)
    pv = proj.reshape(B, S, IN)
    hb = ts // CONV_HALO
    cur = lambda blk: pl.BlockSpec((None, ts, U), lambda b, i, blk=blk: (b, i, blk))
    prv = lambda blk: pl.BlockSpec((None, CONV_HALO, U), lambda b, i, blk=blk: (b, jnp.maximum(i * hb - 1, 0), blk))
    vec = pl.BlockSpec((1, C), lambda b, i: (0, 0))
    out = pl.pallas_call(
        functools.partial(_conv_kernel, ts=ts, U=U),
        grid=(B, S // ts),
        in_specs=[cur(0), cur(1), cur(2), cur(3), prv(0), prv(1), prv(2), prv(3),
                  pl.BlockSpec((CONV_WIDTH, C), lambda b, i: (0, 0)), vec, vec, vec],
        out_specs=pl.BlockSpec((None, ts, C), lambda b, i: (b, i, 0)),
        out_shape=jax.ShapeDtypeStruct((B, S, C), BF16),
        scratch_shapes=[pltpu.VMEM((ts + CONV_HALO, C), F32), pltpu.VMEM((ts, C), F32),
                        pltpu.VMEM((8, ts + CONV_HALO, LANES), F32)],
        compiler_params=_cparams(("parallel", "parallel")),
        name="conv_branch",
    )(pv, pv, pv, pv, pv, pv, pv, pv, conv_dw, conv_bias.reshape(1, C), ln_gain.reshape(1, C), ln_bias.reshape(1, C))
    return out.reshape(B * S, C)


def _gateproj_kernel(cn_ref, am_ref, wc_ref, wa_ref, gc_ref, ga_ref, o_ref):
    conv = jnp.dot(cn_ref[...], wc_ref[...], preferred_element_type=F32)
    z = _sigmoid(gc_ref[...].astype(F32)) * conv
    attn = jnp.dot(am_ref[...], wa_ref[...], preferred_element_type=F32)
    z += _sigmoid(ga_ref[...].astype(F32)) * attn
    o_ref[...] = z.astype(o_ref.dtype)


def gate_proj(cn, am, wc, wa, proj, U):
    N, C = cn.shape
    D = wc.shape[1]
    tn = U
    tm = min(512, N)
    return pl.pallas_call(
        _gateproj_kernel,
        grid=(D // tn, N // tm),
        in_specs=[pl.BlockSpec((tm, C), lambda j, i: (i, 0)),
                  pl.BlockSpec((tm, U), lambda j, i: (i, 0)),
                  pl.BlockSpec((C, tn), lambda j, i: (0, j)),
                  pl.BlockSpec((U, tn), lambda j, i: (0, j)),
                  pl.BlockSpec((tm, tn), lambda j, i: (i, 4 + j)),
                  pl.BlockSpec((tm, tn), lambda j, i: (i, 8 + j))],
        out_specs=pl.BlockSpec((tm, tn), lambda j, i: (i, j)),
        out_shape=jax.ShapeDtypeStruct((N, D), BF16),
        compiler_params=_cparams(("parallel", "parallel")),
        name="gate_proj",
    )(cn, am, wc, wa, proj, proj)


def _router_kernel(y_ref, xin_ref, mod_ref, gpost_ref, gain_ref, r_ref,
                   x1_ref, hp_ref, eid_ref, wt_ref, rank_ref, cnt_ref, carry_ref, *, tr):
    step = pl.program_id(0)

    @pl.when(step == 0)
    def _():
        carry_ref[...] = jnp.zeros_like(carry_ref)

    y = y_ref[...].astype(F32)
    yms = jnp.mean(y * y, axis=-1, keepdims=True)
    x = xin_ref[...] + mod_ref[2:3, :] * (y * lax.rsqrt(yms + NORM_EPS) * gpost_ref[...])
    x1_ref[...] = x
    D = x.shape[1]
    ms = jnp.mean(x * x, axis=-1, keepdims=True)
    h = x * lax.rsqrt(ms + NORM_EPS) * gain_ref[...]
    h = h * (1.0 + mod_ref[4:5, :]) + mod_ref[3:4, :]
    lo = h[:, :D // 2]
    hi = h[:, D // 2:]
    hp_ref[...] = _pack_halves(lo, hi)
    logits = (jnp.dot(lo.astype(BF16), r_ref[:D // 2, :], preferred_element_type=F32)
              + jnp.dot(hi.astype(BF16), r_ref[D // 2:, :], preferred_element_type=F32))
    lane = lax.broadcasted_iota(I32, logits.shape, 1)
    G = N_EXPERT_GROUPS
    is_g = lane < G
    gl = jnp.where(is_g, logits, NEG_BIG)
    gmax = jnp.max(gl, axis=-1, keepdims=True)
    grp = jnp.min(jnp.where(gl == gmax, lane, LANES), axis=-1, keepdims=True)
    p_grp = 1.0 / jnp.sum(jnp.where(is_g, jnp.exp(gl - gmax), 0.0), axis=-1, keepdims=True)
    in_grp = (lane >= G) & (lane < G + N_EXPERTS) & (((lane - G) // EXPERTS_PER_GROUP) == grp)
    el = jnp.where(in_grp, logits, NEG_BIG)
    v0 = jnp.max(el, axis=-1, keepdims=True)
    i0 = jnp.min(jnp.where(in_grp & (el == v0), lane, LANES), axis=-1, keepdims=True)
    in2 = in_grp & (lane != i0)
    el2 = jnp.where(in2, logits, NEG_BIG)
    v1 = jnp.max(el2, axis=-1, keepdims=True)
    i1 = jnp.min(jnp.where(in2 & (el2 == v1), lane, LANES), axis=-1, keepdims=True)
    e1 = jnp.exp(v1 - v0)
    w0 = p_grp / (1.0 + e1)
    w1 = p_grp * e1 / (1.0 + e1)
    ex0 = i0 - G
    ex1 = i1 - G
    eid_ref[...] = jnp.where(lane == 0, ex0, jnp.where(lane == 1, ex1, 0))
    wt_ref[...] = jnp.where(lane == 0, w0, jnp.where(lane == 1, w1, 0.0))
    oh0 = (lane == ex0).astype(F32)
    oh1 = (lane == ex1).astype(F32)
    both = oh0 + oh1
    rr = lax.broadcasted_iota(I32, (tr, tr), 0)
    cc = lax.broadcasted_iota(I32, (tr, tr), 1)
    tril = (cc < rr).astype(BF16)
    before = jnp.dot(tril, both.astype(BF16), preferred_element_type=F32) + carry_ref[0:1, :]
    rk0 = jnp.sum(before * oh0, axis=-1, keepdims=True)
    rk1 = jnp.sum(before * oh1, axis=-1, keepdims=True)
    rank_ref[...] = jnp.where(lane == 0, rk0, jnp.where(lane == 1, rk1, 0.0))
    newc = carry_ref[0:1, :] + jnp.sum(both, axis=0, keepdims=True)
    carry_ref[...] = jnp.broadcast_to(newc, carry_ref.shape)
    cnt_ref[...] = jnp.broadcast_to(newc, cnt_ref.shape)


def residual_prenorm_router(y, x2d, mod, gain_post, gain, rcat, S):
    N, D = x2d.shape
    tr = min(256, S)
    tiles_per_b = S // tr
    lane_spec = pl.BlockSpec((tr, LANES), lambda i: (i, 0))
    row_spec = pl.BlockSpec((tr, D), lambda i: (i, 0))
    vec_spec = pl.BlockSpec((1, D), lambda i: (0, 0))
    return pl.pallas_call(
        functools.partial(_router_kernel, tr=tr),
        grid=(N // tr,),
        in_specs=[row_spec, row_spec,
                  pl.BlockSpec((None, 6, D), lambda i: (i // tiles_per_b, 0, 0)),
                  vec_spec, vec_spec,
                  pl.BlockSpec((D, LANES), lambda i: (0, 0))],
        out_specs=[row_spec, pl.BlockSpec((tr, D // 2), lambda i: (i, 0)), lane_spec, lane_spec, lane_spec,
                   pl.BlockSpec((8, LANES), lambda i: (0, 0))],
        out_shape=[jax.ShapeDtypeStruct((N, D), F32),
                   jax.ShapeDtypeStruct((N, D // 2), U32),
                   jax.ShapeDtypeStruct((N, LANES), I32),
                   jax.ShapeDtypeStruct((N, LANES), F32),
                   jax.ShapeDtypeStruct((N, LANES), F32),
                   jax.ShapeDtypeStruct((8, LANES), F32)],
        scratch_shapes=[pltpu.VMEM((8, LANES), F32)],
        compiler_params=_cparams(("arbitrary",)),
        name="residual_prenorm_router",
    )(y, x2d, mod, gain_post.reshape(1, D), gain.reshape(1, D), rcat)


def _dest_kernel(eid_ref, rank_ref, start_ref, o_ref):
    lane = lax.broadcasted_iota(I32, eid_ref.shape, 1)
    eid = eid_ref[...]
    rank = rank_ref[...]
    start = start_ref[0:1, :]
    d = []
    for k in range(2):
        oh = (lane == eid[:, k:k + 1]).astype(F32)
        d.append(jnp.sum(oh * start, axis=-1, keepdims=True) + rank[:, k:k + 1])
    o_ref[...] = jnp.where(lane == 0, d[0], jnp.where(lane == 1, d[1], 0.0)).astype(I32)


def dest_rows(eid, rank, pad_start):
    N = eid.shape[0]
    tr = min(1024, N)
    spec = pl.BlockSpec((tr, LANES), lambda i: (i, 0))
    return pl.pallas_call(
        _dest_kernel,
        grid=(N // tr,),
        in_specs=[spec, spec, pl.BlockSpec((8, LANES), lambda i: (0, 0))],
        out_specs=spec,
        out_shape=jax.ShapeDtypeStruct((N, LANES), I32),
        compiler_params=_cparams(("parallel",)),
        name="dest_rows",
    )(eid, rank, pad_start)


def _scatter_kernel(dest_ref, h_ref, xs_in_ref, xs_ref, sem, *, T):
    del xs_in_ref

    def issue(t, carry):
        for k in range(2):
            d = dest_ref[0, 0, 2 * t + k]
            pltpu.make_async_copy(h_ref.at[pl.ds(t, 1)], xs_ref.at[pl.ds(d, 1)], sem).start()
        return carry

    lax.fori_loop(0, T, issue, 0, unroll=4)
    for k in range(2):
        pltpu.make_async_copy(h_ref, xs_ref.at[pl.ds(0, T)], sem).wait()


def scatter_rows(hp, dest, P):
    N, W = hp.shape
    T = min(256, N)
    dest_s = dest[:, :2].reshape(N // T, 1, 2 * T)
    xs0 = jnp.zeros((P, W), hp.dtype)
    return pl.pallas_call(
        functools.partial(_scatter_kernel, T=T),
        grid=(N // T,),
        in_specs=[pl.BlockSpec((1, 1, 2 * T), lambda i: (i, 0, 0), memory_space=pltpu.SMEM),
                  pl.BlockSpec((T, W), lambda i: (i, 0)),
                  pl.BlockSpec(memory_space=pl.ANY)],
        out_specs=pl.BlockSpec(memory_space=pl.ANY),
        out_shape=jax.ShapeDtypeStruct((P, W), hp.dtype),
        scratch_shapes=[pltpu.SemaphoreType.DMA(())],
        input_output_aliases={2: 0},
        compiler_params=_cparams(("arbitrary",)),
        name="scatter_rows",
    )(dest_s, hp, xs0)


def _expert_chunks(st_ref, nc_ref, in_hbm, out_hbm, ibuf, obuf, isem, osem, compute):
    e = pl.program_id(0)
    n_e = pl.num_programs(0)
    n = nc_ref[e]
    TM = ibuf.shape[1]
    base = pl.multiple_of(st_ref[e], TM)
    g0 = base // TM

    def icopy(row, slot):
        return pltpu.make_async_copy(in_hbm.at[pl.ds(row, TM)], ibuf.at[slot], isem.at[slot])

    def ocopy(row, slot):
        return pltpu.make_async_copy(obuf.at[slot], out_hbm.at[pl.ds(row, TM)], osem.at[slot])

    @pl.when((e == 0) & (n > 0))
    def _():
        icopy(base, 0).start()

    def body(c, carry):
        g = g0 + c
        slot = g % 2
        row = pl.multiple_of(base + c * TM, TM)

        @pl.when(c + 1 < n)
        def _():
            icopy(row + TM, 1 - slot).start()

        icopy(row, slot).wait()

        @pl.when(g >= 2)
        def _():
            ocopy(row, slot).wait()

        compute(ibuf.at[slot], obuf.at[slot])
        ocopy(row, slot).start()
        return carry

    lax.fori_loop(0, n, body, 0)

    @pl.when(e + 1 < n_e)
    def _():
        @pl.when(nc_ref[e + 1] > 0)
        def _():
            icopy(pl.multiple_of(st_ref[e + 1], TM), (g0 + n) % 2).start()

    @pl.when(e == n_e - 1)
    def _():
        g_end = g0 + n

        @pl.when(g_end >= 2)
        def _():
            ocopy(base, g_end % 2).wait()

        @pl.when(g_end >= 1)
        def _():
            ocopy(base, (g_end + 1) % 2).wait()

        used = pl.multiple_of(base + n * TM, TM)
        n_tail = (out_hbm.shape[0] - used) // TM
        obuf[0] = jnp.zeros(obuf.shape[1:], obuf.dtype)

        def zstart(c, carry):
            ocopy(pl.multiple_of(used + c * TM, TM), 0).start()
            return carry

        def zwait(c, carry):
            ocopy(used, 0).wait()
            return carry

        lax.fori_loop(0, n_tail, zstart, 0)
        lax.fori_loop(0, n_tail, zwait, 0)


def _stream_expert_weights(parts, wsem):
    e = pl.program_id(0)
    n_e = pl.num_programs(0)
    slot = e % 2

    def copy(k, ei, s):
        src, dst = parts[k]
        return pltpu.make_async_copy(src(ei), dst(s), wsem.at[k, s])

    @pl.when(e == 0)
    def _():
        for k in range(len(parts)):
            copy(k, 0, 0).start(priority=(k + 1) % 2)

    @pl.when(e + 1 < n_e)
    def _():
        for k in range(len(parts)):
            copy(k, e + 1, 1 - slot).start(priority=(k + 1) % 2)

    for k in range(len(parts)):
        copy(k, e, slot).wait()
    return slot


def _moe_up_kernel(st_ref, nc_ref, xs_hbm, w1_hbm, w3_hbm, o_hbm, wb_ref, w1_buf, w3_buf, ibuf, obuf,
                   wsem, isem, osem, *, F):
    slot = _stream_expert_weights([(lambda ei: w1_hbm.at[ei], lambda s: w1_buf.at[s]),
                                   (lambda ei: w3_hbm.at[ei], lambda s: w3_buf.at[s])], wsem)

    @pl.when(nc_ref[pl.program_id(0)] > 0)
    def _():
        wb_ref[:, :F] = w1_buf[slot].astype(BF16)
        wb_ref[:, F:] = w3_buf[slot].astype(BF16)

    def compute(x_ref, o_ref):
        lo, hi = _unpack_halves(x_ref[...])
        half = wb_ref.shape[0] // 2
        hcat = (jnp.dot(lo.astype(BF16), wb_ref[:half, :], preferred_element_type=F32)
                + jnp.dot(hi.astype(BF16), wb_ref[half:, :], preferred_element_type=F32))
        a = hcat[:, :F]
        o_ref[...] = (a * _sigmoid(a) * hcat[:, F:]).astype(o_ref.dtype)

    _expert_chunks(st_ref, nc_ref, xs_hbm, o_hbm, ibuf, obuf, isem, osem, compute)


def _moe_down_kernel(st_ref, nc_ref, h_hbm, w2_hbm, o_hbm, wb_ref, w2_buf, ibuf, obuf, wsem, isem, osem):
    fh = w2_buf.shape[1] // 2
    slot = _stream_expert_weights(
        [(lambda ei: w2_hbm.at[ei, pl.ds(0, fh)], lambda s: w2_buf.at[s, pl.ds(0, fh)]),
         (lambda ei: w2_hbm.at[ei, pl.ds(fh, fh)], lambda s: w2_buf.at[s, pl.ds(fh, fh)])], wsem)

    @pl.when(nc_ref[pl.program_id(0)] > 0)
    def _():
        wb_ref[...] = w2_buf[slot].astype(BF16)

    def compute(h_ref, o_ref):
        y = jnp.dot(h_ref[...], wb_ref[...], preferred_element_type=F32)
        half = y.shape[1] // 2
        o_ref[...] = _pack_halves(y[:, :half], y[:, half:])

    _expert_chunks(st_ref, nc_ref, h_hbm, o_hbm, ibuf, obuf, isem, osem, compute)


def expert_ffn(xs, seg_start, seg_chunks, w1, w3, w2):
    P, Wp = xs.shape
    E, D, F = w1.shape
    TM = MOE_ROWS
    any_spec = pl.BlockSpec(memory_space=pl.ANY)
    dma2 = pltpu.SemaphoreType.DMA((2,))
    hmid = pl.pallas_call(
        functools.partial(_moe_up_kernel, F=F),
        grid_spec=pltpu.PrefetchScalarGridSpec(
            num_scalar_prefetch=2,
            grid=(E,),
            in_specs=[any_spec, any_spec, any_spec],
            out_specs=any_spec,
            scratch_shapes=[pltpu.VMEM((D, 2 * F), BF16), pltpu.VMEM((2, D, F), F32), pltpu.VMEM((2, D, F), F32),
                            pltpu.VMEM((2, TM, Wp), U32), pltpu.VMEM((2, TM, F), BF16),
                            pltpu.SemaphoreType.DMA((2, 2)), dma2, dma2]),
        out_shape=jax.ShapeDtypeStruct((P, F), BF16),
        compiler_params=_cparams(("arbitrary",)),
        name="moe_up",
    )(seg_start, seg_chunks, xs, w1, w3)
    return pl.pallas_call(
        _moe_down_kernel,
        grid_spec=pltpu.PrefetchScalarGridSpec(
            num_scalar_prefetch=2,
            grid=(E,),
            in_specs=[any_spec, any_spec],
            out_specs=any_spec,
            scratch_shapes=[pltpu.VMEM((F, D), BF16), pltpu.VMEM((2, F, D), F32), pltpu.VMEM((2, TM, F), BF16),
                            pltpu.VMEM((2, TM, D // 2), U32), pltpu.SemaphoreType.DMA((2, 2)), dma2, dma2]),
        out_shape=jax.ShapeDtypeStruct((P, D // 2), U32),
        compiler_params=_cparams(("arbitrary",)),
        name="moe_down",
    )(seg_start, seg_chunks, hmid, w2)


def _combine_kernel(pos_ref, nxt_ref, wt_ref, x_ref, mod_ref, gain_ref, y_hbm, o_ref, buf_ref, sem, *, T):
    i = pl.program_id(0)
    n = pl.num_programs(0)
    slot = i % 2

    def gather(p_ref, s):
        def issue(t, carry):
            for k in range(2):
                p = p_ref[0, 0, 2 * t + k]
                pltpu.make_async_copy(y_hbm.at[pl.ds(p, 1)], buf_ref.at[s, k, pl.ds(t, 1)], sem.at[s]).start()
            return carry

        lax.fori_loop(0, T, issue, 0, unroll=4)

    @pl.when(i == 0)
    def _():
        gather(pos_ref, 0)

    @pl.when(i + 1 < n)
    def _():
        gather(nxt_ref, 1 - slot)

    for k in range(2):
        pltpu.make_async_copy(y_hbm.at[pl.ds(0, T)], buf_ref.at[slot, k], sem.at[slot]).wait()

    wt = wt_ref[...]
    w0 = wt[:, 0:1]
    w1 = wt[:, 1:2]
    lo0, hi0 = _unpack_halves(buf_ref[slot, 0])
    lo1, hi1 = _unpack_halves(buf_ref[slot, 1])
    ylo = w0 * lo0 + w1 * lo1
    yhi = w0 * hi0 + w1 * hi1
    D = x_ref.shape[1]
    half = D // 2
    ms = (jnp.sum(ylo * ylo, axis=-1, keepdims=True) + jnp.sum(yhi * yhi, axis=-1, keepdims=True)) / D
    inv = lax.rsqrt(ms + NORM_EPS)
    o_ref[:, :half] = x_ref[:, :half] + mod_ref[5:6, :half] * (ylo * inv * gain_ref[:, :half])
    o_ref[:, half:] = x_ref[:, half:] + mod_ref[5:6, half:] * (yhi * inv * gain_ref[:, half:])


def combine(yp, dest, wts, x1, mod, gain, S):
    N, D = x1.shape
    T = min(128, S)
    tiles_per_b = S // T
    pos_s = dest[:, :2].reshape(N // T, 1, 2 * T)
    n_tiles = N // T
    return pl.pallas_call(
        functools.partial(_combine_kernel, T=T),
        grid=(n_tiles,),
        in_specs=[pl.BlockSpec((1, 1, 2 * T), lambda i: (i, 0, 0), memory_space=pltpu.SMEM),
                  pl.BlockSpec((1, 1, 2 * T), lambda i: (jnp.minimum(i + 1, n_tiles - 1), 0, 0),
                               memory_space=pltpu.SMEM),
                  pl.BlockSpec((T, LANES), lambda i: (i, 0)),
                  pl.BlockSpec((T, D), lambda i: (i, 0)),
                  pl.BlockSpec((None, 6, D), lambda i: (i // tiles_per_b, 0, 0)),
                  pl.BlockSpec((1, D), lambda i: (0, 0)),
                  pl.BlockSpec(memory_space=pl.ANY)],
        out_specs=pl.BlockSpec((T, D), lambda i: (i, 0)),
        out_shape=jax.ShapeDtypeStruct((N, D), F32),
        scratch_shapes=[pltpu.VMEM((2, 2, T, D // 2), U32), pltpu.SemaphoreType.DMA((2,))],
        compiler_params=_cparams(("arbitrary",)),
        name="combine",
    )(pos_s, pos_s, wts, x1, mod, gain.reshape(1, D), yp)


def _moe_layout(counts):
    TM = MOE_ROWS
    chunks = (counts.astype(I32) + TM - 1) // TM
    start = (jnp.cumsum(chunks) - chunks) * TM
    return start, chunks


def kernel(x, c, positions, ada_w, ada_b, mix_norm_pre, mix_norm_post, w_in, conv_dw, conv_dw_bias, conv_ln_gain, conv_ln_bias, w_conv_out, w_attn_out, w_out, ffn_norm_pre, ffn_norm_post, router_group, router_expert, expert_w1, expert_w3, expert_w2):
    B, S, D = x.shape
    N = B * S
    U = D // 4
    depth = ada_w.shape[0]
    xc = x.reshape(N, D)
    for layer in range(depth):
        mod = ada_mod(c, ada_w[layer], ada_b[layer]).reshape(B, 6, D)
        h, cosf, sinf = prenorm_rope(xc, mod, mix_norm_pre[layer], positions, S)
        w_in_b = w_in[layer].astype(BF16)
        outs, stats = [], []
        for g, (window, d) in enumerate(ATTN_PATTERNS):
            assert window // d == ATTN_BLOCK and S % (d * ATTN_BLOCK) == 0
            qkv = qkv_proj(h, w_in_b, cosf, sinf, B, S, U, g, d)
            o, st = dilated_attention(qkv, d)
            outs.append(o)
            stats.append(st)
        am = merge_groups(outs, stats, tuple(d for _, d in ATTN_PATTERNS))
        proj = matmul_cols(h, w_in_b, U, 9, "rest_proj")
        cn = conv_branch(proj, B, S, U, conv_dw[layer], conv_dw_bias[layer], conv_ln_gain[layer], conv_ln_bias[layer])
        z = gate_proj(cn, am, w_conv_out[layer].astype(BF16), w_attn_out[layer].astype(BF16), proj, U)
        y = matmul_cols(z, w_out[layer].astype(BF16), U, 0, "out_proj")
        rcat = jnp.zeros((D, LANES), F32)
        rcat = rcat.at[:, :N_EXPERT_GROUPS].set(router_group[layer])
        rcat = rcat.at[:, N_EXPERT_GROUPS:N_EXPERT_GROUPS + N_EXPERTS].set(router_expert[layer]).astype(BF16)
        x1, hp, eid, wts, rank, cnt = residual_prenorm_router(
            y, xc, mod, mix_norm_post[layer], ffn_norm_pre[layer], rcat, S)
        TM = MOE_ROWS
        P = (2 * N + N_EXPERTS * (TM - 1)) // TM * TM
        seg_start, seg_chunks = _moe_layout(cnt[0, :N_EXPERTS])
        start_row = jnp.zeros((8, LANES), F32).at[:, :N_EXPERTS].set(seg_start.astype(F32)[None, :])
        dest = dest_rows(eid, rank, start_row)
        xs = scatter_rows(hp, dest, P)
        yp = expert_ffn(xs, seg_start, seg_chunks, expert_w1[layer], expert_w3[layer], expert_w2[layer])
        xc = combine(yp, dest, wts, x1, mod, ffn_norm_post[layer], S)
    return xc.reshape(B, S, D)
```

```python
import functools

import jax
import jax.numpy as jnp
from jax import lax
from jax.experimental import pallas as pl
from jax.experimental.pallas import tpu as pltpu

F32 = jnp.float32
BF16 = jnp.bfloat16
I32 = jnp.int32
U32 = jnp.uint32

HEAD_DIM = 128
LANES = 128
ATTN_BLOCK = 128
ATTN_PATTERNS = ((128, 1), (512, 4), (2048, 16))
ROPE_THETA = 10000.0
CONV_WIDTH = 31
CONV_HALO = 32
N_EXPERT_GROUPS = 8
EXPERTS_PER_GROUP = 8
N_EXPERTS = 64
NORM_EPS = 1e-6
NEG_BIG = -1e30
MOE_ROWS = 256
V7X_VMEM_LIMIT = 60 * 1024 * 1024


def _cparams(sem):
    return pltpu.CompilerParams(dimension_semantics=sem, vmem_limit_bytes=V7X_VMEM_LIMIT)


def _sigmoid(x):
    return 1.0 / (1.0 + jnp.exp(-x))


def _pack_halves(lo, hi):
    lo_b = lax.bitcast_convert_type(lo.astype(BF16).astype(F32), U32) >> 16
    hi_b = lax.bitcast_convert_type(hi.astype(BF16).astype(F32), U32) & jnp.uint32(0xFFFF0000)
    return hi_b | lo_b


def _unpack_halves(w):
    lo = lax.bitcast_convert_type(w << 16, F32)
    hi = lax.bitcast_convert_type(w & jnp.uint32(0xFFFF0000), F32)
    return lo, hi


def _ada_kernel(c_ref, w_ref, b_ref, o_ref):
    c = c_ref[...]
    cact = (c * _sigmoid(c)).astype(BF16)
    o_ref[...] = jnp.dot(cact, w_ref[...].astype(BF16), preferred_element_type=F32) + b_ref[...]


def ada_mod(c, ada_w, ada_b):
    B, D = c.shape
    W = ada_w.shape[1]
    rows = 8
    cp = jnp.zeros((rows, D), F32).at[:B].set(c)
    tn = min(512, W)
    out = pl.pallas_call(
        _ada_kernel,
        grid=(W // tn,),
        in_specs=[pl.BlockSpec((rows, D), lambda j: (0, 0)),
                  pl.BlockSpec((D, tn), lambda j: (0, j)),
                  pl.BlockSpec((1, tn), lambda j: (0, j))],
        out_specs=pl.BlockSpec((rows, tn), lambda j: (0, j)),
        out_shape=jax.ShapeDtypeStruct((rows, W), F32),
        compiler_params=_cparams(("parallel",)),
        name="ada_mod",
    )(cp, ada_w, ada_b.reshape(1, W))
    return out[:B]


def _prenorm_kernel(x_ref, mod_ref, gain_ref, pos_ref, invf_ref, h_ref, cos_ref, sin_ref):
    x = x_ref[...]
    ms = jnp.mean(x * x, axis=-1, keepdims=True)
    y = x * lax.rsqrt(ms + NORM_EPS) * gain_ref[...]
    y = y * (1.0 + mod_ref[1:2, :]) + mod_ref[0:1, :]
    h_ref[...] = y.astype(BF16)
    ang = pos_ref[...].astype(F32) * invf_ref[...]
    lane = lax.broadcasted_iota(I32, ang.shape, 1)
    sn = jnp.sin(ang)
    cos_ref[...] = jnp.cos(ang)
    sin_ref[...] = jnp.where(lane < HEAD_DIM // 2, -sn, sn)


def prenorm_rope(x2d, mod, gain, positions, S):
    N, D = x2d.shape
    tr = min(256, S)
    tiles_per_b = S // tr
    half = HEAD_DIM // 2
    inv = ROPE_THETA ** (-jnp.arange(0, HEAD_DIM, 2, dtype=F32) / HEAD_DIM)
    invf = jnp.concatenate([inv, inv]).reshape(1, HEAD_DIM)
    del half
    return pl.pallas_call(
        _prenorm_kernel,
        grid=(N // tr,),
        in_specs=[pl.BlockSpec((tr, D), lambda i: (i, 0)),
                  pl.BlockSpec((None, 6, D), lambda i: (i // tiles_per_b, 0, 0)),
                  pl.BlockSpec((1, D), lambda i: (0, 0)),
                  pl.BlockSpec((tr, 1), lambda i: (i, 0)),
                  pl.BlockSpec((1, HEAD_DIM), lambda i: (0, 0))],
        out_specs=[pl.BlockSpec((tr, D), lambda i: (i, 0)),
                   pl.BlockSpec((tr, HEAD_DIM), lambda i: (i, 0)),
                   pl.BlockSpec((tr, HEAD_DIM), lambda i: (i, 0))],
        out_shape=[jax.ShapeDtypeStruct((N, D), BF16),
                   jax.ShapeDtypeStruct((N, HEAD_DIM), F32),
                   jax.ShapeDtypeStruct((N, HEAD_DIM), F32)],
        compiler_params=_cparams(("parallel",)),
        name="prenorm_rope",
    )(x2d, mod, gain.reshape(1, D), positions.reshape(N, 1), invf)


def _residue_rows(ref, r, d):
    if d == 1:
        return ref[...]
    return ref[pl.ds(r, ref.shape[0] // d, stride=d), :]


def _qkv_kernel(a_ref, w_ref, cos_ref, sin_ref, o_ref, acc_ref, *, d, heads, scale, n_i):
    s = pl.program_id(0)

    @pl.when(s == 0)
    def _():
        acc_ref[...] = jnp.zeros_like(acc_ref)

    kind = jnp.maximum(s - 1, 0) // n_i
    sc = jnp.where(kind == 0, scale, 1.0).astype(F32)
    is_v = kind == 2
    for r in range(d):
        c = jnp.where(is_v, 1.0, _residue_rows(cos_ref, r, d) * sc)
        sn = jnp.where(is_v, 0.0, _residue_rows(sin_ref, r, d) * sc)
        for h in range(heads):
            t = _residue_rows(acc_ref.at[h], r, d)
            o_ref[r, :, h * HEAD_DIM:(h + 1) * HEAD_DIM] = (
                t * c + pltpu.roll(t, HEAD_DIM // 2, 1) * sn).astype(o_ref.dtype)
    acc = jnp.dot(a_ref[...], w_ref[...], preferred_element_type=F32)
    for h in range(heads):
        acc_ref[h] = acc[:, h * HEAD_DIM:(h + 1) * HEAD_DIM]


def qkv_proj(h, w_bf16, cosf, sinf, B, S, U, g, d):
    N, D = h.shape
    tm = min(1024, S)
    tiles_per_b = S // tm
    n_i = N // tm
    last = 3 * n_i - 1
    kern = functools.partial(_qkv_kernel, d=d, heads=U // HEAD_DIM, scale=HEAD_DIM ** -0.5, n_i=n_i)

    def cur(s):
        return jnp.minimum(s, last)

    def fin(s):
        return jnp.maximum(s - 1, 0)

    return pl.pallas_call(
        kern,
        grid=(3 * n_i + 1,),
        in_specs=[pl.BlockSpec((tm, D), lambda s: (cur(s) % n_i, 0)),
                  pl.BlockSpec((D, U), lambda s: (0, (cur(s) // n_i) * 3 + g)),
                  pl.BlockSpec((tm, HEAD_DIM), lambda s: (fin(s) % n_i, 0)),
                  pl.BlockSpec((tm, HEAD_DIM), lambda s: (fin(s) % n_i, 0))],
        out_specs=pl.BlockSpec((None, None, d, tm // d, U),
                               lambda s: (fin(s) // n_i, (fin(s) % n_i) // tiles_per_b, 0,
                                          (fin(s) % n_i) % tiles_per_b, 0)),
        out_shape=jax.ShapeDtypeStruct((3, B, d, S // d, U), BF16),
        scratch_shapes=[pltpu.VMEM((U // HEAD_DIM, tm, HEAD_DIM), F32)],
        compiler_params=_cparams(("arbitrary",)),
        name=f"qkv_proj_d{d}",
    )(h, w_bf16, cosf, sinf)


def _matmul_kernel(a_ref, w_ref, o_ref):
    o_ref[...] = jnp.dot(a_ref[...], w_ref[...], preferred_element_type=F32).astype(o_ref.dtype)


def matmul_cols(a, w_bf16, U, first_tile, name):
    N, K = a.shape
    n_tiles = w_bf16.shape[1] // U - first_tile
    tm = min(1024, N)
    return pl.pallas_call(
        _matmul_kernel,
        grid=(n_tiles, N // tm),
        in_specs=[pl.BlockSpec((tm, K), lambda j, i: (i, 0)),
                  pl.BlockSpec((K, U), lambda j, i: (0, first_tile + j))],
        out_specs=pl.BlockSpec((tm, U), lambda j, i: (i, j)),
        out_shape=jax.ShapeDtypeStruct((N, n_tiles * U), BF16),
        compiler_params=_cparams(("parallel", "parallel")),
        name=name,
    )(a, w_bf16)


def _attn_kernel(q_ref, kc_ref, kp_ref, vc_ref, vp_ref, o_ref, st_ref, kx_ref, vx_ref, *, QB, H):
    i = pl.program_id(2)
    blk = ATTN_BLOCK
    kx_ref[0:blk, :] = kp_ref[...]
    kx_ref[blk:, :] = kc_ref[...]
    ones = jnp.ones((vx_ref.shape[0], HEAD_DIM), vx_ref.dtype)
    for h in range(H):
        hs = slice(h * HEAD_DIM, (h + 1) * HEAD_DIM)
        vx_ref[0:blk, 2 * h * HEAD_DIM:(2 * h + 1) * HEAD_DIM] = vp_ref[:, hs]
        vx_ref[blk:, 2 * h * HEAD_DIM:(2 * h + 1) * HEAD_DIM] = vc_ref[:, hs]
        vx_ref[:, (2 * h + 1) * HEAD_DIM:(2 * h + 2) * HEAD_DIM] = ones
    rows = lax.broadcasted_iota(I32, (blk, 2 * blk), 0)
    keys = lax.broadcasted_iota(I32, (blk, 2 * blk), 1)
    band = (keys >= rows) & (keys - blk <= rows)
    lane = lax.broadcasted_iota(I32, (blk, LANES), 1)
    dn = (((1,), (1,)), ((), ()))

    def body(qb, carry):
        r0 = pl.multiple_of(qb * blk, blk)
        ok = band & ((keys >= blk) | (i * QB + qb > 0))
        q = [q_ref[pl.ds(r0, blk), h * HEAD_DIM:(h + 1) * HEAD_DIM] for h in range(H)]
        k = [kx_ref[pl.ds(r0, 2 * blk), h * HEAD_DIM:(h + 1) * HEAD_DIM] for h in range(H)]
        s = [jnp.where(ok, lax.dot_general(q[h], k[h], dn, preferred_element_type=F32), NEG_BIG) for h in range(H)]
        m = [jnp.max(s[h], axis=-1, keepdims=True) for h in range(H)]
        p = [jnp.exp(s[h] - m[h]).astype(BF16) for h in range(H)]
        st = jnp.zeros((blk, LANES), F32)
        for h in range(H):
            v1 = vx_ref[pl.ds(r0, 2 * blk), 2 * h * HEAD_DIM:(2 * h + 2) * HEAD_DIM]
            acc = jnp.dot(p[h], v1, preferred_element_type=F32)
            l = acc[:, HEAD_DIM:]
            o_ref[pl.ds(r0, blk), h * HEAD_DIM:(h + 1) * HEAD_DIM] = (acc[:, :HEAD_DIM] / l).astype(o_ref.dtype)
            st = jnp.where(lane == h, m[h], st)
            st = jnp.where(lane == H + h, l, st)
        st_ref[pl.ds(r0, blk), :] = st
        return carry

    lax.fori_loop(0, QB, body, 0)


def dilated_attention(qkv, d):
    _, B, _, L, U = qkv.shape
    H = U // HEAD_DIM
    R = min(512, L)
    QB = R // ATTN_BLOCK

    def cur(kind):
        return pl.BlockSpec((None, None, None, R, U), lambda b, r, i: (kind, b, r, i, 0))

    def prev(kind):
        return pl.BlockSpec((None, None, None, ATTN_BLOCK, U),
                            lambda b, r, i: (kind, b, r, jnp.maximum(i * QB - 1, 0), 0))

    kern = functools.partial(_attn_kernel, QB=QB, H=H)
    return pl.pallas_call(
        kern,
        grid=(B, d, L // R),
        in_specs=[cur(0), cur(1), prev(1), cur(2), prev(2)],
        out_specs=[pl.BlockSpec((None, None, R, U), lambda b, r, i: (b, r, i, 0)),
                   pl.BlockSpec((None, None, R, LANES), lambda b, r, i: (b, r, i, 0))],
        out_shape=[jax.ShapeDtypeStruct((B, d, L, U), BF16),
                   jax.ShapeDtypeStruct((B, d, L, LANES), F32)],
        scratch_shapes=[pltpu.VMEM((R + ATTN_BLOCK, U), BF16), pltpu.VMEM((R + ATTN_BLOCK, 2 * U), BF16)],
        compiler_params=_cparams(("parallel", "parallel", "parallel")),
        name=f"dilated_attn_d{d}",
    )(qkv, qkv, qkv, qkv, qkv)


def _merge_kernel(o0_ref, o1_ref, o2_ref, s0_ref, s1_ref, s2_ref, out_ref, of_ref, sf_ref, *, H, dils):
    for g, (o_ref, s_ref) in enumerate(((o0_ref, s0_ref), (o1_ref, s1_ref), (o2_ref, s2_ref))):
        d = dils[g]
        n = sf_ref.shape[1] // d
        for r in range(d):
            rows = slice(None) if d == 1 else pl.ds(r, n, stride=d)
            sf_ref[g, rows, :] = s_ref[r]
            for h in range(H):
                of_ref[g, h, rows, :] = o_ref[r, :, h * HEAD_DIM:(h + 1) * HEAD_DIM].astype(F32)
    st = [sf_ref[0], sf_ref[1], sf_ref[2]]
    mx = jnp.maximum(jnp.maximum(st[0], st[1]), st[2])
    w = [pltpu.roll(s, LANES - H, 1) * jnp.exp(s - mx) for s in st]
    tot = w[0] + w[1] + w[2]
    coef = [x / tot for x in w]
    for h in range(H):
        hs = slice(h * HEAD_DIM, (h + 1) * HEAD_DIM)
        acc = coef[0][:, h:h + 1] * of_ref[0, h]
        acc += coef[1][:, h:h + 1] * of_ref[1, h]
        acc += coef[2][:, h:h + 1] * of_ref[2, h]
        out_ref[:, hs] = acc.astype(out_ref.dtype)


def merge_groups(outs, stats, dils):
    B, d0, L0, U = outs[0].shape
    S = d0 * L0
    H = U // HEAD_DIM
    tm = min(512, S)
    tiles_per_b = S // tm

    def ospec(d, w):
        return pl.BlockSpec((None, d, tm // d, w), lambda i: (i // tiles_per_b, 0, i % tiles_per_b, 0))

    return pl.pallas_call(
        functools.partial(_merge_kernel, H=H, dils=dils),
        grid=(B * S // tm,),
        in_specs=[ospec(d, U) for d in dils] + [ospec(d, LANES) for d in dils],
        out_specs=pl.BlockSpec((tm, U), lambda i: (i, 0)),
        out_shape=jax.ShapeDtypeStruct((B * S, U), BF16),
        scratch_shapes=[pltpu.VMEM((3, H, tm, HEAD_DIM), F32), pltpu.VMEM((3, tm, LANES), F32)],
        compiler_params=_cparams(("parallel",)),
        name="merge_groups",
    )(*outs, *stats)


def _conv_kernel(a0_ref, a1_ref, b0_ref, b1_ref, ha0_ref, ha1_ref, hb0_ref, hb1_ref,
                 w_ref, bias_ref, g_ref, be_ref, o_ref, u_ref, c_ref, sh_ref, *, ts, U):
    i = pl.program_id(1)
    halo = CONV_HALO
    for half, (a_ref, b_ref, ha_ref, hb_ref) in enumerate(((a0_ref, b0_ref, ha0_ref, hb0_ref),
                                                            (a1_ref, b1_ref, ha1_ref, hb1_ref))):
        cs = slice(half * U, (half + 1) * U)
        u_ref[halo:halo + ts, cs] = a_ref[...].astype(F32) * _sigmoid(b_ref[...].astype(F32))
        hu = ha_ref[...].astype(F32) * _sigmoid(hb_ref[...].astype(F32))
        u_ref[0:halo, cs] = jnp.where(i > 0, hu, 0.0)
    C = 2 * U
    rc = 64
    off = halo - (CONV_WIDTH - 1)

    sub = 8
    n_al = ts + halo - sub

    def chan_body(cc, carry):
        c0 = pl.multiple_of(cc * LANES, LANES)
        sh_ref[0] = u_ref[:, pl.ds(c0, LANES)]
        for b in range(1, sub):
            sh_ref[b, 0:n_al, :] = u_ref[b:b + n_al, pl.ds(c0, LANES)]
        for rb in range(ts // rc):
            acc = jnp.zeros((rc, LANES), F32) + bias_ref[:, pl.ds(c0, LANES)]
            for j in range(CONV_WIDTH):
                a, b = divmod(off + j, sub)
                r0 = rb * rc + a * sub
                acc += w_ref[j:j + 1, pl.ds(c0, LANES)] * sh_ref[b, r0:r0 + rc, :]
            c_ref[rb * rc:(rb + 1) * rc, pl.ds(c0, LANES)] = acc
        return carry

    lax.fori_loop(0, C // LANES, chan_body, 0)

    rn = 16

    def norm_body(rb, carry):
        r0 = pl.multiple_of(rb * rn, rn)
        v = c_ref[pl.ds(r0, rn), :]
        mu = jnp.mean(v, axis=-1, keepdims=True)
        dv = v - mu
        var = jnp.mean(dv * dv, axis=-1, keepdims=True)
        y = dv * lax.rsqrt(var + NORM_EPS) * g_ref[...] + be_ref[...]
        o_ref[pl.ds(r0, rn), :] = (y * _sigmoid(y)).astype(o_ref.dtype)
        return carry

    lax.fori_loop(0, ts // rn, norm_body, 0, unroll=4)


def conv_branch(proj, B, S, U, conv_dw, conv_bias, ln_gain, ln_bias):
    IN = proj.shape[1]
    C = 2 * U
    ts = min(256, S)
    pv = proj.reshape(B, S, IN)
    hb = ts // CONV_HALO
    cur = lambda blk: pl.BlockSpec((None, ts, U), lambda b, i, blk=blk: (b, i, blk))
    prv = lambda blk: pl.BlockSpec((None, CONV_HALO, U), lambda b, i, blk=blk: (b, jnp.maximum(i * hb - 1, 0), blk))
    vec = pl.BlockSpec((1, C), lambda b, i: (0, 0))
    out = pl.pallas_call(
        functools.partial(_conv_kernel, ts=ts, U=U),
        grid=(B, S // ts),
        in_specs=[cur(0), cur(1), cur(2), cur(3), prv(0), prv(1), prv(2), prv(3),
                  pl.BlockSpec((CONV_WIDTH, C), lambda b, i: (0, 0)), vec, vec, vec],
        out_specs=pl.BlockSpec((None, ts, C), lambda b, i: (b, i, 0)),
        out_shape=jax.ShapeDtypeStruct((B, S, C), BF16),
        scratch_shapes=[pltpu.VMEM((ts + CONV_HALO, C), F32), pltpu.VMEM((ts, C), F32),
                        pltpu.VMEM((8, ts + CONV_HALO, LANES), F32)],
        compiler_params=_cparams(("parallel", "parallel")),
        name="conv_branch",
    )(pv, pv, pv, pv, pv, pv, pv, pv, conv_dw, conv_bias.reshape(1, C), ln_gain.reshape(1, C), ln_bias.reshape(1, C))
    return out.reshape(B * S, C)


def _gateproj_kernel(cn_ref, am_ref, wc_ref, wa_ref, gc_ref, ga_ref, o_ref):
    conv = jnp.dot(cn_ref[...], wc_ref[...], preferred_element_type=F32)
    z = _sigmoid(gc_ref[...].astype(F32)) * conv
    attn = jnp.dot(am_ref[...], wa_ref[...], preferred_element_type=F32)
    z += _sigmoid(ga_ref[...].astype(F32)) * attn
    o_ref[...] = z.astype(o_ref.dtype)


def gate_proj(cn, am, wc, wa, proj, U):
    N, C = cn.shape
    D = wc.shape[1]
    tn = U
    tm = min(1024, N)
    return pl.pallas_call(
        _gateproj_kernel,
        grid=(D // tn, N // tm),
        in_specs=[pl.BlockSpec((tm, C), lambda j, i: (i, 0)),
                  pl.BlockSpec((tm, U), lambda j, i: (i, 0)),
                  pl.BlockSpec((C, tn), lambda j, i: (0, j)),
                  pl.BlockSpec((U, tn), lambda j, i: (0, j)),
                  pl.BlockSpec((tm, tn), lambda j, i: (i, 4 + j)),
                  pl.BlockSpec((tm, tn), lambda j, i: (i, 8 + j))],
        out_specs=pl.BlockSpec((tm, tn), lambda j, i: (i, j)),
        out_shape=jax.ShapeDtypeStruct((N, D), BF16),
        compiler_params=_cparams(("parallel", "parallel")),
        name="gate_proj",
    )(cn, am, wc, wa, proj, proj)


def _router_kernel(y_ref, xin_ref, mod_ref, gpost_ref, gain_ref, r_ref,
                   x1_ref, hp_ref, eid_ref, wt_ref, rank_ref, cnt_ref, carry_ref, *, tr):
    step = pl.program_id(0)

    @pl.when(step == 0)
    def _():
        carry_ref[...] = jnp.zeros_like(carry_ref)

    y = y_ref[...].astype(F32)
    yms = jnp.mean(y * y, axis=-1, keepdims=True)
    x = xin_ref[...] + mod_ref[2:3, :] * (y * lax.rsqrt(yms + NORM_EPS) * gpost_ref[...])
    x1_ref[...] = x
    D = x.shape[1]
    ms = jnp.mean(x * x, axis=-1, keepdims=True)
    h = x * lax.rsqrt(ms + NORM_EPS) * gain_ref[...]
    h = h * (1.0 + mod_ref[4:5, :]) + mod_ref[3:4, :]
    lo = h[:, :D // 2]
    hi = h[:, D // 2:]
    hp_ref[...] = _pack_halves(lo, hi)
    logits = (jnp.dot(lo.astype(BF16), r_ref[:D // 2, :], preferred_element_type=F32)
              + jnp.dot(hi.astype(BF16), r_ref[D // 2:, :], preferred_element_type=F32))
    lane = lax.broadcasted_iota(I32, logits.shape, 1)
    G = N_EXPERT_GROUPS
    is_g = lane < G
    gl = jnp.where(is_g, logits, NEG_BIG)
    gmax = jnp.max(gl, axis=-1, keepdims=True)
    grp = jnp.min(jnp.where(gl == gmax, lane, LANES), axis=-1, keepdims=True)
    p_grp = 1.0 / jnp.sum(jnp.where(is_g, jnp.exp(gl - gmax), 0.0), axis=-1, keepdims=True)
    in_grp = (lane >= G) & (lane < G + N_EXPERTS) & (((lane - G) // EXPERTS_PER_GROUP) == grp)
    el = jnp.where(in_grp, logits, NEG_BIG)
    v0 = jnp.max(el, axis=-1, keepdims=True)
    i0 = jnp.min(jnp.where(in_grp & (el == v0), lane, LANES), axis=-1, keepdims=True)
    in2 = in_grp & (lane != i0)
    el2 = jnp.where(in2, logits, NEG_BIG)
    v1 = jnp.max(el2, axis=-1, keepdims=True)
    i1 = jnp.min(jnp.where(in2 & (el2 == v1), lane, LANES), axis=-1, keepdims=True)
    e1 = jnp.exp(v1 - v0)
    w0 = p_grp / (1.0 + e1)
    w1 = p_grp * e1 / (1.0 + e1)
    ex0 = i0 - G
    ex1 = i1 - G
    eid_ref[...] = jnp.where(lane == 0, ex0, jnp.where(lane == 1, ex1, 0))
    wt_ref[...] = jnp.where(lane == 0, w0, jnp.where(lane == 1, w1, 0.0))
    oh0 = (lane == ex0).astype(F32)
    oh1 = (lane == ex1).astype(F32)
    both = oh0 + oh1
    rr = lax.broadcasted_iota(I32, (tr, tr), 0)
    cc = lax.broadcasted_iota(I32, (tr, tr), 1)
    tril = (cc < rr).astype(BF16)
    before = jnp.dot(tril, both.astype(BF16), preferred_element_type=F32) + carry_ref[0:1, :]
    rk0 = jnp.sum(before * oh0, axis=-1, keepdims=True)
    rk1 = jnp.sum(before * oh1, axis=-1, keepdims=True)
    rank_ref[...] = jnp.where(lane == 0, rk0, jnp.where(lane == 1, rk1, 0.0))
    newc = carry_ref[0:1, :] + jnp.sum(both, axis=0, keepdims=True)
    carry_ref[...] = jnp.broadcast_to(newc, carry_ref.shape)
    cnt_ref[...] = jnp.broadcast_to(newc, cnt_ref.shape)


def residual_prenorm_router(y, x2d, mod, gain_post, gain, rcat, S):
    N, D = x2d.shape
    tr = min(256, S)
    tiles_per_b = S // tr
    lane_spec = pl.BlockSpec((tr, LANES), lambda i: (i, 0))
    row_spec = pl.BlockSpec((tr, D), lambda i: (i, 0))
    vec_spec = pl.BlockSpec((1, D), lambda i: (0, 0))
    return pl.pallas_call(
        functools.partial(_router_kernel, tr=tr),
        grid=(N // tr,),
        in_specs=[row_spec, row_spec,
                  pl.BlockSpec((None, 6, D), lambda i: (i // tiles_per_b, 0, 0)),
                  vec_spec, vec_spec,
                  pl.BlockSpec((D, LANES), lambda i: (0, 0))],
        out_specs=[row_spec, pl.BlockSpec((tr, D // 2), lambda i: (i, 0)), lane_spec, lane_spec, lane_spec,
                   pl.BlockSpec((8, LANES), lambda i: (0, 0))],
        out_shape=[jax.ShapeDtypeStruct((N, D), F32),
                   jax.ShapeDtypeStruct((N, D // 2), U32),
                   jax.ShapeDtypeStruct((N, LANES), I32),
                   jax.ShapeDtypeStruct((N, LANES), F32),
                   jax.ShapeDtypeStruct((N, LANES), F32),
                   jax.ShapeDtypeStruct((8, LANES), F32)],
        scratch_shapes=[pltpu.VMEM((8, LANES), F32)],
        compiler_params=_cparams(("arbitrary",)),
        name="residual_prenorm_router",
    )(y, x2d, mod, gain_post.reshape(1, D), gain.reshape(1, D), rcat)


def _dest_kernel(eid_ref, rank_ref, start_ref, o_ref):
    lane = lax.broadcasted_iota(I32, eid_ref.shape, 1)
    eid = eid_ref[...]
    rank = rank_ref[...]
    start = start_ref[0:1, :]
    d = []
    for k in range(2):
        oh = (lane == eid[:, k:k + 1]).astype(F32)
        d.append(jnp.sum(oh * start, axis=-1, keepdims=True) + rank[:, k:k + 1])
    o_ref[...] = jnp.where(lane == 0, d[0], jnp.where(lane == 1, d[1], 0.0)).astype(I32)


def dest_rows(eid, rank, pad_start):
    N = eid.shape[0]
    tr = min(1024, N)
    spec = pl.BlockSpec((tr, LANES), lambda i: (i, 0))
    return pl.pallas_call(
        _dest_kernel,
        grid=(N // tr,),
        in_specs=[spec, spec, pl.BlockSpec((8, LANES), lambda i: (0, 0))],
        out_specs=spec,
        out_shape=jax.ShapeDtypeStruct((N, LANES), I32),
        compiler_params=_cparams(("parallel",)),
        name="dest_rows",
    )(eid, rank, pad_start)


def _scatter_kernel(dest_ref, h_ref, xs_in_ref, xs_ref, sem, *, T):
    del xs_in_ref

    sub = 8

    def issue(tb, carry):
        for r in range(sub):
            t = tb * sub + r
            for k in range(2):
                d = dest_ref[0, 0, 2 * t + k]
                pltpu.make_async_copy(h_ref.at[pl.ds(t, 1)], xs_ref.at[pl.ds(d, 1)], sem).start()
        return carry

    lax.fori_loop(0, T // sub, issue, 0)
    for k in range(2):
        pltpu.make_async_copy(h_ref, xs_ref.at[pl.ds(0, T)], sem).wait()


def scatter_rows(hp, dest, P):
    N, W = hp.shape
    T = min(256, N)
    dest_s = dest[:, :2].reshape(N // T, 1, 2 * T)
    xs0 = jnp.zeros((P, W), hp.dtype)
    return pl.pallas_call(
        functools.partial(_scatter_kernel, T=T),
        grid=(N // T,),
        in_specs=[pl.BlockSpec((1, 1, 2 * T), lambda i: (i, 0, 0), memory_space=pltpu.SMEM),
                  pl.BlockSpec((T, W), lambda i: (i, 0)),
                  pl.BlockSpec(memory_space=pl.ANY)],
        out_specs=pl.BlockSpec(memory_space=pl.ANY),
        out_shape=jax.ShapeDtypeStruct((P, W), hp.dtype),
        scratch_shapes=[pltpu.SemaphoreType.DMA(())],
        input_output_aliases={2: 0},
        compiler_params=_cparams(("arbitrary",)),
        name="scatter_rows",
    )(dest_s, hp, xs0)


def _expert_chunks(st_ref, nc_ref, in_hbm, out_hbm, ibuf, obuf, isem, osem, compute):
    e = pl.program_id(0)
    n_e = pl.num_programs(0)
    n = nc_ref[e]
    TM = ibuf.shape[1]
    base = pl.multiple_of(st_ref[e], TM)
    g0 = base // TM

    def icopy(row, slot):
        return pltpu.make_async_copy(in_hbm.at[pl.ds(row, TM)], ibuf.at[slot], isem.at[slot])

    def ocopy(row, slot):
        return pltpu.make_async_copy(obuf.at[slot], out_hbm.at[pl.ds(row, TM)], osem.at[slot])

    @pl.when((e == 0) & (n > 0))
    def _():
        icopy(base, 0).start()

    def body(c, carry):
        g = g0 + c
        slot = g % 2
        row = pl.multiple_of(base + c * TM, TM)

        @pl.when(c + 1 < n)
        def _():
            icopy(row + TM, 1 - slot).start()

        icopy(row, slot).wait()

        @pl.when(g >= 2)
        def _():
            ocopy(row, slot).wait()

        compute(ibuf.at[slot], obuf.at[slot])
        ocopy(row, slot).start()
        return carry

    lax.fori_loop(0, n, body, 0)

    @pl.when(e + 1 < n_e)
    def _():
        @pl.when(nc_ref[e + 1] > 0)
        def _():
            icopy(pl.multiple_of(st_ref[e + 1], TM), (g0 + n) % 2).start()

    @pl.when(e == n_e - 1)
    def _():
        g_end = g0 + n

        @pl.when(g_end >= 2)
        def _():
            ocopy(base, g_end % 2).wait()

        @pl.when(g_end >= 1)
        def _():
            ocopy(base, (g_end + 1) % 2).wait()

        used = pl.multiple_of(base + n * TM, TM)
        n_tail = (out_hbm.shape[0] - used) // TM
        obuf[0] = jnp.zeros(obuf.shape[1:], obuf.dtype)

        def zstart(c, carry):
            ocopy(pl.multiple_of(used + c * TM, TM), 0).start()
            return carry

        def zwait(c, carry):
            ocopy(used, 0).wait()
            return carry

        lax.fori_loop(0, n_tail, zstart, 0)
        lax.fori_loop(0, n_tail, zwait, 0)


WEIGHT_DMA_QUEUE = 1


def _stream_expert_weights(parts, wsem):
    e = pl.program_id(0)
    n_e = pl.num_programs(0)
    slot = e % 2

    def copy(k, ei, s):
        src, dst = parts[k]
        return pltpu.make_async_copy(src(ei), dst(s), wsem.at[k, s])

    @pl.when(e == 0)
    def _():
        for k in range(len(parts)):
            copy(k, 0, 0).start(priority=WEIGHT_DMA_QUEUE)

    @pl.when(e + 1 < n_e)
    def _():
        for k in range(len(parts)):
            copy(k, e + 1, 1 - slot).start(priority=WEIGHT_DMA_QUEUE)

    for k in range(len(parts)):
        copy(k, e, slot).wait()
    return slot


def _moe_up_kernel(st_ref, nc_ref, xs_hbm, w1_hbm, w3_hbm, o_hbm, wb_ref, w1_buf, w3_buf, ibuf, obuf,
                   wsem, isem, osem, *, F):
    slot = _stream_expert_weights([(lambda ei: w1_hbm.at[ei], lambda s: w1_buf.at[s]),
                                   (lambda ei: w3_hbm.at[ei], lambda s: w3_buf.at[s])], wsem)

    @pl.when(nc_ref[pl.program_id(0)] > 0)
    def _():
        wb_ref[:, :F] = w1_buf[slot].astype(BF16)
        wb_ref[:, F:] = w3_buf[slot].astype(BF16)

    def compute(x_ref, o_ref):
        lo, hi = _unpack_halves(x_ref[...])
        half = wb_ref.shape[0] // 2
        hcat = (jnp.dot(lo.astype(BF16), wb_ref[:half, :], preferred_element_type=F32)
                + jnp.dot(hi.astype(BF16), wb_ref[half:, :], preferred_element_type=F32))
        a = hcat[:, :F]
        o_ref[...] = (a * _sigmoid(a) * hcat[:, F:]).astype(o_ref.dtype)

    _expert_chunks(st_ref, nc_ref, xs_hbm, o_hbm, ibuf, obuf, isem, osem, compute)


def _moe_down_kernel(st_ref, nc_ref, h_hbm, w2_hbm, o_hbm, wb_ref, w2_buf, ibuf, obuf, wsem, isem, osem):
    slot = _stream_expert_weights([(lambda ei: w2_hbm.at[ei], lambda s: w2_buf.at[s])], wsem)

    @pl.when(nc_ref[pl.program_id(0)] > 0)
    def _():
        wb_ref[...] = w2_buf[slot].astype(BF16)

    def compute(h_ref, o_ref):
        y = jnp.dot(h_ref[...], wb_ref[...], preferred_element_type=F32)
        half = y.shape[1] // 2
        o_ref[...] = _pack_halves(y[:, :half], y[:, half:])

    _expert_chunks(st_ref, nc_ref, h_hbm, o_hbm, ibuf, obuf, isem, osem, compute)


def expert_ffn(xs, seg_start, seg_chunks, w1, w3, w2):
    P, Wp = xs.shape
    E, D, F = w1.shape
    TM = MOE_ROWS
    any_spec = pl.BlockSpec(memory_space=pl.ANY)
    dma2 = pltpu.SemaphoreType.DMA((2,))
    hmid = pl.pallas_call(
        functools.partial(_moe_up_kernel, F=F),
        grid_spec=pltpu.PrefetchScalarGridSpec(
            num_scalar_prefetch=2,
            grid=(E,),
            in_specs=[any_spec, any_spec, any_spec],
            out_specs=any_spec,
            scratch_shapes=[pltpu.VMEM((D, 2 * F), BF16), pltpu.VMEM((2, D, F), F32), pltpu.VMEM((2, D, F), F32),
                            pltpu.VMEM((2, TM, Wp), U32), pltpu.VMEM((2, TM, F), BF16),
                            pltpu.SemaphoreType.DMA((2, 2)), dma2, dma2]),
        out_shape=jax.ShapeDtypeStruct((P, F), BF16),
        compiler_params=_cparams(("arbitrary",)),
        name="moe_up",
    )(seg_start, seg_chunks, xs, w1, w3)
    return pl.pallas_call(
        _moe_down_kernel,
        grid_spec=pltpu.PrefetchScalarGridSpec(
            num_scalar_prefetch=2,
            grid=(E,),
            in_specs=[any_spec, any_spec],
            out_specs=any_spec,
            scratch_shapes=[pltpu.VMEM((F, D), BF16), pltpu.VMEM((2, F, D), F32), pltpu.VMEM((2, TM, F), BF16),
                            pltpu.VMEM((2, TM, D // 2), U32), pltpu.SemaphoreType.DMA((1, 2)), dma2, dma2]),
        out_shape=jax.ShapeDtypeStruct((P, D // 2), U32),
        compiler_params=_cparams(("arbitrary",)),
        name="moe_down",
    )(seg_start, seg_chunks, hmid, w2)


def _combine_kernel(pos_ref, nxt_ref, wt_ref, x_ref, mod_ref, gain_ref, y_hbm, o_ref, buf_ref, sem, *, T):
    i = pl.program_id(0)
    n = pl.num_programs(0)
    slot = i % 2

    def gather(p_ref, s):
        sub = 8

        def issue(tb, carry):
            for r in range(sub):
                t = tb * sub + r
                for k in range(2):
                    p = p_ref[0, 0, 2 * t + k]
                    pltpu.make_async_copy(y_hbm.at[pl.ds(p, 1)], buf_ref.at[s, k, pl.ds(t, 1)],
                                          sem.at[s]).start()
            return carry

        lax.fori_loop(0, T // sub, issue, 0)

    @pl.when(i == 0)
    def _():
        gather(pos_ref, 0)

    @pl.when(i + 1 < n)
    def _():
        gather(nxt_ref, 1 - slot)

    for k in range(2):
        pltpu.make_async_copy(y_hbm.at[pl.ds(0, T)], buf_ref.at[slot, k], sem.at[slot]).wait()

    wt = wt_ref[...]
    w0 = wt[:, 0:1]
    w1 = wt[:, 1:2]
    lo0, hi0 = _unpack_halves(buf_ref[slot, 0])
    lo1, hi1 = _unpack_halves(buf_ref[slot, 1])
    ylo = w0 * lo0 + w1 * lo1
    yhi = w0 * hi0 + w1 * hi1
    D = x_ref.shape[1]
    half = D // 2
    ms = (jnp.sum(ylo * ylo, axis=-1, keepdims=True) + jnp.sum(yhi * yhi, axis=-1, keepdims=True)) / D
    inv = lax.rsqrt(ms + NORM_EPS)
    o_ref[:, :half] = x_ref[:, :half] + mod_ref[5:6, :half] * (ylo * inv * gain_ref[:, :half])
    o_ref[:, half:] = x_ref[:, half:] + mod_ref[5:6, half:] * (yhi * inv * gain_ref[:, half:])


def combine(yp, dest, wts, x1, mod, gain, S):
    N, D = x1.shape
    T = min(128, S)
    tiles_per_b = S // T
    pos_s = dest[:, :2].reshape(N // T, 1, 2 * T)
    n_tiles = N // T
    return pl.pallas_call(
        functools.partial(_combine_kernel, T=T),
        grid=(n_tiles,),
        in_specs=[pl.BlockSpec((1, 1, 2 * T), lambda i: (i, 0, 0), memory_space=pltpu.SMEM),
                  pl.BlockSpec((1, 1, 2 * T), lambda i: (jnp.minimum(i + 1, n_tiles - 1), 0, 0),
                               memory_space=pltpu.SMEM),
                  pl.BlockSpec((T, LANES), lambda i: (i, 0)),
                  pl.BlockSpec((T, D), lambda i: (i, 0)),
                  pl.BlockSpec((None, 6, D), lambda i: (i // tiles_per_b, 0, 0)),
                  pl.BlockSpec((1, D), lambda i: (0, 0)),
                  pl.BlockSpec(memory_space=pl.ANY)],
        out_specs=pl.BlockSpec((T, D), lambda i: (i, 0)),
        out_shape=jax.ShapeDtypeStruct((N, D), F32),
        scratch_shapes=[pltpu.VMEM((2, 2, T, D // 2), U32), pltpu.SemaphoreType.DMA((2,))],
        compiler_params=_cparams(("arbitrary",)),
        name="combine",
    )(pos_s, pos_s, wts, x1, mod, gain.reshape(1, D), yp)


def _moe_layout(counts):
    TM = MOE_ROWS
    chunks = (counts.astype(I32) + TM - 1) // TM
    start = (jnp.cumsum(chunks) - chunks) * TM
    return start, chunks


def kernel(x, c, positions, ada_w, ada_b, mix_norm_pre, mix_norm_post, w_in, conv_dw, conv_dw_bias, conv_ln_gain, conv_ln_bias, w_conv_out, w_attn_out, w_out, ffn_norm_pre, ffn_norm_post, router_group, router_expert, expert_w1, expert_w3, expert_w2):
    B, S, D = x.shape
    N = B * S
    U = D // 4
    depth = ada_w.shape[0]
    xc = x.reshape(N, D)
    for layer in range(depth):
        mod = ada_mod(c, ada_w[layer], ada_b[layer]).reshape(B, 6, D)
        h, cosf, sinf = prenorm_rope(xc, mod, mix_norm_pre[layer], positions, S)
        w_in_b = w_in[layer].astype(BF16)
        outs, stats = [], []
        for g, (window, d) in enumerate(ATTN_PATTERNS):
            assert window // d == ATTN_BLOCK and S % (d * ATTN_BLOCK) == 0
            qkv = qkv_proj(h, w_in_b, cosf, sinf, B, S, U, g, d)
            o, st = dilated_attention(qkv, d)
            outs.append(o)
            stats.append(st)
        am = merge_groups(outs, stats, tuple(d for _, d in ATTN_PATTERNS))
        proj = matmul_cols(h, w_in_b, U, 9, "rest_proj")
        cn = conv_branch(proj, B, S, U, conv_dw[layer], conv_dw_bias[layer], conv_ln_gain[layer], conv_ln_bias[layer])
        z = gate_proj(cn, am, w_conv_out[layer].astype(BF16), w_attn_out[layer].astype(BF16), proj, U)
        y = matmul_cols(z, w_out[layer].astype(BF16), U, 0, "out_proj")
        rcat = jnp.zeros((D, LANES), F32)
        rcat = rcat.at[:, :N_EXPERT_GROUPS].set(router_group[layer])
        rcat = rcat.at[:, N_EXPERT_GROUPS:N_EXPERT_GROUPS + N_EXPERTS].set(router_expert[layer]).astype(BF16)
        x1, hp, eid, wts, rank, cnt = residual_prenorm_router(
            y, xc, mod, mix_norm_post[layer], ffn_norm_pre[layer], rcat, S)
        TM = MOE_ROWS
        P = (2 * N + N_EXPERTS * (TM - 1)) // TM * TM
        seg_start, seg_chunks = _moe_layout(cnt[0, :N_EXPERTS])
        start_row = jnp.zeros((8, LANES), F32).at[:, :N_EXPERTS].set(seg_start.astype(F32)[None, :])
        dest = dest_rows(eid, rank, start_row)
        xs = scatter_rows(hp, dest, P)
        yp = expert_ffn(xs, seg_start, seg_chunks, expert_w1[layer], expert_w3[layer], expert_w2[layer])
        xc = combine(yp, dest, wts, x1, mod, ffn_norm_post[layer], S)
    return xc.reshape(B, S, D)
```

```python
import functools

import jax
import jax.numpy as jnp
from jax import lax
from jax.experimental import pallas as pl
from jax.experimental.pallas import tpu as pltpu

F32 = jnp.float32
BF16 = jnp.bfloat16
I32 = jnp.int32
U32 = jnp.uint32

HEAD_DIM = 128
LANES = 128
ATTN_BLOCK = 128
ATTN_PATTERNS = ((128, 1), (512, 4), (2048, 16))
ROPE_THETA = 10000.0
CONV_WIDTH = 31
CONV_HALO = 32
N_EXPERT_GROUPS = 8
EXPERTS_PER_GROUP = 8
N_EXPERTS = 64
NORM_EPS = 1e-6
NEG_BIG = -1e30
MOE_ROWS = 256
V7X_VMEM_LIMIT = 60 * 1024 * 1024


def _cparams(sem):
    return pltpu.CompilerParams(dimension_semantics=sem, vmem_limit_bytes=V7X_VMEM_LIMIT)


def _sigmoid(x):
    return 1.0 / (1.0 + jnp.exp(-x))


def _pack_halves(lo, hi):
    lo_b = lax.bitcast_convert_type(lo.astype(BF16).astype(F32), U32) >> 16
    hi_b = lax.bitcast_convert_type(hi.astype(BF16).astype(F32), U32) & jnp.uint32(0xFFFF0000)
    return hi_b | lo_b


def _unpack_halves(w):
    lo = lax.bitcast_convert_type(w << 16, F32)
    hi = lax.bitcast_convert_type(w & jnp.uint32(0xFFFF0000), F32)
    return lo, hi


def _ada_kernel(c_ref, w_ref, b_ref, o_ref):
    c = c_ref[...]
    cact = (c * _sigmoid(c)).astype(BF16)
    o_ref[...] = jnp.dot(cact, w_ref[...].astype(BF16), preferred_element_type=F32) + b_ref[...]


def ada_mod(c, ada_w, ada_b):
    B, D = c.shape
    W = ada_w.shape[1]
    rows = 8
    cp = jnp.zeros((rows, D), F32).at[:B].set(c)
    tn = min(512, W)
    out = pl.pallas_call(
        _ada_kernel,
        grid=(W // tn,),
        in_specs=[pl.BlockSpec((rows, D), lambda j: (0, 0)),
                  pl.BlockSpec((D, tn), lambda j: (0, j)),
                  pl.BlockSpec((1, tn), lambda j: (0, j))],
        out_specs=pl.BlockSpec((rows, tn), lambda j: (0, j)),
        out_shape=jax.ShapeDtypeStruct((rows, W), F32),
        compiler_params=_cparams(("parallel",)),
        name="ada_mod",
    )(cp, ada_w, ada_b.reshape(1, W))
    return out[:B]


def _prenorm_kernel(x_ref, mod_ref, gain_ref, pos_ref, invf_ref, h_ref, cos_ref, sin_ref):
    x = x_ref[...]
    ms = jnp.mean(x * x, axis=-1, keepdims=True)
    g = gain_ref[...] * (1.0 + mod_ref[1:2, :])
    h_ref[...] = ((x * lax.rsqrt(ms + NORM_EPS)) * g + mod_ref[0:1, :]).astype(BF16)
    ang = pos_ref[...].astype(F32) * invf_ref[...]
    lane = lax.broadcasted_iota(I32, ang.shape, 1)
    sn = jnp.sin(ang)
    cos_ref[...] = jnp.cos(ang)
    sin_ref[...] = jnp.where(lane < HEAD_DIM // 2, -sn, sn)


def prenorm_rope(x2d, mod, gain, positions, S):
    N, D = x2d.shape
    tr = min(256, S)
    tiles_per_b = S // tr
    half = HEAD_DIM // 2
    inv = ROPE_THETA ** (-jnp.arange(0, HEAD_DIM, 2, dtype=F32) / HEAD_DIM)
    invf = jnp.concatenate([inv, inv]).reshape(1, HEAD_DIM)
    del half
    return pl.pallas_call(
        _prenorm_kernel,
        grid=(N // tr,),
        in_specs=[pl.BlockSpec((tr, D), lambda i: (i, 0)),
                  pl.BlockSpec((None, 6, D), lambda i: (i // tiles_per_b, 0, 0)),
                  pl.BlockSpec((1, D), lambda i: (0, 0)),
                  pl.BlockSpec((tr, 1), lambda i: (i, 0)),
                  pl.BlockSpec((1, HEAD_DIM), lambda i: (0, 0))],
        out_specs=[pl.BlockSpec((tr, D), lambda i: (i, 0)),
                   pl.BlockSpec((tr, HEAD_DIM), lambda i: (i, 0)),
                   pl.BlockSpec((tr, HEAD_DIM), lambda i: (i, 0))],
        out_shape=[jax.ShapeDtypeStruct((N, D), BF16),
                   jax.ShapeDtypeStruct((N, HEAD_DIM), F32),
                   jax.ShapeDtypeStruct((N, HEAD_DIM), F32)],
        compiler_params=_cparams(("parallel",)),
        name="prenorm_rope",
    )(x2d, mod, gain.reshape(1, D), positions.reshape(N, 1), invf)


def _residue_rows(ref, r, d):
    if d == 1:
        return ref[...]
    return ref[pl.ds(r, ref.shape[0] // d, stride=d), :]


def _qkv_kernel(a_ref, w_ref, cos_ref, sin_ref, o_ref, acc_ref, *, d, heads, scale, n_i):
    s = pl.program_id(0)

    @pl.when(s == 0)
    def _():
        acc_ref[...] = jnp.zeros_like(acc_ref)

    kind = jnp.maximum(s - 1, 0) // n_i
    sc = jnp.where(kind == 0, scale, 1.0).astype(F32)
    is_v = kind == 2
    for r in range(d):
        c = jnp.where(is_v, 1.0, _residue_rows(cos_ref, r, d) * sc)
        sn = jnp.where(is_v, 0.0, _residue_rows(sin_ref, r, d) * sc)
        for h in range(heads):
            t = _residue_rows(acc_ref.at[h], r, d)
            o_ref[r, :, h * HEAD_DIM:(h + 1) * HEAD_DIM] = (
                t * c + pltpu.roll(t, HEAD_DIM // 2, 1) * sn).astype(o_ref.dtype)
    acc = jnp.dot(a_ref[...], w_ref[...], preferred_element_type=F32)
    for h in range(heads):
        acc_ref[h] = acc[:, h * HEAD_DIM:(h + 1) * HEAD_DIM]


def qkv_proj(h, w_bf16, cosf, sinf, B, S, U, g, d):
    N, D = h.shape
    tm = min(1024, S)
    tiles_per_b = S // tm
    n_i = N // tm
    last = 3 * n_i - 1
    kern = functools.partial(_qkv_kernel, d=d, heads=U // HEAD_DIM, scale=HEAD_DIM ** -0.5, n_i=n_i)

    def cur(s):
        return jnp.minimum(s, last)

    def fin(s):
        return jnp.maximum(s - 1, 0)

    return pl.pallas_call(
        kern,
        grid=(3 * n_i + 1,),
        in_specs=[pl.BlockSpec((tm, D), lambda s: (cur(s) % n_i, 0)),
                  pl.BlockSpec((D, U), lambda s: (0, (cur(s) // n_i) * 3 + g)),
                  pl.BlockSpec((tm, HEAD_DIM), lambda s: (fin(s) % n_i, 0)),
                  pl.BlockSpec((tm, HEAD_DIM), lambda s: (fin(s) % n_i, 0))],
        out_specs=pl.BlockSpec((None, None, d, tm // d, U),
                               lambda s: (fin(s) // n_i, (fin(s) % n_i) // tiles_per_b, 0,
                                          (fin(s) % n_i) % tiles_per_b, 0)),
        out_shape=jax.ShapeDtypeStruct((3, B, d, S // d, U), BF16),
        scratch_shapes=[pltpu.VMEM((U // HEAD_DIM, tm, HEAD_DIM), F32)],
        compiler_params=_cparams(("arbitrary",)),
        name=f"qkv_proj_d{d}",
    )(h, w_bf16, cosf, sinf)


def _matmul_kernel(a_ref, w_ref, o_ref):
    o_ref[...] = jnp.dot(a_ref[...], w_ref[...], preferred_element_type=F32).astype(o_ref.dtype)


def matmul_cols(a, w_bf16, U, first_tile, name):
    N, K = a.shape
    n_tiles = w_bf16.shape[1] // U - first_tile
    tm = min(1024, N)
    return pl.pallas_call(
        _matmul_kernel,
        grid=(n_tiles, N // tm),
        in_specs=[pl.BlockSpec((tm, K), lambda j, i: (i, 0)),
                  pl.BlockSpec((K, U), lambda j, i: (0, first_tile + j))],
        out_specs=pl.BlockSpec((tm, U), lambda j, i: (i, j)),
        out_shape=jax.ShapeDtypeStruct((N, n_tiles * U), BF16),
        compiler_params=_cparams(("parallel", "parallel")),
        name=name,
    )(a, w_bf16)


def _attn_kernel(q_ref, kc_ref, kp_ref, vc_ref, vp_ref, o_ref, st_ref, kx_ref, vx_ref, *, QB, H):
    i = pl.program_id(2)
    blk = ATTN_BLOCK
    kx_ref[0:blk, :] = kp_ref[...]
    kx_ref[blk:, :] = kc_ref[...]
    ones = jnp.ones((vx_ref.shape[0], HEAD_DIM), vx_ref.dtype)
    for h in range(H):
        hs = slice(h * HEAD_DIM, (h + 1) * HEAD_DIM)
        vx_ref[0:blk, 2 * h * HEAD_DIM:(2 * h + 1) * HEAD_DIM] = vp_ref[:, hs]
        vx_ref[blk:, 2 * h * HEAD_DIM:(2 * h + 1) * HEAD_DIM] = vc_ref[:, hs]
        vx_ref[:, (2 * h + 1) * HEAD_DIM:(2 * h + 2) * HEAD_DIM] = ones
    rows = lax.broadcasted_iota(I32, (blk, 2 * blk), 0)
    keys = lax.broadcasted_iota(I32, (blk, 2 * blk), 1)
    band = (keys >= rows) & (keys - blk <= rows)
    lane = lax.broadcasted_iota(I32, (blk, LANES), 1)
    dn = (((1,), (1,)), ((), ()))

    def body(qb, carry):
        r0 = pl.multiple_of(qb * blk, blk)
        ok = band & ((keys >= blk) | (i * QB + qb > 0))
        q = [q_ref[pl.ds(r0, blk), h * HEAD_DIM:(h + 1) * HEAD_DIM] for h in range(H)]
        k = [kx_ref[pl.ds(r0, 2 * blk), h * HEAD_DIM:(h + 1) * HEAD_DIM] for h in range(H)]
        s = [jnp.where(ok, lax.dot_general(q[h], k[h], dn, preferred_element_type=F32), NEG_BIG) for h in range(H)]
        m = [jnp.max(s[h], axis=-1, keepdims=True) for h in range(H)]
        p = [jnp.exp(s[h] - m[h]).astype(BF16) for h in range(H)]
        st = jnp.zeros((blk, LANES), F32)
        for h in range(H):
            v1 = vx_ref[pl.ds(r0, 2 * blk), 2 * h * HEAD_DIM:(2 * h + 2) * HEAD_DIM]
            acc = jnp.dot(p[h], v1, preferred_element_type=F32)
            l = acc[:, HEAD_DIM:]
            o_ref[pl.ds(r0, blk), h * HEAD_DIM:(h + 1) * HEAD_DIM] = (acc[:, :HEAD_DIM] / l).astype(o_ref.dtype)
            st = jnp.where(lane == h, m[h], st)
            st = jnp.where(lane == H + h, l, st)
        st_ref[pl.ds(r0, blk), :] = st
        return carry

    lax.fori_loop(0, QB, body, 0)


def dilated_attention(qkv, d):
    _, B, _, L, U = qkv.shape
    H = U // HEAD_DIM
    R = min(512, L)
    QB = R // ATTN_BLOCK

    def cur(kind):
        return pl.BlockSpec((None, None, None, R, U), lambda b, r, i: (kind, b, r, i, 0))

    def prev(kind):
        return pl.BlockSpec((None, None, None, ATTN_BLOCK, U),
                            lambda b, r, i: (kind, b, r, jnp.maximum(i * QB - 1, 0), 0))

    kern = functools.partial(_attn_kernel, QB=QB, H=H)
    return pl.pallas_call(
        kern,
        grid=(B, d, L // R),
        in_specs=[cur(0), cur(1), prev(1), cur(2), prev(2)],
        out_specs=[pl.BlockSpec((None, None, R, U), lambda b, r, i: (b, r, i, 0)),
                   pl.BlockSpec((None, None, R, LANES), lambda b, r, i: (b, r, i, 0))],
        out_shape=[jax.ShapeDtypeStruct((B, d, L, U), BF16),
                   jax.ShapeDtypeStruct((B, d, L, LANES), F32)],
        scratch_shapes=[pltpu.VMEM((R + ATTN_BLOCK, U), BF16), pltpu.VMEM((R + ATTN_BLOCK, 2 * U), BF16)],
        compiler_params=_cparams(("parallel", "parallel", "parallel")),
        name=f"dilated_attn_d{d}",
    )(qkv, qkv, qkv, qkv, qkv)


def _merge_kernel(o0_ref, o1_ref, o2_ref, s0_ref, s1_ref, s2_ref, out_ref, of_ref, sf_ref, *, H, dils):
    for g, (o_ref, s_ref) in enumerate(((o0_ref, s0_ref), (o1_ref, s1_ref), (o2_ref, s2_ref))):
        d = dils[g]
        n = sf_ref.shape[1] // d
        for r in range(d):
            rows = slice(None) if d == 1 else pl.ds(r, n, stride=d)
            sf_ref[g, rows, :] = s_ref[r]
            for h in range(H):
                of_ref[g, h, rows, :] = o_ref[r, :, h * HEAD_DIM:(h + 1) * HEAD_DIM].astype(F32)
    st = [sf_ref[0], sf_ref[1], sf_ref[2]]
    mx = jnp.maximum(jnp.maximum(st[0], st[1]), st[2])
    w = [pltpu.roll(s, LANES - H, 1) * jnp.exp(s - mx) for s in st]
    tot = w[0] + w[1] + w[2]
    coef = [x / tot for x in w]
    for h in range(H):
        hs = slice(h * HEAD_DIM, (h + 1) * HEAD_DIM)
        acc = coef[0][:, h:h + 1] * of_ref[0, h]
        acc += coef[1][:, h:h + 1] * of_ref[1, h]
        acc += coef[2][:, h:h + 1] * of_ref[2, h]
        out_ref[:, hs] = acc.astype(out_ref.dtype)


def merge_groups(outs, stats, dils):
    B, d0, L0, U = outs[0].shape
    S = d0 * L0
    H = U // HEAD_DIM
    tm = min(512, S)
    tiles_per_b = S // tm

    def ospec(d, w):
        return pl.BlockSpec((None, d, tm // d, w), lambda i: (i // tiles_per_b, 0, i % tiles_per_b, 0))

    return pl.pallas_call(
        functools.partial(_merge_kernel, H=H, dils=dils),
        grid=(B * S // tm,),
        in_specs=[ospec(d, U) for d in dils] + [ospec(d, LANES) for d in dils],
        out_specs=pl.BlockSpec((tm, U), lambda i: (i, 0)),
        out_shape=jax.ShapeDtypeStruct((B * S, U), BF16),
        scratch_shapes=[pltpu.VMEM((3, H, tm, HEAD_DIM), F32), pltpu.VMEM((3, tm, LANES), F32)],
        compiler_params=_cparams(("parallel",)),
        name="merge_groups",
    )(*outs, *stats)


def _conv_kernel(a0_ref, a1_ref, b0_ref, b1_ref, ha0_ref, ha1_ref, hb0_ref, hb1_ref,
                 w_ref, bias_ref, g_ref, be_ref, o_ref, u_ref, c_ref, sh_ref, *, ts, U):
    i = pl.program_id(1)
    halo = CONV_HALO
    for half, (a_ref, b_ref, ha_ref, hb_ref) in enumerate(((a0_ref, b0_ref, ha0_ref, hb0_ref),
                                                            (a1_ref, b1_ref, ha1_ref, hb1_ref))):
        cs = slice(half * U, (half + 1) * U)
        u_ref[halo:halo + ts, cs] = a_ref[...].astype(F32) * _sigmoid(b_ref[...].astype(F32))
        hu = ha_ref[...].astype(F32) * _sigmoid(hb_ref[...].astype(F32))
        u_ref[0:halo, cs] = jnp.where(i > 0, hu, 0.0)
    C = 2 * U
    rc = 64
    off = halo - (CONV_WIDTH - 1)

    sub = 8
    n_al = ts + halo - sub

    def chan_body(cc, carry):
        c0 = pl.multiple_of(cc * LANES, LANES)
        sh_ref[0] = u_ref[:, pl.ds(c0, LANES)]
        for b in range(1, sub):
            sh_ref[b, 0:n_al, :] = u_ref[b:b + n_al, pl.ds(c0, LANES)]
        for rb in range(ts // rc):
            acc = jnp.zeros((rc, LANES), F32) + bias_ref[:, pl.ds(c0, LANES)]
            for j in range(CONV_WIDTH):
                a, b = divmod(off + j, sub)
                r0 = rb * rc + a * sub
                acc += w_ref[j:j + 1, pl.ds(c0, LANES)] * sh_ref[b, r0:r0 + rc, :]
            c_ref[rb * rc:(rb + 1) * rc, pl.ds(c0, LANES)] = acc
        return carry

    lax.fori_loop(0, C // LANES, chan_body, 0)

    rn = 16

    def norm_body(rb, carry):
        r0 = pl.multiple_of(rb * rn, rn)
        v = c_ref[pl.ds(r0, rn), :]
        mu = jnp.mean(v, axis=-1, keepdims=True)
        dv = v - mu
        var = jnp.mean(dv * dv, axis=-1, keepdims=True)
        y = dv * lax.rsqrt(var + NORM_EPS) * g_ref[...] + be_ref[...]
        o_ref[pl.ds(r0, rn), :] = (y * _sigmoid(y)).astype(o_ref.dtype)
        return carry

    lax.fori_loop(0, ts // rn, norm_body, 0, unroll=4)


def conv_branch(proj, B, S, U, conv_dw, conv_bias, ln_gain, ln_bias):
    IN = proj.shape[1]
    C = 2 * U
    ts = min(256, S)
    pv = proj.reshape(B, S, IN)
    hb = ts // CONV_HALO
    cur = lambda blk: pl.BlockSpec((None, ts, U), lambda b, i, blk=blk: (b, i, blk))
    prv = lambda blk: pl.BlockSpec((None, CONV_HALO, U), lambda b, i, blk=blk: (b, jnp.maximum(i * hb - 1, 0), blk))
    vec = pl.BlockSpec((1, C), lambda b, i: (0, 0))
    out = pl.pallas_call(
        functools.partial(_conv_kernel, ts=ts, U=U),
        grid=(B, S // ts),
        in_specs=[cur(0), cur(1), cur(2), cur(3), prv(0), prv(1), prv(2), prv(3),
                  pl.BlockSpec((CONV_WIDTH, C), lambda b, i: (0, 0)), vec, vec, vec],
        out_specs=pl.BlockSpec((None, ts, C), lambda b, i: (b, i, 0)),
        out_shape=jax.ShapeDtypeStruct((B, S, C), BF16),
        scratch_shapes=[pltpu.VMEM((ts + CONV_HALO, C), F32), pltpu.VMEM((ts, C), F32),
                        pltpu.VMEM((8, ts + CONV_HALO, LANES), F32)],
        compiler_params=_cparams(("parallel", "parallel")),
        name="conv_branch",
    )(pv, pv, pv, pv, pv, pv, pv, pv, conv_dw, conv_bias.reshape(1, C), ln_gain.reshape(1, C), ln_bias.reshape(1, C))
    return out.reshape(B * S, C)


def _gateproj_kernel(cn_ref, am_ref, wc_ref, wa_ref, gc_ref, ga_ref, o_ref):
    conv = jnp.dot(cn_ref[...], wc_ref[...], preferred_element_type=F32)
    z = _sigmoid(gc_ref[...].astype(F32)) * conv
    attn = jnp.dot(am_ref[...], wa_ref[...], preferred_element_type=F32)
    z += _sigmoid(ga_ref[...].astype(F32)) * attn
    o_ref[...] = z.astype(o_ref.dtype)


def gate_proj(cn, am, wc, wa, proj, U):
    N, C = cn.shape
    D = wc.shape[1]
    tn = U
    tm = min(1024, N)
    return pl.pallas_call(
        _gateproj_kernel,
        grid=(D // tn, N // tm),
        in_specs=[pl.BlockSpec((tm, C), lambda j, i: (i, 0)),
                  pl.BlockSpec((tm, U), lambda j, i: (i, 0)),
                  pl.BlockSpec((C, tn), lambda j, i: (0, j)),
                  pl.BlockSpec((U, tn), lambda j, i: (0, j)),
                  pl.BlockSpec((tm, tn), lambda j, i: (i, 4 + j)),
                  pl.BlockSpec((tm, tn), lambda j, i: (i, 8 + j))],
        out_specs=pl.BlockSpec((tm, tn), lambda j, i: (i, j)),
        out_shape=jax.ShapeDtypeStruct((N, D), BF16),
        compiler_params=_cparams(("parallel", "parallel")),
        name="gate_proj",
    )(cn, am, wc, wa, proj, proj)


def _router_kernel(y_ref, xin_ref, mod_ref, gpost_ref, gain_ref, r_ref,
                   x1_ref, hp_ref, eid_ref, wt_ref, rank_ref, cnt_ref, carry_ref, *, tr):
    step = pl.program_id(0)

    @pl.when(step == 0)
    def _():
        carry_ref[...] = jnp.zeros_like(carry_ref)

    y = y_ref[...].astype(F32)
    yms = jnp.mean(y * y, axis=-1, keepdims=True)
    x = xin_ref[...] + (y * lax.rsqrt(yms + NORM_EPS)) * (mod_ref[2:3, :] * gpost_ref[...])
    x1_ref[...] = x
    D = x.shape[1]
    ms = jnp.mean(x * x, axis=-1, keepdims=True)
    h = (x * lax.rsqrt(ms + NORM_EPS)) * (gain_ref[...] * (1.0 + mod_ref[4:5, :])) + mod_ref[3:4, :]
    lo = h[:, :D // 2]
    hi = h[:, D // 2:]
    hp_ref[...] = _pack_halves(lo, hi)
    logits = (jnp.dot(lo.astype(BF16), r_ref[:D // 2, :], preferred_element_type=F32)
              + jnp.dot(hi.astype(BF16), r_ref[D // 2:, :], preferred_element_type=F32))
    lane = lax.broadcasted_iota(I32, logits.shape, 1)
    G = N_EXPERT_GROUPS
    is_g = lane < G
    gl = jnp.where(is_g, logits, NEG_BIG)
    gmax = jnp.max(gl, axis=-1, keepdims=True)
    grp = jnp.min(jnp.where(gl == gmax, lane, LANES), axis=-1, keepdims=True)
    p_grp = 1.0 / jnp.sum(jnp.where(is_g, jnp.exp(gl - gmax), 0.0), axis=-1, keepdims=True)
    in_grp = (lane >= G) & (lane < G + N_EXPERTS) & (((lane - G) // EXPERTS_PER_GROUP) == grp)
    el = jnp.where(in_grp, logits, NEG_BIG)
    v0 = jnp.max(el, axis=-1, keepdims=True)
    i0 = jnp.min(jnp.where(in_grp & (el == v0), lane, LANES), axis=-1, keepdims=True)
    in2 = in_grp & (lane != i0)
    el2 = jnp.where(in2, logits, NEG_BIG)
    v1 = jnp.max(el2, axis=-1, keepdims=True)
    i1 = jnp.min(jnp.where(in2 & (el2 == v1), lane, LANES), axis=-1, keepdims=True)
    e1 = jnp.exp(v1 - v0)
    w0 = p_grp / (1.0 + e1)
    w1 = p_grp * e1 / (1.0 + e1)
    ex0 = i0 - G
    ex1 = i1 - G
    eid_ref[...] = jnp.where(lane == 0, ex0, jnp.where(lane == 1, ex1, 0))
    wt_ref[...] = jnp.where(lane == 0, w0, jnp.where(lane == 1, w1, 0.0))
    oh0 = (lane == ex0).astype(F32)
    oh1 = (lane == ex1).astype(F32)
    both = oh0 + oh1
    rr = lax.broadcasted_iota(I32, (tr, tr), 0)
    cc = lax.broadcasted_iota(I32, (tr, tr), 1)
    tril = (cc < rr).astype(BF16)
    before = jnp.dot(tril, both.astype(BF16), preferred_element_type=F32) + carry_ref[0:1, :]
    rk0 = jnp.sum(before * oh0, axis=-1, keepdims=True)
    rk1 = jnp.sum(before * oh1, axis=-1, keepdims=True)
    rank_ref[...] = jnp.where(lane == 0, rk0, jnp.where(lane == 1, rk1, 0.0))
    newc = carry_ref[0:1, :] + jnp.sum(both, axis=0, keepdims=True)
    carry_ref[...] = jnp.broadcast_to(newc, carry_ref.shape)
    cnt_ref[...] = jnp.broadcast_to(newc, cnt_ref.shape)


def residual_prenorm_router(y, x2d, mod, gain_post, gain, rcat, S):
    N, D = x2d.shape
    tr = min(256, S)
    tiles_per_b = S // tr
    lane_spec = pl.BlockSpec((tr, LANES), lambda i: (i, 0))
    row_spec = pl.BlockSpec((tr, D), lambda i: (i, 0))
    vec_spec = pl.BlockSpec((1, D), lambda i: (0, 0))
    return pl.pallas_call(
        functools.partial(_router_kernel, tr=tr),
        grid=(N // tr,),
        in_specs=[row_spec, row_spec,
                  pl.BlockSpec((None, 6, D), lambda i: (i // tiles_per_b, 0, 0)),
                  vec_spec, vec_spec,
                  pl.BlockSpec((D, LANES), lambda i: (0, 0))],
        out_specs=[row_spec, pl.BlockSpec((tr, D // 2), lambda i: (i, 0)), lane_spec, lane_spec, lane_spec,
                   pl.BlockSpec((8, LANES), lambda i: (0, 0))],
        out_shape=[jax.ShapeDtypeStruct((N, D), F32),
                   jax.ShapeDtypeStruct((N, D // 2), U32),
                   jax.ShapeDtypeStruct((N, LANES), I32),
                   jax.ShapeDtypeStruct((N, LANES), F32),
                   jax.ShapeDtypeStruct((N, LANES), F32),
                   jax.ShapeDtypeStruct((8, LANES), F32)],
        scratch_shapes=[pltpu.VMEM((8, LANES), F32)],
        compiler_params=_cparams(("arbitrary",)),
        name="residual_prenorm_router",
    )(y, x2d, mod, gain_post.reshape(1, D), gain.reshape(1, D), rcat)


def _dest_kernel(eid_ref, rank_ref, start_ref, o_ref):
    lane = lax.broadcasted_iota(I32, eid_ref.shape, 1)
    eid = eid_ref[...]
    rank = rank_ref[...]
    start = start_ref[0:1, :]
    d = []
    for k in range(2):
        oh = (lane == eid[:, k:k + 1]).astype(F32)
        d.append(jnp.sum(oh * start, axis=-1, keepdims=True) + rank[:, k:k + 1])
    o_ref[...] = jnp.where(lane == 0, d[0], jnp.where(lane == 1, d[1], 0.0)).astype(I32)


def dest_rows(eid, rank, pad_start):
    N = eid.shape[0]
    tr = min(1024, N)
    spec = pl.BlockSpec((tr, LANES), lambda i: (i, 0))
    return pl.pallas_call(
        _dest_kernel,
        grid=(N // tr,),
        in_specs=[spec, spec, pl.BlockSpec((8, LANES), lambda i: (0, 0))],
        out_specs=spec,
        out_shape=jax.ShapeDtypeStruct((N, LANES), I32),
        compiler_params=_cparams(("parallel",)),
        name="dest_rows",
    )(eid, rank, pad_start)


def _scatter_kernel(dest_ref, h_ref, xs_in_ref, xs_ref, sem, *, T):
    del xs_in_ref

    sub = 8

    def issue(tb, carry):
        for r in range(sub):
            t = tb * sub + r
            for k in range(2):
                d = dest_ref[0, 0, 2 * t + k]
                pltpu.make_async_copy(h_ref.at[pl.ds(t, 1)], xs_ref.at[pl.ds(d, 1)], sem).start()
        return carry

    lax.fori_loop(0, T // sub, issue, 0)
    for k in range(2):
        pltpu.make_async_copy(h_ref, xs_ref.at[pl.ds(0, T)], sem).wait()


def scatter_rows(hp, dest, P):
    N, W = hp.shape
    T = min(256, N)
    dest_s = dest[:, :2].reshape(N // T, 1, 2 * T)
    xs0 = jnp.zeros((P, W), hp.dtype)
    return pl.pallas_call(
        functools.partial(_scatter_kernel, T=T),
        grid=(N // T,),
        in_specs=[pl.BlockSpec((1, 1, 2 * T), lambda i: (i, 0, 0), memory_space=pltpu.SMEM),
                  pl.BlockSpec((T, W), lambda i: (i, 0)),
                  pl.BlockSpec(memory_space=pl.ANY)],
        out_specs=pl.BlockSpec(memory_space=pl.ANY),
        out_shape=jax.ShapeDtypeStruct((P, W), hp.dtype),
        scratch_shapes=[pltpu.SemaphoreType.DMA(())],
        input_output_aliases={2: 0},
        compiler_params=_cparams(("arbitrary",)),
        name="scatter_rows",
    )(dest_s, hp, xs0)


def _expert_chunks(st_ref, nc_ref, in_hbm, out_hbm, ibuf, obuf, isem, osem, compute):
    e = pl.program_id(0)
    n_e = pl.num_programs(0)
    n = nc_ref[e]
    TM = ibuf.shape[1]
    base = pl.multiple_of(st_ref[e], TM)
    g0 = base // TM

    def icopy(row, slot):
        return pltpu.make_async_copy(in_hbm.at[pl.ds(row, TM)], ibuf.at[slot], isem.at[slot])

    def ocopy(row, slot):
        return pltpu.make_async_copy(obuf.at[slot], out_hbm.at[pl.ds(row, TM)], osem.at[slot])

    @pl.when((e == 0) & (n > 0))
    def _():
        icopy(base, 0).start()

    def body(c, carry):
        g = g0 + c
        slot = g % 2
        row = pl.multiple_of(base + c * TM, TM)

        @pl.when(c + 1 < n)
        def _():
            icopy(row + TM, 1 - slot).start()

        icopy(row, slot).wait()

        @pl.when(g >= 2)
        def _():
            ocopy(row, slot).wait()

        compute(ibuf.at[slot], obuf.at[slot])
        ocopy(row, slot).start()
        return carry

    lax.fori_loop(0, n, body, 0)

    @pl.when(e + 1 < n_e)
    def _():
        @pl.when(nc_ref[e + 1] > 0)
        def _():
            icopy(pl.multiple_of(st_ref[e + 1], TM), (g0 + n) % 2).start()

    @pl.when(e == n_e - 1)
    def _():
        g_end = g0 + n

        @pl.when(g_end >= 2)
        def _():
            ocopy(base, g_end % 2).wait()

        @pl.when(g_end >= 1)
        def _():
            ocopy(base, (g_end + 1) % 2).wait()

        used = pl.multiple_of(base + n * TM, TM)
        n_tail = (out_hbm.shape[0] - used) // TM
        obuf[0] = jnp.zeros(obuf.shape[1:], obuf.dtype)

        def zstart(c, carry):
            ocopy(pl.multiple_of(used + c * TM, TM), 0).start()
            return carry

        def zwait(c, carry):
            ocopy(used, 0).wait()
            return carry

        lax.fori_loop(0, n_tail, zstart, 0)
        lax.fori_loop(0, n_tail, zwait, 0)


WEIGHT_DMA_QUEUE = 1


def _stream_expert_weights(parts, wsem):
    e = pl.program_id(0)
    n_e = pl.num_programs(0)
    slot = e % 2

    def copy(k, ei, s):
        src, dst = parts[k]
        return pltpu.make_async_copy(src(ei), dst(s), wsem.at[k, s])

    @pl.when(e == 0)
    def _():
        for k in range(len(parts)):
            copy(k, 0, 0).start(priority=WEIGHT_DMA_QUEUE)

    @pl.when(e + 1 < n_e)
    def _():
        for k in range(len(parts)):
            copy(k, e + 1, 1 - slot).start(priority=WEIGHT_DMA_QUEUE)

    for k in range(len(parts)):
        copy(k, e, slot).wait()
    return slot


def _moe_up_kernel(st_ref, nc_ref, xs_hbm, w1_hbm, w3_hbm, o_hbm, wb_ref, w1_buf, w3_buf, ibuf, obuf,
                   wsem, isem, osem, *, F):
    slot = _stream_expert_weights([(lambda ei: w1_hbm.at[ei], lambda s: w1_buf.at[s]),
                                   (lambda ei: w3_hbm.at[ei], lambda s: w3_buf.at[s])], wsem)

    @pl.when(nc_ref[pl.program_id(0)] > 0)
    def _():
        wb_ref[:, :F] = w1_buf[slot].astype(BF16)
        wb_ref[:, F:] = w3_buf[slot].astype(BF16)

    def compute(x_ref, o_ref):
        lo, hi = _unpack_halves(x_ref[...])
        half = wb_ref.shape[0] // 2
        hcat = (jnp.dot(lo.astype(BF16), wb_ref[:half, :], preferred_element_type=F32)
                + jnp.dot(hi.astype(BF16), wb_ref[half:, :], preferred_element_type=F32))
        a = hcat[:, :F]
        o_ref[...] = (a * _sigmoid(a) * hcat[:, F:]).astype(o_ref.dtype)

    _expert_chunks(st_ref, nc_ref, xs_hbm, o_hbm, ibuf, obuf, isem, osem, compute)


def _moe_down_kernel(st_ref, nc_ref, h_hbm, w2_hbm, o_hbm, wb_ref, w2_buf, ibuf, obuf, wsem, isem, osem):
    slot = _stream_expert_weights([(lambda ei: w2_hbm.at[ei], lambda s: w2_buf.at[s])], wsem)

    @pl.when(nc_ref[pl.program_id(0)] > 0)
    def _():
        wb_ref[...] = w2_buf[slot].astype(BF16)

    def compute(h_ref, o_ref):
        y = jnp.dot(h_ref[...], wb_ref[...], preferred_element_type=F32)
        half = y.shape[1] // 2
        o_ref[...] = _pack_halves(y[:, :half], y[:, half:])

    _expert_chunks(st_ref, nc_ref, h_hbm, o_hbm, ibuf, obuf, isem, osem, compute)


def expert_ffn(xs, seg_start, seg_chunks, w1, w3, w2):
    P, Wp = xs.shape
    E, D, F = w1.shape
    TM = MOE_ROWS
    any_spec = pl.BlockSpec(memory_space=pl.ANY)
    dma2 = pltpu.SemaphoreType.DMA((2,))
    hmid = pl.pallas_call(
        functools.partial(_moe_up_kernel, F=F),
        grid_spec=pltpu.PrefetchScalarGridSpec(
            num_scalar_prefetch=2,
            grid=(E,),
            in_specs=[any_spec, any_spec, any_spec],
            out_specs=any_spec,
            scratch_shapes=[pltpu.VMEM((D, 2 * F), BF16), pltpu.VMEM((2, D, F), F32), pltpu.VMEM((2, D, F), F32),
                            pltpu.VMEM((2, TM, Wp), U32), pltpu.VMEM((2, TM, F), BF16),
                            pltpu.SemaphoreType.DMA((2, 2)), dma2, dma2]),
        out_shape=jax.ShapeDtypeStruct((P, F), BF16),
        compiler_params=_cparams(("arbitrary",)),
        name="moe_up",
    )(seg_start, seg_chunks, xs, w1, w3)
    return pl.pallas_call(
        _moe_down_kernel,
        grid_spec=pltpu.PrefetchScalarGridSpec(
            num_scalar_prefetch=2,
            grid=(E,),
            in_specs=[any_spec, any_spec],
            out_specs=any_spec,
            scratch_shapes=[pltpu.VMEM((F, D), BF16), pltpu.VMEM((2, F, D), F32), pltpu.VMEM((2, TM, F), BF16),
                            pltpu.VMEM((2, TM, D // 2), U32), pltpu.SemaphoreType.DMA((1, 2)), dma2, dma2]),
        out_shape=jax.ShapeDtypeStruct((P, D // 2), U32),
        compiler_params=_cparams(("arbitrary",)),
        name="moe_down",
    )(seg_start, seg_chunks, hmid, w2)


def _combine_kernel(pos_ref, nxt_ref, wt_ref, x_ref, mod_ref, gain_ref, y_hbm, o_ref, buf_ref, sem, v_ref, *, T):
    i = pl.program_id(0)
    n = pl.num_programs(0)
    slot = i % 2
    sub = 8

    def start_rows(p_ref, s, tb):
        for r in range(sub):
            t = tb * sub + r
            for k in range(2):
                p = p_ref[0, 0, 2 * t + k]
                pltpu.make_async_copy(y_hbm.at[pl.ds(p, 1)], buf_ref.at[s, k, pl.ds(t, 1)], sem.at[s]).start()

    def wait_slot(s):
        for k in range(2):
            pltpu.make_async_copy(y_hbm.at[pl.ds(0, T)], buf_ref.at[s, k], sem.at[s]).wait()

    @pl.when(i == 0)
    def _():
        def first(tb, carry):
            start_rows(pos_ref, 0, tb)
            return carry

        lax.fori_loop(0, T // sub, first, 0)

    wait_slot(slot)
    D = x_ref.shape[1]
    half = D // 2
    v_ref[0:1, :] = mod_ref[5:6, :] * gain_ref[...]

    per = 4

    def body(j, carry):
        for q in range(per):
            rows = pl.ds(pl.multiple_of((j * per + q) * sub, sub), sub)
            wt = wt_ref[rows, :]
            w0 = wt[:, 0:1]
            w1 = wt[:, 1:2]
            lo0, hi0 = _unpack_halves(buf_ref[slot, 0, rows, :])
            lo1, hi1 = _unpack_halves(buf_ref[slot, 1, rows, :])
            ylo = w0 * lo0 + w1 * lo1
            yhi = w0 * hi0 + w1 * hi1
            ms = (jnp.sum(ylo * ylo, axis=-1, keepdims=True) + jnp.sum(yhi * yhi, axis=-1, keepdims=True)) / D
            inv = lax.rsqrt(ms + NORM_EPS)
            o_ref[rows, :half] = x_ref[rows, :half] + (ylo * inv) * v_ref[0:1, :half]
            o_ref[rows, half:] = x_ref[rows, half:] + (yhi * inv) * v_ref[0:1, half:]
        for q in range(per):
            start_rows(nxt_ref, 1 - slot, j * per + q)
        return carry

    lax.fori_loop(0, T // (sub * per), body, 0)

    @pl.when(i == n - 1)
    def _():
        wait_slot(1 - slot)


def combine(yp, dest, wts, x1, mod, gain, S):
    N, D = x1.shape
    T = min(128, S)
    tiles_per_b = S // T
    pos_s = dest[:, :2].reshape(N // T, 1, 2 * T)
    n_tiles = N // T
    return pl.pallas_call(
        functools.partial(_combine_kernel, T=T),
        grid=(n_tiles,),
        in_specs=[pl.BlockSpec((1, 1, 2 * T), lambda i: (i, 0, 0), memory_space=pltpu.SMEM),
                  pl.BlockSpec((1, 1, 2 * T), lambda i: (jnp.minimum(i + 1, n_tiles - 1), 0, 0),
                               memory_space=pltpu.SMEM),
                  pl.BlockSpec((T, LANES), lambda i: (i, 0)),
                  pl.BlockSpec((T, D), lambda i: (i, 0)),
                  pl.BlockSpec((None, 6, D), lambda i: (i // tiles_per_b, 0, 0)),
                  pl.BlockSpec((1, D), lambda i: (0, 0)),
                  pl.BlockSpec(memory_space=pl.ANY)],
        out_specs=pl.BlockSpec((T, D), lambda i: (i, 0)),
        out_shape=jax.ShapeDtypeStruct((N, D), F32),
        scratch_shapes=[pltpu.VMEM((2, 2, T, D // 2), U32), pltpu.SemaphoreType.DMA((2,)),
                        pltpu.VMEM((8, D), F32)],
        compiler_params=_cparams(("arbitrary",)),
        name="combine",
    )(pos_s, pos_s, wts, x1, mod, gain.reshape(1, D), yp)


def _moe_layout(counts):
    TM = MOE_ROWS
    chunks = (counts.astype(I32) + TM - 1) // TM
    start = (jnp.cumsum(chunks) - chunks) * TM
    return start, chunks


def kernel(x, c, positions, ada_w, ada_b, mix_norm_pre, mix_norm_post, w_in, conv_dw, conv_dw_bias, conv_ln_gain, conv_ln_bias, w_conv_out, w_attn_out, w_out, ffn_norm_pre, ffn_norm_post, router_group, router_expert, expert_w1, expert_w3, expert_w2):
    B, S, D = x.shape
    N = B * S
    U = D // 4
    depth = ada_w.shape[0]
    xc = x.reshape(N, D)
    for layer in range(depth):
        mod = ada_mod(c, ada_w[layer], ada_b[layer]).reshape(B, 6, D)
        h, cosf, sinf = prenorm_rope(xc, mod, mix_norm_pre[layer], positions, S)
        w_in_b = w_in[layer].astype(BF16)
        outs, stats = [], []
        for g, (window, d) in enumerate(ATTN_PATTERNS):
            assert window // d == ATTN_BLOCK and S % (d * ATTN_BLOCK) == 0
            qkv = qkv_proj(h, w_in_b, cosf, sinf, B, S, U, g, d)
            o, st = dilated_attention(qkv, d)
            outs.append(o)
            stats.append(st)
        am = merge_groups(outs, stats, tuple(d for _, d in ATTN_PATTERNS))
        proj = matmul_cols(h, w_in_b, U, 9, "rest_proj")
        cn = conv_branch(proj, B, S, U, conv_dw[layer], conv_dw_bias[layer], conv_ln_gain[layer], conv_ln_bias[layer])
        z = gate_proj(cn, am, w_conv_out[layer].astype(BF16), w_attn_out[layer].astype(BF16), proj, U)
        y = matmul_cols(z, w_out[layer].astype(BF16), U, 0, "out_proj")
        rcat = jnp.zeros((D, LANES), F32)
        rcat = rcat.at[:, :N_EXPERT_GROUPS].set(router_group[layer])
        rcat = rcat.at[:, N_EXPERT_GROUPS:N_EXPERT_GROUPS + N_EXPERTS].set(router_expert[layer]).astype(BF16)
        x1, hp, eid, wts, rank, cnt = residual_prenorm_router(
            y, xc, mod, mix_norm_post[layer], ffn_norm_pre[layer], rcat, S)
        TM = MOE_ROWS
        P = (2 * N + N_EXPERTS * (TM - 1)) // TM * TM
        seg_start, seg_chunks = _moe_layout(cnt[0, :N_EXPERTS])
        start_row = jnp.zeros((8, LANES), F32).at[:, :N_EXPERTS].set(seg_start.astype(F32)[None, :])
        dest = dest_rows(eid, rank, start_row)
        xs = scatter_rows(hp, dest, P)
        yp = expert_ffn(xs, seg_start, seg_chunks, expert_w1[layer], expert_w3[layer], expert_w2[layer])
        xc = combine(yp, dest, wts, x1, mod, ffn_norm_post[layer], S)
    return xc.reshape(B, S, D)
```

```python
import functools

import jax
import jax.numpy as jnp
from jax import lax
from jax.experimental import pallas as pl
from jax.experimental.pallas import tpu as pltpu

F32 = jnp.float32
BF16 = jnp.bfloat16
I32 = jnp.int32
U32 = jnp.uint32

HEAD_DIM = 128
LANES = 128
ATTN_BLOCK = 128
ATTN_PATTERNS = ((128, 1), (512, 4), (2048, 16))
ROPE_THETA = 10000.0
CONV_WIDTH = 31
CONV_HALO = 32
N_EXPERT_GROUPS = 8
EXPERTS_PER_GROUP = 8
N_EXPERTS = 64
NORM_EPS = 1e-6
NEG_BIG = -1e30
MOE_ROWS = 256
V7X_VMEM_LIMIT = 60 * 1024 * 1024


def _cparams(sem):
    return pltpu.CompilerParams(dimension_semantics=sem, vmem_limit_bytes=V7X_VMEM_LIMIT)


def _sigmoid(x):
    return 1.0 / (1.0 + jnp.exp(-x))


def _pack_halves(lo, hi):
    lo_b = lax.bitcast_convert_type(lo.astype(BF16).astype(F32), U32) >> 16
    hi_b = lax.bitcast_convert_type(hi.astype(BF16).astype(F32), U32) & jnp.uint32(0xFFFF0000)
    return hi_b | lo_b


def _unpack_halves(w):
    lo = lax.bitcast_convert_type(w << 16, F32)
    hi = lax.bitcast_convert_type(w & jnp.uint32(0xFFFF0000), F32)
    return lo, hi


def _ada_kernel(c_ref, w_ref, b_ref, o_ref):
    c = c_ref[...]
    cact = (c * _sigmoid(c)).astype(BF16)
    o_ref[...] = jnp.dot(cact, w_ref[...].astype(BF16), preferred_element_type=F32) + b_ref[...]


def ada_mod(c, ada_w, ada_b):
    B, D = c.shape
    W = ada_w.shape[1]
    rows = 8
    cp = jnp.zeros((rows, D), F32).at[:B].set(c)
    tn = min(512, W)
    out = pl.pallas_call(
        _ada_kernel,
        grid=(W // tn,),
        in_specs=[pl.BlockSpec((rows, D), lambda j: (0, 0)),
                  pl.BlockSpec((D, tn), lambda j: (0, j)),
                  pl.BlockSpec((1, tn), lambda j: (0, j))],
        out_specs=pl.BlockSpec((rows, tn), lambda j: (0, j)),
        out_shape=jax.ShapeDtypeStruct((rows, W), F32),
        compiler_params=_cparams(("parallel",)),
        name="ada_mod",
    )(cp, ada_w, ada_b.reshape(1, W))
    return out[:B]


def _prenorm_kernel(x_ref, mod_ref, gain_ref, pos_ref, invf_ref, h_ref, cos_ref, sin_ref):
    x = x_ref[...]
    ms = jnp.mean(x * x, axis=-1, keepdims=True)
    g = gain_ref[...] * (1.0 + mod_ref[1:2, :])
    h_ref[...] = ((x * lax.rsqrt(ms + NORM_EPS)) * g + mod_ref[0:1, :]).astype(BF16)
    ang = pos_ref[...].astype(F32) * invf_ref[...]
    lane = lax.broadcasted_iota(I32, ang.shape, 1)
    sn = jnp.sin(ang)
    cos_ref[...] = jnp.cos(ang)
    sin_ref[...] = jnp.where(lane < HEAD_DIM // 2, -sn, sn)


def prenorm_rope(x2d, mod, gain, positions, S):
    N, D = x2d.shape
    tr = min(256, S)
    tiles_per_b = S // tr
    half = HEAD_DIM // 2
    inv = ROPE_THETA ** (-jnp.arange(0, HEAD_DIM, 2, dtype=F32) / HEAD_DIM)
    invf = jnp.concatenate([inv, inv]).reshape(1, HEAD_DIM)
    del half
    return pl.pallas_call(
        _prenorm_kernel,
        grid=(N // tr,),
        in_specs=[pl.BlockSpec((tr, D), lambda i: (i, 0)),
                  pl.BlockSpec((None, 6, D), lambda i: (i // tiles_per_b, 0, 0)),
                  pl.BlockSpec((1, D), lambda i: (0, 0)),
                  pl.BlockSpec((tr, 1), lambda i: (i, 0)),
                  pl.BlockSpec((1, HEAD_DIM), lambda i: (0, 0))],
        out_specs=[pl.BlockSpec((tr, D), lambda i: (i, 0)),
                   pl.BlockSpec((tr, HEAD_DIM), lambda i: (i, 0)),
                   pl.BlockSpec((tr, HEAD_DIM), lambda i: (i, 0))],
        out_shape=[jax.ShapeDtypeStruct((N, D), BF16),
                   jax.ShapeDtypeStruct((N, HEAD_DIM), F32),
                   jax.ShapeDtypeStruct((N, HEAD_DIM), F32)],
        compiler_params=_cparams(("parallel",)),
        name="prenorm_rope",
    )(x2d, mod, gain.reshape(1, D), positions.reshape(N, 1), invf)


def _residue_rows(ref, r, d):
    if d == 1:
        return ref[...]
    return ref[pl.ds(r, ref.shape[0] // d, stride=d), :]


def _qkv_kernel(a_ref, w_ref, cos_ref, sin_ref, o_ref, acc_ref, *, d, heads, scale, n_i):
    s = pl.program_id(0)

    @pl.when(s == 0)
    def _():
        acc_ref[...] = jnp.zeros_like(acc_ref)

    kind = jnp.maximum(s - 1, 0) // n_i
    sc = jnp.where(kind == 0, scale, 1.0).astype(F32)
    is_v = kind == 2
    for r in range(d):
        c = jnp.where(is_v, 1.0, _residue_rows(cos_ref, r, d) * sc)
        sn = jnp.where(is_v, 0.0, _residue_rows(sin_ref, r, d) * sc)
        for h in range(heads):
            t = _residue_rows(acc_ref.at[h], r, d)
            o_ref[r, :, h * HEAD_DIM:(h + 1) * HEAD_DIM] = (
                t * c + pltpu.roll(t, HEAD_DIM // 2, 1) * sn).astype(o_ref.dtype)
    acc = jnp.dot(a_ref[...], w_ref[...], preferred_element_type=F32)
    for h in range(heads):
        acc_ref[h] = acc[:, h * HEAD_DIM:(h + 1) * HEAD_DIM]


def qkv_proj(h, w_bf16, cosf, sinf, B, S, U, g, d):
    N, D = h.shape
    tm = min(1024, S)
    tiles_per_b = S // tm
    n_i = N // tm
    last = 3 * n_i - 1
    kern = functools.partial(_qkv_kernel, d=d, heads=U // HEAD_DIM, scale=HEAD_DIM ** -0.5, n_i=n_i)

    def cur(s):
        return jnp.minimum(s, last)

    def fin(s):
        return jnp.maximum(s - 1, 0)

    return pl.pallas_call(
        kern,
        grid=(3 * n_i + 1,),
        in_specs=[pl.BlockSpec((tm, D), lambda s: (cur(s) % n_i, 0)),
                  pl.BlockSpec((D, U), lambda s: (0, (cur(s) // n_i) * 3 + g)),
                  pl.BlockSpec((tm, HEAD_DIM), lambda s: (fin(s) % n_i, 0)),
                  pl.BlockSpec((tm, HEAD_DIM), lambda s: (fin(s) % n_i, 0))],
        out_specs=pl.BlockSpec((None, None, d, tm // d, U),
                               lambda s: (fin(s) // n_i, (fin(s) % n_i) // tiles_per_b, 0,
                                          (fin(s) % n_i) % tiles_per_b, 0)),
        out_shape=jax.ShapeDtypeStruct((3, B, d, S // d, U), BF16),
        scratch_shapes=[pltpu.VMEM((U // HEAD_DIM, tm, HEAD_DIM), F32)],
        compiler_params=_cparams(("arbitrary",)),
        name=f"qkv_proj_d{d}",
    )(h, w_bf16, cosf, sinf)


def _matmul_kernel(a_ref, w_ref, o_ref):
    o_ref[...] = jnp.dot(a_ref[...], w_ref[...], preferred_element_type=F32).astype(o_ref.dtype)


def matmul_cols(a, w_bf16, U, first_tile, name):
    N, K = a.shape
    n_tiles = w_bf16.shape[1] // U - first_tile
    tm = min(1024, N)
    return pl.pallas_call(
        _matmul_kernel,
        grid=(n_tiles, N // tm),
        in_specs=[pl.BlockSpec((tm, K), lambda j, i: (i, 0)),
                  pl.BlockSpec((K, U), lambda j, i: (0, first_tile + j))],
        out_specs=pl.BlockSpec((tm, U), lambda j, i: (i, j)),
        out_shape=jax.ShapeDtypeStruct((N, n_tiles * U), BF16),
        compiler_params=_cparams(("parallel", "parallel")),
        name=name,
    )(a, w_bf16)


def _attn_kernel(q_ref, kc_ref, kp_ref, vc_ref, vp_ref, o_ref, st_ref, kx_ref, vx_ref, *, QB, H):
    i = pl.program_id(2)
    blk = ATTN_BLOCK
    kx_ref[0:blk, :] = kp_ref[...]
    kx_ref[blk:, :] = kc_ref[...]
    ones = jnp.ones((vx_ref.shape[0], HEAD_DIM), vx_ref.dtype)
    for h in range(H):
        hs = slice(h * HEAD_DIM, (h + 1) * HEAD_DIM)
        vx_ref[0:blk, 2 * h * HEAD_DIM:(2 * h + 1) * HEAD_DIM] = vp_ref[:, hs]
        vx_ref[blk:, 2 * h * HEAD_DIM:(2 * h + 1) * HEAD_DIM] = vc_ref[:, hs]
        vx_ref[:, (2 * h + 1) * HEAD_DIM:(2 * h + 2) * HEAD_DIM] = ones
    rows = lax.broadcasted_iota(I32, (blk, 2 * blk), 0)
    keys = lax.broadcasted_iota(I32, (blk, 2 * blk), 1)
    band = (keys >= rows) & (keys - blk <= rows)
    lane = lax.broadcasted_iota(I32, (blk, LANES), 1)
    dn = (((1,), (1,)), ((), ()))

    def body(qb, carry):
        r0 = pl.multiple_of(qb * blk, blk)
        ok = band & ((keys >= blk) | (i * QB + qb > 0))
        q = [q_ref[pl.ds(r0, blk), h * HEAD_DIM:(h + 1) * HEAD_DIM] for h in range(H)]
        k = [kx_ref[pl.ds(r0, 2 * blk), h * HEAD_DIM:(h + 1) * HEAD_DIM] for h in range(H)]
        s = [jnp.where(ok, lax.dot_general(q[h], k[h], dn, preferred_element_type=F32), NEG_BIG) for h in range(H)]
        m = [jnp.max(s[h], axis=-1, keepdims=True) for h in range(H)]
        p = [jnp.exp(s[h] - m[h]).astype(BF16) for h in range(H)]
        st = jnp.zeros((blk, LANES), F32)
        for h in range(H):
            v1 = vx_ref[pl.ds(r0, 2 * blk), 2 * h * HEAD_DIM:(2 * h + 2) * HEAD_DIM]
            acc = jnp.dot(p[h], v1, preferred_element_type=F32)
            l = acc[:, HEAD_DIM:]
            o_ref[pl.ds(r0, blk), h * HEAD_DIM:(h + 1) * HEAD_DIM] = (acc[:, :HEAD_DIM] / l).astype(o_ref.dtype)
            st = jnp.where(lane == h, m[h], st)
            st = jnp.where(lane == H + h, l, st)
        st_ref[pl.ds(r0, blk), :] = st
        return carry

    lax.fori_loop(0, QB, body, 0)


def dilated_attention(qkv, d):
    _, B, _, L, U = qkv.shape
    H = U // HEAD_DIM
    R = min(512, L)
    QB = R // ATTN_BLOCK

    def cur(kind):
        return pl.BlockSpec((None, None, None, R, U), lambda b, r, i: (kind, b, r, i, 0))

    def prev(kind):
        return pl.BlockSpec((None, None, None, ATTN_BLOCK, U),
                            lambda b, r, i: (kind, b, r, jnp.maximum(i * QB - 1, 0), 0))

    kern = functools.partial(_attn_kernel, QB=QB, H=H)
    return pl.pallas_call(
        kern,
        grid=(B, d, L // R),
        in_specs=[cur(0), cur(1), prev(1), cur(2), prev(2)],
        out_specs=[pl.BlockSpec((None, None, R, U), lambda b, r, i: (b, r, i, 0)),
                   pl.BlockSpec((None, None, R, LANES), lambda b, r, i: (b, r, i, 0))],
        out_shape=[jax.ShapeDtypeStruct((B, d, L, U), BF16),
                   jax.ShapeDtypeStruct((B, d, L, LANES), F32)],
        scratch_shapes=[pltpu.VMEM((R + ATTN_BLOCK, U), BF16), pltpu.VMEM((R + ATTN_BLOCK, 2 * U), BF16)],
        compiler_params=_cparams(("parallel", "parallel", "parallel")),
        name=f"dilated_attn_d{d}",
    )(qkv, qkv, qkv, qkv, qkv)


def _merge_kernel(o0_ref, o1_ref, o2_ref, s0_ref, s1_ref, s2_ref, out_ref, of_ref, sf_ref, *, H, dils):
    for g, (o_ref, s_ref) in enumerate(((o0_ref, s0_ref), (o1_ref, s1_ref), (o2_ref, s2_ref))):
        d = dils[g]
        n = sf_ref.shape[1] // d
        for r in range(d):
            rows = slice(None) if d == 1 else pl.ds(r, n, stride=d)
            sf_ref[g, rows, :] = s_ref[r]
            for h in range(H):
                of_ref[g, h, rows, :] = o_ref[r, :, h * HEAD_DIM:(h + 1) * HEAD_DIM].astype(F32)
    st = [sf_ref[0], sf_ref[1], sf_ref[2]]
    mx = jnp.maximum(jnp.maximum(st[0], st[1]), st[2])
    w = [pltpu.roll(s, LANES - H, 1) * jnp.exp(s - mx) for s in st]
    tot = w[0] + w[1] + w[2]
    coef = [x / tot for x in w]
    for h in range(H):
        hs = slice(h * HEAD_DIM, (h + 1) * HEAD_DIM)
        acc = coef[0][:, h:h + 1] * of_ref[0, h]
        acc += coef[1][:, h:h + 1] * of_ref[1, h]
        acc += coef[2][:, h:h + 1] * of_ref[2, h]
        out_ref[:, hs] = acc.astype(out_ref.dtype)


def merge_groups(outs, stats, dils):
    B, d0, L0, U = outs[0].shape
    S = d0 * L0
    H = U // HEAD_DIM
    tm = min(512, S)
    tiles_per_b = S // tm

    def ospec(d, w):
        return pl.BlockSpec((None, d, tm // d, w), lambda i: (i // tiles_per_b, 0, i % tiles_per_b, 0))

    return pl.pallas_call(
        functools.partial(_merge_kernel, H=H, dils=dils),
        grid=(B * S // tm,),
        in_specs=[ospec(d, U) for d in dils] + [ospec(d, LANES) for d in dils],
        out_specs=pl.BlockSpec((tm, U), lambda i: (i, 0)),
        out_shape=jax.ShapeDtypeStruct((B * S, U), BF16),
        scratch_shapes=[pltpu.VMEM((3, H, tm, HEAD_DIM), F32), pltpu.VMEM((3, tm, LANES), F32)],
        compiler_params=_cparams(("parallel",)),
        name="merge_groups",
    )(*outs, *stats)


def _conv_kernel(a0_ref, a1_ref, b0_ref, b1_ref, ha0_ref, ha1_ref, hb0_ref, hb1_ref,
                 w_ref, bias_ref, g_ref, be_ref, o_ref, u_ref, c_ref, sh_ref, *, ts, U):
    i = pl.program_id(1)
    halo = CONV_HALO
    for half, (a_ref, b_ref, ha_ref, hb_ref) in enumerate(((a0_ref, b0_ref, ha0_ref, hb0_ref),
                                                            (a1_ref, b1_ref, ha1_ref, hb1_ref))):
        cs = slice(half * U, (half + 1) * U)
        u_ref[halo:halo + ts, cs] = a_ref[...].astype(F32) * _sigmoid(b_ref[...].astype(F32))
        hu = ha_ref[...].astype(F32) * _sigmoid(hb_ref[...].astype(F32))
        u_ref[0:halo, cs] = jnp.where(i > 0, hu, 0.0)
    C = 2 * U
    rc = 64
    off = halo - (CONV_WIDTH - 1)

    sub = 8
    n_al = ts + halo - sub

    def chan_body(cc, carry):
        c0 = pl.multiple_of(cc * LANES, LANES)
        sh_ref[0] = u_ref[:, pl.ds(c0, LANES)]
        for b in range(1, sub):
            sh_ref[b, 0:n_al, :] = u_ref[b:b + n_al, pl.ds(c0, LANES)]
        for rb in range(ts // rc):
            acc = jnp.zeros((rc, LANES), F32) + bias_ref[:, pl.ds(c0, LANES)]
            for j in range(CONV_WIDTH):
                a, b = divmod(off + j, sub)
                r0 = rb * rc + a * sub
                acc += w_ref[j:j + 1, pl.ds(c0, LANES)] * sh_ref[b, r0:r0 + rc, :]
            c_ref[rb * rc:(rb + 1) * rc, pl.ds(c0, LANES)] = acc
        return carry

    lax.fori_loop(0, C // LANES, chan_body, 0)

    rn = 16

    def norm_body(rb, carry):
        r0 = pl.multiple_of(rb * rn, rn)
        v = c_ref[pl.ds(r0, rn), :]
        mu = jnp.mean(v, axis=-1, keepdims=True)
        dv = v - mu
        var = jnp.mean(dv * dv, axis=-1, keepdims=True)
        y = dv * lax.rsqrt(var + NORM_EPS) * g_ref[...] + be_ref[...]
        o_ref[pl.ds(r0, rn), :] = (y * _sigmoid(y)).astype(o_ref.dtype)
        return carry

    lax.fori_loop(0, ts // rn, norm_body, 0, unroll=4)


def conv_branch(proj, B, S, U, conv_dw, conv_bias, ln_gain, ln_bias):
    IN = proj.shape[1]
    C = 2 * U
    ts = min(256, S)
    pv = proj.reshape(B, S, IN)
    hb = ts // CONV_HALO
    cur = lambda blk: pl.BlockSpec((None, ts, U), lambda b, i, blk=blk: (b, i, blk))
    prv = lambda blk: pl.BlockSpec((None, CONV_HALO, U), lambda b, i, blk=blk: (b, jnp.maximum(i * hb - 1, 0), blk))
    vec = pl.BlockSpec((1, C), lambda b, i: (0, 0))
    out = pl.pallas_call(
        functools.partial(_conv_kernel, ts=ts, U=U),
        grid=(B, S // ts),
        in_specs=[cur(0), cur(1), cur(2), cur(3), prv(0), prv(1), prv(2), prv(3),
                  pl.BlockSpec((CONV_WIDTH, C), lambda b, i: (0, 0)), vec, vec, vec],
        out_specs=pl.BlockSpec((None, ts, C), lambda b, i: (b, i, 0)),
        out_shape=jax.ShapeDtypeStruct((B, S, C), BF16),
        scratch_shapes=[pltpu.VMEM((ts + CONV_HALO, C), F32), pltpu.VMEM((ts, C), F32),
                        pltpu.VMEM((8, ts + CONV_HALO, LANES), F32)],
        compiler_params=_cparams(("parallel", "parallel")),
        name="conv_branch",
    )(pv, pv, pv, pv, pv, pv, pv, pv, conv_dw, conv_bias.reshape(1, C), ln_gain.reshape(1, C), ln_bias.reshape(1, C))
    return out.reshape(B * S, C)


def _gateproj_kernel(cn_ref, am_ref, wc_ref, wa_ref, gc_ref, ga_ref, o_ref):
    conv = jnp.dot(cn_ref[...], wc_ref[...], preferred_element_type=F32)
    z = _sigmoid(gc_ref[...].astype(F32)) * conv
    attn = jnp.dot(am_ref[...], wa_ref[...], preferred_element_type=F32)
    z += _sigmoid(ga_ref[...].astype(F32)) * attn
    o_ref[...] = z.astype(o_ref.dtype)


def gate_proj(cn, am, wc, wa, proj, U):
    N, C = cn.shape
    D = wc.shape[1]
    tn = U
    tm = min(1024, N)
    return pl.pallas_call(
        _gateproj_kernel,
        grid=(D // tn, N // tm),
        in_specs=[pl.BlockSpec((tm, C), lambda j, i: (i, 0)),
                  pl.BlockSpec((tm, U), lambda j, i: (i, 0)),
                  pl.BlockSpec((C, tn), lambda j, i: (0, j)),
                  pl.BlockSpec((U, tn), lambda j, i: (0, j)),
                  pl.BlockSpec((tm, tn), lambda j, i: (i, 4 + j)),
                  pl.BlockSpec((tm, tn), lambda j, i: (i, 8 + j))],
        out_specs=pl.BlockSpec((tm, tn), lambda j, i: (i, j)),
        out_shape=jax.ShapeDtypeStruct((N, D), BF16),
        compiler_params=_cparams(("parallel", "parallel")),
        name="gate_proj",
    )(cn, am, wc, wa, proj, proj)


def _router_kernel(y_ref, xin_ref, mod_ref, gpost_ref, gain_ref, r_ref,
                   x1_ref, hp_ref, eid_ref, wt_ref, rank_ref, cnt_ref, carry_ref, *, tr):
    step = pl.program_id(0)

    @pl.when(step == 0)
    def _():
        carry_ref[...] = jnp.zeros_like(carry_ref)

    y = y_ref[...].astype(F32)
    yms = jnp.mean(y * y, axis=-1, keepdims=True)
    x = xin_ref[...] + (y * lax.rsqrt(yms + NORM_EPS)) * (mod_ref[2:3, :] * gpost_ref[...])
    x1_ref[...] = x
    D = x.shape[1]
    ms = jnp.mean(x * x, axis=-1, keepdims=True)
    h = (x * lax.rsqrt(ms + NORM_EPS)) * (gain_ref[...] * (1.0 + mod_ref[4:5, :])) + mod_ref[3:4, :]
    lo = h[:, :D // 2]
    hi = h[:, D // 2:]
    hp_ref[...] = _pack_halves(lo, hi)
    logits = (jnp.dot(lo.astype(BF16), r_ref[:D // 2, :], preferred_element_type=F32)
              + jnp.dot(hi.astype(BF16), r_ref[D // 2:, :], preferred_element_type=F32))
    lane = lax.broadcasted_iota(I32, logits.shape, 1)
    G = N_EXPERT_GROUPS
    is_g = lane < G
    gl = jnp.where(is_g, logits, NEG_BIG)
    gmax = jnp.max(gl, axis=-1, keepdims=True)
    grp = jnp.min(jnp.where(gl == gmax, lane, LANES), axis=-1, keepdims=True)
    p_grp = 1.0 / jnp.sum(jnp.where(is_g, jnp.exp(gl - gmax), 0.0), axis=-1, keepdims=True)
    in_grp = (lane >= G) & (lane < G + N_EXPERTS) & (((lane - G) // EXPERTS_PER_GROUP) == grp)
    el = jnp.where(in_grp, logits, NEG_BIG)
    v0 = jnp.max(el, axis=-1, keepdims=True)
    i0 = jnp.min(jnp.where(in_grp & (el == v0), lane, LANES), axis=-1, keepdims=True)
    in2 = in_grp & (lane != i0)
    el2 = jnp.where(in2, logits, NEG_BIG)
    v1 = jnp.max(el2, axis=-1, keepdims=True)
    i1 = jnp.min(jnp.where(in2 & (el2 == v1), lane, LANES), axis=-1, keepdims=True)
    e1 = jnp.exp(v1 - v0)
    w0 = p_grp / (1.0 + e1)
    w1 = p_grp * e1 / (1.0 + e1)
    ex0 = i0 - G
    ex1 = i1 - G
    eid_ref[...] = jnp.where(lane == 0, ex0, jnp.where(lane == 1, ex1, 0))
    wt_ref[...] = jnp.where(lane == 0, w0, jnp.where(lane == 1, w1, 0.0))
    oh0 = (lane == ex0).astype(F32)
    oh1 = (lane == ex1).astype(F32)
    both = oh0 + oh1
    rr = lax.broadcasted_iota(I32, (tr, tr), 0)
    cc = lax.broadcasted_iota(I32, (tr, tr), 1)
    tril = (cc < rr).astype(BF16)
    before = jnp.dot(tril, both.astype(BF16), preferred_element_type=F32) + carry_ref[0:1, :]
    rk0 = jnp.sum(before * oh0, axis=-1, keepdims=True)
    rk1 = jnp.sum(before * oh1, axis=-1, keepdims=True)
    rank_ref[...] = jnp.where(lane == 0, rk0, jnp.where(lane == 1, rk1, 0.0))
    newc = carry_ref[0:1, :] + jnp.sum(both, axis=0, keepdims=True)
    carry_ref[...] = jnp.broadcast_to(newc, carry_ref.shape)
    cnt_ref[...] = jnp.broadcast_to(newc, cnt_ref.shape)


def residual_prenorm_router(y, x2d, mod, gain_post, gain, rcat, S):
    N, D = x2d.shape
    tr = min(256, S)
    tiles_per_b = S // tr
    lane_spec = pl.BlockSpec((tr, LANES), lambda i: (i, 0))
    row_spec = pl.BlockSpec((tr, D), lambda i: (i, 0))
    vec_spec = pl.BlockSpec((1, D), lambda i: (0, 0))
    return pl.pallas_call(
        functools.partial(_router_kernel, tr=tr),
        grid=(N // tr,),
        in_specs=[row_spec, row_spec,
                  pl.BlockSpec((None, 6, D), lambda i: (i // tiles_per_b, 0, 0)),
                  vec_spec, vec_spec,
                  pl.BlockSpec((D, LANES), lambda i: (0, 0))],
        out_specs=[row_spec, pl.BlockSpec((tr, D // 2), lambda i: (i, 0)), lane_spec, lane_spec, lane_spec,
                   pl.BlockSpec((8, LANES), lambda i: (0, 0))],
        out_shape=[jax.ShapeDtypeStruct((N, D), F32),
                   jax.ShapeDtypeStruct((N, D // 2), U32),
                   jax.ShapeDtypeStruct((N, LANES), I32),
                   jax.ShapeDtypeStruct((N, LANES), F32),
                   jax.ShapeDtypeStruct((N, LANES), F32),
                   jax.ShapeDtypeStruct((8, LANES), F32)],
        scratch_shapes=[pltpu.VMEM((8, LANES), F32)],
        compiler_params=_cparams(("arbitrary",)),
        name="residual_prenorm_router",
    )(y, x2d, mod, gain_post.reshape(1, D), gain.reshape(1, D), rcat)


def _dest_kernel(eid_ref, rank_ref, start_ref, o_ref):
    lane = lax.broadcasted_iota(I32, eid_ref.shape, 1)
    eid = eid_ref[...]
    rank = rank_ref[...]
    start = start_ref[0:1, :]
    d = []
    for k in range(2):
        oh = (lane == eid[:, k:k + 1]).astype(F32)
        d.append(jnp.sum(oh * start, axis=-1, keepdims=True) + rank[:, k:k + 1])
    o_ref[...] = jnp.where(lane == 0, d[0], jnp.where(lane == 1, d[1], 0.0)).astype(I32)


def dest_rows(eid, rank, pad_start):
    N = eid.shape[0]
    tr = min(1024, N)
    spec = pl.BlockSpec((tr, LANES), lambda i: (i, 0))
    return pl.pallas_call(
        _dest_kernel,
        grid=(N // tr,),
        in_specs=[spec, spec, pl.BlockSpec((8, LANES), lambda i: (0, 0))],
        out_specs=spec,
        out_shape=jax.ShapeDtypeStruct((N, LANES), I32),
        compiler_params=_cparams(("parallel",)),
        name="dest_rows",
    )(eid, rank, pad_start)


def _scatter_kernel(dest_ref, h_ref, xs_in_ref, xs_ref, sem, *, T):
    del xs_in_ref

    sub = 8

    def issue(tb, carry):
        for r in range(sub):
            t = tb * sub + r
            for k in range(2):
                d = dest_ref[0, 0, 2 * t + k]
                pltpu.make_async_copy(h_ref.at[pl.ds(t, 1)], xs_ref.at[pl.ds(d, 1)], sem).start(priority=k)
        return carry

    lax.fori_loop(0, T // sub, issue, 0)
    for k in range(2):
        pltpu.make_async_copy(h_ref, xs_ref.at[pl.ds(0, T)], sem).wait()


def scatter_rows(hp, dest, P):
    N, W = hp.shape
    T = min(256, N)
    dest_s = dest[:, :2].reshape(N // T, 1, 2 * T)
    xs0 = jnp.zeros((P, W), hp.dtype)
    return pl.pallas_call(
        functools.partial(_scatter_kernel, T=T),
        grid=(N // T,),
        in_specs=[pl.BlockSpec((1, 1, 2 * T), lambda i: (i, 0, 0), memory_space=pltpu.SMEM),
                  pl.BlockSpec((T, W), lambda i: (i, 0)),
                  pl.BlockSpec(memory_space=pl.ANY)],
        out_specs=pl.BlockSpec(memory_space=pl.ANY),
        out_shape=jax.ShapeDtypeStruct((P, W), hp.dtype),
        scratch_shapes=[pltpu.SemaphoreType.DMA(())],
        input_output_aliases={2: 0},
        compiler_params=_cparams(("arbitrary",)),
        name="scatter_rows",
    )(dest_s, hp, xs0)


def _expert_chunks(st_ref, nc_ref, in_hbm, out_hbm, ibuf, obuf, isem, osem, compute):
    e = pl.program_id(0)
    n_e = pl.num_programs(0)
    n = nc_ref[e]
    TM = ibuf.shape[1]
    base = pl.multiple_of(st_ref[e], TM)
    g0 = base // TM

    def icopy(row, slot):
        return pltpu.make_async_copy(in_hbm.at[pl.ds(row, TM)], ibuf.at[slot], isem.at[slot])

    def ocopy(row, slot):
        return pltpu.make_async_copy(obuf.at[slot], out_hbm.at[pl.ds(row, TM)], osem.at[slot])

    @pl.when((e == 0) & (n > 0))
    def _():
        icopy(base, 0).start()

    def body(c, carry):
        g = g0 + c
        slot = g % 2
        row = pl.multiple_of(base + c * TM, TM)

        @pl.when(c + 1 < n)
        def _():
            icopy(row + TM, 1 - slot).start()

        icopy(row, slot).wait()

        @pl.when(g >= 2)
        def _():
            ocopy(row, slot).wait()

        compute(ibuf.at[slot], obuf.at[slot])
        ocopy(row, slot).start()
        return carry

    lax.fori_loop(0, n, body, 0)

    @pl.when(e + 1 < n_e)
    def _():
        @pl.when(nc_ref[e + 1] > 0)
        def _():
            icopy(pl.multiple_of(st_ref[e + 1], TM), (g0 + n) % 2).start()

    @pl.when(e == n_e - 1)
    def _():
        g_end = g0 + n

        @pl.when(g_end >= 2)
        def _():
            ocopy(base, g_end % 2).wait()

        @pl.when(g_end >= 1)
        def _():
            ocopy(base, (g_end + 1) % 2).wait()

        used = pl.multiple_of(base + n * TM, TM)
        n_tail = (out_hbm.shape[0] - used) // TM
        obuf[0] = jnp.zeros(obuf.shape[1:], obuf.dtype)

        def zstart(c, carry):
            ocopy(pl.multiple_of(used + c * TM, TM), 0).start()
            return carry

        def zwait(c, carry):
            ocopy(used, 0).wait()
            return carry

        lax.fori_loop(0, n_tail, zstart, 0)
        lax.fori_loop(0, n_tail, zwait, 0)


WEIGHT_DMA_QUEUE = 1


def _stream_expert_weights(parts, wsem):
    e = pl.program_id(0)
    n_e = pl.num_programs(0)
    slot = e % 2

    def copy(k, ei, s):
        src, dst = parts[k]
        return pltpu.make_async_copy(src(ei), dst(s), wsem.at[k, s])

    @pl.when(e == 0)
    def _():
        for k in range(len(parts)):
            copy(k, 0, 0).start(priority=WEIGHT_DMA_QUEUE)

    @pl.when(e + 1 < n_e)
    def _():
        for k in range(len(parts)):
            copy(k, e + 1, 1 - slot).start(priority=WEIGHT_DMA_QUEUE)

    for k in range(len(parts)):
        copy(k, e, slot).wait()
    return slot


def _moe_up_kernel(st_ref, nc_ref, xs_hbm, w1_hbm, w3_hbm, o_hbm, wb_ref, w1_buf, w3_buf, ibuf, obuf,
                   wsem, isem, osem, *, F):
    slot = _stream_expert_weights([(lambda ei: w1_hbm.at[ei], lambda s: w1_buf.at[s]),
                                   (lambda ei: w3_hbm.at[ei], lambda s: w3_buf.at[s])], wsem)

    @pl.when(nc_ref[pl.program_id(0)] > 0)
    def _():
        wb_ref[:, :F] = w1_buf[slot].astype(BF16)
        wb_ref[:, F:] = w3_buf[slot].astype(BF16)

    def compute(x_ref, o_ref):
        lo, hi = _unpack_halves(x_ref[...])
        half = wb_ref.shape[0] // 2
        hcat = (jnp.dot(lo.astype(BF16), wb_ref[:half, :], preferred_element_type=F32)
                + jnp.dot(hi.astype(BF16), wb_ref[half:, :], preferred_element_type=F32))
        a = hcat[:, :F]
        o_ref[...] = (a * _sigmoid(a) * hcat[:, F:]).astype(o_ref.dtype)

    _expert_chunks(st_ref, nc_ref, xs_hbm, o_hbm, ibuf, obuf, isem, osem, compute)


def _moe_down_kernel(st_ref, nc_ref, h_hbm, w2_hbm, o_hbm, wb_ref, w2_buf, ibuf, obuf, wsem, isem, osem):
    slot = _stream_expert_weights([(lambda ei: w2_hbm.at[ei], lambda s: w2_buf.at[s])], wsem)

    @pl.when(nc_ref[pl.program_id(0)] > 0)
    def _():
        wb_ref[...] = w2_buf[slot].astype(BF16)

    def compute(h_ref, o_ref):
        y = jnp.dot(h_ref[...], wb_ref[...], preferred_element_type=F32)
        half = y.shape[1] // 2
        o_ref[...] = _pack_halves(y[:, :half], y[:, half:])

    _expert_chunks(st_ref, nc_ref, h_hbm, o_hbm, ibuf, obuf, isem, osem, compute)


def expert_ffn(xs, seg_start, seg_chunks, w1, w3, w2):
    P, Wp = xs.shape
    E, D, F = w1.shape
    TM = MOE_ROWS
    any_spec = pl.BlockSpec(memory_space=pl.ANY)
    dma2 = pltpu.SemaphoreType.DMA((2,))
    hmid = pl.pallas_call(
        functools.partial(_moe_up_kernel, F=F),
        grid_spec=pltpu.PrefetchScalarGridSpec(
            num_scalar_prefetch=2,
            grid=(E,),
            in_specs=[any_spec, any_spec, any_spec],
            out_specs=any_spec,
            scratch_shapes=[pltpu.VMEM((D, 2 * F), BF16), pltpu.VMEM((2, D, F), F32), pltpu.VMEM((2, D, F), F32),
                            pltpu.VMEM((2, TM, Wp), U32), pltpu.VMEM((2, TM, F), BF16),
                            pltpu.SemaphoreType.DMA((2, 2)), dma2, dma2]),
        out_shape=jax.ShapeDtypeStruct((P, F), BF16),
        compiler_params=_cparams(("arbitrary",)),
        name="moe_up",
    )(seg_start, seg_chunks, xs, w1, w3)
    return pl.pallas_call(
        _moe_down_kernel,
        grid_spec=pltpu.PrefetchScalarGridSpec(
            num_scalar_prefetch=2,
            grid=(E,),
            in_specs=[any_spec, any_spec],
            out_specs=any_spec,
            scratch_shapes=[pltpu.VMEM((F, D), BF16), pltpu.VMEM((2, F, D), F32), pltpu.VMEM((2, TM, F), BF16),
                            pltpu.VMEM((2, TM, D // 2), U32), pltpu.SemaphoreType.DMA((1, 2)), dma2, dma2]),
        out_shape=jax.ShapeDtypeStruct((P, D // 2), U32),
        compiler_params=_cparams(("arbitrary",)),
        name="moe_down",
    )(seg_start, seg_chunks, hmid, w2)


def _combine_kernel(pos_ref, nxt_ref, wt_ref, x_ref, mod_ref, gain_ref, y_hbm, o_ref, buf_ref, sem, v_ref, *, T):
    i = pl.program_id(0)
    n = pl.num_programs(0)
    slot = i % 2
    sub = 8

    def start_rows(p_ref, s, tb):
        for r in range(sub):
            t = tb * sub + r
            for k in range(2):
                p = p_ref[0, 0, 2 * t + k]
                pltpu.make_async_copy(y_hbm.at[pl.ds(p, 1)], buf_ref.at[s, k, pl.ds(t, 1)],
                                      sem.at[s]).start(priority=k)

    def wait_slot(s):
        for k in range(2):
            pltpu.make_async_copy(y_hbm.at[pl.ds(0, T)], buf_ref.at[s, k], sem.at[s]).wait()

    @pl.when(i == 0)
    def _():
        def first(tb, carry):
            start_rows(pos_ref, 0, tb)
            return carry

        lax.fori_loop(0, T // sub, first, 0)

    wait_slot(slot)
    D = x_ref.shape[1]
    half = D // 2
    v_ref[0:1, :] = mod_ref[5:6, :] * gain_ref[...]

    per = 4

    def body(j, carry):
        for q in range(per):
            rows = pl.ds(pl.multiple_of((j * per + q) * sub, sub), sub)
            wt = wt_ref[rows, :]
            w0 = wt[:, 0:1]
            w1 = wt[:, 1:2]
            lo0, hi0 = _unpack_halves(buf_ref[slot, 0, rows, :])
            lo1, hi1 = _unpack_halves(buf_ref[slot, 1, rows, :])
            ylo = w0 * lo0 + w1 * lo1
            yhi = w0 * hi0 + w1 * hi1
            ms = (jnp.sum(ylo * ylo, axis=-1, keepdims=True) + jnp.sum(yhi * yhi, axis=-1, keepdims=True)) / D
            inv = lax.rsqrt(ms + NORM_EPS)
            o_ref[rows, :half] = x_ref[rows, :half] + (ylo * inv) * v_ref[0:1, :half]
            o_ref[rows, half:] = x_ref[rows, half:] + (yhi * inv) * v_ref[0:1, half:]
        for q in range(per):
            start_rows(nxt_ref, 1 - slot, j * per + q)
        return carry

    lax.fori_loop(0, T // (sub * per), body, 0)

    @pl.when(i == n - 1)
    def _():
        wait_slot(1 - slot)


def combine(yp, dest, wts, x1, mod, gain, S):
    N, D = x1.shape
    T = min(128, S)
    tiles_per_b = S // T
    pos_s = dest[:, :2].reshape(N // T, 1, 2 * T)
    n_tiles = N // T
    return pl.pallas_call(
        functools.partial(_combine_kernel, T=T),
        grid=(n_tiles,),
        in_specs=[pl.BlockSpec((1, 1, 2 * T), lambda i: (i, 0, 0), memory_space=pltpu.SMEM),
                  pl.BlockSpec((1, 1, 2 * T), lambda i: (jnp.minimum(i + 1, n_tiles - 1), 0, 0),
                               memory_space=pltpu.SMEM),
                  pl.BlockSpec((T, LANES), lambda i: (i, 0)),
                  pl.BlockSpec((T, D), lambda i: (i, 0)),
                  pl.BlockSpec((None, 6, D), lambda i: (i // tiles_per_b, 0, 0)),
                  pl.BlockSpec((1, D), lambda i: (0, 0)),
                  pl.BlockSpec(memory_space=pl.ANY)],
        out_specs=pl.BlockSpec((T, D), lambda i: (i, 0)),
        out_shape=jax.ShapeDtypeStruct((N, D), F32),
        scratch_shapes=[pltpu.VMEM((2, 2, T, D // 2), U32), pltpu.SemaphoreType.DMA((2,)),
                        pltpu.VMEM((8, D), F32)],
        compiler_params=_cparams(("arbitrary",)),
        name="combine",
    )(pos_s, pos_s, wts, x1, mod, gain.reshape(1, D), yp)


def _moe_layout(counts):
    TM = MOE_ROWS
    chunks = (counts.astype(I32) + TM - 1) // TM
    start = (jnp.cumsum(chunks) - chunks) * TM
    return start, chunks


def kernel(x, c, positions, ada_w, ada_b, mix_norm_pre, mix_norm_post, w_in, conv_dw, conv_dw_bias, conv_ln_gain, conv_ln_bias, w_conv_out, w_attn_out, w_out, ffn_norm_pre, ffn_norm_post, router_group, router_expert, expert_w1, expert_w3, expert_w2):
    B, S, D = x.shape
    N = B * S
    U = D // 4
    depth = ada_w.shape[0]
    xc = x.reshape(N, D)
    for layer in range(depth):
        mod = ada_mod(c, ada_w[layer], ada_b[layer]).reshape(B, 6, D)
        h, cosf, sinf = prenorm_rope(xc, mod, mix_norm_pre[layer], positions, S)
        w_in_b = w_in[layer].astype(BF16)
        outs, stats = [], []
        for g, (window, d) in enumerate(ATTN_PATTERNS):
            assert window // d == ATTN_BLOCK and S % (d * ATTN_BLOCK) == 0
            qkv = qkv_proj(h, w_in_b, cosf, sinf, B, S, U, g, d)
            o, st = dilated_attention(qkv, d)
            outs.append(o)
            stats.append(st)
        am = merge_groups(outs, stats, tuple(d for _, d in ATTN_PATTERNS))
        proj = matmul_cols(h, w_in_b, U, 9, "rest_proj")
        cn = conv_branch(proj, B, S, U, conv_dw[layer], conv_dw_bias[layer], conv_ln_gain[layer], conv_ln_bias[layer])
        z = gate_proj(cn, am, w_conv_out[layer].astype(BF16), w_attn_out[layer].astype(BF16), proj, U)
        y = matmul_cols(z, w_out[layer].astype(BF16), U, 0, "out_proj")
        rcat = jnp.zeros((D, LANES), F32)
        rcat = rcat.at[:, :N_EXPERT_GROUPS].set(router_group[layer])
        rcat = rcat.at[:, N_EXPERT_GROUPS:N_EXPERT_GROUPS + N_EXPERTS].set(router_expert[layer]).astype(BF16)
        x1, hp, eid, wts, rank, cnt = residual_prenorm_router(
            y, xc, mod, mix_norm_post[layer], ffn_norm_pre[layer], rcat, S)
        TM = MOE_ROWS
        P = (2 * N + N_EXPERTS * (TM - 1)) // TM * TM
        seg_start, seg_chunks = _moe_layout(cnt[0, :N_EXPERTS])
        start_row = jnp.zeros((8, LANES), F32).at[:, :N_EXPERTS].set(seg_start.astype(F32)[None, :])
        dest = dest_rows(eid, rank, start_row)
        xs = scatter_rows(hp, dest, P)
        yp = expert_ffn(xs, seg_start, seg_chunks, expert_w1[layer], expert_w3[layer], expert_w2[layer])
        xc = combine(yp, dest, wts, x1, mod, ffn_norm_post[layer], S)
    return xc.reshape(B, S, D)
```

```python
import functools

import jax
import jax.numpy as jnp
from jax import lax
from jax.experimental import pallas as pl
from jax.experimental.pallas import tpu as pltpu

F32 = jnp.float32
BF16 = jnp.bfloat16
I32 = jnp.int32
U32 = jnp.uint32

HEAD_DIM = 128
LANES = 128
ATTN_BLOCK = 128
ATTN_PATTERNS = ((128, 1), (512, 4), (2048, 16))
ROPE_THETA = 10000.0
CONV_WIDTH = 31
CONV_HALO = 32
N_EXPERT_GROUPS = 8
EXPERTS_PER_GROUP = 8
N_EXPERTS = 64
NORM_EPS = 1e-6
NEG_BIG = -1e30
MOE_ROWS = 256
V7X_VMEM_LIMIT = 60 * 1024 * 1024


def _cparams(sem):
    return pltpu.CompilerParams(dimension_semantics=sem, vmem_limit_bytes=V7X_VMEM_LIMIT)


def _sigmoid(x):
    return 1.0 / (1.0 + jnp.exp(-x))


def _pack_halves(lo, hi):
    lo_b = lax.bitcast_convert_type(lo.astype(BF16).astype(F32), U32) >> 16
    hi_b = lax.bitcast_convert_type(hi.astype(BF16).astype(F32), U32) & jnp.uint32(0xFFFF0000)
    return hi_b | lo_b


def _unpack_halves(w):
    lo = lax.bitcast_convert_type(w << 16, F32)
    hi = lax.bitcast_convert_type(w & jnp.uint32(0xFFFF0000), F32)
    return lo, hi


def _ada_kernel(c_ref, w_ref, b_ref, o_ref):
    c = c_ref[...]
    cact = (c * _sigmoid(c)).astype(BF16)
    o_ref[...] = jnp.dot(cact, w_ref[...].astype(BF16), preferred_element_type=F32) + b_ref[...]


def ada_mod(c, ada_w, ada_b):
    B, D = c.shape
    W = ada_w.shape[1]
    rows = 8
    cp = jnp.zeros((rows, D), F32).at[:B].set(c)
    tn = min(512, W)
    out = pl.pallas_call(
        _ada_kernel,
        grid=(W // tn,),
        in_specs=[pl.BlockSpec((rows, D), lambda j: (0, 0)),
                  pl.BlockSpec((D, tn), lambda j: (0, j)),
                  pl.BlockSpec((1, tn), lambda j: (0, j))],
        out_specs=pl.BlockSpec((rows, tn), lambda j: (0, j)),
        out_shape=jax.ShapeDtypeStruct((rows, W), F32),
        compiler_params=_cparams(("parallel",)),
        name="ada_mod",
    )(cp, ada_w, ada_b.reshape(1, W))
    return out[:B]


def _prenorm_kernel(x_ref, mod_ref, gain_ref, pos_ref, invf_ref, h_ref, cos_ref, sin_ref):
    x = x_ref[...]
    ms = jnp.mean(x * x, axis=-1, keepdims=True)
    g = gain_ref[...] * (1.0 + mod_ref[1:2, :])
    h_ref[...] = ((x * lax.rsqrt(ms + NORM_EPS)) * g + mod_ref[0:1, :]).astype(BF16)
    ang = pos_ref[...].astype(F32) * invf_ref[...]
    lane = lax.broadcasted_iota(I32, ang.shape, 1)
    sn = jnp.sin(ang)
    cos_ref[...] = jnp.cos(ang)
    sin_ref[...] = jnp.where(lane < HEAD_DIM // 2, -sn, sn)


def prenorm_rope(x2d, mod, gain, positions, S):
    N, D = x2d.shape
    tr = min(256, S)
    tiles_per_b = S // tr
    half = HEAD_DIM // 2
    inv = ROPE_THETA ** (-jnp.arange(0, HEAD_DIM, 2, dtype=F32) / HEAD_DIM)
    invf = jnp.concatenate([inv, inv]).reshape(1, HEAD_DIM)
    del half
    return pl.pallas_call(
        _prenorm_kernel,
        grid=(N // tr,),
        in_specs=[pl.BlockSpec((tr, D), lambda i: (i, 0)),
                  pl.BlockSpec((None, 6, D), lambda i: (i // tiles_per_b, 0, 0)),
                  pl.BlockSpec((1, D), lambda i: (0, 0)),
                  pl.BlockSpec((tr, 1), lambda i: (i, 0)),
                  pl.BlockSpec((1, HEAD_DIM), lambda i: (0, 0))],
        out_specs=[pl.BlockSpec((tr, D), lambda i: (i, 0)),
                   pl.BlockSpec((tr, HEAD_DIM), lambda i: (i, 0)),
                   pl.BlockSpec((tr, HEAD_DIM), lambda i: (i, 0))],
        out_shape=[jax.ShapeDtypeStruct((N, D), BF16),
                   jax.ShapeDtypeStruct((N, HEAD_DIM), F32),
                   jax.ShapeDtypeStruct((N, HEAD_DIM), F32)],
        compiler_params=_cparams(("parallel",)),
        name="prenorm_rope",
    )(x2d, mod, gain.reshape(1, D), positions.reshape(N, 1), invf)


def _residue_rows(ref, r, d):
    if d == 1:
        return ref[...]
    return ref[pl.ds(r, ref.shape[0] // d, stride=d), :]


def _qkv_kernel(a_ref, w_ref, cos_ref, sin_ref, o_ref, acc_ref, *, d, heads, scale, n_i):
    s = pl.program_id(0)

    @pl.when(s == 0)
    def _():
        acc_ref[...] = jnp.zeros_like(acc_ref)

    kind = jnp.maximum(s - 1, 0) // n_i
    sc = jnp.where(kind == 0, scale, 1.0).astype(F32)
    is_v = kind == 2
    for r in range(d):
        c = jnp.where(is_v, 1.0, _residue_rows(cos_ref, r, d) * sc)
        sn = jnp.where(is_v, 0.0, _residue_rows(sin_ref, r, d) * sc)
        for h in range(heads):
            t = _residue_rows(acc_ref.at[h], r, d)
            o_ref[r, :, h * HEAD_DIM:(h + 1) * HEAD_DIM] = (
                t * c + pltpu.roll(t, HEAD_DIM // 2, 1) * sn).astype(o_ref.dtype)
    acc = jnp.dot(a_ref[...], w_ref[...], preferred_element_type=F32)
    for h in range(heads):
        acc_ref[h] = acc[:, h * HEAD_DIM:(h + 1) * HEAD_DIM]


def qkv_proj(h, w_bf16, cosf, sinf, B, S, U, g, d):
    N, D = h.shape
    tm = min(1024, S)
    tiles_per_b = S // tm
    n_i = N // tm
    last = 3 * n_i - 1
    kern = functools.partial(_qkv_kernel, d=d, heads=U // HEAD_DIM, scale=HEAD_DIM ** -0.5, n_i=n_i)

    def cur(s):
        return jnp.minimum(s, last)

    def fin(s):
        return jnp.maximum(s - 1, 0)

    return pl.pallas_call(
        kern,
        grid=(3 * n_i + 1,),
        in_specs=[pl.BlockSpec((tm, D), lambda s: (cur(s) % n_i, 0)),
                  pl.BlockSpec((D, U), lambda s: (0, (cur(s) // n_i) * 3 + g)),
                  pl.BlockSpec((tm, HEAD_DIM), lambda s: (fin(s) % n_i, 0)),
                  pl.BlockSpec((tm, HEAD_DIM), lambda s: (fin(s) % n_i, 0))],
        out_specs=pl.BlockSpec((None, None, d, tm // d, U),
                               lambda s: (fin(s) // n_i, (fin(s) % n_i) // tiles_per_b, 0,
                                          (fin(s) % n_i) % tiles_per_b, 0)),
        out_shape=jax.ShapeDtypeStruct((3, B, d, S // d, U), BF16),
        scratch_shapes=[pltpu.VMEM((U // HEAD_DIM, tm, HEAD_DIM), F32)],
        compiler_params=_cparams(("arbitrary",)),
        name=f"qkv_proj_d{d}",
    )(h, w_bf16, cosf, sinf)


def _matmul_kernel(a_ref, w_ref, o_ref):
    o_ref[...] = jnp.dot(a_ref[...], w_ref[...], preferred_element_type=F32).astype(o_ref.dtype)


def matmul_cols(a, w_bf16, U, first_tile, name):
    N, K = a.shape
    n_tiles = w_bf16.shape[1] // U - first_tile
    tm = min(1024, N)
    return pl.pallas_call(
        _matmul_kernel,
        grid=(n_tiles, N // tm),
        in_specs=[pl.BlockSpec((tm, K), lambda j, i: (i, 0)),
                  pl.BlockSpec((K, U), lambda j, i: (0, first_tile + j))],
        out_specs=pl.BlockSpec((tm, U), lambda j, i: (i, j)),
        out_shape=jax.ShapeDtypeStruct((N, n_tiles * U), BF16),
        compiler_params=_cparams(("parallel", "parallel")),
        name=name,
    )(a, w_bf16)


def _attn_kernel(q_ref, kc_ref, kp_ref, vc_ref, vp_ref, o_ref, st_ref, kx_ref, vx_ref, *, QB, H):
    i = pl.program_id(2)
    blk = ATTN_BLOCK
    kx_ref[0:blk, :] = kp_ref[...]
    kx_ref[blk:, :] = kc_ref[...]
    ones = jnp.ones((vx_ref.shape[0], HEAD_DIM), vx_ref.dtype)
    for h in range(H):
        hs = slice(h * HEAD_DIM, (h + 1) * HEAD_DIM)
        vx_ref[0:blk, 2 * h * HEAD_DIM:(2 * h + 1) * HEAD_DIM] = vp_ref[:, hs]
        vx_ref[blk:, 2 * h * HEAD_DIM:(2 * h + 1) * HEAD_DIM] = vc_ref[:, hs]
        vx_ref[:, (2 * h + 1) * HEAD_DIM:(2 * h + 2) * HEAD_DIM] = ones
    rows = lax.broadcasted_iota(I32, (blk, 2 * blk), 0)
    keys = lax.broadcasted_iota(I32, (blk, 2 * blk), 1)
    band = (keys >= rows) & (keys - blk <= rows)
    lane = lax.broadcasted_iota(I32, (blk, LANES), 1)
    dn = (((1,), (1,)), ((), ()))

    def body(qb, carry):
        r0 = pl.multiple_of(qb * blk, blk)
        ok = band & ((keys >= blk) | (i * QB + qb > 0))
        q = [q_ref[pl.ds(r0, blk), h * HEAD_DIM:(h + 1) * HEAD_DIM] for h in range(H)]
        k = [kx_ref[pl.ds(r0, 2 * blk), h * HEAD_DIM:(h + 1) * HEAD_DIM] for h in range(H)]
        s = [jnp.where(ok, lax.dot_general(q[h], k[h], dn, preferred_element_type=F32), NEG_BIG) for h in range(H)]
        m = [jnp.max(s[h], axis=-1, keepdims=True) for h in range(H)]
        p = [jnp.exp(s[h] - m[h]).astype(BF16) for h in range(H)]
        st = jnp.zeros((blk, LANES), F32)
        for h in range(H):
            v1 = vx_ref[pl.ds(r0, 2 * blk), 2 * h * HEAD_DIM:(2 * h + 2) * HEAD_DIM]
            acc = jnp.dot(p[h], v1, preferred_element_type=F32)
            l = acc[:, HEAD_DIM:]
            o_ref[pl.ds(r0, blk), h * HEAD_DIM:(h + 1) * HEAD_DIM] = (acc[:, :HEAD_DIM] / l).astype(o_ref.dtype)
            st = jnp.where(lane == h, m[h], st)
            st = jnp.where(lane == H + h, l, st)
        st_ref[pl.ds(r0, blk), :] = st
        return carry

    lax.fori_loop(0, QB, body, 0)


def dilated_attention(qkv, d):
    _, B, _, L, U = qkv.shape
    H = U // HEAD_DIM
    R = min(512, L)
    QB = R // ATTN_BLOCK

    def cur(kind):
        return pl.BlockSpec((None, None, None, R, U), lambda b, r, i: (kind, b, r, i, 0))

    def prev(kind):
        return pl.BlockSpec((None, None, None, ATTN_BLOCK, U),
                            lambda b, r, i: (kind, b, r, jnp.maximum(i * QB - 1, 0), 0))

    kern = functools.partial(_attn_kernel, QB=QB, H=H)
    return pl.pallas_call(
        kern,
        grid=(B, d, L // R),
        in_specs=[cur(0), cur(1), prev(1), cur(2), prev(2)],
        out_specs=[pl.BlockSpec((None, None, R, U), lambda b, r, i: (b, r, i, 0)),
                   pl.BlockSpec((None, None, R, LANES), lambda b, r, i: (b, r, i, 0))],
        out_shape=[jax.ShapeDtypeStruct((B, d, L, U), BF16),
                   jax.ShapeDtypeStruct((B, d, L, LANES), F32)],
        scratch_shapes=[pltpu.VMEM((R + ATTN_BLOCK, U), BF16), pltpu.VMEM((R + ATTN_BLOCK, 2 * U), BF16)],
        compiler_params=_cparams(("parallel", "parallel", "parallel")),
        name=f"dilated_attn_d{d}",
    )(qkv, qkv, qkv, qkv, qkv)


def _merge_kernel(o0_ref, o1_ref, o2_ref, s0_ref, s1_ref, s2_ref, out_ref, of_ref, sf_ref, *, H, dils):
    for g, (o_ref, s_ref) in enumerate(((o0_ref, s0_ref), (o1_ref, s1_ref), (o2_ref, s2_ref))):
        d = dils[g]
        n = sf_ref.shape[1] // d
        for r in range(d):
            rows = slice(None) if d == 1 else pl.ds(r, n, stride=d)
            sf_ref[g, rows, :] = s_ref[r]
            for h in range(H):
                of_ref[g, h, rows, :] = o_ref[r, :, h * HEAD_DIM:(h + 1) * HEAD_DIM].astype(F32)
    st = [sf_ref[0], sf_ref[1], sf_ref[2]]
    mx = jnp.maximum(jnp.maximum(st[0], st[1]), st[2])
    w = [pltpu.roll(s, LANES - H, 1) * jnp.exp(s - mx) for s in st]
    tot = w[0] + w[1] + w[2]
    coef = [x / tot for x in w]
    for h in range(H):
        hs = slice(h * HEAD_DIM, (h + 1) * HEAD_DIM)
        acc = coef[0][:, h:h + 1] * of_ref[0, h]
        acc += coef[1][:, h:h + 1] * of_ref[1, h]
        acc += coef[2][:, h:h + 1] * of_ref[2, h]
        out_ref[:, hs] = acc.astype(out_ref.dtype)


def merge_groups(outs, stats, dils):
    B, d0, L0, U = outs[0].shape
    S = d0 * L0
    H = U // HEAD_DIM
    tm = min(512, S)
    tiles_per_b = S // tm

    def ospec(d, w):
        return pl.BlockSpec((None, d, tm // d, w), lambda i: (i // tiles_per_b, 0, i % tiles_per_b, 0))

    return pl.pallas_call(
        functools.partial(_merge_kernel, H=H, dils=dils),
        grid=(B * S // tm,),
        in_specs=[ospec(d, U) for d in dils] + [ospec(d, LANES) for d in dils],
        out_specs=pl.BlockSpec((tm, U), lambda i: (i, 0)),
        out_shape=jax.ShapeDtypeStruct((B * S, U), BF16),
        scratch_shapes=[pltpu.VMEM((3, H, tm, HEAD_DIM), F32), pltpu.VMEM((3, tm, LANES), F32)],
        compiler_params=_cparams(("parallel",)),
        name="merge_groups",
    )(*outs, *stats)


def _conv_kernel(a0_ref, a1_ref, b0_ref, b1_ref, ha0_ref, ha1_ref, hb0_ref, hb1_ref,
                 w_ref, bias_ref, g_ref, be_ref, o_ref, u_ref, c_ref, sh_ref, *, ts, U):
    i = pl.program_id(1)
    halo = CONV_HALO
    for half, (a_ref, b_ref, ha_ref, hb_ref) in enumerate(((a0_ref, b0_ref, ha0_ref, hb0_ref),
                                                            (a1_ref, b1_ref, ha1_ref, hb1_ref))):
        cs = slice(half * U, (half + 1) * U)
        u_ref[halo:halo + ts, cs] = a_ref[...].astype(F32) * _sigmoid(b_ref[...].astype(F32))
        hu = ha_ref[...].astype(F32) * _sigmoid(hb_ref[...].astype(F32))
        u_ref[0:halo, cs] = jnp.where(i > 0, hu, 0.0)
    C = 2 * U
    rc = 64
    off = halo - (CONV_WIDTH - 1)

    sub = 8
    n_al = ts + halo - sub

    def chan_body(cc, carry):
        c0 = pl.multiple_of(cc * LANES, LANES)
        sh_ref[0] = u_ref[:, pl.ds(c0, LANES)]
        for b in range(1, sub):
            sh_ref[b, 0:n_al, :] = u_ref[b:b + n_al, pl.ds(c0, LANES)]
        for rb in range(ts // rc):
            acc = jnp.zeros((rc, LANES), F32) + bias_ref[:, pl.ds(c0, LANES)]
            for j in range(CONV_WIDTH):
                a, b = divmod(off + j, sub)
                r0 = rb * rc + a * sub
                acc += w_ref[j:j + 1, pl.ds(c0, LANES)] * sh_ref[b, r0:r0 + rc, :]
            c_ref[rb * rc:(rb + 1) * rc, pl.ds(c0, LANES)] = acc
        return carry

    lax.fori_loop(0, C // LANES, chan_body, 0)

    rn = 16

    def norm_body(rb, carry):
        r0 = pl.multiple_of(rb * rn, rn)
        v = c_ref[pl.ds(r0, rn), :]
        mu = jnp.mean(v, axis=-1, keepdims=True)
        dv = v - mu
        var = jnp.mean(dv * dv, axis=-1, keepdims=True)
        y = dv * lax.rsqrt(var + NORM_EPS) * g_ref[...] + be_ref[...]
        o_ref[pl.ds(r0, rn), :] = (y * _sigmoid(y)).astype(o_ref.dtype)
        return carry

    lax.fori_loop(0, ts // rn, norm_body, 0, unroll=4)


def conv_branch(proj, B, S, U, conv_dw, conv_bias, ln_gain, ln_bias):
    IN = proj.shape[1]
    C = 2 * U
    ts = min(256, S)
    pv = proj.reshape(B, S, IN)
    hb = ts // CONV_HALO
    cur = lambda blk: pl.BlockSpec((None, ts, U), lambda b, i, blk=blk: (b, i, blk))
    prv = lambda blk: pl.BlockSpec((None, CONV_HALO, U), lambda b, i, blk=blk: (b, jnp.maximum(i * hb - 1, 0), blk))
    vec = pl.BlockSpec((1, C), lambda b, i: (0, 0))
    out = pl.pallas_call(
        functools.partial(_conv_kernel, ts=ts, U=U),
        grid=(B, S // ts),
        in_specs=[cur(0), cur(1), cur(2), cur(3), prv(0), prv(1), prv(2), prv(3),
                  pl.BlockSpec((CONV_WIDTH, C), lambda b, i: (0, 0)), vec, vec, vec],
        out_specs=pl.BlockSpec((None, ts, C), lambda b, i: (b, i, 0)),
        out_shape=jax.ShapeDtypeStruct((B, S, C), BF16),
        scratch_shapes=[pltpu.VMEM((ts + CONV_HALO, C), F32), pltpu.VMEM((ts, C), F32),
                        pltpu.VMEM((8, ts + CONV_HALO, LANES), F32)],
        compiler_params=_cparams(("parallel", "parallel")),
        name="conv_branch",
    )(pv, pv, pv, pv, pv, pv, pv, pv, conv_dw, conv_bias.reshape(1, C), ln_gain.reshape(1, C), ln_bias.reshape(1, C))
    return out.reshape(B * S, C)


def _gateproj_kernel(cn_ref, am_ref, wc_ref, wa_ref, gc_ref, ga_ref, o_ref):
    conv = jnp.dot(cn_ref[...], wc_ref[...], preferred_element_type=F32)
    z = _sigmoid(gc_ref[...].astype(F32)) * conv
    attn = jnp.dot(am_ref[...], wa_ref[...], preferred_element_type=F32)
    z += _sigmoid(ga_ref[...].astype(F32)) * attn
    o_ref[...] = z.astype(o_ref.dtype)


def gate_proj(cn, am, wc, wa, proj, U):
    N, C = cn.shape
    D = wc.shape[1]
    tn = U
    tm = min(1024, N)
    return pl.pallas_call(
        _gateproj_kernel,
        grid=(D // tn, N // tm),
        in_specs=[pl.BlockSpec((tm, C), lambda j, i: (i, 0)),
                  pl.BlockSpec((tm, U), lambda j, i: (i, 0)),
                  pl.BlockSpec((C, tn), lambda j, i: (0, j)),
                  pl.BlockSpec((U, tn), lambda j, i: (0, j)),
                  pl.BlockSpec((tm, tn), lambda j, i: (i, 4 + j)),
                  pl.BlockSpec((tm, tn), lambda j, i: (i, 8 + j))],
        out_specs=pl.BlockSpec((tm, tn), lambda j, i: (i, j)),
        out_shape=jax.ShapeDtypeStruct((N, D), BF16),
        compiler_params=_cparams(("parallel", "parallel")),
        name="gate_proj",
    )(cn, am, wc, wa, proj, proj)


def _router_kernel(y_ref, xin_ref, mod_ref, gpost_ref, gain_ref, r_ref,
                   x1_ref, hp_ref, eid_ref, wt_ref, rank_ref, cnt_ref, carry_ref, *, tr):
    step = pl.program_id(0)

    @pl.when(step == 0)
    def _():
        carry_ref[...] = jnp.zeros_like(carry_ref)

    y = y_ref[...].astype(F32)
    yms = jnp.mean(y * y, axis=-1, keepdims=True)
    x = xin_ref[...] + (y * lax.rsqrt(yms + NORM_EPS)) * (mod_ref[2:3, :] * gpost_ref[...])
    x1_ref[...] = x
    D = x.shape[1]
    ms = jnp.mean(x * x, axis=-1, keepdims=True)
    h = (x * lax.rsqrt(ms + NORM_EPS)) * (gain_ref[...] * (1.0 + mod_ref[4:5, :])) + mod_ref[3:4, :]
    lo = h[:, :D // 2]
    hi = h[:, D // 2:]
    hp_ref[...] = _pack_halves(lo, hi)
    logits = (jnp.dot(lo.astype(BF16), r_ref[:D // 2, :], preferred_element_type=F32)
              + jnp.dot(hi.astype(BF16), r_ref[D // 2:, :], preferred_element_type=F32))
    lane = lax.broadcasted_iota(I32, logits.shape, 1)
    G = N_EXPERT_GROUPS
    is_g = lane < G
    gl = jnp.where(is_g, logits, NEG_BIG)
    gmax = jnp.max(gl, axis=-1, keepdims=True)
    grp = jnp.min(jnp.where(gl == gmax, lane, LANES), axis=-1, keepdims=True)
    p_grp = 1.0 / jnp.sum(jnp.where(is_g, jnp.exp(gl - gmax), 0.0), axis=-1, keepdims=True)
    in_grp = (lane >= G) & (lane < G + N_EXPERTS) & (((lane - G) // EXPERTS_PER_GROUP) == grp)
    el = jnp.where(in_grp, logits, NEG_BIG)
    v0 = jnp.max(el, axis=-1, keepdims=True)
    i0 = jnp.min(jnp.where(in_grp & (el == v0), lane, LANES), axis=-1, keepdims=True)
    in2 = in_grp & (lane != i0)
    el2 = jnp.where(in2, logits, NEG_BIG)
    v1 = jnp.max(el2, axis=-1, keepdims=True)
    i1 = jnp.min(jnp.where(in2 & (el2 == v1), lane, LANES), axis=-1, keepdims=True)
    e1 = jnp.exp(v1 - v0)
    w0 = p_grp / (1.0 + e1)
    w1 = p_grp * e1 / (1.0 + e1)
    ex0 = i0 - G
    ex1 = i1 - G
    eid_ref[...] = jnp.where(lane == 0, ex0, jnp.where(lane == 1, ex1, 0))
    wt_ref[...] = jnp.where(lane == 0, w0, jnp.where(lane == 1, w1, 0.0))
    oh0 = (lane == ex0).astype(F32)
    oh1 = (lane == ex1).astype(F32)
    both = oh0 + oh1
    rr = lax.broadcasted_iota(I32, (tr, tr), 0)
    cc = lax.broadcasted_iota(I32, (tr, tr), 1)
    tril = (cc < rr).astype(BF16)
    before = jnp.dot(tril, both.astype(BF16), preferred_element_type=F32) + carry_ref[0:1, :]
    rk0 = jnp.sum(before * oh0, axis=-1, keepdims=True)
    rk1 = jnp.sum(before * oh1, axis=-1, keepdims=True)
    rank_ref[...] = jnp.where(lane == 0, rk0, jnp.where(lane == 1, rk1, 0.0))
    newc = carry_ref[0:1, :] + jnp.sum(both, axis=0, keepdims=True)
    carry_ref[...] = jnp.broadcast_to(newc, carry_ref.shape)
    cnt_ref[...] = jnp.broadcast_to(newc, cnt_ref.shape)


def residual_prenorm_router(y, x2d, mod, gain_post, gain, rcat, S):
    N, D = x2d.shape
    tr = min(256, S)
    tiles_per_b = S // tr
    lane_spec = pl.BlockSpec((tr, LANES), lambda i: (i, 0))
    row_spec = pl.BlockSpec((tr, D), lambda i: (i, 0))
    vec_spec = pl.BlockSpec((1, D), lambda i: (0, 0))
    return pl.pallas_call(
        functools.partial(_router_kernel, tr=tr),
        grid=(N // tr,),
        in_specs=[row_spec, row_spec,
                  pl.BlockSpec((None, 6, D), lambda i: (i // tiles_per_b, 0, 0)),
                  vec_spec, vec_spec,
                  pl.BlockSpec((D, LANES), lambda i: (0, 0))],
        out_specs=[row_spec, pl.BlockSpec((tr, D // 2), lambda i: (i, 0)), lane_spec, lane_spec, lane_spec,
                   pl.BlockSpec((8, LANES), lambda i: (0, 0))],
        out_shape=[jax.ShapeDtypeStruct((N, D), F32),
                   jax.ShapeDtypeStruct((N, D // 2), U32),
                   jax.ShapeDtypeStruct((N, LANES), I32),
                   jax.ShapeDtypeStruct((N, LANES), F32),
                   jax.ShapeDtypeStruct((N, LANES), F32),
                   jax.ShapeDtypeStruct((8, LANES), F32)],
        scratch_shapes=[pltpu.VMEM((8, LANES), F32)],
        compiler_params=_cparams(("arbitrary",)),
        name="residual_prenorm_router",
    )(y, x2d, mod, gain_post.reshape(1, D), gain.reshape(1, D), rcat)


def _dest_kernel(eid_ref, rank_ref, start_ref, o_ref):
    lane = lax.broadcasted_iota(I32, eid_ref.shape, 1)
    eid = eid_ref[...]
    rank = rank_ref[...]
    start = start_ref[0:1, :]
    d = []
    for k in range(2):
        oh = (lane == eid[:, k:k + 1]).astype(F32)
        d.append(jnp.sum(oh * start, axis=-1, keepdims=True) + rank[:, k:k + 1])
    o_ref[...] = jnp.where(lane == 0, d[0], jnp.where(lane == 1, d[1], 0.0)).astype(I32)


def dest_rows(eid, rank, pad_start):
    N = eid.shape[0]
    tr = min(1024, N)
    spec = pl.BlockSpec((tr, LANES), lambda i: (i, 0))
    return pl.pallas_call(
        _dest_kernel,
        grid=(N // tr,),
        in_specs=[spec, spec, pl.BlockSpec((8, LANES), lambda i: (0, 0))],
        out_specs=spec,
        out_shape=jax.ShapeDtypeStruct((N, LANES), I32),
        compiler_params=_cparams(("parallel",)),
        name="dest_rows",
    )(eid, rank, pad_start)


def _scatter_kernel(dest_ref, h_ref, xs_in_ref, xs_ref, sem, *, T):
    del xs_in_ref

    sub = 8

    def issue(tb, carry):
        for r in range(sub):
            t = tb * sub + r
            for k in range(2):
                d = dest_ref[0, 0, 2 * t + k]
                pltpu.make_async_copy(h_ref.at[pl.ds(t, 1)], xs_ref.at[pl.ds(d, 1)], sem).start()
        return carry

    lax.fori_loop(0, T // sub, issue, 0)
    for k in range(2):
        pltpu.make_async_copy(h_ref, xs_ref.at[pl.ds(0, T)], sem).wait()


def scatter_rows(hp, dest, P):
    N, W = hp.shape
    T = min(256, N)
    dest_s = dest[:, :2].reshape(N // T, 1, 2 * T)
    xs0 = jnp.zeros((P, W), hp.dtype)
    return pl.pallas_call(
        functools.partial(_scatter_kernel, T=T),
        grid=(N // T,),
        in_specs=[pl.BlockSpec((1, 1, 2 * T), lambda i: (i, 0, 0), memory_space=pltpu.SMEM),
                  pl.BlockSpec((T, W), lambda i: (i, 0)),
                  pl.BlockSpec(memory_space=pl.ANY)],
        out_specs=pl.BlockSpec(memory_space=pl.ANY),
        out_shape=jax.ShapeDtypeStruct((P, W), hp.dtype),
        scratch_shapes=[pltpu.SemaphoreType.DMA(())],
        input_output_aliases={2: 0},
        compiler_params=_cparams(("arbitrary",)),
        name="scatter_rows",
    )(dest_s, hp, xs0)


def _expert_chunks(st_ref, nc_ref, in_hbm, out_hbm, ibuf, obuf, isem, osem, compute):
    e = pl.program_id(0)
    n_e = pl.num_programs(0)
    n = nc_ref[e]
    TM = ibuf.shape[1]
    base = pl.multiple_of(st_ref[e], TM)
    g0 = base // TM

    def icopy(row, slot):
        return pltpu.make_async_copy(in_hbm.at[pl.ds(row, TM)], ibuf.at[slot], isem.at[slot])

    def ocopy(row, slot):
        return pltpu.make_async_copy(obuf.at[slot], out_hbm.at[pl.ds(row, TM)], osem.at[slot])

    @pl.when((e == 0) & (n > 0))
    def _():
        icopy(base, 0).start()

    def body(c, carry):
        g = g0 + c
        slot = g % 2
        row = pl.multiple_of(base + c * TM, TM)

        @pl.when(c + 1 < n)
        def _():
            icopy(row + TM, 1 - slot).start()

        icopy(row, slot).wait()

        @pl.when(g >= 2)
        def _():
            ocopy(row, slot).wait()

        compute(ibuf.at[slot], obuf.at[slot])
        ocopy(row, slot).start()
        return carry

    lax.fori_loop(0, n, body, 0)

    @pl.when(e + 1 < n_e)
    def _():
        @pl.when(nc_ref[e + 1] > 0)
        def _():
            icopy(pl.multiple_of(st_ref[e + 1], TM), (g0 + n) % 2).start()

    @pl.when(e == n_e - 1)
    def _():
        g_end = g0 + n

        @pl.when(g_end >= 2)
        def _():
            ocopy(base, g_end % 2).wait()

        @pl.when(g_end >= 1)
        def _():
            ocopy(base, (g_end + 1) % 2).wait()

        used = pl.multiple_of(base + n * TM, TM)
        n_tail = (out_hbm.shape[0] - used) // TM
        obuf[0] = jnp.zeros(obuf.shape[1:], obuf.dtype)

        def zstart(c, carry):
            ocopy(pl.multiple_of(used + c * TM, TM), 0).start()
            return carry

        def zwait(c, carry):
            ocopy(used, 0).wait()
            return carry

        lax.fori_loop(0, n_tail, zstart, 0)
        lax.fori_loop(0, n_tail, zwait, 0)


WEIGHT_DMA_QUEUE = 1


def _stream_expert_weights(parts, wsem):
    e = pl.program_id(0)
    n_e = pl.num_programs(0)
    slot = e % 2

    def copy(k, ei, s):
        src, dst = parts[k]
        return pltpu.make_async_copy(src(ei), dst(s), wsem.at[k, s])

    @pl.when(e == 0)
    def _():
        for k in range(len(parts)):
            copy(k, 0, 0).start(priority=WEIGHT_DMA_QUEUE)

    @pl.when(e + 1 < n_e)
    def _():
        for k in range(len(parts)):
            copy(k, e + 1, 1 - slot).start(priority=WEIGHT_DMA_QUEUE)

    for k in range(len(parts)):
        copy(k, e, slot).wait()
    return slot


def _moe_up_kernel(st_ref, nc_ref, xs_hbm, w1_hbm, w3_hbm, o_hbm, wb_ref, w1_buf, w3_buf, ibuf, obuf,
                   wsem, isem, osem, *, F):
    slot = _stream_expert_weights([(lambda ei: w1_hbm.at[ei], lambda s: w1_buf.at[s]),
                                   (lambda ei: w3_hbm.at[ei], lambda s: w3_buf.at[s])], wsem)

    @pl.when(nc_ref[pl.program_id(0)] > 0)
    def _():
        wb_ref[:, :F] = w1_buf[slot].astype(BF16)
        wb_ref[:, F:] = w3_buf[slot].astype(BF16)

    def compute(x_ref, o_ref):
        lo, hi = _unpack_halves(x_ref[...])
        half = wb_ref.shape[0] // 2
        hcat = (jnp.dot(lo.astype(BF16), wb_ref[:half, :], preferred_element_type=F32)
                + jnp.dot(hi.astype(BF16), wb_ref[half:, :], preferred_element_type=F32))
        a = hcat[:, :F]
        o_ref[...] = (a * _sigmoid(a) * hcat[:, F:]).astype(o_ref.dtype)

    _expert_chunks(st_ref, nc_ref, xs_hbm, o_hbm, ibuf, obuf, isem, osem, compute)


def _moe_down_kernel(st_ref, nc_ref, h_hbm, w2_hbm, o_hbm, wb_ref, w2_buf, ibuf, obuf, wsem, isem, osem):
    slot = _stream_expert_weights([(lambda ei: w2_hbm.at[ei], lambda s: w2_buf.at[s])], wsem)

    @pl.when(nc_ref[pl.program_id(0)] > 0)
    def _():
        wb_ref[...] = w2_buf[slot].astype(BF16)

    def compute(h_ref, o_ref):
        y = jnp.dot(h_ref[...], wb_ref[...], preferred_element_type=F32)
        half = y.shape[1] // 2
        o_ref[...] = _pack_halves(y[:, :half], y[:, half:])

    _expert_chunks(st_ref, nc_ref, h_hbm, o_hbm, ibuf, obuf, isem, osem, compute)


def expert_ffn(xs, seg_start, seg_chunks, w1, w3, w2):
    P, Wp = xs.shape
    E, D, F = w1.shape
    TM = MOE_ROWS
    any_spec = pl.BlockSpec(memory_space=pl.ANY)
    dma2 = pltpu.SemaphoreType.DMA((2,))
    hmid = pl.pallas_call(
        functools.partial(_moe_up_kernel, F=F),
        grid_spec=pltpu.PrefetchScalarGridSpec(
            num_scalar_prefetch=2,
            grid=(E,),
            in_specs=[any_spec, any_spec, any_spec],
            out_specs=any_spec,
            scratch_shapes=[pltpu.VMEM((D, 2 * F), BF16), pltpu.VMEM((2, D, F), F32), pltpu.VMEM((2, D, F), F32),
                            pltpu.VMEM((2, TM, Wp), U32), pltpu.VMEM((2, TM, F), BF16),
                            pltpu.SemaphoreType.DMA((2, 2)), dma2, dma2]),
        out_shape=jax.ShapeDtypeStruct((P, F), BF16),
        compiler_params=_cparams(("arbitrary",)),
        name="moe_up",
    )(seg_start, seg_chunks, xs, w1, w3)
    return pl.pallas_call(
        _moe_down_kernel,
        grid_spec=pltpu.PrefetchScalarGridSpec(
            num_scalar_prefetch=2,
            grid=(E,),
            in_specs=[any_spec, any_spec],
            out_specs=any_spec,
            scratch_shapes=[pltpu.VMEM((F, D), BF16), pltpu.VMEM((2, F, D), F32), pltpu.VMEM((2, TM, F), BF16),
                            pltpu.VMEM((2, TM, D // 2), U32), pltpu.SemaphoreType.DMA((1, 2)), dma2, dma2]),
        out_shape=jax.ShapeDtypeStruct((P, D // 2), U32),
        compiler_params=_cparams(("arbitrary",)),
        name="moe_down",
    )(seg_start, seg_chunks, hmid, w2)


def _combine_kernel(pos_ref, nxt_ref, wt_ref, x_ref, mod_ref, gain_ref, y_hbm, o_ref, buf_ref, sem, *, T):
    i = pl.program_id(0)
    n = pl.num_programs(0)
    slot = i % 2

    def gather(p_ref, s):
        sub = 8

        def issue(tb, carry):
            for r in range(sub):
                t = tb * sub + r
                for k in range(2):
                    p = p_ref[0, 0, 2 * t + k]
                    pltpu.make_async_copy(y_hbm.at[pl.ds(p, 1)], buf_ref.at[s, k, pl.ds(t, 1)],
                                          sem.at[s]).start()
            return carry

        lax.fori_loop(0, T // sub, issue, 0)

    @pl.when(i == 0)
    def _():
        gather(pos_ref, 0)

    @pl.when(i + 1 < n)
    def _():
        gather(nxt_ref, 1 - slot)

    for k in range(2):
        pltpu.make_async_copy(y_hbm.at[pl.ds(0, T)], buf_ref.at[slot, k], sem.at[slot]).wait()

    wt = wt_ref[...]
    w0 = wt[:, 0:1]
    w1 = wt[:, 1:2]
    lo0, hi0 = _unpack_halves(buf_ref[slot, 0])
    lo1, hi1 = _unpack_halves(buf_ref[slot, 1])
    ylo = w0 * lo0 + w1 * lo1
    yhi = w0 * hi0 + w1 * hi1
    D = x_ref.shape[1]
    half = D // 2
    ms = (jnp.sum(ylo * ylo, axis=-1, keepdims=True) + jnp.sum(yhi * yhi, axis=-1, keepdims=True)) / D
    inv = lax.rsqrt(ms + NORM_EPS)
    o_ref[:, :half] = x_ref[:, :half] + mod_ref[5:6, :half] * (ylo * inv * gain_ref[:, :half])
    o_ref[:, half:] = x_ref[:, half:] + mod_ref[5:6, half:] * (yhi * inv * gain_ref[:, half:])


def combine(yp, dest, wts, x1, mod, gain, S):
    N, D = x1.shape
    T = min(128, S)
    tiles_per_b = S // T
    pos_s = dest[:, :2].reshape(N // T, 1, 2 * T)
    n_tiles = N // T
    return pl.pallas_call(
        functools.partial(_combine_kernel, T=T),
        grid=(n_tiles,),
        in_specs=[pl.BlockSpec((1, 1, 2 * T), lambda i: (i, 0, 0), memory_space=pltpu.SMEM),
                  pl.BlockSpec((1, 1, 2 * T), lambda i: (jnp.minimum(i + 1, n_tiles - 1), 0, 0),
                               memory_space=pltpu.SMEM),
                  pl.BlockSpec((T, LANES), lambda i: (i, 0)),
                  pl.BlockSpec((T, D), lambda i: (i, 0)),
                  pl.BlockSpec((None, 6, D), lambda i: (i // tiles_per_b, 0, 0)),
                  pl.BlockSpec((1, D), lambda i: (0, 0)),
                  pl.BlockSpec(memory_space=pl.ANY)],
        out_specs=pl.BlockSpec((T, D), lambda i: (i, 0)),
        out_shape=jax.ShapeDtypeStruct((N, D), F32),
        scratch_shapes=[pltpu.VMEM((2, 2, T, D // 2), U32), pltpu.SemaphoreType.DMA((2,))],
        compiler_params=_cparams(("arbitrary",)),
        name="combine",
    )(pos_s, pos_s, wts, x1, mod, gain.reshape(1, D), yp)


def _moe_layout(counts):
    TM = MOE_ROWS
    chunks = (counts.astype(I32) + TM - 1) // TM
    start = (jnp.cumsum(chunks) - chunks) * TM
    return start, chunks


def kernel(x, c, positions, ada_w, ada_b, mix_norm_pre, mix_norm_post, w_in, conv_dw, conv_dw_bias, conv_ln_gain, conv_ln_bias, w_conv_out, w_attn_out, w_out, ffn_norm_pre, ffn_norm_post, router_group, router_expert, expert_w1, expert_w3, expert_w2):
    B, S, D = x.shape
    N = B * S
    U = D // 4
    depth = ada_w.shape[0]
    xc = x.reshape(N, D)
    for layer in range(depth):
        mod = ada_mod(c, ada_w[layer], ada_b[layer]).reshape(B, 6, D)
        h, cosf, sinf = prenorm_rope(xc, mod, mix_norm_pre[layer], positions, S)
        w_in_b = w_in[layer].astype(BF16)
        outs, stats = [], []
        for g, (window, d) in enumerate(ATTN_PATTERNS):
            assert window // d == ATTN_BLOCK and S % (d * ATTN_BLOCK) == 0
            qkv = qkv_proj(h, w_in_b, cosf, sinf, B, S, U, g, d)
            o, st = dilated_attention(qkv, d)
            outs.append(o)
            stats.append(st)
        am = merge_groups(outs, stats, tuple(d for _, d in ATTN_PATTERNS))
        proj = matmul_cols(h, w_in_b, U, 9, "rest_proj")
        cn = conv_branch(proj, B, S, U, conv_dw[layer], conv_dw_bias[layer], conv_ln_gain[layer], conv_ln_bias[layer])
        z = gate_proj(cn, am, w_conv_out[layer].astype(BF16), w_attn_out[layer].astype(BF16), proj, U)
        y = matmul_cols(z, w_out[layer].astype(BF16), U, 0, "out_proj")
        rcat = jnp.zeros((D, LANES), F32)
        rcat = rcat.at[:, :N_EXPERT_GROUPS].set(router_group[layer])
        rcat = rcat.at[:, N_EXPERT_GROUPS:N_EXPERT_GROUPS + N_EXPERTS].set(router_expert[layer]).astype(BF16)
        x1, hp, eid, wts, rank, cnt = residual_prenorm_router(
            y, xc, mod, mix_norm_post[layer], ffn_norm_pre[layer], rcat, S)
        TM = MOE_ROWS
        P = (2 * N + N_EXPERTS * (TM - 1)) // TM * TM
        seg_start, seg_chunks = _moe_layout(cnt[0, :N_EXPERTS])
        start_row = jnp.zeros((8, LANES), F32).at[:, :N_EXPERTS].set(seg_start.astype(F32)[None, :])
        dest = dest_rows(eid, rank, start_row)
        xs = scatter_rows(hp, dest, P)
        yp = expert_ffn(xs, seg_start, seg_chunks, expert_w1[layer], expert_w3[layer], expert_w2[layer])
        xc = combine(yp, dest, wts, x1, mod, ffn_norm_post[layer], S)
    return xc.reshape(B, S, D)
```

```python
import functools

import jax
import jax.numpy as jnp
from jax import lax
from jax.experimental import pallas as pl
from jax.experimental.pallas import tpu as pltpu

F32 = jnp.float32
BF16 = jnp.bfloat16
I32 = jnp.int32
U32 = jnp.uint32

HEAD_DIM = 128
LANES = 128
ATTN_BLOCK = 128
ATTN_PATTERNS = ((128, 1), (512, 4), (2048, 16))
ROPE_THETA = 10000.0
CONV_WIDTH = 31
CONV_HALO = 32
N_EXPERT_GROUPS = 8
EXPERTS_PER_GROUP = 8
N_EXPERTS = 64
NORM_EPS = 1e-6
NEG_BIG = -1e30
MOE_ROWS = 256
V7X_VMEM_LIMIT = 60 * 1024 * 1024


def _cparams(sem):
    return pltpu.CompilerParams(dimension_semantics=sem, vmem_limit_bytes=V7X_VMEM_LIMIT)


def _sigmoid(x):
    return 1.0 / (1.0 + jnp.exp(-x))


def _pack_halves(lo, hi):
    lo_b = lax.bitcast_convert_type(lo.astype(BF16).astype(F32), U32) >> 16
    hi_b = lax.bitcast_convert_type(hi.astype(BF16).astype(F32), U32) & jnp.uint32(0xFFFF0000)
    return hi_b | lo_b


def _unpack_halves(w):
    lo = lax.bitcast_convert_type(w << 16, F32)
    hi = lax.bitcast_convert_type(w & jnp.uint32(0xFFFF0000), F32)
    return lo, hi


def _ada_kernel(c_ref, w_ref, b_ref, o_ref):
    c = c_ref[...]
    cact = (c * _sigmoid(c)).astype(BF16)
    o_ref[...] = jnp.dot(cact, w_ref[...].astype(BF16), preferred_element_type=F32) + b_ref[...]


def ada_mod(c, ada_w, ada_b):
    B, D = c.shape
    W = ada_w.shape[1]
    rows = 8
    cp = jnp.zeros((rows, D), F32).at[:B].set(c)
    tn = min(1024, W)
    out = pl.pallas_call(
        _ada_kernel,
        grid=(W // tn,),
        in_specs=[pl.BlockSpec((rows, D), lambda j: (0, 0)),
                  pl.BlockSpec((D, tn), lambda j: (0, j)),
                  pl.BlockSpec((1, tn), lambda j: (0, j))],
        out_specs=pl.BlockSpec((rows, tn), lambda j: (0, j)),
        out_shape=jax.ShapeDtypeStruct((rows, W), F32),
        compiler_params=_cparams(("parallel",)),
        name="ada_mod",
    )(cp, ada_w, ada_b.reshape(1, W))
    return out[:B]


def _prenorm_kernel(x_ref, mod_ref, gain_ref, pos_ref, invf_ref, h_ref, cos_ref, sin_ref):
    x = x_ref[...]
    ms = jnp.mean(x * x, axis=-1, keepdims=True)
    g = gain_ref[...] * (1.0 + mod_ref[1:2, :])
    h_ref[...] = ((x * lax.rsqrt(ms + NORM_EPS)) * g + mod_ref[0:1, :]).astype(BF16)
    ang = pos_ref[...].astype(F32) * invf_ref[...]
    lane = lax.broadcasted_iota(I32, ang.shape, 1)
    sn = jnp.sin(ang)
    cos_ref[...] = jnp.cos(ang)
    sin_ref[...] = jnp.where(lane < HEAD_DIM // 2, -sn, sn)


def prenorm_rope(x2d, mod, gain, positions, S):
    N, D = x2d.shape
    tr = min(256, S)
    tiles_per_b = S // tr
    inv = ROPE_THETA ** (-jnp.arange(0, HEAD_DIM, 2, dtype=F32) / HEAD_DIM)
    invf = jnp.concatenate([inv, inv]).reshape(1, HEAD_DIM)
    return pl.pallas_call(
        _prenorm_kernel,
        grid=(N // tr,),
        in_specs=[pl.BlockSpec((tr, D), lambda i: (i, 0)),
                  pl.BlockSpec((None, 6, D), lambda i: (i // tiles_per_b, 0, 0)),
                  pl.BlockSpec((1, D), lambda i: (0, 0)),
                  pl.BlockSpec((tr, 1), lambda i: (i, 0)),
                  pl.BlockSpec((1, HEAD_DIM), lambda i: (0, 0))],
        out_specs=[pl.BlockSpec((tr, D), lambda i: (i, 0)),
                   pl.BlockSpec((tr, HEAD_DIM), lambda i: (i, 0)),
                   pl.BlockSpec((tr, HEAD_DIM), lambda i: (i, 0))],
        out_shape=[jax.ShapeDtypeStruct((N, D), BF16),
                   jax.ShapeDtypeStruct((N, HEAD_DIM), F32),
                   jax.ShapeDtypeStruct((N, HEAD_DIM), F32)],
        compiler_params=_cparams(("parallel",)),
        name="prenorm_rope",
    )(x2d, mod, gain.reshape(1, D), positions.reshape(N, 1), invf)


def _residue_rows(ref, r, d):
    if d == 1:
        return ref[...]
    return ref[pl.ds(r, ref.shape[0] // d, stride=d), :]


def _qkv_kernel(a_ref, w_ref, cos_ref, sin_ref, o_ref, acc_ref, *, d, heads, scale, n_i):
    s = pl.program_id(0)

    @pl.when(s == 0)
    def _():
        acc_ref[...] = jnp.zeros_like(acc_ref)

    kind = jnp.maximum(s - 1, 0) // n_i
    sc = jnp.where(kind == 0, scale, 1.0).astype(F32)
    is_v = kind == 2
    for r in range(d):
        c = jnp.where(is_v, 1.0, _residue_rows(cos_ref, r, d) * sc)
        sn = jnp.where(is_v, 0.0, _residue_rows(sin_ref, r, d) * sc)
        for h in range(heads):
            t = _residue_rows(acc_ref.at[h], r, d)
            o_ref[r, :, h * HEAD_DIM:(h + 1) * HEAD_DIM] = (
                t * c + pltpu.roll(t, HEAD_DIM // 2, 1) * sn).astype(o_ref.dtype)
    acc = jnp.dot(a_ref[...], w_ref[...], preferred_element_type=F32)
    for h in range(heads):
        acc_ref[h] = acc[:, h * HEAD_DIM:(h + 1) * HEAD_DIM]


def qkv_proj(h, w_bf16, cosf, sinf, B, S, U, g, d):
    N, D = h.shape
    tm = min(1024, S)
    tiles_per_b = S // tm
    n_i = N // tm
    last = 3 * n_i - 1
    kern = functools.partial(_qkv_kernel, d=d, heads=U // HEAD_DIM, scale=HEAD_DIM ** -0.5, n_i=n_i)

    def cur(s):
        return jnp.minimum(s, last)

    def fin(s):
        return jnp.maximum(s - 1, 0)

    return pl.pallas_call(
        kern,
        grid=(3 * n_i + 1,),
        in_specs=[pl.BlockSpec((tm, D), lambda s: (cur(s) % n_i, 0)),
                  pl.BlockSpec((D, U), lambda s: (0, (cur(s) // n_i) * 3 + g)),
                  pl.BlockSpec((tm, HEAD_DIM), lambda s: (fin(s) % n_i, 0)),
                  pl.BlockSpec((tm, HEAD_DIM), lambda s: (fin(s) % n_i, 0))],
        out_specs=pl.BlockSpec((None, None, d, tm // d, U),
                               lambda s: (fin(s) // n_i, (fin(s) % n_i) // tiles_per_b, 0,
                                          (fin(s) % n_i) % tiles_per_b, 0)),
        out_shape=jax.ShapeDtypeStruct((3, B, d, S // d, U), BF16),
        scratch_shapes=[pltpu.VMEM((U // HEAD_DIM, tm, HEAD_DIM), F32)],
        compiler_params=_cparams(("arbitrary",)),
        name=f"qkv_proj_d{d}",
    )(h, w_bf16, cosf, sinf)


def _matmul_kernel(a_ref, w_ref, o_ref):
    o_ref[...] = jnp.dot(a_ref[...], w_ref[...], preferred_element_type=F32).astype(o_ref.dtype)


def matmul_cols(a, w_bf16, U, first_tile, name):
    N, K = a.shape
    n_tiles = w_bf16.shape[1] // U - first_tile
    tm = min(1024, N)
    return pl.pallas_call(
        _matmul_kernel,
        grid=(n_tiles, N // tm),
        in_specs=[pl.BlockSpec((tm, K), lambda j, i: (i, 0)),
                  pl.BlockSpec((K, U), lambda j, i: (0, first_tile + j))],
        out_specs=pl.BlockSpec((tm, U), lambda j, i: (i, j)),
        out_shape=jax.ShapeDtypeStruct((N, n_tiles * U), BF16),
        compiler_params=_cparams(("parallel", "parallel")),
        name=name,
    )(a, w_bf16)


def _attn_kernel(q_ref, kc_ref, kp_ref, vc_ref, vp_ref, o_ref, st_ref, kx_ref, vx_ref, *, QB, H):
    i = pl.program_id(2)
    blk = ATTN_BLOCK
    kx_ref[0:blk, :] = kp_ref[...]
    kx_ref[blk:, :] = kc_ref[...]
    ones = jnp.ones((vx_ref.shape[0], HEAD_DIM), vx_ref.dtype)
    for h in range(H):
        hs = slice(h * HEAD_DIM, (h + 1) * HEAD_DIM)
        vx_ref[0:blk, 2 * h * HEAD_DIM:(2 * h + 1) * HEAD_DIM] = vp_ref[:, hs]
        vx_ref[blk:, 2 * h * HEAD_DIM:(2 * h + 1) * HEAD_DIM] = vc_ref[:, hs]
        vx_ref[:, (2 * h + 1) * HEAD_DIM:(2 * h + 2) * HEAD_DIM] = ones
    rows = lax.broadcasted_iota(I32, (blk, 2 * blk), 0)
    keys = lax.broadcasted_iota(I32, (blk, 2 * blk), 1)
    band = (keys >= rows) & (keys - blk <= rows)
    lane = lax.broadcasted_iota(I32, (blk, LANES), 1)
    dn = (((1,), (1,)), ((), ()))

    def body(qb, carry):
        r0 = pl.multiple_of(qb * blk, blk)
        ok = band & ((keys >= blk) | (i * QB + qb > 0))
        q = [q_ref[pl.ds(r0, blk), h * HEAD_DIM:(h + 1) * HEAD_DIM] for h in range(H)]
        k = [kx_ref[pl.ds(r0, 2 * blk), h * HEAD_DIM:(h + 1) * HEAD_DIM] for h in range(H)]
        s = [jnp.where(ok, lax.dot_general(q[h], k[h], dn, preferred_element_type=F32), NEG_BIG) for h in range(H)]
        m = [jnp.max(s[h], axis=-1, keepdims=True) for h in range(H)]
        p = [jnp.exp(s[h] - m[h]).astype(BF16) for h in range(H)]
        st = jnp.zeros((blk, LANES), F32)
        for h in range(H):
            v1 = vx_ref[pl.ds(r0, 2 * blk), 2 * h * HEAD_DIM:(2 * h + 2) * HEAD_DIM]
            acc = jnp.dot(p[h], v1, preferred_element_type=F32)
            l = acc[:, HEAD_DIM:]
            o_ref[pl.ds(r0, blk), h * HEAD_DIM:(h + 1) * HEAD_DIM] = (acc[:, :HEAD_DIM] / l).astype(o_ref.dtype)
            st = jnp.where(lane == h, m[h], st)
            st = jnp.where(lane == H + h, l, st)
        st_ref[pl.ds(r0, blk), :] = st
        return carry

    lax.fori_loop(0, QB, body, 0)


def dilated_attention(qkv, d):
    _, B, _, L, U = qkv.shape
    H = U // HEAD_DIM
    R = min(512, L)
    QB = R // ATTN_BLOCK

    def cur(kind):
        return pl.BlockSpec((None, None, None, R, U), lambda b, r, i: (kind, b, r, i, 0))

    def prev(kind):
        return pl.BlockSpec((None, None, None, ATTN_BLOCK, U),
                            lambda b, r, i: (kind, b, r, jnp.maximum(i * QB - 1, 0), 0))

    kern = functools.partial(_attn_kernel, QB=QB, H=H)
    return pl.pallas_call(
        kern,
        grid=(B, d, L // R),
        in_specs=[cur(0), cur(1), prev(1), cur(2), prev(2)],
        out_specs=[pl.BlockSpec((None, None, R, U), lambda b, r, i: (b, r, i, 0)),
                   pl.BlockSpec((None, None, R, LANES), lambda b, r, i: (b, r, i, 0))],
        out_shape=[jax.ShapeDtypeStruct((B, d, L, U), BF16),
                   jax.ShapeDtypeStruct((B, d, L, LANES), F32)],
        scratch_shapes=[pltpu.VMEM((R + ATTN_BLOCK, U), BF16), pltpu.VMEM((R + ATTN_BLOCK, 2 * U), BF16)],
        compiler_params=_cparams(("parallel", "parallel", "parallel")),
        name=f"dilated_attn_d{d}",
    )(qkv, qkv, qkv, qkv, qkv)


def _merge_kernel(o0_ref, o1_ref, o2_ref, s0_ref, s1_ref, s2_ref, out_ref, of_ref, sf_ref, *, H, dils):
    for g, (o_ref, s_ref) in enumerate(((o0_ref, s0_ref), (o1_ref, s1_ref), (o2_ref, s2_ref))):
        d = dils[g]
        n = sf_ref.shape[1] // d
        for r in range(d):
            rows = slice(None) if d == 1 else pl.ds(r, n, stride=d)
            sf_ref[g, rows, :] = s_ref[r]
            for h in range(H):
                of_ref[g, h, rows, :] = o_ref[r, :, h * HEAD_DIM:(h + 1) * HEAD_DIM].astype(F32)
    st = [sf_ref[0], sf_ref[1], sf_ref[2]]
    mx = jnp.maximum(jnp.maximum(st[0], st[1]), st[2])
    w = [pltpu.roll(s, LANES - H, 1) * jnp.exp(s - mx) for s in st]
    tot = w[0] + w[1] + w[2]
    coef = [x / tot for x in w]
    for h in range(H):
        hs = slice(h * HEAD_DIM, (h + 1) * HEAD_DIM)
        acc = coef[0][:, h:h + 1] * of_ref[0, h]
        acc += coef[1][:, h:h + 1] * of_ref[1, h]
        acc += coef[2][:, h:h + 1] * of_ref[2, h]
        out_ref[:, hs] = acc.astype(out_ref.dtype)


def merge_groups(outs, stats, dils):
    B, d0, L0, U = outs[0].shape
    S = d0 * L0
    H = U // HEAD_DIM
    tm = min(512, S)
    tiles_per_b = S // tm

    def ospec(d, w):
        return pl.BlockSpec((None, d, tm // d, w), lambda i: (i // tiles_per_b, 0, i % tiles_per_b, 0))

    return pl.pallas_call(
        functools.partial(_merge_kernel, H=H, dils=dils),
        grid=(B * S // tm,),
        in_specs=[ospec(d, U) for d in dils] + [ospec(d, LANES) for d in dils],
        out_specs=pl.BlockSpec((tm, U), lambda i: (i, 0)),
        out_shape=jax.ShapeDtypeStruct((B * S, U), BF16),
        scratch_shapes=[pltpu.VMEM((3, H, tm, HEAD_DIM), F32), pltpu.VMEM((3, tm, LANES), F32)],
        compiler_params=_cparams(("parallel",)),
        name="merge_groups",
    )(*outs, *stats)


def _conv_kernel(a0_ref, a1_ref, b0_ref, b1_ref, ha0_ref, ha1_ref, hb0_ref, hb1_ref,
                 w_ref, bias_ref, g_ref, be_ref, o_ref, u_ref, c_ref, sh_ref, *, ts, U):
    i = pl.program_id(1)
    halo = CONV_HALO
    for half, (a_ref, b_ref, ha_ref, hb_ref) in enumerate(((a0_ref, b0_ref, ha0_ref, hb0_ref),
                                                            (a1_ref, b1_ref, ha1_ref, hb1_ref))):
        cs = slice(half * U, (half + 1) * U)
        u_ref[halo:halo + ts, cs] = a_ref[...].astype(F32) * _sigmoid(b_ref[...].astype(F32))
        hu = ha_ref[...].astype(F32) * _sigmoid(hb_ref[...].astype(F32))
        u_ref[0:halo, cs] = jnp.where(i > 0, hu, 0.0)
    C = 2 * U
    rc = 64
    off = halo - (CONV_WIDTH - 1)

    sub = 8
    n_al = ts + halo - sub

    def chan_body(cc, carry):
        c0 = pl.multiple_of(cc * LANES, LANES)
        sh_ref[0] = u_ref[:, pl.ds(c0, LANES)]
        for b in range(1, sub):
            sh_ref[b, 0:n_al, :] = u_ref[b:b + n_al, pl.ds(c0, LANES)]
        for rb in range(ts // rc):
            acc = jnp.zeros((rc, LANES), F32) + bias_ref[:, pl.ds(c0, LANES)]
            for j in range(CONV_WIDTH):
                a, b = divmod(off + j, sub)
                r0 = rb * rc + a * sub
                acc += w_ref[j:j + 1, pl.ds(c0, LANES)] * sh_ref[b, r0:r0 + rc, :]
            c_ref[rb * rc:(rb + 1) * rc, pl.ds(c0, LANES)] = acc
        return carry

    lax.fori_loop(0, C // LANES, chan_body, 0)

    rn = 16

    def norm_body(rb, carry):
        r0 = pl.multiple_of(rb * rn, rn)
        v = c_ref[pl.ds(r0, rn), :]
        mu = jnp.mean(v, axis=-1, keepdims=True)
        dv = v - mu
        var = jnp.mean(dv * dv, axis=-1, keepdims=True)
        y = dv * lax.rsqrt(var + NORM_EPS) * g_ref[...] + be_ref[...]
        o_ref[pl.ds(r0, rn), :] = (y * _sigmoid(y)).astype(o_ref.dtype)
        return carry

    lax.fori_loop(0, ts // rn, norm_body, 0, unroll=4)


def conv_branch(proj, B, S, U, conv_dw, conv_bias, ln_gain, ln_bias):
    IN = proj.shape[1]
    C = 2 * U
    ts = min(256, S)
    pv = proj.reshape(B, S, IN)
    hb = ts // CONV_HALO
    cur = lambda blk: pl.BlockSpec((None, ts, U), lambda b, i, blk=blk: (b, i, blk))
    prv = lambda blk: pl.BlockSpec((None, CONV_HALO, U), lambda b, i, blk=blk: (b, jnp.maximum(i * hb - 1, 0), blk))
    vec = pl.BlockSpec((1, C), lambda b, i: (0, 0))
    out = pl.pallas_call(
        functools.partial(_conv_kernel, ts=ts, U=U),
        grid=(B, S // ts),
        in_specs=[cur(0), cur(1), cur(2), cur(3), prv(0), prv(1), prv(2), prv(3),
                  pl.BlockSpec((CONV_WIDTH, C), lambda b, i: (0, 0)), vec, vec, vec],
        out_specs=pl.BlockSpec((None, ts, C), lambda b, i: (b, i, 0)),
        out_shape=jax.ShapeDtypeStruct((B, S, C), BF16),
        scratch_shapes=[pltpu.VMEM((ts + CONV_HALO, C), F32), pltpu.VMEM((ts, C), F32),
                        pltpu.VMEM((8, ts + CONV_HALO, LANES), F32)],
        compiler_params=_cparams(("parallel", "parallel")),
        name="conv_branch",
    )(pv, pv, pv, pv, pv, pv, pv, pv, conv_dw, conv_bias.reshape(1, C), ln_gain.reshape(1, C), ln_bias.reshape(1, C))
    return out.reshape(B * S, C)


def _gateproj_kernel(cn_ref, am_ref, wc_ref, wa_ref, gc_ref, ga_ref, o_ref):
    conv = jnp.dot(cn_ref[...], wc_ref[...], preferred_element_type=F32)
    z = _sigmoid(gc_ref[...].astype(F32)) * conv
    attn = jnp.dot(am_ref[...], wa_ref[...], preferred_element_type=F32)
    z += _sigmoid(ga_ref[...].astype(F32)) * attn
    o_ref[...] = z.astype(o_ref.dtype)


def gate_proj(cn, am, wc, wa, proj, U):
    N, C = cn.shape
    D = wc.shape[1]
    tn = U
    tm = min(1024, N)
    return pl.pallas_call(
        _gateproj_kernel,
        grid=(D // tn, N // tm),
        in_specs=[pl.BlockSpec((tm, C), lambda j, i: (i, 0)),
                  pl.BlockSpec((tm, U), lambda j, i: (i, 0)),
                  pl.BlockSpec((C, tn), lambda j, i: (0, j)),
                  pl.BlockSpec((U, tn), lambda j, i: (0, j)),
                  pl.BlockSpec((tm, tn), lambda j, i: (i, 4 + j)),
                  pl.BlockSpec((tm, tn), lambda j, i: (i, 8 + j))],
        out_specs=pl.BlockSpec((tm, tn), lambda j, i: (i, j)),
        out_shape=jax.ShapeDtypeStruct((N, D), BF16),
        compiler_params=_cparams(("parallel", "parallel")),
        name="gate_proj",
    )(cn, am, wc, wa, proj, proj)


def _router_kernel(y_ref, xin_ref, mod_ref, gpost_ref, gain_ref, r_ref,
                   x1_ref, hp_ref, eid_ref, wt_ref, rank_ref, cnt_ref, carry_ref, *, tr):
    step = pl.program_id(0)

    @pl.when(step == 0)
    def _():
        carry_ref[...] = jnp.zeros_like(carry_ref)

    y = y_ref[...].astype(F32)
    yms = jnp.mean(y * y, axis=-1, keepdims=True)
    x = xin_ref[...] + (y * lax.rsqrt(yms + NORM_EPS)) * (mod_ref[2:3, :] * gpost_ref[...])
    x1_ref[...] = x
    D = x.shape[1]
    ms = jnp.mean(x * x, axis=-1, keepdims=True)
    h = (x * lax.rsqrt(ms + NORM_EPS)) * (gain_ref[...] * (1.0 + mod_ref[4:5, :])) + mod_ref[3:4, :]
    lo = h[:, :D // 2]
    hi = h[:, D // 2:]
    hp_ref[...] = _pack_halves(lo, hi)
    logits = (jnp.dot(lo.astype(BF16), r_ref[:D // 2, :], preferred_element_type=F32)
              + jnp.dot(hi.astype(BF16), r_ref[D // 2:, :], preferred_element_type=F32))
    lane = lax.broadcasted_iota(I32, logits.shape, 1)
    G = N_EXPERT_GROUPS
    is_g = lane < G
    gl = jnp.where(is_g, logits, NEG_BIG)
    gmax = jnp.max(gl, axis=-1, keepdims=True)
    grp = jnp.min(jnp.where(gl == gmax, lane, LANES), axis=-1, keepdims=True)
    p_grp = 1.0 / jnp.sum(jnp.where(is_g, jnp.exp(gl - gmax), 0.0), axis=-1, keepdims=True)
    in_grp = (lane >= G) & (lane < G + N_EXPERTS) & (((lane - G) // EXPERTS_PER_GROUP) == grp)
    el = jnp.where(in_grp, logits, NEG_BIG)
    v0 = jnp.max(el, axis=-1, keepdims=True)
    i0 = jnp.min(jnp.where(in_grp & (el == v0), lane, LANES), axis=-1, keepdims=True)
    in2 = in_grp & (lane != i0)
    el2 = jnp.where(in2, logits, NEG_BIG)
    v1 = jnp.max(el2, axis=-1, keepdims=True)
    i1 = jnp.min(jnp.where(in2 & (el2 == v1), lane, LANES), axis=-1, keepdims=True)
    e1 = jnp.exp(v1 - v0)
    w0 = p_grp / (1.0 + e1)
    w1 = p_grp * e1 / (1.0 + e1)
    ex0 = i0 - G
    ex1 = i1 - G
    eid_ref[...] = jnp.where(lane == 0, ex0, jnp.where(lane == 1, ex1, 0))
    wt_ref[...] = jnp.where(lane == 0, w0, jnp.where(lane == 1, w1, 0.0))
    oh0 = (lane == ex0).astype(F32)
    oh1 = (lane == ex1).astype(F32)
    both = oh0 + oh1
    rr = lax.broadcasted_iota(I32, (tr, tr), 0)
    cc = lax.broadcasted_iota(I32, (tr, tr), 1)
    tril = (cc < rr).astype(BF16)
    before = jnp.dot(tril, both.astype(BF16), preferred_element_type=F32) + carry_ref[0:1, :]
    rk0 = jnp.sum(before * oh0, axis=-1, keepdims=True)
    rk1 = jnp.sum(before * oh1, axis=-1, keepdims=True)
    rank_ref[...] = jnp.where(lane == 0, rk0, jnp.where(lane == 1, rk1, 0.0))
    newc = carry_ref[0:1, :] + jnp.sum(both, axis=0, keepdims=True)
    carry_ref[...] = jnp.broadcast_to(newc, carry_ref.shape)
    cnt_ref[...] = jnp.broadcast_to(newc, cnt_ref.shape)


def residual_prenorm_router(y, x2d, mod, gain_post, gain, rcat, S):
    N, D = x2d.shape
    tr = min(256, S)
    tiles_per_b = S // tr
    lane_spec = pl.BlockSpec((tr, LANES), lambda i: (i, 0))
    row_spec = pl.BlockSpec((tr, D), lambda i: (i, 0))
    vec_spec = pl.BlockSpec((1, D), lambda i: (0, 0))
    return pl.pallas_call(
        functools.partial(_router_kernel, tr=tr),
        grid=(N // tr,),
        in_specs=[row_spec, row_spec,
                  pl.BlockSpec((None, 6, D), lambda i: (i // tiles_per_b, 0, 0)),
                  vec_spec, vec_spec,
                  pl.BlockSpec((D, LANES), lambda i: (0, 0))],
        out_specs=[row_spec, pl.BlockSpec((tr, D // 2), lambda i: (i, 0)), lane_spec, lane_spec, lane_spec,
                   pl.BlockSpec((8, LANES), lambda i: (0, 0))],
        out_shape=[jax.ShapeDtypeStruct((N, D), F32),
                   jax.ShapeDtypeStruct((N, D // 2), U32),
                   jax.ShapeDtypeStruct((N, LANES), I32),
                   jax.ShapeDtypeStruct((N, LANES), F32),
                   jax.ShapeDtypeStruct((N, LANES), F32),
                   jax.ShapeDtypeStruct((8, LANES), F32)],
        scratch_shapes=[pltpu.VMEM((8, LANES), F32)],
        compiler_params=_cparams(("arbitrary",)),
        name="residual_prenorm_router",
    )(y, x2d, mod, gain_post.reshape(1, D), gain.reshape(1, D), rcat)


def _dest_kernel(eid_ref, rank_ref, start_ref, o_ref):
    lane = lax.broadcasted_iota(I32, eid_ref.shape, 1)
    eid = eid_ref[...]
    rank = rank_ref[...]
    start = start_ref[0:1, :]
    d = []
    for k in range(2):
        oh = (lane == eid[:, k:k + 1]).astype(F32)
        d.append(jnp.sum(oh * start, axis=-1, keepdims=True) + rank[:, k:k + 1])
    o_ref[...] = jnp.where(lane == 0, d[0], jnp.where(lane == 1, d[1], 0.0)).astype(I32)


def dest_rows(eid, rank, pad_start):
    N = eid.shape[0]
    tr = min(1024, N)
    spec = pl.BlockSpec((tr, LANES), lambda i: (i, 0))
    return pl.pallas_call(
        _dest_kernel,
        grid=(N // tr,),
        in_specs=[spec, spec, pl.BlockSpec((8, LANES), lambda i: (0, 0))],
        out_specs=spec,
        out_shape=jax.ShapeDtypeStruct((N, LANES), I32),
        compiler_params=_cparams(("parallel",)),
        name="dest_rows",
    )(eid, rank, pad_start)


def _scatter_kernel(dest_ref, h_ref, xs_in_ref, xs_ref, sem, *, T):
    del xs_in_ref

    sub = 8

    def issue(tb, carry):
        for r in range(sub):
            t = tb * sub + r
            for k in range(2):
                d = dest_ref[0, 0, 2 * t + k]
                pltpu.make_async_copy(h_ref.at[pl.ds(t, 1)], xs_ref.at[pl.ds(d, 1)], sem).start()
        return carry

    lax.fori_loop(0, T // sub, issue, 0)
    for k in range(2):
        pltpu.make_async_copy(h_ref, xs_ref.at[pl.ds(0, T)], sem).wait()


def scatter_rows(hp, dest, P):
    N, W = hp.shape
    T = min(256, N)
    dest_s = dest[:, :2].reshape(N // T, 1, 2 * T)
    xs0 = jnp.zeros((P, W), hp.dtype)
    return pl.pallas_call(
        functools.partial(_scatter_kernel, T=T),
        grid=(N // T,),
        in_specs=[pl.BlockSpec((1, 1, 2 * T), lambda i: (i, 0, 0), memory_space=pltpu.SMEM),
                  pl.BlockSpec((T, W), lambda i: (i, 0)),
                  pl.BlockSpec(memory_space=pl.ANY)],
        out_specs=pl.BlockSpec(memory_space=pl.ANY),
        out_shape=jax.ShapeDtypeStruct((P, W), hp.dtype),
        scratch_shapes=[pltpu.SemaphoreType.DMA(())],
        input_output_aliases={2: 0},
        compiler_params=_cparams(("arbitrary",)),
        name="scatter_rows",
    )(dest_s, hp, xs0)


def _expert_chunks(st_ref, nc_ref, in_hbm, out_hbm, ibuf, obuf, isem, osem, compute):
    e = pl.program_id(0)
    n_e = pl.num_programs(0)
    n = nc_ref[e]
    TM = ibuf.shape[1]
    base = pl.multiple_of(st_ref[e], TM)
    g0 = base // TM

    def icopy(row, slot):
        return pltpu.make_async_copy(in_hbm.at[pl.ds(row, TM)], ibuf.at[slot], isem.at[slot])

    def ocopy(row, slot):
        return pltpu.make_async_copy(obuf.at[slot], out_hbm.at[pl.ds(row, TM)], osem.at[slot])

    @pl.when((e == 0) & (n > 0))
    def _():
        icopy(base, 0).start()

    def body(c, carry):
        g = g0 + c
        slot = g % 2
        row = pl.multiple_of(base + c * TM, TM)

        @pl.when(c + 1 < n)
        def _():
            icopy(row + TM, 1 - slot).start()

        icopy(row, slot).wait()

        @pl.when(g >= 2)
        def _():
            ocopy(row, slot).wait()

        compute(ibuf.at[slot], obuf.at[slot])
        ocopy(row, slot).start()
        return carry

    lax.fori_loop(0, n, body, 0)

    @pl.when(e + 1 < n_e)
    def _():
        @pl.when(nc_ref[e + 1] > 0)
        def _():
            icopy(pl.multiple_of(st_ref[e + 1], TM), (g0 + n) % 2).start()

    @pl.when(e == n_e - 1)
    def _():
        g_end = g0 + n

        @pl.when(g_end >= 2)
        def _():
            ocopy(base, g_end % 2).wait()

        @pl.when(g_end >= 1)
        def _():
            ocopy(base, (g_end + 1) % 2).wait()

        used = pl.multiple_of(base + n * TM, TM)
        n_tail = (out_hbm.shape[0] - used) // TM
        obuf[0] = jnp.zeros(obuf.shape[1:], obuf.dtype)

        def zstart(c, carry):
            ocopy(pl.multiple_of(used + c * TM, TM), 0).start()
            return carry

        def zwait(c, carry):
            ocopy(used, 0).wait()
            return carry

        lax.fori_loop(0, n_tail, zstart, 0)
        lax.fori_loop(0, n_tail, zwait, 0)


WEIGHT_DMA_QUEUE = 1


def _stream_expert_weights(parts, wsem):
    e = pl.program_id(0)
    n_e = pl.num_programs(0)
    slot = e % 2

    def copy(k, ei, s):
        src, dst = parts[k]
        return pltpu.make_async_copy(src(ei), dst(s), wsem.at[k, s])

    @pl.when(e == 0)
    def _():
        for k in range(len(parts)):
            copy(k, 0, 0).start(priority=WEIGHT_DMA_QUEUE)

    @pl.when(e + 1 < n_e)
    def _():
        for k in range(len(parts)):
            copy(k, e + 1, 1 - slot).start(priority=WEIGHT_DMA_QUEUE)

    for k in range(len(parts)):
        copy(k, e, slot).wait()
    return slot


def _moe_up_kernel(st_ref, nc_ref, xs_hbm, w1_hbm, w3_hbm, o_hbm, wb_ref, w1_buf, w3_buf, ibuf, obuf,
                   wsem, isem, osem, *, F):
    slot = _stream_expert_weights([(lambda ei: w1_hbm.at[ei], lambda s: w1_buf.at[s]),
                                   (lambda ei: w3_hbm.at[ei], lambda s: w3_buf.at[s])], wsem)

    @pl.when(nc_ref[pl.program_id(0)] > 0)
    def _():
        wb_ref[:, :F] = w1_buf[slot].astype(BF16)
        wb_ref[:, F:] = w3_buf[slot].astype(BF16)

    def compute(x_ref, o_ref):
        lo, hi = _unpack_halves(x_ref[...])
        half = wb_ref.shape[0] // 2
        hcat = (jnp.dot(lo.astype(BF16), wb_ref[:half, :], preferred_element_type=F32)
                + jnp.dot(hi.astype(BF16), wb_ref[half:, :], preferred_element_type=F32))
        a = hcat[:, :F]
        o_ref[...] = (a * _sigmoid(a) * hcat[:, F:]).astype(o_ref.dtype)

    _expert_chunks(st_ref, nc_ref, xs_hbm, o_hbm, ibuf, obuf, isem, osem, compute)


def _moe_down_kernel(st_ref, nc_ref, h_hbm, w2_hbm, o_hbm, wb_ref, w2_buf, ibuf, obuf, wsem, isem, osem):
    slot = _stream_expert_weights([(lambda ei: w2_hbm.at[ei], lambda s: w2_buf.at[s])], wsem)

    @pl.when(nc_ref[pl.program_id(0)] > 0)
    def _():
        wb_ref[...] = w2_buf[slot].astype(BF16)

    def compute(h_ref, o_ref):
        y = jnp.dot(h_ref[...], wb_ref[...], preferred_element_type=F32)
        half = y.shape[1] // 2
        o_ref[...] = _pack_halves(y[:, :half], y[:, half:])

    _expert_chunks(st_ref, nc_ref, h_hbm, o_hbm, ibuf, obuf, isem, osem, compute)


def expert_ffn(xs, seg_start, seg_chunks, w1, w3, w2):
    P, Wp = xs.shape
    E, D, F = w1.shape
    TM = MOE_ROWS
    any_spec = pl.BlockSpec(memory_space=pl.ANY)
    dma2 = pltpu.SemaphoreType.DMA((2,))
    hmid = pl.pallas_call(
        functools.partial(_moe_up_kernel, F=F),
        grid_spec=pltpu.PrefetchScalarGridSpec(
            num_scalar_prefetch=2,
            grid=(E,),
            in_specs=[any_spec, any_spec, any_spec],
            out_specs=any_spec,
            scratch_shapes=[pltpu.VMEM((D, 2 * F), BF16), pltpu.VMEM((2, D, F), F32), pltpu.VMEM((2, D, F), F32),
                            pltpu.VMEM((2, TM, Wp), U32), pltpu.VMEM((2, TM, F), BF16),
                            pltpu.SemaphoreType.DMA((2, 2)), dma2, dma2]),
        out_shape=jax.ShapeDtypeStruct((P, F), BF16),
        compiler_params=_cparams(("arbitrary",)),
        name="moe_up",
    )(seg_start, seg_chunks, xs, w1, w3)
    return pl.pallas_call(
        _moe_down_kernel,
        grid_spec=pltpu.PrefetchScalarGridSpec(
            num_scalar_prefetch=2,
            grid=(E,),
            in_specs=[any_spec, any_spec],
            out_specs=any_spec,
            scratch_shapes=[pltpu.VMEM((F, D), BF16), pltpu.VMEM((2, F, D), F32), pltpu.VMEM((2, TM, F), BF16),
                            pltpu.VMEM((2, TM, D // 2), U32), pltpu.SemaphoreType.DMA((1, 2)), dma2, dma2]),
        out_shape=jax.ShapeDtypeStruct((P, D // 2), U32),
        compiler_params=_cparams(("arbitrary",)),
        name="moe_down",
    )(seg_start, seg_chunks, hmid, w2)


def _combine_kernel(pos_ref, nxt_ref, wt_ref, x_ref, mod_ref, gain_ref, y_hbm, o_ref, buf_ref, sem, *, T):
    i = pl.program_id(0)
    n = pl.num_programs(0)
    slot = i % 2

    def gather(p_ref, s):
        sub = 8

        def issue(tb, carry):
            for r in range(sub):
                t = tb * sub + r
                for k in range(2):
                    p = p_ref[0, 0, 2 * t + k]
                    pltpu.make_async_copy(y_hbm.at[pl.ds(p, 1)], buf_ref.at[s, k, pl.ds(t, 1)],
                                          sem.at[s]).start()
            return carry

        lax.fori_loop(0, T // sub, issue, 0)

    @pl.when(i == 0)
    def _():
        gather(pos_ref, 0)

    @pl.when(i + 1 < n)
    def _():
        gather(nxt_ref, 1 - slot)

    for k in range(2):
        pltpu.make_async_copy(y_hbm.at[pl.ds(0, T)], buf_ref.at[slot, k], sem.at[slot]).wait()

    wt = wt_ref[...]
    w0 = wt[:, 0:1]
    w1 = wt[:, 1:2]
    lo0, hi0 = _unpack_halves(buf_ref[slot, 0])
    lo1, hi1 = _unpack_halves(buf_ref[slot, 1])
    ylo = w0 * lo0 + w1 * lo1
    yhi = w0 * hi0 + w1 * hi1
    D = x_ref.shape[1]
    half = D // 2
    ms = (jnp.sum(ylo * ylo, axis=-1, keepdims=True) + jnp.sum(yhi * yhi, axis=-1, keepdims=True)) / D
    inv = lax.rsqrt(ms + NORM_EPS)
    o_ref[:, :half] = x_ref[:, :half] + mod_ref[5:6, :half] * (ylo * inv * gain_ref[:, :half])
    o_ref[:, half:] = x_ref[:, half:] + mod_ref[5:6, half:] * (yhi * inv * gain_ref[:, half:])


def combine(yp, dest, wts, x1, mod, gain, S):
    N, D = x1.shape
    T = min(128, S)
    tiles_per_b = S // T
    pos_s = dest[:, :2].reshape(N // T, 1, 2 * T)
    n_tiles = N // T
    return pl.pallas_call(
        functools.partial(_combine_kernel, T=T),
        grid=(n_tiles,),
        in_specs=[pl.BlockSpec((1, 1, 2 * T), lambda i: (i, 0, 0), memory_space=pltpu.SMEM),
                  pl.BlockSpec((1, 1, 2 * T), lambda i: (jnp.minimum(i + 1, n_tiles - 1), 0, 0),
                               memory_space=pltpu.SMEM),
                  pl.BlockSpec((T, LANES), lambda i: (i, 0)),
                  pl.BlockSpec((T, D), lambda i: (i, 0)),
                  pl.BlockSpec((None, 6, D), lambda i: (i // tiles_per_b, 0, 0)),
                  pl.BlockSpec((1, D), lambda i: (0, 0)),
                  pl.BlockSpec(memory_space=pl.ANY)],
        out_specs=pl.BlockSpec((T, D), lambda i: (i, 0)),
        out_shape=jax.ShapeDtypeStruct((N, D), F32),
        scratch_shapes=[pltpu.VMEM((2, 2, T, D // 2), U32), pltpu.SemaphoreType.DMA((2,))],
        compiler_params=_cparams(("arbitrary",)),
        name="combine",
    )(pos_s, pos_s, wts, x1, mod, gain.reshape(1, D), yp)


def _moe_layout(counts):
    TM = MOE_ROWS
    chunks = (counts.astype(I32) + TM - 1) // TM
    start = (jnp.cumsum(chunks) - chunks) * TM
    return start, chunks


def kernel(x, c, positions, ada_w, ada_b, mix_norm_pre, mix_norm_post, w_in, conv_dw, conv_dw_bias, conv_ln_gain, conv_ln_bias, w_conv_out, w_attn_out, w_out, ffn_norm_pre, ffn_norm_post, router_group, router_expert, expert_w1, expert_w3, expert_w2):
    B, S, D = x.shape
    N = B * S
    U = D // 4
    depth = ada_w.shape[0]
    xc = x.reshape(N, D)
    for layer in range(depth):
        mod = ada_mod(c, ada_w[layer], ada_b[layer]).reshape(B, 6, D)
        h, cosf, sinf = prenorm_rope(xc, mod, mix_norm_pre[layer], positions, S)
        w_in_b = w_in[layer].astype(BF16)
        outs, stats = [], []
        for g, (window, d) in enumerate(ATTN_PATTERNS):
            assert window // d == ATTN_BLOCK and S % (d * ATTN_BLOCK) == 0
            qkv = qkv_proj(h, w_in_b, cosf, sinf, B, S, U, g, d)
            o, st = dilated_attention(qkv, d)
            outs.append(o)
            stats.append(st)
        am = merge_groups(outs, stats, tuple(d for _, d in ATTN_PATTERNS))
        proj = matmul_cols(h, w_in_b, U, 9, "rest_proj")
        cn = conv_branch(proj, B, S, U, conv_dw[layer], conv_dw_bias[layer], conv_ln_gain[layer], conv_ln_bias[layer])
        z = gate_proj(cn, am, w_conv_out[layer].astype(BF16), w_attn_out[layer].astype(BF16), proj, U)
        y = matmul_cols(z, w_out[layer].astype(BF16), U, 0, "out_proj")
        rcat = jnp.zeros((D, LANES), F32)
        rcat = rcat.at[:, :N_EXPERT_GROUPS].set(router_group[layer])
        rcat = rcat.at[:, N_EXPERT_GROUPS:N_EXPERT_GROUPS + N_EXPERTS].set(router_expert[layer]).astype(BF16)
        x1, hp, eid, wts, rank, cnt = residual_prenorm_router(
            y, xc, mod, mix_norm_post[layer], ffn_norm_pre[layer], rcat, S)
        TM = MOE_ROWS
        P = (2 * N + N_EXPERTS * (TM - 1)) // TM * TM
        seg_start, seg_chunks = _moe_layout(cnt[0, :N_EXPERTS])
        start_row = jnp.zeros((8, LANES), F32).at[:, :N_EXPERTS].set(seg_start.astype(F32)[None, :])
        dest = dest_rows(eid, rank, start_row)
        xs = scatter_rows(hp, dest, P)
        yp = expert_ffn(xs, seg_start, seg_chunks, expert_w1[layer], expert_w3[layer], expert_w2[layer])
        xc = combine(yp, dest, wts, x1, mod, ffn_norm_post[layer], S)
    return xc.reshape(B, S, D)
```

```python
import functools

import jax
import jax.numpy as jnp
from jax import lax
from jax.experimental import pallas as pl
from jax.experimental.pallas import tpu as pltpu

F32 = jnp.float32
BF16 = jnp.bfloat16
I32 = jnp.int32
U32 = jnp.uint32

HEAD_DIM = 128
LANES = 128
ATTN_BLOCK = 128
ATTN_PATTERNS = ((128, 1), (512, 4), (2048, 16))
ROPE_THETA = 10000.0
CONV_WIDTH = 31
CONV_HALO = 32
N_EXPERT_GROUPS = 8
EXPERTS_PER_GROUP = 8
N_EXPERTS = 64
NORM_EPS = 1e-6
NEG_BIG = -1e30
MOE_ROWS = 256
V7X_VMEM_LIMIT = 60 * 1024 * 1024


def _cparams(sem):
    return pltpu.CompilerParams(dimension_semantics=sem, vmem_limit_bytes=V7X_VMEM_LIMIT)


def _sigmoid(x):
    return 1.0 / (1.0 + jnp.exp(-x))


def _pack_halves(lo, hi):
    lo_b = lax.bitcast_convert_type(lo.astype(BF16).astype(F32), U32) >> 16
    hi_b = lax.bitcast_convert_type(hi.astype(BF16).astype(F32), U32) & jnp.uint32(0xFFFF0000)
    return hi_b | lo_b


def _unpack_halves(w):
    lo = lax.bitcast_convert_type(w << 16, F32)
    hi = lax.bitcast_convert_type(w & jnp.uint32(0xFFFF0000), F32)
    return lo, hi


def _ada_kernel(c_ref, w_ref, b_ref, o_ref):
    c = c_ref[...]
    cact = (c * _sigmoid(c)).astype(BF16)
    o_ref[...] = jnp.dot(cact, w_ref[...].astype(BF16), preferred_element_type=F32) + b_ref[...]


def ada_mod(c, ada_w, ada_b):
    B, D = c.shape
    W = ada_w.shape[1]
    rows = 8
    cp = jnp.zeros((rows, D), F32).at[:B].set(c)
    tn = min(1024, W)
    out = pl.pallas_call(
        _ada_kernel,
        grid=(W // tn,),
        in_specs=[pl.BlockSpec((rows, D), lambda j: (0, 0)),
                  pl.BlockSpec((D, tn), lambda j: (0, j)),
                  pl.BlockSpec((1, tn), lambda j: (0, j))],
        out_specs=pl.BlockSpec((rows, tn), lambda j: (0, j)),
        out_shape=jax.ShapeDtypeStruct((rows, W), F32),
        compiler_params=_cparams(("parallel",)),
        name="ada_mod",
    )(cp, ada_w, ada_b.reshape(1, W))
    return out[:B]


def _prenorm_kernel(x_ref, mod_ref, gain_ref, pos_ref, invf_ref, h_ref, cos_ref, sin_ref):
    x = x_ref[...]
    ms = jnp.mean(x * x, axis=-1, keepdims=True)
    g = gain_ref[...] * (1.0 + mod_ref[1:2, :])
    h_ref[...] = ((x * lax.rsqrt(ms + NORM_EPS)) * g + mod_ref[0:1, :]).astype(BF16)
    ang = pos_ref[...].astype(F32) * invf_ref[...]
    lane = lax.broadcasted_iota(I32, ang.shape, 1)
    sn = jnp.sin(ang)
    cos_ref[...] = jnp.cos(ang)
    sin_ref[...] = jnp.where(lane < HEAD_DIM // 2, -sn, sn)


def prenorm_rope(x2d, mod, gain, positions, S):
    N, D = x2d.shape
    tr = min(256, S)
    tiles_per_b = S // tr
    inv = ROPE_THETA ** (-jnp.arange(0, HEAD_DIM, 2, dtype=F32) / HEAD_DIM)
    invf = jnp.concatenate([inv, inv]).reshape(1, HEAD_DIM)
    return pl.pallas_call(
        _prenorm_kernel,
        grid=(N // tr,),
        in_specs=[pl.BlockSpec((tr, D), lambda i: (i, 0)),
                  pl.BlockSpec((None, 6, D), lambda i: (i // tiles_per_b, 0, 0)),
                  pl.BlockSpec((1, D), lambda i: (0, 0)),
                  pl.BlockSpec((tr, 1), lambda i: (i, 0)),
                  pl.BlockSpec((1, HEAD_DIM), lambda i: (0, 0))],
        out_specs=[pl.BlockSpec((tr, D), lambda i: (i, 0)),
                   pl.BlockSpec((tr, HEAD_DIM), lambda i: (i, 0)),
                   pl.BlockSpec((tr, HEAD_DIM), lambda i: (i, 0))],
        out_shape=[jax.ShapeDtypeStruct((N, D), BF16),
                   jax.ShapeDtypeStruct((N, HEAD_DIM), F32),
                   jax.ShapeDtypeStruct((N, HEAD_DIM), F32)],
        compiler_params=_cparams(("parallel",)),
        name="prenorm_rope",
    )(x2d, mod, gain.reshape(1, D), positions.reshape(N, 1), invf)


def _residue_rows(ref, r, d):
    if d == 1:
        return ref[...]
    return ref[pl.ds(r, ref.shape[0] // d, stride=d), :]


def _qkv_kernel(a_ref, w_ref, cos_ref, sin_ref, o_ref, acc_ref, *, d, heads, scale, n_i):
    s = pl.program_id(0)

    @pl.when(s == 0)
    def _():
        acc_ref[...] = jnp.zeros_like(acc_ref)

    kind = jnp.maximum(s - 1, 0) // n_i
    sc = jnp.where(kind == 0, scale, 1.0).astype(F32)
    is_v = kind == 2
    for r in range(d):
        c = jnp.where(is_v, 1.0, _residue_rows(cos_ref, r, d) * sc)
        sn = jnp.where(is_v, 0.0, _residue_rows(sin_ref, r, d) * sc)
        for h in range(heads):
            t = _residue_rows(acc_ref.at[h], r, d)
            o_ref[r, :, h * HEAD_DIM:(h + 1) * HEAD_DIM] = (
                t * c + pltpu.roll(t, HEAD_DIM // 2, 1) * sn).astype(o_ref.dtype)
    acc = jnp.dot(a_ref[...], w_ref[...], preferred_element_type=F32)
    for h in range(heads):
        acc_ref[h] = acc[:, h * HEAD_DIM:(h + 1) * HEAD_DIM]


def qkv_proj(h, w_bf16, cosf, sinf, B, S, U, g, d):
    N, D = h.shape
    tm = min(1024, S)
    tiles_per_b = S // tm
    n_i = N // tm
    last = 3 * n_i - 1
    kern = functools.partial(_qkv_kernel, d=d, heads=U // HEAD_DIM, scale=HEAD_DIM ** -0.5, n_i=n_i)

    def cur(s):
        return jnp.minimum(s, last)

    def fin(s):
        return jnp.maximum(s - 1, 0)

    return pl.pallas_call(
        kern,
        grid=(3 * n_i + 1,),
        in_specs=[pl.BlockSpec((tm, D), lambda s: (cur(s) % n_i, 0)),
                  pl.BlockSpec((D, U), lambda s: (0, (cur(s) // n_i) * 3 + g)),
                  pl.BlockSpec((tm, HEAD_DIM), lambda s: (fin(s) % n_i, 0)),
                  pl.BlockSpec((tm, HEAD_DIM), lambda s: (fin(s) % n_i, 0))],
        out_specs=pl.BlockSpec((None, None, d, tm // d, U),
                               lambda s: (fin(s) // n_i, (fin(s) % n_i) // tiles_per_b, 0,
                                          (fin(s) % n_i) % tiles_per_b, 0)),
        out_shape=jax.ShapeDtypeStruct((3, B, d, S // d, U), BF16),
        scratch_shapes=[pltpu.VMEM((U // HEAD_DIM, tm, HEAD_DIM), F32)],
        compiler_params=_cparams(("arbitrary",)),
        name=f"qkv_proj_d{d}",
    )(h, w_bf16, cosf, sinf)


def _matmul_kernel(a_ref, w_ref, o_ref):
    o_ref[...] = jnp.dot(a_ref[...], w_ref[...], preferred_element_type=F32).astype(o_ref.dtype)


def matmul_cols(a, w_bf16, U, first_tile, name):
    N, K = a.shape
    n_tiles = w_bf16.shape[1] // U - first_tile
    tm = min(1024, N)
    return pl.pallas_call(
        _matmul_kernel,
        grid=(n_tiles, N // tm),
        in_specs=[pl.BlockSpec((tm, K), lambda j, i: (i, 0)),
                  pl.BlockSpec((K, U), lambda j, i: (0, first_tile + j))],
        out_specs=pl.BlockSpec((tm, U), lambda j, i: (i, j)),
        out_shape=jax.ShapeDtypeStruct((N, n_tiles * U), BF16),
        compiler_params=_cparams(("parallel", "parallel")),
        name=name,
    )(a, w_bf16)


def _attn_kernel(q_ref, kc_ref, kp_ref, vc_ref, vp_ref, o_ref, st_ref, kx_ref, vx_ref, *, QB, H):
    i = pl.program_id(2)
    blk = ATTN_BLOCK
    kx_ref[0:blk, :] = kp_ref[...]
    kx_ref[blk:, :] = kc_ref[...]
    ones = jnp.ones((vx_ref.shape[0], HEAD_DIM), vx_ref.dtype)
    for h in range(H):
        hs = slice(h * HEAD_DIM, (h + 1) * HEAD_DIM)
        vx_ref[0:blk, 2 * h * HEAD_DIM:(2 * h + 1) * HEAD_DIM] = vp_ref[:, hs]
        vx_ref[blk:, 2 * h * HEAD_DIM:(2 * h + 1) * HEAD_DIM] = vc_ref[:, hs]
        vx_ref[:, (2 * h + 1) * HEAD_DIM:(2 * h + 2) * HEAD_DIM] = ones
    rows = lax.broadcasted_iota(I32, (blk, 2 * blk), 0)
    keys = lax.broadcasted_iota(I32, (blk, 2 * blk), 1)
    band = (keys >= rows) & (keys - blk <= rows)
    lane = lax.broadcasted_iota(I32, (blk, LANES), 1)
    dn = (((1,), (1,)), ((), ()))

    def body(qb, carry):
        r0 = pl.multiple_of(qb * blk, blk)
        ok = band & ((keys >= blk) | (i * QB + qb > 0))
        q = [q_ref[pl.ds(r0, blk), h * HEAD_DIM:(h + 1) * HEAD_DIM] for h in range(H)]
        k = [kx_ref[pl.ds(r0, 2 * blk), h * HEAD_DIM:(h + 1) * HEAD_DIM] for h in range(H)]
        s = [jnp.where(ok, lax.dot_general(q[h], k[h], dn, preferred_element_type=F32), NEG_BIG) for h in range(H)]
        m = [jnp.max(s[h], axis=-1, keepdims=True) for h in range(H)]
        p = [jnp.exp(s[h] - m[h]).astype(BF16) for h in range(H)]
        st = jnp.zeros((blk, LANES), F32)
        for h in range(H):
            v1 = vx_ref[pl.ds(r0, 2 * blk), 2 * h * HEAD_DIM:(2 * h + 2) * HEAD_DIM]
            acc = jnp.dot(p[h], v1, preferred_element_type=F32)
            l = acc[:, HEAD_DIM:]
            o_ref[pl.ds(r0, blk), h * HEAD_DIM:(h + 1) * HEAD_DIM] = (acc[:, :HEAD_DIM] / l).astype(o_ref.dtype)
            st = jnp.where(lane == h, m[h], st)
            st = jnp.where(lane == H + h, l, st)
        st_ref[pl.ds(r0, blk), :] = st
        return carry

    lax.fori_loop(0, QB, body, 0)


def dilated_attention(qkv, d):
    _, B, _, L, U = qkv.shape
    H = U // HEAD_DIM
    R = min(512, L)
    QB = R // ATTN_BLOCK

    def cur(kind):
        return pl.BlockSpec((None, None, None, R, U), lambda b, r, i: (kind, b, r, i, 0))

    def prev(kind):
        return pl.BlockSpec((None, None, None, ATTN_BLOCK, U),
                            lambda b, r, i: (kind, b, r, jnp.maximum(i * QB - 1, 0), 0))

    kern = functools.partial(_attn_kernel, QB=QB, H=H)
    return pl.pallas_call(
        kern,
        grid=(B, d, L // R),
        in_specs=[cur(0), cur(1), prev(1), cur(2), prev(2)],
        out_specs=[pl.BlockSpec((None, None, R, U), lambda b, r, i: (b, r, i, 0)),
                   pl.BlockSpec((None, None, R, LANES), lambda b, r, i: (b, r, i, 0))],
        out_shape=[jax.ShapeDtypeStruct((B, d, L, U), BF16),
                   jax.ShapeDtypeStruct((B, d, L, LANES), F32)],
        scratch_shapes=[pltpu.VMEM((R + ATTN_BLOCK, U), BF16), pltpu.VMEM((R + ATTN_BLOCK, 2 * U), BF16)],
        compiler_params=_cparams(("parallel", "parallel", "parallel")),
        name=f"dilated_attn_d{d}",
    )(qkv, qkv, qkv, qkv, qkv)


def _merge_kernel(o0_ref, o1_ref, o2_ref, s0_ref, s1_ref, s2_ref, out_ref, of_ref, sf_ref, *, H, dils):
    for g, (o_ref, s_ref) in enumerate(((o0_ref, s0_ref), (o1_ref, s1_ref), (o2_ref, s2_ref))):
        d = dils[g]
        n = sf_ref.shape[1] // d
        for r in range(d):
            rows = slice(None) if d == 1 else pl.ds(r, n, stride=d)
            sf_ref[g, rows, :] = s_ref[r]
            for h in range(H):
                of_ref[g, h, rows, :] = o_ref[r, :, h * HEAD_DIM:(h + 1) * HEAD_DIM].astype(F32)
    st = [sf_ref[0], sf_ref[1], sf_ref[2]]
    mx = jnp.maximum(jnp.maximum(st[0], st[1]), st[2])
    w = [pltpu.roll(s, LANES - H, 1) * jnp.exp(s - mx) for s in st]
    tot = w[0] + w[1] + w[2]
    coef = [x / tot for x in w]
    for h in range(H):
        hs = slice(h * HEAD_DIM, (h + 1) * HEAD_DIM)
        acc = coef[0][:, h:h + 1] * of_ref[0, h]
        acc += coef[1][:, h:h + 1] * of_ref[1, h]
        acc += coef[2][:, h:h + 1] * of_ref[2, h]
        out_ref[:, hs] = acc.astype(out_ref.dtype)


def merge_groups(outs, stats, dils):
    B, d0, L0, U = outs[0].shape
    S = d0 * L0
    H = U // HEAD_DIM
    tm = min(512, S)
    tiles_per_b = S // tm

    def ospec(d, w):
        return pl.BlockSpec((None, d, tm // d, w), lambda i: (i // tiles_per_b, 0, i % tiles_per_b, 0))

    return pl.pallas_call(
        functools.partial(_merge_kernel, H=H, dils=dils),
        grid=(B * S // tm,),
        in_specs=[ospec(d, U) for d in dils] + [ospec(d, LANES) for d in dils],
        out_specs=pl.BlockSpec((tm, U), lambda i: (i, 0)),
        out_shape=jax.ShapeDtypeStruct((B * S, U), BF16),
        scratch_shapes=[pltpu.VMEM((3, H, tm, HEAD_DIM), F32), pltpu.VMEM((3, tm, LANES), F32)],
        compiler_params=_cparams(("parallel",)),
        name="merge_groups",
    )(*outs, *stats)


def _conv_kernel(a0_ref, a1_ref, b0_ref, b1_ref, ha0_ref, ha1_ref, hb0_ref, hb1_ref,
                 w_ref, bias_ref, g_ref, be_ref, o_ref, u_ref, c_ref, sh_ref, *, ts, U):
    i = pl.program_id(1)
    halo = CONV_HALO
    for half, (a_ref, b_ref, ha_ref, hb_ref) in enumerate(((a0_ref, b0_ref, ha0_ref, hb0_ref),
                                                            (a1_ref, b1_ref, ha1_ref, hb1_ref))):
        cs = slice(half * U, (half + 1) * U)
        u_ref[halo:halo + ts, cs] = a_ref[...].astype(F32) * _sigmoid(b_ref[...].astype(F32))
        hu = ha_ref[...].astype(F32) * _sigmoid(hb_ref[...].astype(F32))
        u_ref[0:halo, cs] = jnp.where(i > 0, hu, 0.0)
    C = 2 * U
    rc = 64
    off = halo - (CONV_WIDTH - 1)

    sub = 8
    n_al = ts + halo - sub

    def chan_body(cc, carry):
        c0 = pl.multiple_of(cc * LANES, LANES)
        sh_ref[0] = u_ref[:, pl.ds(c0, LANES)]
        for b in range(1, sub):
            sh_ref[b, 0:n_al, :] = u_ref[b:b + n_al, pl.ds(c0, LANES)]
        for rb in range(ts // rc):
            acc = jnp.zeros((rc, LANES), F32) + bias_ref[:, pl.ds(c0, LANES)]
            for j in range(CONV_WIDTH):
                a, b = divmod(off + j, sub)
                r0 = rb * rc + a * sub
                acc += w_ref[j:j + 1, pl.ds(c0, LANES)] * sh_ref[b, r0:r0 + rc, :]
            c_ref[rb * rc:(rb + 1) * rc, pl.ds(c0, LANES)] = acc
        return carry

    lax.fori_loop(0, C // LANES, chan_body, 0)

    rn = 16

    def norm_body(rb, carry):
        r0 = pl.multiple_of(rb * rn, rn)
        v = c_ref[pl.ds(r0, rn), :]
        mu = jnp.mean(v, axis=-1, keepdims=True)
        dv = v - mu
        var = jnp.mean(dv * dv, axis=-1, keepdims=True)
        y = dv * lax.rsqrt(var + NORM_EPS) * g_ref[...] + be_ref[...]
        o_ref[pl.ds(r0, rn), :] = (y * _sigmoid(y)).astype(o_ref.dtype)
        return carry

    lax.fori_loop(0, ts // rn, norm_body, 0, unroll=4)


def conv_branch(proj, B, S, U, conv_dw, conv_bias, ln_gain, ln_bias):
    IN = proj.shape[1]
    C = 2 * U
    ts = min(256, S)
    pv = proj.reshape(B, S, IN)
    hb = ts // CONV_HALO
    cur = lambda blk: pl.BlockSpec((None, ts, U), lambda b, i, blk=blk: (b, i, blk))
    prv = lambda blk: pl.BlockSpec((None, CONV_HALO, U), lambda b, i, blk=blk: (b, jnp.maximum(i * hb - 1, 0), blk))
    vec = pl.BlockSpec((1, C), lambda b, i: (0, 0))
    out = pl.pallas_call(
        functools.partial(_conv_kernel, ts=ts, U=U),
        grid=(B, S // ts),
        in_specs=[cur(0), cur(1), cur(2), cur(3), prv(0), prv(1), prv(2), prv(3),
                  pl.BlockSpec((CONV_WIDTH, C), lambda b, i: (0, 0)), vec, vec, vec],
        out_specs=pl.BlockSpec((None, ts, C), lambda b, i: (b, i, 0)),
        out_shape=jax.ShapeDtypeStruct((B, S, C), BF16),
        scratch_shapes=[pltpu.VMEM((ts + CONV_HALO, C), F32), pltpu.VMEM((ts, C), F32),
                        pltpu.VMEM((8, ts + CONV_HALO, LANES), F32)],
        compiler_params=_cparams(("parallel", "parallel")),
        name="conv_branch",
    )(pv, pv, pv, pv, pv, pv, pv, pv, conv_dw, conv_bias.reshape(1, C), ln_gain.reshape(1, C), ln_bias.reshape(1, C))
    return out.reshape(B * S, C)


def _gateproj_kernel(cn_ref, am_ref, wc_ref, wa_ref, gc_ref, ga_ref, o_ref):
    conv = jnp.dot(cn_ref[...], wc_ref[...], preferred_element_type=F32)
    z = _sigmoid(gc_ref[...].astype(F32)) * conv
    attn = jnp.dot(am_ref[...], wa_ref[...], preferred_element_type=F32)
    z += _sigmoid(ga_ref[...].astype(F32)) * attn
    o_ref[...] = z.astype(o_ref.dtype)


def gate_proj(cn, am, wc, wa, proj, U):
    N, C = cn.shape
    D = wc.shape[1]
    tn = U
    tm = min(1024, N)
    return pl.pallas_call(
        _gateproj_kernel,
        grid=(D // tn, N // tm),
        in_specs=[pl.BlockSpec((tm, C), lambda j, i: (i, 0)),
                  pl.BlockSpec((tm, U), lambda j, i: (i, 0)),
                  pl.BlockSpec((C, tn), lambda j, i: (0, j)),
                  pl.BlockSpec((U, tn), lambda j, i: (0, j)),
                  pl.BlockSpec((tm, tn), lambda j, i: (i, 4 + j)),
                  pl.BlockSpec((tm, tn), lambda j, i: (i, 8 + j))],
        out_specs=pl.BlockSpec((tm, tn), lambda j, i: (i, j)),
        out_shape=jax.ShapeDtypeStruct((N, D), BF16),
        compiler_params=_cparams(("parallel", "parallel")),
        name="gate_proj",
    )(cn, am, wc, wa, proj, proj)


def _router_kernel(y_ref, xin_ref, mod_ref, gpost_ref, gain_ref, r_ref,
                   x1_ref, hp_ref, eid_ref, wt_ref, rank_ref, cnt_ref, carry_ref, *, tr):
    step = pl.program_id(0)

    @pl.when(step == 0)
    def _():
        carry_ref[...] = jnp.zeros_like(carry_ref)

    y = y_ref[...].astype(F32)
    yms = jnp.mean(y * y, axis=-1, keepdims=True)
    x = xin_ref[...] + (y * lax.rsqrt(yms + NORM_EPS)) * (mod_ref[2:3, :] * gpost_ref[...])
    x1_ref[...] = x
    D = x.shape[1]
    ms = jnp.mean(x * x, axis=-1, keepdims=True)
    h = (x * lax.rsqrt(ms + NORM_EPS)) * (gain_ref[...] * (1.0 + mod_ref[4:5, :])) + mod_ref[3:4, :]
    lo = h[:, :D // 2]
    hi = h[:, D // 2:]
    hp_ref[...] = _pack_halves(lo, hi)
    logits = (jnp.dot(lo.astype(BF16), r_ref[:D // 2, :], preferred_element_type=F32)
              + jnp.dot(hi.astype(BF16), r_ref[D // 2:, :], preferred_element_type=F32))
    lane = lax.broadcasted_iota(I32, logits.shape, 1)
    G = N_EXPERT_GROUPS
    is_g = lane < G
    gl = jnp.where(is_g, logits, NEG_BIG)
    gmax = jnp.max(gl, axis=-1, keepdims=True)
    grp = jnp.min(jnp.where(gl == gmax, lane, LANES), axis=-1, keepdims=True)
    p_grp = 1.0 / jnp.sum(jnp.where(is_g, jnp.exp(gl - gmax), 0.0), axis=-1, keepdims=True)
    in_grp = (lane >= G) & (lane < G + N_EXPERTS) & (((lane - G) // EXPERTS_PER_GROUP) == grp)
    el = jnp.where(in_grp, logits, NEG_BIG)
    v0 = jnp.max(el, axis=-1, keepdims=True)
    i0 = jnp.min(jnp.where(in_grp & (el == v0), lane, LANES), axis=-1, keepdims=True)
    in2 = in_grp & (lane != i0)
    el2 = jnp.where(in2, logits, NEG_BIG)
    v1 = jnp.max(el2, axis=-1, keepdims=True)
    i1 = jnp.min(jnp.where(in2 & (el2 == v1), lane, LANES), axis=-1, keepdims=True)
    e1 = jnp.exp(v1 - v0)
    w0 = p_grp / (1.0 + e1)
    w1 = p_grp * e1 / (1.0 + e1)
    ex0 = i0 - G
    ex1 = i1 - G
    eid_ref[...] = jnp.where(lane == 0, ex0, jnp.where(lane == 1, ex1, 0))
    wt_ref[...] = jnp.where(lane == 0, w0, jnp.where(lane == 1, w1, 0.0))
    oh0 = (lane == ex0).astype(F32)
    oh1 = (lane == ex1).astype(F32)
    both = oh0 + oh1
    rr = lax.broadcasted_iota(I32, (tr, tr), 0)
    cc = lax.broadcasted_iota(I32, (tr, tr), 1)
    tril = (cc < rr).astype(BF16)
    before = jnp.dot(tril, both.astype(BF16), preferred_element_type=F32) + carry_ref[0:1, :]
    rk0 = jnp.sum(before * oh0, axis=-1, keepdims=True)
    rk1 = jnp.sum(before * oh1, axis=-1, keepdims=True)
    rank_ref[...] = jnp.where(lane == 0, rk0, jnp.where(lane == 1, rk1, 0.0))
    newc = carry_ref[0:1, :] + jnp.sum(both, axis=0, keepdims=True)
    carry_ref[...] = jnp.broadcast_to(newc, carry_ref.shape)
    cnt_ref[...] = jnp.broadcast_to(newc, cnt_ref.shape)


def residual_prenorm_router(y, x2d, mod, gain_post, gain, rcat, S):
    N, D = x2d.shape
    tr = min(256, S)
    tiles_per_b = S // tr
    lane_spec = pl.BlockSpec((tr, LANES), lambda i: (i, 0))
    row_spec = pl.BlockSpec((tr, D), lambda i: (i, 0))
    vec_spec = pl.BlockSpec((1, D), lambda i: (0, 0))
    return pl.pallas_call(
        functools.partial(_router_kernel, tr=tr),
        grid=(N // tr,),
        in_specs=[row_spec, row_spec,
                  pl.BlockSpec((None, 6, D), lambda i: (i // tiles_per_b, 0, 0)),
                  vec_spec, vec_spec,
                  pl.BlockSpec((D, LANES), lambda i: (0, 0))],
        out_specs=[row_spec, pl.BlockSpec((tr, D // 2), lambda i: (i, 0)), lane_spec, lane_spec, lane_spec,
                   pl.BlockSpec((8, LANES), lambda i: (0, 0))],
        out_shape=[jax.ShapeDtypeStruct((N, D), F32),
                   jax.ShapeDtypeStruct((N, D // 2), U32),
                   jax.ShapeDtypeStruct((N, LANES), I32),
                   jax.ShapeDtypeStruct((N, LANES), F32),
                   jax.ShapeDtypeStruct((N, LANES), F32),
                   jax.ShapeDtypeStruct((8, LANES), F32)],
        scratch_shapes=[pltpu.VMEM((8, LANES), F32)],
        compiler_params=_cparams(("arbitrary",)),
        name="residual_prenorm_router",
    )(y, x2d, mod, gain_post.reshape(1, D), gain.reshape(1, D), rcat)


def _dest_kernel(eid_ref, rank_ref, start_ref, o_ref):
    lane = lax.broadcasted_iota(I32, eid_ref.shape, 1)
    eid = eid_ref[...]
    rank = rank_ref[...]
    start = start_ref[0:1, :]
    d = []
    for k in range(2):
        oh = (lane == eid[:, k:k + 1]).astype(F32)
        d.append(jnp.sum(oh * start, axis=-1, keepdims=True) + rank[:, k:k + 1])
    o_ref[...] = jnp.where(lane == 0, d[0], jnp.where(lane == 1, d[1], 0.0)).astype(I32)


def dest_rows(eid, rank, pad_start):
    N = eid.shape[0]
    tr = min(1024, N)
    spec = pl.BlockSpec((tr, LANES), lambda i: (i, 0))
    return pl.pallas_call(
        _dest_kernel,
        grid=(N // tr,),
        in_specs=[spec, spec, pl.BlockSpec((8, LANES), lambda i: (0, 0))],
        out_specs=spec,
        out_shape=jax.ShapeDtypeStruct((N, LANES), I32),
        compiler_params=_cparams(("parallel",)),
        name="dest_rows",
    )(eid, rank, pad_start)


def _scatter_kernel(st_ref, nc_ref, cnt_ref, dest_ref, h_ref, xs_ref, z_ref, sem, zsem, *, T, TM, E):
    i = pl.program_id(0)
    n_steps = pl.num_programs(0)
    sub = 8

    @pl.when(i == 0)
    def _():
        z_ref[...] = jnp.zeros_like(z_ref)

    def issue(tb, carry):
        for r in range(sub):
            t = tb * sub + r
            for k in range(2):
                d = dest_ref[0, 0, 2 * t + k]
                pltpu.make_async_copy(h_ref.at[pl.ds(t, 1)], xs_ref.at[pl.ds(d, 1)], sem).start()
        return carry

    lax.fori_loop(0, T // sub, issue, 0)

    def zero_row(row):
        return pltpu.make_async_copy(z_ref.at[pl.ds(0, 1)], xs_ref.at[pl.ds(row, 1)], zsem)

    def pad_expert(j, n_pad):
        e = i + j * n_steps
        first = st_ref[e] + cnt_ref[e]
        count = nc_ref[e] * TM - cnt_ref[e]

        def one(r, carry):
            zero_row(first + r).start()
            return carry

        lax.fori_loop(0, count, one, 0)
        return n_pad + count

    n_pad = lax.fori_loop(0, (E - i + n_steps - 1) // n_steps, pad_expert, 0)

    for k in range(2):
        pltpu.make_async_copy(h_ref, xs_ref.at[pl.ds(0, T)], sem).wait()

    def drain(r, carry):
        zero_row(0).wait()
        return carry

    lax.fori_loop(0, n_pad, drain, 0)

    @pl.when(i == n_steps - 1)
    def _():
        used = pl.multiple_of(st_ref[E - 1] + nc_ref[E - 1] * TM, TM)
        n_tail = (xs_ref.shape[0] - used) // TM

        def tail(c):
            return pltpu.make_async_copy(z_ref, xs_ref.at[pl.ds(pl.multiple_of(used + c * TM, TM), TM)], zsem)

        def tstart(c, carry):
            tail(c).start()
            return carry

        def twait(c, carry):
            tail(0).wait()
            return carry

        lax.fori_loop(0, n_tail, tstart, 0)
        lax.fori_loop(0, n_tail, twait, 0)


def scatter_rows(hp, dest, P, seg_start, seg_chunks, counts):
    N, W = hp.shape
    T = min(256, N)
    E = seg_start.shape[0]
    dest_s = dest[:, :2].reshape(N // T, 1, 2 * T)
    return pl.pallas_call(
        functools.partial(_scatter_kernel, T=T, TM=MOE_ROWS, E=E),
        grid_spec=pltpu.PrefetchScalarGridSpec(
            num_scalar_prefetch=3,
            grid=(N // T,),
            in_specs=[pl.BlockSpec((1, 1, 2 * T), lambda i, st, nc, cn: (i, 0, 0), memory_space=pltpu.SMEM),
                      pl.BlockSpec((T, W), lambda i, st, nc, cn: (i, 0))],
            out_specs=pl.BlockSpec(memory_space=pl.ANY),
            scratch_shapes=[pltpu.VMEM((MOE_ROWS, W), hp.dtype), pltpu.SemaphoreType.DMA(()),
                            pltpu.SemaphoreType.DMA(())]),
        out_shape=jax.ShapeDtypeStruct((P, W), hp.dtype),
        compiler_params=_cparams(("arbitrary",)),
        name="scatter_rows",
    )(seg_start, seg_chunks, counts, dest_s, hp)


def _expert_chunks(st_ref, nc_ref, in_hbm, out_hbm, ibuf, obuf, isem, osem, compute):
    e = pl.program_id(0)
    n_e = pl.num_programs(0)
    n = nc_ref[e]
    TM = ibuf.shape[1]
    base = pl.multiple_of(st_ref[e], TM)
    g0 = base // TM

    def icopy(row, slot):
        return pltpu.make_async_copy(in_hbm.at[pl.ds(row, TM)], ibuf.at[slot], isem.at[slot])

    def ocopy(row, slot):
        return pltpu.make_async_copy(obuf.at[slot], out_hbm.at[pl.ds(row, TM)], osem.at[slot])

    @pl.when((e == 0) & (n > 0))
    def _():
        icopy(base, 0).start()

    def body(c, carry):
        g = g0 + c
        slot = g % 2
        row = pl.multiple_of(base + c * TM, TM)

        @pl.when(c + 1 < n)
        def _():
            icopy(row + TM, 1 - slot).start()

        icopy(row, slot).wait()

        @pl.when(g >= 2)
        def _():
            ocopy(row, slot).wait()

        compute(ibuf.at[slot], obuf.at[slot])
        ocopy(row, slot).start()
        return carry

    lax.fori_loop(0, n, body, 0)

    @pl.when(e + 1 < n_e)
    def _():
        @pl.when(nc_ref[e + 1] > 0)
        def _():
            icopy(pl.multiple_of(st_ref[e + 1], TM), (g0 + n) % 2).start()

    @pl.when(e == n_e - 1)
    def _():
        g_end = g0 + n

        @pl.when(g_end >= 2)
        def _():
            ocopy(base, g_end % 2).wait()

        @pl.when(g_end >= 1)
        def _():
            ocopy(base, (g_end + 1) % 2).wait()

        used = pl.multiple_of(base + n * TM, TM)
        n_tail = (out_hbm.shape[0] - used) // TM
        obuf[0] = jnp.zeros(obuf.shape[1:], obuf.dtype)

        def zstart(c, carry):
            ocopy(pl.multiple_of(used + c * TM, TM), 0).start()
            return carry

        def zwait(c, carry):
            ocopy(used, 0).wait()
            return carry

        lax.fori_loop(0, n_tail, zstart, 0)
        lax.fori_loop(0, n_tail, zwait, 0)


WEIGHT_DMA_QUEUE = 1


def _stream_expert_weights(parts, wsem):
    e = pl.program_id(0)
    n_e = pl.num_programs(0)
    slot = e % 2

    def copy(k, ei, s):
        src, dst = parts[k]
        return pltpu.make_async_copy(src(ei), dst(s), wsem.at[k, s])

    @pl.when(e == 0)
    def _():
        for k in range(len(parts)):
            copy(k, 0, 0).start(priority=WEIGHT_DMA_QUEUE)

    @pl.when(e + 1 < n_e)
    def _():
        for k in range(len(parts)):
            copy(k, e + 1, 1 - slot).start(priority=WEIGHT_DMA_QUEUE)

    for k in range(len(parts)):
        copy(k, e, slot).wait()
    return slot


def _moe_up_kernel(st_ref, nc_ref, xs_hbm, w1_hbm, w3_hbm, o_hbm, wb_ref, w1_buf, w3_buf, ibuf, obuf,
                   wsem, isem, osem, *, F):
    slot = _stream_expert_weights([(lambda ei: w1_hbm.at[ei], lambda s: w1_buf.at[s]),
                                   (lambda ei: w3_hbm.at[ei], lambda s: w3_buf.at[s])], wsem)

    @pl.when(nc_ref[pl.program_id(0)] > 0)
    def _():
        wb_ref[:, :F] = w1_buf[slot].astype(BF16)
        wb_ref[:, F:] = w3_buf[slot].astype(BF16)

    def compute(x_ref, o_ref):
        lo, hi = _unpack_halves(x_ref[...])
        half = wb_ref.shape[0] // 2
        hcat = (jnp.dot(lo.astype(BF16), wb_ref[:half, :], preferred_element_type=F32)
                + jnp.dot(hi.astype(BF16), wb_ref[half:, :], preferred_element_type=F32))
        a = hcat[:, :F]
        o_ref[...] = (a * _sigmoid(a) * hcat[:, F:]).astype(o_ref.dtype)

    _expert_chunks(st_ref, nc_ref, xs_hbm, o_hbm, ibuf, obuf, isem, osem, compute)


def _moe_down_kernel(st_ref, nc_ref, h_hbm, w2_hbm, o_hbm, wb_ref, w2_buf, ibuf, obuf, wsem, isem, osem):
    slot = _stream_expert_weights([(lambda ei: w2_hbm.at[ei], lambda s: w2_buf.at[s])], wsem)

    @pl.when(nc_ref[pl.program_id(0)] > 0)
    def _():
        wb_ref[...] = w2_buf[slot].astype(BF16)

    def compute(h_ref, o_ref):
        y = jnp.dot(h_ref[...], wb_ref[...], preferred_element_type=F32)
        half = y.shape[1] // 2
        o_ref[...] = _pack_halves(y[:, :half], y[:, half:])

    _expert_chunks(st_ref, nc_ref, h_hbm, o_hbm, ibuf, obuf, isem, osem, compute)


def expert_ffn(xs, seg_start, seg_chunks, w1, w3, w2):
    P, Wp = xs.shape
    E, D, F = w1.shape
    TM = MOE_ROWS
    any_spec = pl.BlockSpec(memory_space=pl.ANY)
    dma2 = pltpu.SemaphoreType.DMA((2,))
    hmid = pl.pallas_call(
        functools.partial(_moe_up_kernel, F=F),
        grid_spec=pltpu.PrefetchScalarGridSpec(
            num_scalar_prefetch=2,
            grid=(E,),
            in_specs=[any_spec, any_spec, any_spec],
            out_specs=any_spec,
            scratch_shapes=[pltpu.VMEM((D, 2 * F), BF16), pltpu.VMEM((2, D, F), F32), pltpu.VMEM((2, D, F), F32),
                            pltpu.VMEM((2, TM, Wp), U32), pltpu.VMEM((2, TM, F), BF16),
                            pltpu.SemaphoreType.DMA((2, 2)), dma2, dma2]),
        out_shape=jax.ShapeDtypeStruct((P, F), BF16),
        compiler_params=_cparams(("arbitrary",)),
        name="moe_up",
    )(seg_start, seg_chunks, xs, w1, w3)
    return pl.pallas_call(
        _moe_down_kernel,
        grid_spec=pltpu.PrefetchScalarGridSpec(
            num_scalar_prefetch=2,
            grid=(E,),
            in_specs=[any_spec, any_spec],
            out_specs=any_spec,
            scratch_shapes=[pltpu.VMEM((F, D), BF16), pltpu.VMEM((2, F, D), F32), pltpu.VMEM((2, TM, F), BF16),
                            pltpu.VMEM((2, TM, D // 2), U32), pltpu.SemaphoreType.DMA((1, 2)), dma2, dma2]),
        out_shape=jax.ShapeDtypeStruct((P, D // 2), U32),
        compiler_params=_cparams(("arbitrary",)),
        name="moe_down",
    )(seg_start, seg_chunks, hmid, w2)


def _combine_kernel(pos_ref, nxt_ref, wt_ref, x_ref, mod_ref, gain_ref, y_hbm, o_ref, buf_ref, sem, *, T):
    i = pl.program_id(0)
    n = pl.num_programs(0)
    slot = i % 2

    def gather(p_ref, s):
        sub = 8

        def issue(tb, carry):
            for r in range(sub):
                t = tb * sub + r
                for k in range(2):
                    p = p_ref[0, 0, 2 * t + k]
                    pltpu.make_async_copy(y_hbm.at[pl.ds(p, 1)], buf_ref.at[s, k, pl.ds(t, 1)],
                                          sem.at[s]).start()
            return carry

        lax.fori_loop(0, T // sub, issue, 0)

    @pl.when(i == 0)
    def _():
        gather(pos_ref, 0)

    @pl.when(i + 1 < n)
    def _():
        gather(nxt_ref, 1 - slot)

    for k in range(2):
        pltpu.make_async_copy(y_hbm.at[pl.ds(0, T)], buf_ref.at[slot, k], sem.at[slot]).wait()

    wt = wt_ref[...]
    w0 = wt[:, 0:1]
    w1 = wt[:, 1:2]
    lo0, hi0 = _unpack_halves(buf_ref[slot, 0])
    lo1, hi1 = _unpack_halves(buf_ref[slot, 1])
    ylo = w0 * lo0 + w1 * lo1
    yhi = w0 * hi0 + w1 * hi1
    D = x_ref.shape[1]
    half = D // 2
    ms = (jnp.sum(ylo * ylo, axis=-1, keepdims=True) + jnp.sum(yhi * yhi, axis=-1, keepdims=True)) / D
    inv = lax.rsqrt(ms + NORM_EPS)
    o_ref[:, :half] = x_ref[:, :half] + mod_ref[5:6, :half] * (ylo * inv * gain_ref[:, :half])
    o_ref[:, half:] = x_ref[:, half:] + mod_ref[5:6, half:] * (yhi * inv * gain_ref[:, half:])


def combine(yp, dest, wts, x1, mod, gain, S):
    N, D = x1.shape
    T = min(128, S)
    tiles_per_b = S // T
    pos_s = dest[:, :2].reshape(N // T, 1, 2 * T)
    n_tiles = N // T
    return pl.pallas_call(
        functools.partial(_combine_kernel, T=T),
        grid=(n_tiles,),
        in_specs=[pl.BlockSpec((1, 1, 2 * T), lambda i: (i, 0, 0), memory_space=pltpu.SMEM),
                  pl.BlockSpec((1, 1, 2 * T), lambda i: (jnp.minimum(i + 1, n_tiles - 1), 0, 0),
                               memory_space=pltpu.SMEM),
                  pl.BlockSpec((T, LANES), lambda i: (i, 0)),
                  pl.BlockSpec((T, D), lambda i: (i, 0)),
                  pl.BlockSpec((None, 6, D), lambda i: (i // tiles_per_b, 0, 0)),
                  pl.BlockSpec((1, D), lambda i: (0, 0)),
                  pl.BlockSpec(memory_space=pl.ANY)],
        out_specs=pl.BlockSpec((T, D), lambda i: (i, 0)),
        out_shape=jax.ShapeDtypeStruct((N, D), F32),
        scratch_shapes=[pltpu.VMEM((2, 2, T, D // 2), U32), pltpu.SemaphoreType.DMA((2,))],
        compiler_params=_cparams(("arbitrary",)),
        name="combine",
    )(pos_s, pos_s, wts, x1, mod, gain.reshape(1, D), yp)


def _moe_layout(counts):
    TM = MOE_ROWS
    chunks = (counts.astype(I32) + TM - 1) // TM
    start = (jnp.cumsum(chunks) - chunks) * TM
    return start, chunks


def kernel(x, c, positions, ada_w, ada_b, mix_norm_pre, mix_norm_post, w_in, conv_dw, conv_dw_bias, conv_ln_gain, conv_ln_bias, w_conv_out, w_attn_out, w_out, ffn_norm_pre, ffn_norm_post, router_group, router_expert, expert_w1, expert_w3, expert_w2):
    B, S, D = x.shape
    N = B * S
    U = D // 4
    depth = ada_w.shape[0]
    xc = x.reshape(N, D)
    for layer in range(depth):
        mod = ada_mod(c, ada_w[layer], ada_b[layer]).reshape(B, 6, D)
        h, cosf, sinf = prenorm_rope(xc, mod, mix_norm_pre[layer], positions, S)
        w_in_b = w_in[layer].astype(BF16)
        outs, stats = [], []
        for g, (window, d) in enumerate(ATTN_PATTERNS):
            assert window // d == ATTN_BLOCK and S % (d * ATTN_BLOCK) == 0
            qkv = qkv_proj(h, w_in_b, cosf, sinf, B, S, U, g, d)
            o, st = dilated_attention(qkv, d)
            outs.append(o)
            stats.append(st)
        am = merge_groups(outs, stats, tuple(d for _, d in ATTN_PATTERNS))
        proj = matmul_cols(h, w_in_b, U, 9, "rest_proj")
        cn = conv_branch(proj, B, S, U, conv_dw[layer], conv_dw_bias[layer], conv_ln_gain[layer], conv_ln_bias[layer])
        z = gate_proj(cn, am, w_conv_out[layer].astype(BF16), w_attn_out[layer].astype(BF16), proj, U)
        y = matmul_cols(z, w_out[layer].astype(BF16), U, 0, "out_proj")
        rcat = jnp.zeros((D, LANES), F32)
        rcat = rcat.at[:, :N_EXPERT_GROUPS].set(router_group[layer])
        rcat = rcat.at[:, N_EXPERT_GROUPS:N_EXPERT_GROUPS + N_EXPERTS].set(router_expert[layer]).astype(BF16)
        x1, hp, eid, wts, rank, cnt = residual_prenorm_router(
            y, xc, mod, mix_norm_post[layer], ffn_norm_pre[layer], rcat, S)
        TM = MOE_ROWS
        P = (2 * N + N_EXPERTS * (TM - 1)) // TM * TM
        seg_start, seg_chunks = _moe_layout(cnt[0, :N_EXPERTS])
        start_row = jnp.zeros((8, LANES), F32).at[:, :N_EXPERTS].set(seg_start.astype(F32)[None, :])
        dest = dest_rows(eid, rank, start_row)
        xs = scatter_rows(hp, dest, P, seg_start, seg_chunks, cnt[0, :N_EXPERTS].astype(I32))
        yp = expert_ffn(xs, seg_start, seg_chunks, expert_w1[layer], expert_w3[layer], expert_w2[layer])
        xc = combine(yp, dest, wts, x1, mod, ffn_norm_post[layer], S)
    return xc.reshape(B, S, D)
```

```python
import functools

import jax
import jax.numpy as jnp
from jax import lax
from jax.experimental import pallas as pl
from jax.experimental.pallas import tpu as pltpu

F32 = jnp.float32
BF16 = jnp.bfloat16
I32 = jnp.int32
U32 = jnp.uint32

HEAD_DIM = 128
LANES = 128
ATTN_BLOCK = 128
ATTN_PATTERNS = ((128, 1), (512, 4), (2048, 16))
ROPE_THETA = 10000.0
CONV_WIDTH = 31
CONV_HALO = 32
N_EXPERT_GROUPS = 8
EXPERTS_PER_GROUP = 8
N_EXPERTS = 64
NORM_EPS = 1e-6
NEG_BIG = -1e30
MOE_ROWS = 256
V7X_VMEM_LIMIT = 60 * 1024 * 1024


def _cparams(sem):
    return pltpu.CompilerParams(dimension_semantics=sem, vmem_limit_bytes=V7X_VMEM_LIMIT)


def _sigmoid(x):
    return 1.0 / (1.0 + jnp.exp(-x))


def _pack_halves(lo, hi):
    lo_b = lax.bitcast_convert_type(lo.astype(BF16).astype(F32), U32) >> 16
    hi_b = lax.bitcast_convert_type(hi.astype(BF16).astype(F32), U32) & jnp.uint32(0xFFFF0000)
    return hi_b | lo_b


def _unpack_halves(w):
    lo = lax.bitcast_convert_type(w << 16, F32)
    hi = lax.bitcast_convert_type(w & jnp.uint32(0xFFFF0000), F32)
    return lo, hi


def _ada_kernel(c_ref, w_ref, b_ref, o_ref):
    c = c_ref[...]
    cact = (c * _sigmoid(c)).astype(BF16)
    o_ref[...] = jnp.dot(cact, w_ref[...].astype(BF16), preferred_element_type=F32) + b_ref[...]


def ada_mod(c, ada_w, ada_b):
    B, D = c.shape
    W = ada_w.shape[1]
    rows = 8
    cp = jnp.zeros((rows, D), F32).at[:B].set(c)
    tn = min(1024, W)
    out = pl.pallas_call(
        _ada_kernel,
        grid=(W // tn,),
        in_specs=[pl.BlockSpec((rows, D), lambda j: (0, 0)),
                  pl.BlockSpec((D, tn), lambda j: (0, j)),
                  pl.BlockSpec((1, tn), lambda j: (0, j))],
        out_specs=pl.BlockSpec((rows, tn), lambda j: (0, j)),
        out_shape=jax.ShapeDtypeStruct((rows, W), F32),
        compiler_params=_cparams(("parallel",)),
        name="ada_mod",
    )(cp, ada_w, ada_b.reshape(1, W))
    return out[:B]


def _prenorm_kernel(x_ref, mod_ref, gain_ref, pos_ref, invf_ref, h_ref, cos_ref, sin_ref):
    x = x_ref[...]
    ms = jnp.mean(x * x, axis=-1, keepdims=True)
    g = gain_ref[...] * (1.0 + mod_ref[1:2, :])
    h_ref[...] = ((x * lax.rsqrt(ms + NORM_EPS)) * g + mod_ref[0:1, :]).astype(BF16)
    ang = pos_ref[...].astype(F32) * invf_ref[...]
    lane = lax.broadcasted_iota(I32, ang.shape, 1)
    sn = jnp.sin(ang)
    cos_ref[...] = jnp.cos(ang)
    sin_ref[...] = jnp.where(lane < HEAD_DIM // 2, -sn, sn)


def prenorm_rope(x2d, mod, gain, positions, S):
    N, D = x2d.shape
    tr = min(256, S)
    tiles_per_b = S // tr
    inv = ROPE_THETA ** (-jnp.arange(0, HEAD_DIM, 2, dtype=F32) / HEAD_DIM)
    invf = jnp.concatenate([inv, inv]).reshape(1, HEAD_DIM)
    return pl.pallas_call(
        _prenorm_kernel,
        grid=(N // tr,),
        in_specs=[pl.BlockSpec((tr, D), lambda i: (i, 0)),
                  pl.BlockSpec((None, 6, D), lambda i: (i // tiles_per_b, 0, 0)),
                  pl.BlockSpec((1, D), lambda i: (0, 0)),
                  pl.BlockSpec((tr, 1), lambda i: (i, 0)),
                  pl.BlockSpec((1, HEAD_DIM), lambda i: (0, 0))],
        out_specs=[pl.BlockSpec((tr, D), lambda i: (i, 0)),
                   pl.BlockSpec((tr, HEAD_DIM), lambda i: (i, 0)),
                   pl.BlockSpec((tr, HEAD_DIM), lambda i: (i, 0))],
        out_shape=[jax.ShapeDtypeStruct((N, D), BF16),
                   jax.ShapeDtypeStruct((N, HEAD_DIM), F32),
                   jax.ShapeDtypeStruct((N, HEAD_DIM), F32)],
        compiler_params=_cparams(("parallel",)),
        name="prenorm_rope",
    )(x2d, mod, gain.reshape(1, D), positions.reshape(N, 1), invf)


def _residue_rows(ref, r, d):
    if d == 1:
        return ref[...]
    return ref[pl.ds(r, ref.shape[0] // d, stride=d), :]


def _qkv_kernel(a_ref, w_ref, cos_ref, sin_ref, o_ref, acc_ref, *, d, heads, scale, n_i):
    s = pl.program_id(0)

    @pl.when(s == 0)
    def _():
        acc_ref[...] = jnp.zeros_like(acc_ref)

    kind = jnp.maximum(s - 1, 0) // n_i
    sc = jnp.where(kind == 0, scale, 1.0).astype(F32)
    is_v = kind == 2
    for r in range(d):
        c = jnp.where(is_v, 1.0, _residue_rows(cos_ref, r, d) * sc)
        sn = jnp.where(is_v, 0.0, _residue_rows(sin_ref, r, d) * sc)
        for h in range(heads):
            t = _residue_rows(acc_ref.at[h], r, d)
            o_ref[r, :, h * HEAD_DIM:(h + 1) * HEAD_DIM] = (
                t * c + pltpu.roll(t, HEAD_DIM // 2, 1) * sn).astype(o_ref.dtype)
    acc = jnp.dot(a_ref[...], w_ref[...], preferred_element_type=F32)
    for h in range(heads):
        acc_ref[h] = acc[:, h * HEAD_DIM:(h + 1) * HEAD_DIM]


def qkv_proj(h, w_bf16, cosf, sinf, B, S, U, g, d):
    N, D = h.shape
    tm = min(1024, S)
    tiles_per_b = S // tm
    n_i = N // tm
    last = 3 * n_i - 1
    kern = functools.partial(_qkv_kernel, d=d, heads=U // HEAD_DIM, scale=HEAD_DIM ** -0.5, n_i=n_i)

    def cur(s):
        return jnp.minimum(s, last)

    def fin(s):
        return jnp.maximum(s - 1, 0)

    return pl.pallas_call(
        kern,
        grid=(3 * n_i + 1,),
        in_specs=[pl.BlockSpec((tm, D), lambda s: (cur(s) % n_i, 0)),
                  pl.BlockSpec((D, U), lambda s: (0, (cur(s) // n_i) * 3 + g)),
                  pl.BlockSpec((tm, HEAD_DIM), lambda s: (fin(s) % n_i, 0)),
                  pl.BlockSpec((tm, HEAD_DIM), lambda s: (fin(s) % n_i, 0))],
        out_specs=pl.BlockSpec((None, None, d, tm // d, U),
                               lambda s: (fin(s) // n_i, (fin(s) % n_i) // tiles_per_b, 0,
                                          (fin(s) % n_i) % tiles_per_b, 0)),
        out_shape=jax.ShapeDtypeStruct((3, B, d, S // d, U), BF16),
        scratch_shapes=[pltpu.VMEM((U // HEAD_DIM, tm, HEAD_DIM), F32)],
        compiler_params=_cparams(("arbitrary",)),
        name=f"qkv_proj_d{d}",
    )(h, w_bf16, cosf, sinf)


def _matmul_kernel(a_ref, w_ref, o_ref):
    o_ref[...] = jnp.dot(a_ref[...], w_ref[...], preferred_element_type=F32).astype(o_ref.dtype)


def matmul_cols(a, w_bf16, U, first_tile, name):
    N, K = a.shape
    n_tiles = w_bf16.shape[1] // U - first_tile
    tm = min(1024, N)
    return pl.pallas_call(
        _matmul_kernel,
        grid=(n_tiles, N // tm),
        in_specs=[pl.BlockSpec((tm, K), lambda j, i: (i, 0)),
                  pl.BlockSpec((K, U), lambda j, i: (0, first_tile + j))],
        out_specs=pl.BlockSpec((tm, U), lambda j, i: (i, j)),
        out_shape=jax.ShapeDtypeStruct((N, n_tiles * U), BF16),
        compiler_params=_cparams(("parallel", "parallel")),
        name=name,
    )(a, w_bf16)


def _attn_kernel(q_ref, kc_ref, kp_ref, vc_ref, vp_ref, o_ref, st_ref, kx_ref, vx_ref, *, QB, H):
    i = pl.program_id(2)
    blk = ATTN_BLOCK
    kx_ref[0:blk, :] = kp_ref[...]
    kx_ref[blk:, :] = kc_ref[...]
    ones = jnp.ones((vx_ref.shape[0], HEAD_DIM), vx_ref.dtype)
    for h in range(H):
        hs = slice(h * HEAD_DIM, (h + 1) * HEAD_DIM)
        vx_ref[0:blk, 2 * h * HEAD_DIM:(2 * h + 1) * HEAD_DIM] = vp_ref[:, hs]
        vx_ref[blk:, 2 * h * HEAD_DIM:(2 * h + 1) * HEAD_DIM] = vc_ref[:, hs]
        vx_ref[:, (2 * h + 1) * HEAD_DIM:(2 * h + 2) * HEAD_DIM] = ones
    rows = lax.broadcasted_iota(I32, (blk, 2 * blk), 0)
    keys = lax.broadcasted_iota(I32, (blk, 2 * blk), 1)
    band = (keys >= rows) & (keys - blk <= rows)
    lane = lax.broadcasted_iota(I32, (blk, LANES), 1)
    dn = (((1,), (1,)), ((), ()))

    def body(qb, carry):
        r0 = pl.multiple_of(qb * blk, blk)
        ok = band & ((keys >= blk) | (i * QB + qb > 0))
        q = [q_ref[pl.ds(r0, blk), h * HEAD_DIM:(h + 1) * HEAD_DIM] for h in range(H)]
        k = [kx_ref[pl.ds(r0, 2 * blk), h * HEAD_DIM:(h + 1) * HEAD_DIM] for h in range(H)]
        s = [jnp.where(ok, lax.dot_general(q[h], k[h], dn, preferred_element_type=F32), NEG_BIG) for h in range(H)]
        m = [jnp.max(s[h], axis=-1, keepdims=True) for h in range(H)]
        p = [jnp.exp(s[h] - m[h]).astype(BF16) for h in range(H)]
        st = jnp.zeros((blk, LANES), F32)
        for h in range(H):
            v1 = vx_ref[pl.ds(r0, 2 * blk), 2 * h * HEAD_DIM:(2 * h + 2) * HEAD_DIM]
            acc = jnp.dot(p[h], v1, preferred_element_type=F32)
            l = acc[:, HEAD_DIM:]
            o_ref[pl.ds(r0, blk), h * HEAD_DIM:(h + 1) * HEAD_DIM] = (acc[:, :HEAD_DIM] / l).astype(o_ref.dtype)
            st = jnp.where(lane == h, m[h], st)
            st = jnp.where(lane == H + h, l, st)
        st_ref[pl.ds(r0, blk), :] = st
        return carry

    lax.fori_loop(0, QB, body, 0)


def dilated_attention(qkv, d):
    _, B, _, L, U = qkv.shape
    H = U // HEAD_DIM
    R = min(512, L)
    QB = R // ATTN_BLOCK

    def cur(kind):
        return pl.BlockSpec((None, None, None, R, U), lambda b, r, i: (kind, b, r, i, 0))

    def prev(kind):
        return pl.BlockSpec((None, None, None, ATTN_BLOCK, U),
                            lambda b, r, i: (kind, b, r, jnp.maximum(i * QB - 1, 0), 0))

    kern = functools.partial(_attn_kernel, QB=QB, H=H)
    return pl.pallas_call(
        kern,
        grid=(B, d, L // R),
        in_specs=[cur(0), cur(1), prev(1), cur(2), prev(2)],
        out_specs=[pl.BlockSpec((None, None, R, U), lambda b, r, i: (b, r, i, 0)),
                   pl.BlockSpec((None, None, R, LANES), lambda b, r, i: (b, r, i, 0))],
        out_shape=[jax.ShapeDtypeStruct((B, d, L, U), BF16),
                   jax.ShapeDtypeStruct((B, d, L, LANES), F32)],
        scratch_shapes=[pltpu.VMEM((R + ATTN_BLOCK, U), BF16), pltpu.VMEM((R + ATTN_BLOCK, 2 * U), BF16)],
        compiler_params=_cparams(("parallel", "parallel", "parallel")),
        name=f"dilated_attn_d{d}",
    )(qkv, qkv, qkv, qkv, qkv)


def _merge_kernel(o0_ref, o1_ref, o2_ref, s0_ref, s1_ref, s2_ref, out_ref, of_ref, sf_ref, *, H, dils):
    for g, (o_ref, s_ref) in enumerate(((o0_ref, s0_ref), (o1_ref, s1_ref), (o2_ref, s2_ref))):
        d = dils[g]
        n = sf_ref.shape[1] // d
        for r in range(d):
            rows = slice(None) if d == 1 else pl.ds(r, n, stride=d)
            sf_ref[g, rows, :] = s_ref[r]
            for h in range(H):
                of_ref[g, h, rows, :] = o_ref[r, :, h * HEAD_DIM:(h + 1) * HEAD_DIM].astype(F32)
    st = [sf_ref[0], sf_ref[1], sf_ref[2]]
    mx = jnp.maximum(jnp.maximum(st[0], st[1]), st[2])
    w = [pltpu.roll(s, LANES - H, 1) * jnp.exp(s - mx) for s in st]
    tot = w[0] + w[1] + w[2]
    coef = [x / tot for x in w]
    for h in range(H):
        hs = slice(h * HEAD_DIM, (h + 1) * HEAD_DIM)
        acc = coef[0][:, h:h + 1] * of_ref[0, h]
        acc += coef[1][:, h:h + 1] * of_ref[1, h]
        acc += coef[2][:, h:h + 1] * of_ref[2, h]
        out_ref[:, hs] = acc.astype(out_ref.dtype)


def merge_groups(outs, stats, dils):
    B, d0, L0, U = outs[0].shape
    S = d0 * L0
    H = U // HEAD_DIM
    tm = min(512, S)
    tiles_per_b = S // tm

    def ospec(d, w):
        return pl.BlockSpec((None, d, tm // d, w), lambda i: (i // tiles_per_b, 0, i % tiles_per_b, 0))

    return pl.pallas_call(
        functools.partial(_merge_kernel, H=H, dils=dils),
        grid=(B * S // tm,),
        in_specs=[ospec(d, U) for d in dils] + [ospec(d, LANES) for d in dils],
        out_specs=pl.BlockSpec((tm, U), lambda i: (i, 0)),
        out_shape=jax.ShapeDtypeStruct((B * S, U), BF16),
        scratch_shapes=[pltpu.VMEM((3, H, tm, HEAD_DIM), F32), pltpu.VMEM((3, tm, LANES), F32)],
        compiler_params=_cparams(("parallel",)),
        name="merge_groups",
    )(*outs, *stats)


def _conv_kernel(a0_ref, a1_ref, b0_ref, b1_ref, ha0_ref, ha1_ref, hb0_ref, hb1_ref,
                 w_ref, bias_ref, g_ref, be_ref, o_ref, u_ref, c_ref, sh_ref, *, ts, U):
    i = pl.program_id(1)
    halo = CONV_HALO
    for half, (a_ref, b_ref, ha_ref, hb_ref) in enumerate(((a0_ref, b0_ref, ha0_ref, hb0_ref),
                                                            (a1_ref, b1_ref, ha1_ref, hb1_ref))):
        cs = slice(half * U, (half + 1) * U)
        u_ref[halo:halo + ts, cs] = a_ref[...].astype(F32) * _sigmoid(b_ref[...].astype(F32))
        hu = ha_ref[...].astype(F32) * _sigmoid(hb_ref[...].astype(F32))
        u_ref[0:halo, cs] = jnp.where(i > 0, hu, 0.0)
    C = 2 * U
    rc = 64
    off = halo - (CONV_WIDTH - 1)

    sub = 8
    n_al = ts + halo - sub

    def chan_body(cc, carry):
        c0 = pl.multiple_of(cc * LANES, LANES)
        sh_ref[0] = u_ref[:, pl.ds(c0, LANES)]
        for b in range(1, sub):
            sh_ref[b, 0:n_al, :] = u_ref[b:b + n_al, pl.ds(c0, LANES)]
        for rb in range(ts // rc):
            acc = jnp.zeros((rc, LANES), F32) + bias_ref[:, pl.ds(c0, LANES)]
            for j in range(CONV_WIDTH):
                a, b = divmod(off + j, sub)
                r0 = rb * rc + a * sub
                acc += w_ref[j:j + 1, pl.ds(c0, LANES)] * sh_ref[b, r0:r0 + rc, :]
            c_ref[rb * rc:(rb + 1) * rc, pl.ds(c0, LANES)] = acc
        return carry

    lax.fori_loop(0, C // LANES, chan_body, 0)

    rn = 16

    def norm_body(rb, carry):
        r0 = pl.multiple_of(rb * rn, rn)
        v = c_ref[pl.ds(r0, rn), :]
        mu = jnp.mean(v, axis=-1, keepdims=True)
        dv = v - mu
        var = jnp.mean(dv * dv, axis=-1, keepdims=True)
        y = dv * lax.rsqrt(var + NORM_EPS) * g_ref[...] + be_ref[...]
        o_ref[pl.ds(r0, rn), :] = (y * _sigmoid(y)).astype(o_ref.dtype)
        return carry

    lax.fori_loop(0, ts // rn, norm_body, 0, unroll=4)


def conv_branch(proj, B, S, U, conv_dw, conv_bias, ln_gain, ln_bias):
    IN = proj.shape[1]
    C = 2 * U
    ts = min(256, S)
    pv = proj.reshape(B, S, IN)
    hb = ts // CONV_HALO
    cur = lambda blk: pl.BlockSpec((None, ts, U), lambda b, i, blk=blk: (b, i, blk))
    prv = lambda blk: pl.BlockSpec((None, CONV_HALO, U), lambda b, i, blk=blk: (b, jnp.maximum(i * hb - 1, 0), blk))
    vec = pl.BlockSpec((1, C), lambda b, i: (0, 0))
    out = pl.pallas_call(
        functools.partial(_conv_kernel, ts=ts, U=U),
        grid=(B, S // ts),
        in_specs=[cur(0), cur(1), cur(2), cur(3), prv(0), prv(1), prv(2), prv(3),
                  pl.BlockSpec((CONV_WIDTH, C), lambda b, i: (0, 0)), vec, vec, vec],
        out_specs=pl.BlockSpec((None, ts, C), lambda b, i: (b, i, 0)),
        out_shape=jax.ShapeDtypeStruct((B, S, C), BF16),
        scratch_shapes=[pltpu.VMEM((ts + CONV_HALO, C), F32), pltpu.VMEM((ts, C), F32),
                        pltpu.VMEM((8, ts + CONV_HALO, LANES), F32)],
        compiler_params=_cparams(("parallel", "parallel")),
        name="conv_branch",
    )(pv, pv, pv, pv, pv, pv, pv, pv, conv_dw, conv_bias.reshape(1, C), ln_gain.reshape(1, C), ln_bias.reshape(1, C))
    return out.reshape(B * S, C)


def _gateproj_kernel(cn_ref, am_ref, wc_ref, wa_ref, gc_ref, ga_ref, o_ref):
    conv = jnp.dot(cn_ref[...], wc_ref[...], preferred_element_type=F32)
    z = _sigmoid(gc_ref[...].astype(F32)) * conv
    attn = jnp.dot(am_ref[...], wa_ref[...], preferred_element_type=F32)
    z += _sigmoid(ga_ref[...].astype(F32)) * attn
    o_ref[...] = z.astype(o_ref.dtype)


def gate_proj(cn, am, wc, wa, proj, U):
    N, C = cn.shape
    D = wc.shape[1]
    tn = U
    tm = min(1024, N)
    return pl.pallas_call(
        _gateproj_kernel,
        grid=(D // tn, N // tm),
        in_specs=[pl.BlockSpec((tm, C), lambda j, i: (i, 0)),
                  pl.BlockSpec((tm, U), lambda j, i: (i, 0)),
                  pl.BlockSpec((C, tn), lambda j, i: (0, j)),
                  pl.BlockSpec((U, tn), lambda j, i: (0, j)),
                  pl.BlockSpec((tm, tn), lambda j, i: (i, 4 + j)),
                  pl.BlockSpec((tm, tn), lambda j, i: (i, 8 + j))],
        out_specs=pl.BlockSpec((tm, tn), lambda j, i: (i, j)),
        out_shape=jax.ShapeDtypeStruct((N, D), BF16),
        compiler_params=_cparams(("parallel", "parallel")),
        name="gate_proj",
    )(cn, am, wc, wa, proj, proj)


def _router_kernel(y_ref, xin_ref, mod_ref, gpost_ref, gain_ref, r_ref,
                   x1_ref, hp_ref, eid_ref, wt_ref, rank_ref, cnt_ref, carry_ref, *, tr):
    step = pl.program_id(0)

    @pl.when(step == 0)
    def _():
        carry_ref[...] = jnp.zeros_like(carry_ref)

    y = y_ref[...].astype(F32)
    yms = jnp.mean(y * y, axis=-1, keepdims=True)
    x = xin_ref[...] + (y * lax.rsqrt(yms + NORM_EPS)) * (mod_ref[2:3, :] * gpost_ref[...])
    x1_ref[...] = x
    D = x.shape[1]
    ms = jnp.mean(x * x, axis=-1, keepdims=True)
    h = (x * lax.rsqrt(ms + NORM_EPS)) * (gain_ref[...] * (1.0 + mod_ref[4:5, :])) + mod_ref[3:4, :]
    lo = h[:, :D // 2]
    hi = h[:, D // 2:]
    hp_ref[...] = _pack_halves(lo, hi)
    logits = (jnp.dot(lo.astype(BF16), r_ref[:D // 2, :], preferred_element_type=F32)
              + jnp.dot(hi.astype(BF16), r_ref[D // 2:, :], preferred_element_type=F32))
    lane = lax.broadcasted_iota(I32, logits.shape, 1)
    G = N_EXPERT_GROUPS
    is_g = lane < G
    gl = jnp.where(is_g, logits, NEG_BIG)
    gmax = jnp.max(gl, axis=-1, keepdims=True)
    grp = jnp.min(jnp.where(gl == gmax, lane, LANES), axis=-1, keepdims=True)
    p_grp = 1.0 / jnp.sum(jnp.where(is_g, jnp.exp(gl - gmax), 0.0), axis=-1, keepdims=True)
    in_grp = (lane >= G) & (lane < G + N_EXPERTS) & (((lane - G) // EXPERTS_PER_GROUP) == grp)
    el = jnp.where(in_grp, logits, NEG_BIG)
    v0 = jnp.max(el, axis=-1, keepdims=True)
    i0 = jnp.min(jnp.where(in_grp & (el == v0), lane, LANES), axis=-1, keepdims=True)
    in2 = in_grp & (lane != i0)
    el2 = jnp.where(in2, logits, NEG_BIG)
    v1 = jnp.max(el2, axis=-1, keepdims=True)
    i1 = jnp.min(jnp.where(in2 & (el2 == v1), lane, LANES), axis=-1, keepdims=True)
    e1 = jnp.exp(v1 - v0)
    w0 = p_grp / (1.0 + e1)
    w1 = p_grp * e1 / (1.0 + e1)
    ex0 = i0 - G
    ex1 = i1 - G
    eid_ref[...] = jnp.where(lane == 0, ex0, jnp.where(lane == 1, ex1, 0))
    wt_ref[...] = jnp.where(lane == 0, w0, jnp.where(lane == 1, w1, 0.0))
    oh0 = (lane == ex0).astype(F32)
    oh1 = (lane == ex1).astype(F32)
    both = oh0 + oh1
    rr = lax.broadcasted_iota(I32, (tr, tr), 0)
    cc = lax.broadcasted_iota(I32, (tr, tr), 1)
    tril = (cc < rr).astype(BF16)
    before = jnp.dot(tril, both.astype(BF16), preferred_element_type=F32) + carry_ref[0:1, :]
    rk0 = jnp.sum(before * oh0, axis=-1, keepdims=True)
    rk1 = jnp.sum(before * oh1, axis=-1, keepdims=True)
    rank_ref[...] = jnp.where(lane == 0, rk0, jnp.where(lane == 1, rk1, 0.0))
    newc = carry_ref[0:1, :] + jnp.sum(both, axis=0, keepdims=True)
    carry_ref[...] = jnp.broadcast_to(newc, carry_ref.shape)
    cnt_ref[...] = jnp.broadcast_to(newc, cnt_ref.shape)


def residual_prenorm_router(y, x2d, mod, gain_post, gain, rcat, S):
    N, D = x2d.shape
    tr = min(256, S)
    tiles_per_b = S // tr
    lane_spec = pl.BlockSpec((tr, LANES), lambda i: (i, 0))
    row_spec = pl.BlockSpec((tr, D), lambda i: (i, 0))
    vec_spec = pl.BlockSpec((1, D), lambda i: (0, 0))
    return pl.pallas_call(
        functools.partial(_router_kernel, tr=tr),
        grid=(N // tr,),
        in_specs=[row_spec, row_spec,
                  pl.BlockSpec((None, 6, D), lambda i: (i // tiles_per_b, 0, 0)),
                  vec_spec, vec_spec,
                  pl.BlockSpec((D, LANES), lambda i: (0, 0))],
        out_specs=[row_spec, pl.BlockSpec((tr, D // 2), lambda i: (i, 0)), lane_spec, lane_spec, lane_spec,
                   pl.BlockSpec((8, LANES), lambda i: (0, 0))],
        out_shape=[jax.ShapeDtypeStruct((N, D), F32),
                   jax.ShapeDtypeStruct((N, D // 2), U32),
                   jax.ShapeDtypeStruct((N, LANES), I32),
                   jax.ShapeDtypeStruct((N, LANES), F32),
                   jax.ShapeDtypeStruct((N, LANES), F32),
                   jax.ShapeDtypeStruct((8, LANES), F32)],
        scratch_shapes=[pltpu.VMEM((8, LANES), F32)],
        compiler_params=_cparams(("arbitrary",)),
        name="residual_prenorm_router",
    )(y, x2d, mod, gain_post.reshape(1, D), gain.reshape(1, D), rcat)


def _dest_kernel(eid_ref, rank_ref, start_ref, o_ref):
    lane = lax.broadcasted_iota(I32, eid_ref.shape, 1)
    eid = eid_ref[...]
    rank = rank_ref[...]
    start = start_ref[0:1, :]
    d = []
    for k in range(2):
        oh = (lane == eid[:, k:k + 1]).astype(F32)
        d.append(jnp.sum(oh * start, axis=-1, keepdims=True) + rank[:, k:k + 1])
    o_ref[...] = jnp.where(lane == 0, d[0], jnp.where(lane == 1, d[1], 0.0)).astype(I32)


def dest_rows(eid, rank, pad_start):
    N = eid.shape[0]
    tr = min(1024, N)
    spec = pl.BlockSpec((tr, LANES), lambda i: (i, 0))
    return pl.pallas_call(
        _dest_kernel,
        grid=(N // tr,),
        in_specs=[spec, spec, pl.BlockSpec((8, LANES), lambda i: (0, 0))],
        out_specs=spec,
        out_shape=jax.ShapeDtypeStruct((N, LANES), I32),
        compiler_params=_cparams(("parallel",)),
        name="dest_rows",
    )(eid, rank, pad_start)


def _scatter_kernel(st_ref, nc_ref, cnt_ref, dest_ref, h_ref, xs_ref, z_ref, sem, zsem, *, T, TM, E):
    i = pl.program_id(0)
    n_steps = pl.num_programs(0)
    sub = 8

    @pl.when(i == 0)
    def _():
        z_ref[...] = jnp.zeros_like(z_ref)

    def issue(tb, carry):
        for r in range(sub):
            t = tb * sub + r
            for k in range(2):
                d = dest_ref[0, 0, 2 * t + k]
                pltpu.make_async_copy(h_ref.at[pl.ds(t, 1)], xs_ref.at[pl.ds(d, 1)], sem).start()
        return carry

    lax.fori_loop(0, T // sub, issue, 0)

    def zero_rows(row, size):
        return pltpu.make_async_copy(z_ref.at[pl.ds(0, size)], xs_ref.at[pl.ds(row, size)], zsem)

    pieces = [s for s in (sub, 2 * sub, 4 * sub, 8 * sub, 16 * sub) if s < TM]

    def pad_expert(wait):
        def run(j, carry):
            e = i + j * n_steps
            first = st_ref[e] + cnt_ref[e]
            end = st_ref[e] + nc_ref[e] * TM
            head = jnp.minimum((sub - (first & (sub - 1))) & (sub - 1), end - first)

            def one(r, c):
                cp = zero_rows(first + r, 1)
                cp.wait() if wait else cp.start()
                return c

            lax.fori_loop(0, head, one, 0)
            row = first + head
            tiles = (end - row) // sub
            for b, size in enumerate(pieces):
                @pl.when((tiles >> b) & 1 == 1)
                def _(row=row, size=size):
                    cp = zero_rows(pl.multiple_of(row, sub), size)
                    cp.wait() if wait else cp.start()

                row = row + jnp.where((tiles >> b) & 1 == 1, size, 0)
            return carry

        lax.fori_loop(0, (E - i + n_steps - 1) // n_steps, run, 0)

    pad_expert(wait=False)

    for k in range(2):
        pltpu.make_async_copy(h_ref, xs_ref.at[pl.ds(0, T)], sem).wait()

    pad_expert(wait=True)

    @pl.when(i == n_steps - 1)
    def _():
        used = pl.multiple_of(st_ref[E - 1] + nc_ref[E - 1] * TM, TM)
        n_tail = (xs_ref.shape[0] - used) // TM

        def tail(c):
            return pltpu.make_async_copy(z_ref, xs_ref.at[pl.ds(pl.multiple_of(used + c * TM, TM), TM)], zsem)

        def tstart(c, carry):
            tail(c).start()
            return carry

        def twait(c, carry):
            tail(0).wait()
            return carry

        lax.fori_loop(0, n_tail, tstart, 0)
        lax.fori_loop(0, n_tail, twait, 0)


def scatter_rows(hp, dest, P, seg_start, seg_chunks, counts):
    N, W = hp.shape
    T = min(256, N)
    E = seg_start.shape[0]
    dest_s = dest[:, :2].reshape(N // T, 1, 2 * T)
    return pl.pallas_call(
        functools.partial(_scatter_kernel, T=T, TM=MOE_ROWS, E=E),
        grid_spec=pltpu.PrefetchScalarGridSpec(
            num_scalar_prefetch=3,
            grid=(N // T,),
            in_specs=[pl.BlockSpec((1, 1, 2 * T), lambda i, st, nc, cn: (i, 0, 0), memory_space=pltpu.SMEM),
                      pl.BlockSpec((T, W), lambda i, st, nc, cn: (i, 0))],
            out_specs=pl.BlockSpec(memory_space=pl.ANY),
            scratch_shapes=[pltpu.VMEM((MOE_ROWS, W), hp.dtype), pltpu.SemaphoreType.DMA(()),
                            pltpu.SemaphoreType.DMA(())]),
        out_shape=jax.ShapeDtypeStruct((P, W), hp.dtype),
        compiler_params=_cparams(("arbitrary",)),
        name="scatter_rows",
    )(seg_start, seg_chunks, counts, dest_s, hp)


def _expert_chunks(st_ref, nc_ref, in_hbm, out_hbm, ibuf, obuf, isem, osem, compute):
    e = pl.program_id(0)
    n_e = pl.num_programs(0)
    n = nc_ref[e]
    TM = ibuf.shape[1]
    base = pl.multiple_of(st_ref[e], TM)
    g0 = base // TM

    def icopy(row, slot):
        return pltpu.make_async_copy(in_hbm.at[pl.ds(row, TM)], ibuf.at[slot], isem.at[slot])

    def ocopy(row, slot):
        return pltpu.make_async_copy(obuf.at[slot], out_hbm.at[pl.ds(row, TM)], osem.at[slot])

    @pl.when((e == 0) & (n > 0))
    def _():
        icopy(base, 0).start()

    def body(c, carry):
        g = g0 + c
        slot = g % 2
        row = pl.multiple_of(base + c * TM, TM)

        @pl.when(c + 1 < n)
        def _():
            icopy(row + TM, 1 - slot).start()

        icopy(row, slot).wait()

        @pl.when(g >= 2)
        def _():
            ocopy(row, slot).wait()

        compute(ibuf.at[slot], obuf.at[slot])
        ocopy(row, slot).start()
        return carry

    lax.fori_loop(0, n, body, 0)

    @pl.when(e + 1 < n_e)
    def _():
        @pl.when(nc_ref[e + 1] > 0)
        def _():
            icopy(pl.multiple_of(st_ref[e + 1], TM), (g0 + n) % 2).start()

    @pl.when(e == n_e - 1)
    def _():
        g_end = g0 + n

        @pl.when(g_end >= 2)
        def _():
            ocopy(base, g_end % 2).wait()

        @pl.when(g_end >= 1)
        def _():
            ocopy(base, (g_end + 1) % 2).wait()

        used = pl.multiple_of(base + n * TM, TM)
        n_tail = (out_hbm.shape[0] - used) // TM
        obuf[0] = jnp.zeros(obuf.shape[1:], obuf.dtype)

        def zstart(c, carry):
            ocopy(pl.multiple_of(used + c * TM, TM), 0).start()
            return carry

        def zwait(c, carry):
            ocopy(used, 0).wait()
            return carry

        lax.fori_loop(0, n_tail, zstart, 0)
        lax.fori_loop(0, n_tail, zwait, 0)


WEIGHT_DMA_QUEUE = 1


def _stream_expert_weights(parts, wsem):
    e = pl.program_id(0)
    n_e = pl.num_programs(0)
    slot = e % 2

    def copy(k, ei, s):
        src, dst = parts[k]
        return pltpu.make_async_copy(src(ei), dst(s), wsem.at[k, s])

    @pl.when(e == 0)
    def _():
        for k in range(len(parts)):
            copy(k, 0, 0).start(priority=WEIGHT_DMA_QUEUE)

    @pl.when(e + 1 < n_e)
    def _():
        for k in range(len(parts)):
            copy(k, e + 1, 1 - slot).start(priority=WEIGHT_DMA_QUEUE)

    for k in range(len(parts)):
        copy(k, e, slot).wait()
    return slot


def _moe_up_kernel(st_ref, nc_ref, xs_hbm, w1_hbm, w3_hbm, o_hbm, wb_ref, w1_buf, w3_buf, ibuf, obuf,
                   wsem, isem, osem, *, F):
    slot = _stream_expert_weights([(lambda ei: w1_hbm.at[ei], lambda s: w1_buf.at[s]),
                                   (lambda ei: w3_hbm.at[ei], lambda s: w3_buf.at[s])], wsem)

    @pl.when(nc_ref[pl.program_id(0)] > 0)
    def _():
        wb_ref[:, :F] = w1_buf[slot].astype(BF16)
        wb_ref[:, F:] = w3_buf[slot].astype(BF16)

    def compute(x_ref, o_ref):
        lo, hi = _unpack_halves(x_ref[...])
        half = wb_ref.shape[0] // 2
        hcat = (jnp.dot(lo.astype(BF16), wb_ref[:half, :], preferred_element_type=F32)
                + jnp.dot(hi.astype(BF16), wb_ref[half:, :], preferred_element_type=F32))
        a = hcat[:, :F]
        o_ref[...] = (a * _sigmoid(a) * hcat[:, F:]).astype(o_ref.dtype)

    _expert_chunks(st_ref, nc_ref, xs_hbm, o_hbm, ibuf, obuf, isem, osem, compute)


def _moe_down_kernel(st_ref, nc_ref, h_hbm, w2_hbm, o_hbm, wb_ref, w2_buf, ibuf, obuf, wsem, isem, osem):
    slot = _stream_expert_weights([(lambda ei: w2_hbm.at[ei], lambda s: w2_buf.at[s])], wsem)

    @pl.when(nc_ref[pl.program_id(0)] > 0)
    def _():
        wb_ref[...] = w2_buf[slot].astype(BF16)

    def compute(h_ref, o_ref):
        y = jnp.dot(h_ref[...], wb_ref[...], preferred_element_type=F32)
        half = y.shape[1] // 2
        o_ref[...] = _pack_halves(y[:, :half], y[:, half:])

    _expert_chunks(st_ref, nc_ref, h_hbm, o_hbm, ibuf, obuf, isem, osem, compute)


def expert_ffn(xs, seg_start, seg_chunks, w1, w3, w2):
    P, Wp = xs.shape
    E, D, F = w1.shape
    TM = MOE_ROWS
    any_spec = pl.BlockSpec(memory_space=pl.ANY)
    dma2 = pltpu.SemaphoreType.DMA((2,))
    hmid = pl.pallas_call(
        functools.partial(_moe_up_kernel, F=F),
        grid_spec=pltpu.PrefetchScalarGridSpec(
            num_scalar_prefetch=2,
            grid=(E,),
            in_specs=[any_spec, any_spec, any_spec],
            out_specs=any_spec,
            scratch_shapes=[pltpu.VMEM((D, 2 * F), BF16), pltpu.VMEM((2, D, F), F32), pltpu.VMEM((2, D, F), F32),
                            pltpu.VMEM((2, TM, Wp), U32), pltpu.VMEM((2, TM, F), BF16),
                            pltpu.SemaphoreType.DMA((2, 2)), dma2, dma2]),
        out_shape=jax.ShapeDtypeStruct((P, F), BF16),
        compiler_params=_cparams(("arbitrary",)),
        name="moe_up",
    )(seg_start, seg_chunks, xs, w1, w3)
    return pl.pallas_call(
        _moe_down_kernel,
        grid_spec=pltpu.PrefetchScalarGridSpec(
            num_scalar_prefetch=2,
            grid=(E,),
            in_specs=[any_spec, any_spec],
            out_specs=any_spec,
            scratch_shapes=[pltpu.VMEM((F, D), BF16), pltpu.VMEM((2, F, D), F32), pltpu.VMEM((2, TM, F), BF16),
                            pltpu.VMEM((2, TM, D // 2), U32), pltpu.SemaphoreType.DMA((1, 2)), dma2, dma2]),
        out_shape=jax.ShapeDtypeStruct((P, D // 2), U32),
        compiler_params=_cparams(("arbitrary",)),
        name="moe_down",
    )(seg_start, seg_chunks, hmid, w2)


def _combine_kernel(pos_ref, nxt_ref, wt_ref, x_ref, mod_ref, gain_ref, y_hbm, o_ref, buf_ref, sem, *, T):
    i = pl.program_id(0)
    n = pl.num_programs(0)
    slot = i % 2

    def gather(p_ref, s):
        sub = 8

        def issue(tb, carry):
            for r in range(sub):
                t = tb * sub + r
                for k in range(2):
                    p = p_ref[0, 0, 2 * t + k]
                    pltpu.make_async_copy(y_hbm.at[pl.ds(p, 1)], buf_ref.at[s, k, pl.ds(t, 1)],
                                          sem.at[s]).start()
            return carry

        lax.fori_loop(0, T // sub, issue, 0)

    @pl.when(i == 0)
    def _():
        gather(pos_ref, 0)

    @pl.when(i + 1 < n)
    def _():
        gather(nxt_ref, 1 - slot)

    for k in range(2):
        pltpu.make_async_copy(y_hbm.at[pl.ds(0, T)], buf_ref.at[slot, k], sem.at[slot]).wait()

    wt = wt_ref[...]
    w0 = wt[:, 0:1]
    w1 = wt[:, 1:2]
    lo0, hi0 = _unpack_halves(buf_ref[slot, 0])
    lo1, hi1 = _unpack_halves(buf_ref[slot, 1])
    ylo = w0 * lo0 + w1 * lo1
    yhi = w0 * hi0 + w1 * hi1
    D = x_ref.shape[1]
    half = D // 2
    ms = (jnp.sum(ylo * ylo, axis=-1, keepdims=True) + jnp.sum(yhi * yhi, axis=-1, keepdims=True)) / D
    inv = lax.rsqrt(ms + NORM_EPS)
    o_ref[:, :half] = x_ref[:, :half] + mod_ref[5:6, :half] * (ylo * inv * gain_ref[:, :half])
    o_ref[:, half:] = x_ref[:, half:] + mod_ref[5:6, half:] * (yhi * inv * gain_ref[:, half:])


def combine(yp, dest, wts, x1, mod, gain, S):
    N, D = x1.shape
    T = min(128, S)
    tiles_per_b = S // T
    pos_s = dest[:, :2].reshape(N // T, 1, 2 * T)
    n_tiles = N // T
    return pl.pallas_call(
        functools.partial(_combine_kernel, T=T),
        grid=(n_tiles,),
        in_specs=[pl.BlockSpec((1, 1, 2 * T), lambda i: (i, 0, 0), memory_space=pltpu.SMEM),
                  pl.BlockSpec((1, 1, 2 * T), lambda i: (jnp.minimum(i + 1, n_tiles - 1), 0, 0),
                               memory_space=pltpu.SMEM),
                  pl.BlockSpec((T, LANES), lambda i: (i, 0)),
                  pl.BlockSpec((T, D), lambda i: (i, 0)),
                  pl.BlockSpec((None, 6, D), lambda i: (i // tiles_per_b, 0, 0)),
                  pl.BlockSpec((1, D), lambda i: (0, 0)),
                  pl.BlockSpec(memory_space=pl.ANY)],
        out_specs=pl.BlockSpec((T, D), lambda i: (i, 0)),
        out_shape=jax.ShapeDtypeStruct((N, D), F32),
        scratch_shapes=[pltpu.VMEM((2, 2, T, D // 2), U32), pltpu.SemaphoreType.DMA((2,))],
        compiler_params=_cparams(("arbitrary",)),
        name="combine",
    )(pos_s, pos_s, wts, x1, mod, gain.reshape(1, D), yp)


def _moe_layout(counts):
    TM = MOE_ROWS
    chunks = (counts.astype(I32) + TM - 1) // TM
    start = (jnp.cumsum(chunks) - chunks) * TM
    return start, chunks


def kernel(x, c, positions, ada_w, ada_b, mix_norm_pre, mix_norm_post, w_in, conv_dw, conv_dw_bias, conv_ln_gain, conv_ln_bias, w_conv_out, w_attn_out, w_out, ffn_norm_pre, ffn_norm_post, router_group, router_expert, expert_w1, expert_w3, expert_w2):
    B, S, D = x.shape
    N = B * S
    U = D // 4
    depth = ada_w.shape[0]
    xc = x.reshape(N, D)
    for layer in range(depth):
        mod = ada_mod(c, ada_w[layer], ada_b[layer]).reshape(B, 6, D)
        h, cosf, sinf = prenorm_rope(xc, mod, mix_norm_pre[layer], positions, S)
        w_in_b = w_in[layer].astype(BF16)
        outs, stats = [], []
        for g, (window, d) in enumerate(ATTN_PATTERNS):
            assert window // d == ATTN_BLOCK and S % (d * ATTN_BLOCK) == 0
            qkv = qkv_proj(h, w_in_b, cosf, sinf, B, S, U, g, d)
            o, st = dilated_attention(qkv, d)
            outs.append(o)
            stats.append(st)
        am = merge_groups(outs, stats, tuple(d for _, d in ATTN_PATTERNS))
        proj = matmul_cols(h, w_in_b, U, 9, "rest_proj")
        cn = conv_branch(proj, B, S, U, conv_dw[layer], conv_dw_bias[layer], conv_ln_gain[layer], conv_ln_bias[layer])
        z = gate_proj(cn, am, w_conv_out[layer].astype(BF16), w_attn_out[layer].astype(BF16), proj, U)
        y = matmul_cols(z, w_out[layer].astype(BF16), U, 0, "out_proj")
        rcat = jnp.zeros((D, LANES), F32)
        rcat = rcat.at[:, :N_EXPERT_GROUPS].set(router_group[layer])
        rcat = rcat.at[:, N_EXPERT_GROUPS:N_EXPERT_GROUPS + N_EXPERTS].set(router_expert[layer]).astype(BF16)
        x1, hp, eid, wts, rank, cnt = residual_prenorm_router(
            y, xc, mod, mix_norm_post[layer], ffn_norm_pre[layer], rcat, S)
        TM = MOE_ROWS
        P = (2 * N + N_EXPERTS * (TM - 1)) // TM * TM
        seg_start, seg_chunks = _moe_layout(cnt[0, :N_EXPERTS])
        start_row = jnp.zeros((8, LANES), F32).at[:, :N_EXPERTS].set(seg_start.astype(F32)[None, :])
        dest = dest_rows(eid, rank, start_row)
        xs = scatter_rows(hp, dest, P, seg_start, seg_chunks, cnt[0, :N_EXPERTS].astype(I32))
        yp = expert_ffn(xs, seg_start, seg_chunks, expert_w1[layer], expert_w3[layer], expert_w2[layer])
        xc = combine(yp, dest, wts, x1, mod, ffn_norm_post[layer], S)
    return xc.reshape(B, S, D)
```
